```python
import math
import jax
import jax.numpy as jnp
from jax import lax
import numpy as np

D_MODEL = 2048
BATCH = 2
SEQ = 4096
DEPTH = 2
DEC_BATCH = 128
DEC_SEQ = 4
PAST_LEN = 2048
PAGE_SIZE = 128

HEAD_DIM = 128
CONV_A_WIDTH = D_MODEL // 4
CONV_A_K = 3
NSA_HEADS = D_MODEL // (2 * HEAD_DIM)
NSA_KV_HEADS = 2
NSA_REP = NSA_HEADS // NSA_KV_HEADS
NSA_WIDTH = NSA_HEADS * HEAD_DIM
CMP_BLOCK = 32
CMP_STRIDE = 16
CMP_HIDDEN = 2 * HEAD_DIM
SLC_BLOCK = 64
N_SELECT = 16
WINDOW = 512
Q_BLOCK = 128
KV_SLOTS = 4
GDN_HEADS = D_MODEL // (4 * HEAD_DIM)
GDN_WIDTH = GDN_HEADS * HEAD_DIM
GDN_CONV_K = 4
GDN_CHUNK = 64
MIX_WIDTH = CONV_A_WIDTH + NSA_WIDTH + GDN_WIDTH
REL_BUCKETS = 32
REL_MAX_DIST = 1024
D_FF = 11 * D_MODEL // 4
FFN_CONV_K = 3
EPS = 1e-6
NEG_INF = -1e30
FORCE_SCORE = 1e4
IN_SPLITS = (CONV_A_WIDTH, CONV_A_WIDTH, CONV_A_WIDTH,
             NSA_WIDTH, 6 * NSA_KV_HEADS * HEAD_DIM, 3 * NSA_HEADS,
             3 * GDN_WIDTH, GDN_HEADS, GDN_HEADS, GDN_WIDTH)
IN_WIDTH = sum(IN_SPLITS)

kernel_name = 'hybrid_conv_nsa_gdn_decoder_step'


def rmsnorm(x, g):
    xf = x.astype(jnp.float32)
    y = xf * lax.rsqrt(jnp.mean(xf * xf, axis=-1, keepdims=True) + EPS)
    return (y * g.astype(jnp.float32)).astype(x.dtype)


def l2norm(x):
    xf = x.astype(jnp.float32)
    return (xf * lax.rsqrt(jnp.sum(xf * xf, axis=-1, keepdims=True) + EPS)).astype(x.dtype)


def masked_softmax(s, mask):
    s = jnp.where(mask, s, NEG_INF)
    m = jnp.max(s, axis=-1, keepdims=True)
    e = jnp.where(mask, jnp.exp(s - m), 0.0)
    z = jnp.sum(e, axis=-1, keepdims=True)
    return e / jnp.where(z > 0, z, 1.0)


def split_proj(z):
    offs = []
    acc = 0
    for s in IN_SPLITS[:-1]:
        acc += s
        offs.append(acc)
    return jnp.split(z, offs, axis=-1)


def causal_dwconv(x, hist, w):
    K = w.shape[0]
    T = x.shape[1]
    xx = jnp.concatenate([hist.astype(x.dtype), x], axis=1)
    out = xx[:, 0:T] * w[0]
    for i in range(1, K):
        out = out + xx[:, i:i + T] * w[i]
    return out, xx[:, T:]


def rel_bucket(dist):
    n = jnp.maximum(dist, 0)
    exact = REL_BUCKETS // 2
    nf = jnp.maximum(n, 1).astype(jnp.float32)
    far = exact + (jnp.log(nf / exact) / math.log(REL_MAX_DIST / exact)
                   * (REL_BUCKETS - exact)).astype(jnp.int32)
    return jnp.where(n < exact, n, jnp.minimum(far, REL_BUCKETS - 1))


def compress_blocks(rows, pe, w1, w2):
    Bn, L = rows.shape[:2]
    nc = (L - CMP_BLOCK) // CMP_STRIDE + 1
    idx = CMP_STRIDE * jnp.arange(nc)[:, None] + jnp.arange(CMP_BLOCK)[None, :]
    blk = rows[:, idx] + pe[:, None, :]
    blk = jnp.swapaxes(blk, 2, 3).reshape(Bn, nc, NSA_KV_HEADS, CMP_BLOCK * HEAD_DIM)
    return jax.nn.gelu(blk @ w1) @ w2


def cmp_attend(q, q_pos, kc, vc, rel_bias):
    T = q.shape[1]
    nc = kc.shape[1]
    ends = CMP_STRIDE * jnp.arange(nc) + CMP_BLOCK - 1
    dist = q_pos[:, None] - ends[None, :]
    s = jnp.einsum('btgrd,bcgd->bgrtc', q, kc).astype(jnp.float32)
    bias = rel_bias[rel_bucket(dist)].reshape(T, nc, NSA_KV_HEADS, NSA_REP).transpose(2, 3, 0, 1)
    p = masked_softmax(s + bias.astype(jnp.float32), dist >= 0)
    o = jnp.einsum('bgrtc,bcgd->btgrd', p.astype(vc.dtype), vc)
    return o, p


def select_blocks(p_cmp, q_pos, n_slc):
    nc = p_cmp.shape[-1]
    cs = CMP_STRIDE * jnp.arange(nc)
    ss = SLC_BLOCK * jnp.arange(n_slc)
    overlap = ((cs[:, None] < ss[None, :] + SLC_BLOCK)
               & (cs[:, None] + CMP_BLOCK > ss[None, :])).astype(jnp.float32)
    imp = jnp.einsum('bgrtc,cs->bgts', p_cmp, overlap)
    cur = (q_pos // SLC_BLOCK)[:, None]
    blk = jnp.arange(n_slc)[None, :]
    forced = (blk == 0) | (blk == cur) | (blk == cur - 1)
    imp = jnp.where(forced, FORCE_SCORE, imp)
    imp = jnp.where(blk <= cur, imp, -1.0)
    _, idx = lax.top_k(imp, min(N_SELECT, n_slc))
    return idx


def to_blocks(rows, n_slc):
    Bn, L = rows.shape[:2]
    rows = jnp.pad(rows, ((0, 0), (0, n_slc * SLC_BLOCK - L), (0, 0), (0, 0)))
    return rows.reshape(Bn, n_slc, SLC_BLOCK, NSA_KV_HEADS, HEAD_DIM).transpose(0, 3, 1, 2, 4)


def slc_attend(q, q_pos, idx, kb, vb, rel_bias):
    Bn, T = q.shape[:2]
    n_sel = idx.shape[-1]
    bi = jnp.arange(Bn)[:, None, None]
    gi = jnp.arange(NSA_KV_HEADS)[None, :, None]
    flat = idx.reshape(Bn, NSA_KV_HEADS, T * n_sel)
    ks = kb[bi, gi, flat].reshape(Bn, NSA_KV_HEADS, T, n_sel * SLC_BLOCK, HEAD_DIM)
    vs = vb[bi, gi, flat].reshape(Bn, NSA_KV_HEADS, T, n_sel * SLC_BLOCK, HEAD_DIM)
    s = jnp.einsum('btgrd,bgtmd->bgrtm', q, ks).astype(jnp.float32)
    kpos = (idx[..., None] * SLC_BLOCK + jnp.arange(SLC_BLOCK)).reshape(Bn, NSA_KV_HEADS, T, n_sel * SLC_BLOCK)
    dist = q_pos[None, None, :, None] - kpos
    tbl = rel_bias.reshape(REL_BUCKETS, NSA_KV_HEADS, NSA_REP).transpose(1, 0, 2)
    bias = tbl[jnp.arange(NSA_KV_HEADS)[None, :, None, None], rel_bucket(dist)]
    s = s + jnp.moveaxis(bias, -1, 2).astype(jnp.float32)
    p = masked_softmax(s, (dist >= 0)[:, :, None])
    return jnp.einsum('bgrtm,bgtmd->btgrd', p.astype(vs.dtype), vs)


def win_attend_banded(q, k, v, rel_bias):
    Bn, T = q.shape[:2]
    nqb = T // Q_BLOCK
    span = WINDOW + Q_BLOCK
    kp = jnp.pad(k, ((0, 0), (WINDOW, 0), (0, 0), (0, 0)))
    vp = jnp.pad(v, ((0, 0), (WINDOW, 0), (0, 0), (0, 0)))
    kidx = Q_BLOCK * jnp.arange(nqb)[:, None] + jnp.arange(span)[None, :]
    kw = kp[:, kidx]
    vw = vp[:, kidx]
    qb = q.reshape(Bn, nqb, Q_BLOCK, NSA_KV_HEADS, NSA_REP, HEAD_DIM)
    s = jnp.einsum('bnqgrd,bnkgd->bgrnqk', qb, kw).astype(jnp.float32)
    qpos = Q_BLOCK * jnp.arange(nqb)[:, None] + jnp.arange(Q_BLOCK)[None, :]
    kpos = kidx - WINDOW
    dist = qpos[:, :, None] - kpos[:, None, :]
    mask = (dist >= 0) & (dist <= WINDOW) & (kpos[:, None, :] >= 0)
    bias = rel_bias[rel_bucket(dist)].reshape(nqb, Q_BLOCK, span, NSA_KV_HEADS, NSA_REP).transpose(3, 4, 0, 1, 2)
    p = masked_softmax(s + bias.astype(jnp.float32), mask)
    o = jnp.einsum('bgrnqk,bnkgd->bnqgrd', p.astype(vw.dtype), vw)
    return o.reshape(Bn, T, NSA_KV_HEADS, NSA_REP, HEAD_DIM)


def win_attend_direct(q, q_pos, k, v, k_pos, rel_bias):
    T = q.shape[1]
    nk = k.shape[1]
    s = jnp.einsum('btgrd,bkgd->bgrtk', q, k).astype(jnp.float32)
    dist = q_pos[:, None] - k_pos[None, :]
    mask = (dist >= 0) & (dist <= WINDOW)
    bias = rel_bias[rel_bucket(dist)].reshape(T, nk, NSA_KV_HEADS, NSA_REP).transpose(2, 3, 0, 1)
    p = masked_softmax(s + bias.astype(jnp.float32), mask)
    return jnp.einsum('bgrtk,bkgd->btgrd', p.astype(v.dtype), v)


def nsa_mixer(q, q_pos, rows, win_k, win_v, win_pos, gates, k_norm_g, cmp_pe, cmp_w1, cmp_w2, rel_bias):
    Bn, T = q.shape[:2]
    L = rows.shape[1]
    kc = rmsnorm(compress_blocks(rows[:, :, 0], cmp_pe[0], cmp_w1[0], cmp_w2[0]), k_norm_g)
    vc = compress_blocks(rows[:, :, 1], cmp_pe[1], cmp_w1[1], cmp_w2[1])
    o_cmp, p_cmp = cmp_attend(q, q_pos, kc, vc, rel_bias)
    n_slc = -(-L // SLC_BLOCK)
    idx = select_blocks(p_cmp, q_pos, n_slc)
    kb = to_blocks(rows[:, :, 2], n_slc)
    vb = to_blocks(rows[:, :, 3], n_slc)
    if win_pos is None:
        nqb = T // Q_BLOCK
        q_blocks = jnp.moveaxis(q.reshape(Bn, nqb, Q_BLOCK, NSA_KV_HEADS, NSA_REP, HEAD_DIM), 1, 0)
        i_blocks = jnp.moveaxis(idx.reshape(Bn, NSA_KV_HEADS, nqb, Q_BLOCK, idx.shape[-1]), 2, 0)
        p_blocks = q_pos.reshape(nqb, Q_BLOCK)
        o_slc = lax.map(lambda a: slc_attend(a[0], a[2], a[1], kb, vb, rel_bias), (q_blocks, i_blocks, p_blocks))
        o_slc = jnp.moveaxis(o_slc, 0, 1).reshape(Bn, T, NSA_KV_HEADS, NSA_REP, HEAD_DIM)
        o_win = win_attend_banded(q, win_k, win_v, rel_bias)
    else:
        o_slc = slc_attend(q, q_pos, idx, kb, vb, rel_bias)
        o_win = win_attend_direct(q, q_pos, win_k, win_v, win_pos, rel_bias)
    o = (gates[:, :, 0, :, :, None] * o_cmp + gates[:, :, 1, :, :, None] * o_slc
         + gates[:, :, 2, :, :, None] * o_win)
    return o.reshape(Bn, T, NSA_WIDTH)


def gated_delta_chunked(q, k, v, g, beta, s0):
    f32 = jnp.float32
    Bn, T, H, DK = q.shape
    DV = v.shape[-1]
    C = min(GDN_CHUNK, T)
    Tp = -(-T // C) * C
    pad = Tp - T

    def prep(a):
        a = a.astype(f32)
        a = jnp.pad(a, [(0, 0), (0, pad)] + [(0, 0)] * (a.ndim - 2))
        a = a.reshape((Bn, Tp // C, C) + a.shape[2:])
        return jnp.moveaxis(jnp.moveaxis(a, 3, 2), 1, 0)

    q, k, v, g, beta = prep(q), prep(k), prep(v), prep(g), prep(beta)
    gc = jnp.cumsum(g, axis=-1)
    tri_incl = jnp.tril(jnp.ones((C, C), bool))
    tri_strict = jnp.tril(jnp.ones((C, C), bool), -1)
    decay = jnp.exp(jnp.where(tri_incl, gc[..., :, None] - gc[..., None, :], -jnp.inf))
    kb = k * beta[..., None]
    low = jnp.where(tri_strict, jnp.einsum('nbhid,nbhjd->nbhij', kb, k) * decay, 0.0)
    a_mat = low + jnp.eye(C, dtype=f32)
    u = lax.linalg.triangular_solve(a_mat, v * beta[..., None], left_side=True, lower=True, unit_diagonal=True)
    w = lax.linalg.triangular_solve(a_mat, kb * jnp.exp(gc)[..., None], left_side=True, lower=True, unit_diagonal=True)
    qk = jnp.where(tri_incl, jnp.einsum('nbhid,nbhjd->nbhij', q, k) * decay, 0.0)

    def step(S, xs):
        q_c, k_c, u_c, w_c, g_c, qk_c = xs
        v_new = u_c - jnp.einsum('bhcd,bhde->bhce', w_c, S)
        o = (jnp.einsum('bhcd,bhde->bhce', q_c * jnp.exp(g_c)[..., None], S)
             + jnp.einsum('bhij,bhje->bhie', qk_c, v_new))
        g_last = g_c[..., -1]
        S = (S * jnp.exp(g_last)[..., None, None]
             + jnp.einsum('bhcd,bhce->bhde', k_c * jnp.exp(g_last[..., None] - g_c)[..., None], v_new))
        return S, o

    S, o = lax.scan(step, s0.astype(f32), (q, k, u, w, gc, qk))
    o = jnp.swapaxes(jnp.moveaxis(o, 0, 1), 2, 3).reshape(Bn, Tp, H, DV)[:, :T]
    return o, S


def gdn_mixer(qkv, a_in, b_in, z, hist, s0, conv_w, a_log, dt_bias, norm_g):
    Bn, T = qkv.shape[:2]
    u, new_hist = causal_dwconv(qkv, hist, conv_w)
    u = jax.nn.silu(u)
    q, k, v = jnp.split(u, 3, axis=-1)
    q = l2norm(q.reshape(Bn, T, GDN_HEADS, HEAD_DIM)) * HEAD_DIM ** -0.5
    k = l2norm(k.reshape(Bn, T, GDN_HEADS, HEAD_DIM))
    v = v.reshape(Bn, T, GDN_HEADS, HEAD_DIM)
    beta = jax.nn.sigmoid(b_in.astype(jnp.float32))
    g = -jnp.exp(a_log.astype(jnp.float32)) * jax.nn.softplus(a_in.astype(jnp.float32) + dt_bias.astype(jnp.float32))
    o, s_new = gated_delta_chunked(q, k, v, g, beta, s0)
    o = rmsnorm(o.astype(z.dtype), norm_g) * jax.nn.silu(z.reshape(Bn, T, GDN_HEADS, HEAD_DIM))
    return o.reshape(Bn, T, GDN_WIDTH), new_hist, s_new.astype(s0.dtype)


def trunk_layer(x, pos, past_rows, win_buf, hist_a, hist_gdn, s_gdn, hist_ffn, rel_bias, lw):
    (norm_mix_g, w_in, conv_a_w, q_norm_g, k_norm_g, cmp_pe, cmp_w1, cmp_w2,
     gdn_conv_w, gdn_a_log, gdn_dt_bias, gdn_norm_g, out_norm_a, out_norm_b, w_out,
     norm_ffn_g, ffn_up, ffn_conv_w, ffn_down) = lw
    Bn, T, _ = x.shape
    (a_b, a_c, a_h, nq, nkv, ngate, gqkv, ga, gb, gz) = split_proj(rmsnorm(x, norm_mix_g) @ w_in)
    conv_out, new_hist_a = causal_dwconv(a_c * a_h, hist_a, conv_a_w)
    y_a = a_b * conv_out
    q = rmsnorm(nq.reshape(Bn, T, NSA_KV_HEADS, NSA_REP, HEAD_DIM), q_norm_g) * HEAD_DIM ** -0.5
    kv = nkv.reshape(Bn, T, 3, 2, NSA_KV_HEADS, HEAD_DIM)
    rows = jnp.stack([kv[:, :, 0, 0], kv[:, :, 0, 1], rmsnorm(kv[:, :, 1, 0], k_norm_g), kv[:, :, 1, 1]], axis=2)
    win_rows = jnp.stack([rmsnorm(kv[:, :, 2, 0], k_norm_g), kv[:, :, 2, 1]], axis=2)
    gates = jax.nn.sigmoid(ngate.astype(jnp.float32)).reshape(Bn, T, 3, NSA_KV_HEADS, NSA_REP).astype(x.dtype)
    if past_rows is None:
        full_rows = rows
        win_all = win_rows
        win_pos = None
        new_win = win_rows[:, -min(WINDOW, T):]
    else:
        past_len = past_rows.shape[1]
        wb = win_buf.shape[1]
        full_rows = jnp.concatenate([past_rows, rows], axis=1)
        win_all = jnp.concatenate([win_buf, win_rows], axis=1)
        win_pos = past_len - wb + jnp.arange(wb + T)
        new_win = win_all[:, -min(WINDOW, past_len + T):]
    y_b = nsa_mixer(q, pos, full_rows, win_all[:, :, 0], win_all[:, :, 1], win_pos, gates,
                    k_norm_g, cmp_pe, cmp_w1, cmp_w2, rel_bias)
    y_c, new_hist_gdn, s_new = gdn_mixer(gqkv, ga, gb, gz, hist_gdn, s_gdn,
                                         gdn_conv_w, gdn_a_log, gdn_dt_bias, gdn_norm_g)
    mix = jnp.concatenate([rmsnorm(y_a, out_norm_a), rmsnorm(y_b, out_norm_b), y_c], axis=-1)
    h = x + mix @ w_out
    up = rmsnorm(h, norm_ffn_g) @ ffn_up
    up, new_hist_ffn = causal_dwconv(up, hist_ffn, ffn_conv_w)
    ga_f, gv_f = jnp.split(up, 2, axis=-1)
    out = h + (jax.nn.silu(ga_f) * gv_f) @ ffn_down
    return out, (rows, new_win, new_hist_a, new_hist_gdn, s_new, new_hist_ffn)


def setup_inputs(seed: int = 0) -> dict:
    key = jax.random.key(seed)
    ks = iter(jax.random.split(key, 40))
    f32 = jnp.float32

    def nrm(shape, scale):
        return scale * jax.random.normal(next(ks), shape, f32)

    def gain(shape):
        return 1.0 + nrm(shape, 0.02)

    n_pages = PAST_LEN // PAGE_SIZE
    n_used = DEC_BATCH * n_pages
    n_pool = n_used + max(1, n_used // 4)
    wb = min(WINDOW, PAST_LEN)
    x_prompt = nrm((BATCH, SEQ, D_MODEL), 1.0)
    x_sample = nrm((DEC_BATCH, DEC_SEQ, D_MODEL), 1.0)
    cache_kv = nrm((DEPTH, n_pool, PAGE_SIZE, KV_SLOTS, NSA_KV_HEADS, HEAD_DIM), 1.0)
    cache_win = nrm((DEPTH, DEC_BATCH, wb, 2, NSA_KV_HEADS, HEAD_DIM), 1.0)
    state_conv_a = nrm((DEPTH, DEC_BATCH, CONV_A_K - 1, CONV_A_WIDTH), 1.0)
    state_gdn_conv = nrm((DEPTH, DEC_BATCH, GDN_CONV_K - 1, 3 * GDN_WIDTH), 1.0)
    state_gdn = nrm((DEPTH, DEC_BATCH, GDN_HEADS, HEAD_DIM, HEAD_DIM), 0.1)
    state_ffn_conv = nrm((DEPTH, DEC_BATCH, FFN_CONV_K - 1, 2 * D_FF), 1.0)
    page_table = jax.random.permutation(next(ks), n_pool)[:n_used].reshape(DEC_BATCH, n_pages).astype(jnp.int32)
    rel_bias = nrm((REL_BUCKETS, NSA_HEADS), 0.5)
    dt = jnp.exp(jax.random.uniform(next(ks), (DEPTH, GDN_HEADS), f32, math.log(1e-3), math.log(1e-1)))
    return {
        'x_prompt': x_prompt,
        'x_sample': x_sample,
        'cache_kv': cache_kv,
        'cache_win': cache_win,
        'state_conv_a': state_conv_a,
        'state_gdn_conv': state_gdn_conv,
        'state_gdn': state_gdn,
        'state_ffn_conv': state_ffn_conv,
        'page_table': page_table,
        'rel_bias': rel_bias,
        'norm_mix_g': gain((DEPTH, D_MODEL)),
        'w_in': nrm((DEPTH, D_MODEL, IN_WIDTH), D_MODEL ** -0.5),
        'conv_a_w': nrm((DEPTH, CONV_A_K, CONV_A_WIDTH), CONV_A_K ** -0.5),
        'q_norm_g': gain((DEPTH, HEAD_DIM)),
        'k_norm_g': gain((DEPTH, HEAD_DIM)),
        'cmp_pe': nrm((DEPTH, 2, CMP_BLOCK, HEAD_DIM), 0.1),
        'cmp_w1': nrm((DEPTH, 2, CMP_BLOCK * HEAD_DIM, CMP_HIDDEN), (CMP_BLOCK * HEAD_DIM) ** -0.5),
        'cmp_w2': nrm((DEPTH, 2, CMP_HIDDEN, HEAD_DIM), CMP_HIDDEN ** -0.5),
        'gdn_conv_w': nrm((DEPTH, GDN_CONV_K, 3 * GDN_WIDTH), GDN_CONV_K ** -0.5),
        'gdn_a_log': jnp.log(jax.random.uniform(next(ks), (DEPTH, GDN_HEADS), f32, 1.0, 16.0)),
        'gdn_dt_bias': dt + jnp.log(-jnp.expm1(-dt)),
        'gdn_norm_g': gain((DEPTH, HEAD_DIM)),
        'out_norm_a': gain((DEPTH, CONV_A_WIDTH)),
        'out_norm_b': gain((DEPTH, NSA_WIDTH)),
        'w_out': nrm((DEPTH, MIX_WIDTH, D_MODEL), MIX_WIDTH ** -0.5),
        'norm_ffn_g': gain((DEPTH, D_MODEL)),
        'ffn_up': nrm((DEPTH, D_MODEL, 2 * D_FF), D_MODEL ** -0.5),
        'ffn_conv_w': nrm((DEPTH, FFN_CONV_K, 2 * D_FF), FFN_CONV_K ** -0.5),
        'ffn_down': nrm((DEPTH, D_FF, D_MODEL), D_FF ** -0.5),
    }


def reference(x_prompt, x_sample, cache_kv, cache_win, state_conv_a, state_gdn_conv, state_gdn,
              state_ffn_conv, page_table, rel_bias, norm_mix_g, w_in, conv_a_w, q_norm_g, k_norm_g,
              cmp_pe, cmp_w1, cmp_w2, gdn_conv_w, gdn_a_log, gdn_dt_bias, gdn_norm_g, out_norm_a,
              out_norm_b, w_out, norm_ffn_g, ffn_up, ffn_conv_w, ffn_down):
    Bp, T = x_prompt.shape[:2]
    Ds, Ts = x_sample.shape[:2]
    past_len = page_table.shape[1] * cache_kv.shape[2]
    pos_p = jnp.arange(T)
    pos_s = past_len + jnp.arange(Ts)
    dt = x_prompt.dtype
    xp, xs = x_prompt, x_sample
    st_p, st_s = [], []
    for l in range(DEPTH):
        lw = (norm_mix_g[l], w_in[l], conv_a_w[l], q_norm_g[l], k_norm_g[l], cmp_pe[l], cmp_w1[l], cmp_w2[l],
              gdn_conv_w[l], gdn_a_log[l], gdn_dt_bias[l], gdn_norm_g[l], out_norm_a[l], out_norm_b[l], w_out[l],
              norm_ffn_g[l], ffn_up[l], ffn_conv_w[l], ffn_down[l])
        xp, sp = trunk_layer(
            xp, pos_p, None, None,
            jnp.zeros((Bp, CONV_A_K - 1, CONV_A_WIDTH), dt),
            jnp.zeros((Bp, GDN_CONV_K - 1, 3 * GDN_WIDTH), dt),
            jnp.zeros((Bp, GDN_HEADS, HEAD_DIM, HEAD_DIM), dt),
            jnp.zeros((Bp, FFN_CONV_K - 1, 2 * D_FF), dt),
            rel_bias, lw)
        past_rows = cache_kv[l, page_table].reshape(Ds, past_len, KV_SLOTS, NSA_KV_HEADS, HEAD_DIM)
        xs, ss = trunk_layer(
            xs, pos_s, past_rows, cache_win[l], state_conv_a[l], state_gdn_conv[l], state_gdn[l],
            state_ffn_conv[l], rel_bias, lw)
        st_p.append(sp)
        st_s.append(ss)
    kv_rows_prompt = jnp.stack([s[0] for s in st_p])
    kv_rows_sample = jnp.stack([s[0] for s in st_s])
    win_prompt = jnp.stack([s[1] for s in st_p])
    win_sample = jnp.stack([s[1] for s in st_s])
    conv_a_prompt = jnp.stack([s[2] for s in st_p])
    conv_a_sample = jnp.stack([s[2] for s in st_s])
    gdn_conv_prompt = jnp.stack([s[3] for s in st_p])
    gdn_conv_sample = jnp.stack([s[3] for s in st_s])
    gdn_state_prompt = jnp.stack([s[4] for s in st_p])
    gdn_state_sample = jnp.stack([s[4] for s in st_s])
    ffn_conv_prompt = jnp.stack([s[5] for s in st_p])
    ffn_conv_sample = jnp.stack([s[5] for s in st_s])
    return (xp, xs, kv_rows_prompt, kv_rows_sample, win_prompt, win_sample, conv_a_prompt, conv_a_sample,
            gdn_conv_prompt, gdn_conv_sample, gdn_state_prompt, gdn_state_sample, ffn_conv_prompt, ffn_conv_sample)
```

```python
import functools
import math

import jax
import jax.numpy as jnp
from jax import lax
from jax.experimental import pallas as pl
from jax.experimental.pallas import tpu as pltpu

F32 = jnp.float32
BF16 = jnp.bfloat16
HIGHEST = lax.Precision.HIGHEST

SUBLANES = 8
LANES = 128
VMEM_LIMIT_BYTES = 52 * 1024 * 1024

HEAD_DIM = 128
NSA_KV_HEADS = 2
NSA_REP = 4
NSA_HEADS = NSA_KV_HEADS * NSA_REP
GDN_HEADS = 4
CMP_BLOCK = 32
CMP_STRIDE = 16
CMP_HIDDEN = 256
SLC_BLOCK = 64
N_SELECT = 16
WINDOW = 512
Q_BLOCK = 128
GDN_CHUNK = 64
REL_BUCKETS = 32
REL_MAX_DIST = 1024
EPS = 1e-6
NEG_INF = -1e30
FORCE_SCORE = 1e4

C_AB, C_AC, C_AH, C_NQ, C_NKV, C_GQKV, C_GZ, C_END = 0, 512, 1024, 1536, 2560, 4096, 5632, 6144
S_GATE, S_GA, S_GB = 0, 24, 28


def _cp(sem):
    return pltpu.CompilerParams(dimension_semantics=sem, vmem_limit_bytes=VMEM_LIMIT_BYTES)


def _div_pow2(x, d):
    assert d & (d - 1) == 0
    return jnp.right_shift(x, d.bit_length() - 1)


def _mod_pow2(x, d):
    assert d & (d - 1) == 0
    return jnp.bitwise_and(x, d - 1)


def _sigmoid(x):
    return 1.0 / (1.0 + jnp.exp(-x))


def _rms(x, g):
    return x * lax.rsqrt(jnp.mean(x * x, axis=-1, keepdims=True) + EPS) * g


def _l2n(x):
    return x * lax.rsqrt(jnp.sum(x * x, axis=-1, keepdims=True) + EPS)


def _dot_t(a, b, precision=None):
    return lax.dot_general(a, b, (((1,), (1,)), ((), ())), precision=precision, preferred_element_type=F32)


def _ld(ref, c0, c1):
    if len(ref.shape) == 3:
        v = ref[:, :, c0:c1]
        return v.reshape(v.shape[0] * v.shape[1], v.shape[2])
    return ref[:, c0:c1]


def _st(ref, c0, c1, val):
    if len(ref.shape) == 3:
        ref[:, :, c0:c1] = val.reshape(ref.shape[0], ref.shape[1], c1 - c0).astype(ref.dtype)
    else:
        ref[:, c0:c1] = val.astype(ref.dtype)


def _shift(x, halo, s, tstride):
    n = s * tstride
    rows = x.shape[0]
    if tstride % SUBLANES == 0:
        hr = halo.shape[0]
        return jnp.concatenate([halo[hr - n:], x[:rows - n]], axis=0)
    xs = pltpu.roll(x, n, 0)
    hs = pltpu.roll(halo, n, 0)
    rid = lax.broadcasted_iota(jnp.int32, (SUBLANES, x.shape[1]), 0)
    head = jnp.where(rid < n, hs, xs[:SUBLANES])
    return jnp.concatenate([head, xs[SUBLANES:]], axis=0)


def _norm_matmul_kernel(x_ref, g_ref, w_ref, *rest, has_small):
    if has_small:
        ws_ref, o_ref, os_ref, xn_ref = rest
    else:
        o_ref, xn_ref = rest

    @pl.when(pl.program_id(1) == 0)
    def _():
        xn = _rms(x_ref[...], g_ref[...]).astype(BF16)
        xn_ref[...] = xn
        if has_small:
            os_ref[...] = jnp.dot(xn, ws_ref[...], preferred_element_type=F32)

    o_ref[...] = jnp.dot(xn_ref[...], w_ref[...], preferred_element_type=F32).astype(o_ref.dtype)


def _norm_matmul(x, g, w, w_small=None, *, tm=512, tn=512, out_dtype=F32):
    M, K = x.shape
    N = w.shape[1]
    tm = min(tm, M)
    has_small = w_small is not None
    in_specs = [pl.BlockSpec((tm, K), lambda i, j: (i, 0)),
                pl.BlockSpec((1, K), lambda i, j: (0, 0)),
                pl.BlockSpec((K, tn), lambda i, j: (0, j))]
    out_specs = [pl.BlockSpec((tm, tn), lambda i, j: (i, j))]
    out_shape = [jax.ShapeDtypeStruct((M, N), out_dtype)]
    args = [x, g.reshape(1, K), w]
    if has_small:
        in_specs.append(pl.BlockSpec((K, LANES), lambda i, j: (0, 0)))
        out_specs.append(pl.BlockSpec((tm, LANES), lambda i, j: (i, 0)))
        out_shape.append(jax.ShapeDtypeStruct((M, LANES), F32))
        args.append(w_small)
    res = pl.pallas_call(
        functools.partial(_norm_matmul_kernel, has_small=has_small),
        grid=(M // tm, N // tn),
        in_specs=in_specs, out_specs=out_specs, out_shape=out_shape,
        scratch_shapes=[pltpu.VMEM((tm, K), BF16)],
        compiler_params=_cp(("parallel", "arbitrary")),
        name="norm_matmul_small" if has_small else "norm_matmul",
    )(*args)
    return res if has_small else res[0]


def _prep_kernel(z_ref, zp_ref, zs_ref, hu_ref, hq_ref, caw_ref, gcw_ref, qg_ref, kg_ref, ona_ref,
                 alog_ref, dtb_ref,
                 ya_ref, q_ref, rows_ref, win_ref, kvb_ref, gates_ref, gq_ref, ggb_ref, huo_ref,
                 *, tstride, tiles_per_seq):
    hd = HEAD_DIM
    u = _ld(z_ref, C_AC, C_AH) * _ld(z_ref, C_AH, C_NQ)
    if tstride == 1:
        start = (pl.program_id(0) % tiles_per_seq) == 0
        halo_u = jnp.where(start, hu_ref[0], zp_ref[:, C_AC:C_AH] * zp_ref[:, C_AH:C_NQ])
        halo_q = jnp.where(start, hq_ref[0], zp_ref[:, C_GQKV:C_GZ])
    else:
        halo_u = _ld(hu_ref, 0, 512)
        halo_q = _ld(hq_ref, 0, 1536)
    caw = caw_ref[...]
    conv = caw[0:1] * _shift(u, halo_u, 2, tstride) + caw[1:2] * _shift(u, halo_u, 1, tstride) + caw[2:3] * u
    y = _ld(z_ref, C_AB, C_AC) * conv
    _st(ya_ref, 0, 512, _rms(y, ona_ref[...]))
    hu_rows = huo_ref.shape[0] * huo_ref.shape[1] if tstride != 1 else SUBLANES
    if tstride == 1:
        huo_ref[0] = u[u.shape[0] - hu_rows:]
    else:
        _st(huo_ref, 0, 512, u[u.shape[0] - hu_rows:])

    qg = qg_ref[...]
    kg = kg_ref[...]
    for h in range(NSA_HEADS):
        c0 = C_NQ + h * hd
        _st(q_ref, h * hd, (h + 1) * hd, _rms(_ld(z_ref, c0, c0 + hd), qg) * (hd ** -0.5))
    _st(rows_ref, 0, 512, _ld(z_ref, C_NKV, C_NKV + 512))
    for g in range(NSA_KV_HEADS):
        c0 = C_NKV + 512 + g * hd
        kn = _rms(_ld(z_ref, c0, c0 + hd), kg)
        _st(rows_ref, 512 + g * hd, 512 + (g + 1) * hd, kn)
        _st(kvb_ref, g * hd, (g + 1) * hd, kn)
    vs = _ld(z_ref, C_NKV + 768, C_NKV + 1024)
    _st(rows_ref, 768, 1024, vs)
    _st(kvb_ref, 256, 512, vs)
    for g in range(NSA_KV_HEADS):
        c0 = C_NKV + 1024 + g * hd
        kn = _rms(_ld(z_ref, c0, c0 + hd), kg)
        _st(win_ref, g * hd, (g + 1) * hd, kn)
        _st(kvb_ref, 512 + g * hd, 512 + (g + 1) * hd, kn)
    vw = _ld(z_ref, C_NKV + 1280, C_NKV + 1536)
    _st(win_ref, 256, 512, vw)
    _st(kvb_ref, 768, 1024, vw)

    zs = _ld(zs_ref, 0, LANES)
    sg = _sigmoid(zs)
    _st(gates_ref, 0, LANES, sg)
    xs = zs + dtb_ref[...]
    softplus = jnp.maximum(xs, 0.0) + jnp.log(1.0 + jnp.exp(-jnp.abs(xs)))
    gdec = -jnp.exp(alog_ref[...]) * softplus
    rows = zs.shape[0]
    for h in range(GDN_HEADS):
        _st(ggb_ref, h * hd, (h + 1) * hd, jnp.broadcast_to(gdec[:, S_GA + h:S_GA + h + 1], (rows, hd)))
        _st(ggb_ref, 512 + h * hd, 512 + (h + 1) * hd, jnp.broadcast_to(sg[:, S_GB + h:S_GB + h + 1], (rows, hd)))

    gcw = gcw_ref[...]
    for part in range(3):
        c0 = C_GQKV + part * 512
        x = _ld(z_ref, c0, c0 + 512)
        hq = halo_q[:, part * 512:(part + 1) * 512]
        w = gcw[:, part * 512:(part + 1) * 512]
        c = (w[0:1] * _shift(x, hq, 3, tstride) + w[1:2] * _shift(x, hq, 2, tstride)
             + w[2:3] * _shift(x, hq, 1, tstride) + w[3:4] * x)
        c = c * _sigmoid(c)
        if part == 2:
            _st(gq_ref, 1024, 1536, c)
        else:
            for h in range(GDN_HEADS):
                v = _l2n(c[:, h * hd:(h + 1) * hd])
                if part == 0:
                    v = v * (hd ** -0.5)
                _st(gq_ref, part * 512 + h * hd, part * 512 + (h + 1) * hd, v)


def _prep(z, zs, hist_u, hist_q, lw, *, sample, nseq, seq_len):
    M = z.shape[0]
    wts = [lw["conv_a_w"], lw["gdn_conv_w"], lw["q_norm_g"].reshape(1, -1), lw["k_norm_g"].reshape(1, -1),
           lw["out_norm_a"].reshape(1, -1), lw["alog_v"], lw["dtb_v"]]
    wspecs2 = [pl.BlockSpec(w.shape, lambda i: (0, 0)) for w in wts]
    widths = [(512, BF16), (1024, BF16), (1024, F32), (512, F32), (1024, BF16), (LANES, F32), (1536, F32), (1024, F32)]
    if not sample:
        tm = 256
        tps = seq_len // tm
        in_specs = [pl.BlockSpec((tm, C_END), lambda i: (i, 0)),
                    pl.BlockSpec((SUBLANES, C_END), lambda i: (jnp.maximum(i * (tm // SUBLANES) - 1, 0), 0)),
                    pl.BlockSpec((tm, LANES), lambda i: (i, 0)),
                    pl.BlockSpec((1, SUBLANES, 512), lambda i: (i // tps, 0, 0)),
                    pl.BlockSpec((1, SUBLANES, 1536), lambda i: (i // tps, 0, 0))] + wspecs2
        out_specs = [pl.BlockSpec((tm, w), lambda i: (i, 0)) for w, _ in widths]
        out_specs.append(pl.BlockSpec((1, SUBLANES, 512), lambda i: (i // tps, 0, 0)))
        out_shape = [jax.ShapeDtypeStruct((M, w), d) for w, d in widths]
        out_shape.append(jax.ShapeDtypeStruct((nseq, SUBLANES, 512), F32))
        return pl.pallas_call(
            functools.partial(_prep_kernel, tstride=1, tiles_per_seq=tps),
            grid=(M // tm,), in_specs=in_specs, out_specs=out_specs, out_shape=out_shape,
            compiler_params=_cp(("arbitrary",)), name="prep_prompt",
        )(z, z, zs, hist_u, hist_q, *wts)
    T = seq_len
    bt = 64
    z3 = z.reshape(T, nseq, C_END)
    zs3 = zs.reshape(T, nseq, LANES)
    in_specs = [pl.BlockSpec((T, bt, C_END), lambda i: (0, i, 0)),
                pl.BlockSpec((SUBLANES, C_END), lambda i: (0, 0)),
                pl.BlockSpec((T, bt, LANES), lambda i: (0, i, 0)),
                pl.BlockSpec((hist_u.shape[0], bt, 512), lambda i: (0, i, 0)),
                pl.BlockSpec((hist_q.shape[0], bt, 1536), lambda i: (0, i, 0))] + wspecs2
    out_specs = [pl.BlockSpec((T, bt, w), lambda i: (0, i, 0)) for w, _ in widths]
    out_specs.append(pl.BlockSpec((hist_u.shape[0], bt, 512), lambda i: (0, i, 0)))
    out_shape = [jax.ShapeDtypeStruct((T, nseq, w), d) for w, d in widths]
    out_shape.append(jax.ShapeDtypeStruct((hist_u.shape[0], nseq, 512), F32))
    outs = pl.pallas_call(
        functools.partial(_prep_kernel, tstride=bt, tiles_per_seq=1),
        grid=(nseq // bt,), in_specs=in_specs, out_specs=out_specs, out_shape=out_shape,
        compiler_params=_cp(("arbitrary",)), name="prep_sample",
    )(z3, z, zs3, hist_u, hist_q, *wts)
    return [o.reshape(M, o.shape[-1]) for o in outs[:-1]] + [outs[-1]]


def _bias_kernel(thr_ref, rb_ref, o_ref, *, a0, an, qs, ks):
    n = pl.program_id(0)
    h = pl.program_id(1)
    R, C = o_ref.shape[-2:]
    dist = (a0 + an * n + qs * lax.broadcasted_iota(jnp.int32, (R, C), 0)
            - ks * lax.broadcasted_iota(jnp.int32, (R, C), 1))
    b = jnp.full((R, C), rb_ref[h], F32)
    for k in range(1, REL_BUCKETS):
        b = jnp.where(dist >= thr_ref[k], rb_ref[k * NSA_HEADS + h], b)
    o_ref[0, 0] = b


def _bias_table(thr, rb_flat, n, R, C, *, a0, an, qs, ks):
    return pl.pallas_call(
        functools.partial(_bias_kernel, a0=a0, an=an, qs=qs, ks=ks),
        grid=(n, NSA_HEADS),
        in_specs=[pl.BlockSpec(memory_space=pltpu.SMEM), pl.BlockSpec(memory_space=pltpu.SMEM)],
        out_specs=pl.BlockSpec((1, 1, R, C), lambda i, j: (i, j, 0, 0)),
        out_shape=jax.ShapeDtypeStruct((n, NSA_HEADS, R, C), F32),
        compiler_params=_cp(("parallel", "parallel")), name="bias_table",
    )(thr, rb_flat)


def _bucket_thresholds():
    n = jnp.arange(REL_MAX_DIST + 1)
    exact = REL_BUCKETS // 2
    nf = jnp.maximum(n, 1).astype(F32)
    far = exact + (jnp.log(nf / exact) / math.log(REL_MAX_DIST / exact) * (REL_BUCKETS - exact)).astype(jnp.int32)
    bucket = jnp.where(n < exact, n, jnp.minimum(far, REL_BUCKETS - 1))
    return jnp.sum(bucket[None, :] < jnp.arange(REL_BUCKETS)[:, None], axis=1).astype(jnp.int32)


def _gelu_tanh(x):
    return x * (0.5 * (1.0 + jnp.tanh(math.sqrt(2.0 / math.pi) * (x + 0.044715 * (x * x * x)))))


def _compress_math(get_rows, ng, pe, w1_ref, w2):
    half = CMP_BLOCK // 2
    top = jnp.zeros((ng, CMP_HIDDEN), F32)
    bot = jnp.zeros((ng, CMP_HIDDEN), F32)
    for r in range(half):
        s = get_rows(r)
        top = top + jnp.dot((s + pe[r:r + 1]).astype(BF16), w1_ref[r], preferred_element_type=F32)
        bot = bot + jnp.dot((s + pe[r + half:r + half + 1]).astype(BF16), w1_ref[r + half],
                            preferred_element_type=F32)
    h = top + pltpu.roll(bot, ng - 1, 0)
    return jnp.dot(_gelu_tanh(h).astype(BF16), w2, preferred_element_type=F32)


def _compress_kernel(x_ref, pe_ref, w1_ref, w2_ref, kg_ref, o_ref):
    ng = x_ref.shape[1] // CMP_STRIDE
    slot = pl.program_id(1)

    def get_rows(r):
        return x_ref[0, pl.ds(r, ng, stride=CMP_STRIDE), :]

    out = _compress_math(get_rows, ng, pe_ref[0], w1_ref.at[0], w2_ref[0])
    o_ref[0, 0, 0] = jnp.where(slot == 0, _rms(out, kg_ref[...]), out).astype(o_ref.dtype)


def _compress_prompt(rows3, lw):
    B, T, _ = rows3.shape
    ng = T // CMP_STRIDE
    G = NSA_KV_HEADS
    return pl.pallas_call(
        _compress_kernel,
        grid=(B, 2, G),
        in_specs=[pl.BlockSpec((1, T, HEAD_DIM), lambda b, s, g: (b, 0, s * G + g)),
                  pl.BlockSpec((1, CMP_BLOCK, HEAD_DIM), lambda b, s, g: (s, 0, 0)),
                  pl.BlockSpec((1, CMP_BLOCK, HEAD_DIM, CMP_HIDDEN), lambda b, s, g: (s, 0, 0, 0)),
                  pl.BlockSpec((1, CMP_HIDDEN, HEAD_DIM), lambda b, s, g: (s, 0, 0)),
                  pl.BlockSpec((1, HEAD_DIM), lambda b, s, g: (0, 0))],
        out_specs=pl.BlockSpec((1, 1, 1, ng, HEAD_DIM), lambda b, s, g: (b, s, g, 0, 0)),
        out_shape=jax.ShapeDtypeStruct((B, 2, G, ng, HEAD_DIM), BF16),
        compiler_params=_cp(("parallel", "parallel", "parallel")), name="compress_prompt",
    )(rows3, lw["cmp_pe"], lw["cmp_w1"], lw["cmp_w2"], lw["k_norm_g"].reshape(1, -1))


def _masked_softmax(s, mask):
    s = jnp.where(mask, s, NEG_INF)
    m = jnp.max(s, axis=-1, keepdims=True)
    e = jnp.where(mask, jnp.exp(s - m), 0.0)
    z = jnp.sum(e, axis=-1, keepdims=True)
    return e / jnp.where(z > 0, z, 1.0)


def _overlap_matrix(ncp):
    c = CMP_STRIDE * lax.broadcasted_iota(jnp.int32, (ncp, LANES), 0)
    s = SLC_BLOCK * lax.broadcasted_iota(jnp.int32, (ncp, LANES), 1)
    return jnp.where(c < s + SLC_BLOCK, jnp.where(c + CMP_BLOCK > s, 1.0, 0.0), 0.0).astype(F32)


def _select_mask(imp, cur, n_slc):
    R = imp.shape[0]
    blk = lax.broadcasted_iota(jnp.int32, (R, LANES), 1)
    for forced_blk in (0, cur, cur - 1):
        imp = jnp.where(blk == forced_blk, FORCE_SCORE, imp)
    imp = jnp.where(blk <= cur, imp, -1.0)
    imp = jnp.where(blk < n_slc, imp, -2.0)
    cnt = jnp.zeros((R, LANES), F32)
    for k in range(n_slc):
        col = imp[:, k:k + 1]
        cnt = cnt + jnp.where(col > imp, 1.0, jnp.where(col == imp, jnp.where(blk > k, 1.0, 0.0), 0.0))
    return jnp.where(cnt < float(min(N_SELECT, n_slc)), jnp.where(blk < n_slc, 1.0, 0.0), 0.0)


def _expand_matrix(nkeys):
    s = lax.broadcasted_iota(jnp.int32, (LANES, nkeys), 0)
    j = lax.broadcasted_iota(jnp.int32, (LANES, nkeys), 1)
    lo = s * SLC_BLOCK
    return jnp.where(j >= lo, jnp.where(j < lo + SLC_BLOCK, 1.0, 0.0), 0.0).astype(BF16)


def _nsa_prompt_kernel(q_ref, kc_ref, vc_ref, ks_ref, vs_ref, kw_ref, vw_ref, bt_ref, bc_ref, gates_ref,
                       o_ref, selk_ref, m_ref, l_ref, acc_ref, *, T):
    qb = pl.program_id(2)
    QB = Q_BLOCK
    R = NSA_REP
    nc = (T - CMP_BLOCK) // CMP_STRIDE + 1
    ncp = kc_ref.shape[-2]
    n_slc = -(-T // SLC_BLOCK)
    q = jnp.concatenate([q_ref[:, r * HEAD_DIM:(r + 1) * HEAD_DIM] for r in range(R)], axis=0)

    s = _dot_t(q, kc_ref[0, 0, 0]) + bc_ref[0].reshape(R * QB, ncp)
    t_row = lax.broadcasted_iota(jnp.int32, (R, QB, ncp), 1).reshape(R * QB, ncp)
    c_col = lax.broadcasted_iota(jnp.int32, (R * QB, ncp), 1)
    dist = qb * QB + t_row - (CMP_STRIDE * c_col + CMP_BLOCK - 1)
    p = _masked_softmax(s, jnp.where(c_col < nc, dist, -1) >= 0)
    o_cmp = jnp.dot(p.astype(BF16), vc_ref[0, 0, 0], preferred_element_type=F32)

    psum = p[0:QB]
    for r in range(1, R):
        psum = psum + p[r * QB:(r + 1) * QB]
    imp = jnp.dot(psum, _overlap_matrix(ncp), precision=HIGHEST, preferred_element_type=F32)
    cur = _div_pow2(qb * QB + lax.broadcasted_iota(jnp.int32, (QB, 1), 0), SLC_BLOCK)
    sel = _select_mask(imp, cur, n_slc)
    selk_ref[...] = jnp.dot(sel.astype(BF16), _expand_matrix(T), preferred_element_type=F32)

    ti = lax.broadcasted_iota(jnp.int32, (QB, QB), 0)
    kj = lax.broadcasted_iota(jnp.int32, (QB, QB), 1)
    rel = ti - kj

    def flash(k_ref, v_ref, lo, use_sel):
        m_ref[...] = jnp.full(m_ref.shape, NEG_INF, F32)
        l_ref[...] = jnp.zeros(l_ref.shape, F32)
        acc_ref[...] = jnp.zeros(acc_ref.shape, F32)

        def body(kb, carry):
            off = pl.multiple_of(kb * QB, QB)
            delta = qb - kb
            sc = _dot_t(q, k_ref[0, pl.ds(off, QB), :]) + bt_ref[delta].reshape(R * QB, QB)
            d = delta * QB + rel
            if use_sel:
                d = jnp.where(selk_ref[:, pl.ds(off, QB)] > 0.5, d, -1)
            else:
                d = jnp.where(d <= WINDOW, d, -1)
            mask = jnp.concatenate([d] * R, axis=0) >= 0
            sc = jnp.where(mask, sc, NEG_INF)
            m_old = m_ref[...]
            m_new = jnp.maximum(m_old, jnp.max(sc, axis=-1, keepdims=True))
            pe = jnp.where(mask, jnp.exp(sc - m_new), 0.0)
            alpha = jnp.exp(m_old - m_new)
            l_ref[...] = alpha * l_ref[...] + jnp.sum(pe, axis=-1, keepdims=True)
            acc_ref[...] = alpha * acc_ref[...] + jnp.dot(pe.astype(BF16), v_ref[0, pl.ds(off, QB), :],
                                                          preferred_element_type=F32)
            m_ref[...] = m_new
            return carry

        lax.fori_loop(lo, qb + 1, body, 0)
        l = l_ref[...]
        return acc_ref[...] / jnp.where(l > 0, l, 1.0)

    o_slc = flash(ks_ref, vs_ref, 0, True)
    o_win = flash(kw_ref, vw_ref, jnp.maximum(qb - WINDOW // QB, 0), False)

    gt = gates_ref[0]
    for r in range(R):
        sl = slice(r * QB, (r + 1) * QB)
        o = (gt[:, r:r + 1] * o_cmp[sl] + gt[:, R + r:R + r + 1] * o_slc[sl]
             + gt[:, 2 * R + r:2 * R + r + 1] * o_win[sl])
        o_ref[:, r * HEAD_DIM:(r + 1) * HEAD_DIM] = o


def _nsa_prompt(q, cmp_kv, kvb3, gates_g, bias_tiles, bias_cmp, *, B, T):
    G = NSA_KV_HEADS
    R = NSA_REP
    nqb = T // Q_BLOCK
    ncp = T // CMP_STRIDE
    kv_spec = lambda col: pl.BlockSpec((1, T, HEAD_DIM), lambda b, g, i, col=col: (b, 0, col + g))
    return pl.pallas_call(
        functools.partial(_nsa_prompt_kernel, T=T),
        grid=(B, G, nqb),
        in_specs=[pl.BlockSpec((Q_BLOCK, R * HEAD_DIM), lambda b, g, i: (b * nqb + i, g)),
                  pl.BlockSpec((1, 1, 1, ncp, HEAD_DIM), lambda b, g, i: (b, 0, g, 0, 0)),
                  pl.BlockSpec((1, 1, 1, ncp, HEAD_DIM), lambda b, g, i: (b, 1, g, 0, 0)),
                  kv_spec(0), kv_spec(2), kv_spec(4), kv_spec(6),
                  pl.BlockSpec((nqb, R, Q_BLOCK, Q_BLOCK), lambda b, g, i: (0, g, 0, 0)),
                  pl.BlockSpec((1, R, Q_BLOCK, ncp), lambda b, g, i: (i, g, 0, 0)),
                  pl.BlockSpec((1, Q_BLOCK, LANES), lambda b, g, i: (g, b * nqb + i, 0))],
        out_specs=pl.BlockSpec((Q_BLOCK, R * HEAD_DIM), lambda b, g, i: (b * nqb + i, g)),
        out_shape=jax.ShapeDtypeStruct((B * T, G * R * HEAD_DIM), F32),
        scratch_shapes=[pltpu.VMEM((Q_BLOCK, T), F32),
                        pltpu.VMEM((R * Q_BLOCK, 1), F32),
                        pltpu.VMEM((R * Q_BLOCK, 1), F32),
                        pltpu.VMEM((R * Q_BLOCK, HEAD_DIM), F32)],
        compiler_params=_cp(("parallel", "parallel", "arbitrary")), name="nsa_prompt",
    )(q, cmp_kv, cmp_kv, kvb3, kvb3, kvb3, kvb3, bias_tiles, bias_cmp, gates_g)


def _nsa_sample_kernel(pt_ref, *refs, npages, tq):
    pages = refs[:npages]
    (win_ref, q_ref, kvn_ref, gates_ref, pe_ref, w1_ref, w2_ref, kg_ref, bc_ref, bs_ref, bw_ref, o_ref,
     xc_ref) = refs[npages:]
    G, R, hd = NSA_KV_HEADS, NSA_REP, HEAD_DIM
    page = pages[0].shape[2]
    past = npages * page
    L = past + tq
    nc = (L - CMP_BLOCK) // CMP_STRIDE + 1
    ng = past // CMP_STRIDE
    n_slc = -(-L // SLC_BLOCK)
    rows = R * tq
    wb = win_ref.shape[2]
    pad_new = lambda v: jnp.concatenate([v, jnp.zeros((LANES - v.shape[0], v.shape[1]), v.dtype)], axis=0)
    t_of_row = lax.broadcasted_iota(jnp.int32, (R, tq, 1), 1).reshape(rows, 1)
    pos = past + t_of_row
    ra = lax.broadcasted_iota(jnp.int32, (rows, rows), 0)
    rb = lax.broadcasted_iota(jnp.int32, (rows, rows), 1)
    same_t = jnp.where(_mod_pow2(ra, tq) == _mod_pow2(rb, tq), 1.0, 0.0).astype(F32)
    kg = kg_ref[...]
    kvn = kvn_ref[0]

    for g in range(G):
        q = q_ref[0, g]
        gt = gates_ref[0, g]

        def compress(slot):
            col = (slot * G + g) * hd
            for i, pg in enumerate(pages):
                xc_ref[i * page:(i + 1) * page, :] = pg[0, 0, :, col:col + hd]

            def get_rows(r):
                return xc_ref[pl.ds(r, ng, stride=CMP_STRIDE), :]

            return _compress_math(get_rows, ng, pe_ref[slot], w1_ref.at[slot], w2_ref[slot])

        kc = _rms(compress(0), kg).astype(BF16)
        vc = compress(1).astype(BF16)

        c_col = lax.broadcasted_iota(jnp.int32, (rows, ng), 1)
        dist = pos - (CMP_STRIDE * c_col + CMP_BLOCK - 1)
        p = _masked_softmax(_dot_t(q, kc) + bc_ref[g], jnp.where(c_col < nc, dist, -1) >= 0)
        o_cmp = jnp.dot(p.astype(BF16), vc, preferred_element_type=F32)

        psum = jnp.dot(same_t, p, precision=HIGHEST, preferred_element_type=F32)
        imp = jnp.dot(psum, _overlap_matrix(ng), precision=HIGHEST, preferred_element_type=F32)
        sel = _select_mask(imp, _div_pow2(pos, SLC_BLOCK), n_slc)
        nk = past + LANES
        selk = jnp.dot(sel.astype(BF16), _expand_matrix(nk), preferred_element_type=F32)

        k_new = pad_new(kvn[:, g * hd:(g + 1) * hd])
        v_new = pad_new(kvn[:, 256 + g * hd:256 + (g + 1) * hd])
        ck = 512 + g * hd
        cv = 768 + g * hd
        sc = jnp.concatenate([_dot_t(q, pg[0, 0, :, ck:ck + hd].astype(BF16)) for pg in pages]
                             + [_dot_t(q, k_new)], axis=-1) + bs_ref[g]
        j = lax.broadcasted_iota(jnp.int32, (rows, nk), 1)
        p = _masked_softmax(sc, jnp.where(selk > 0.5, pos - j, -1) >= 0).astype(BF16)
        o_slc = jnp.dot(p[:, past:], v_new, preferred_element_type=F32)
        for i, pg in enumerate(pages):
            o_slc = o_slc + jnp.dot(p[:, i * page:(i + 1) * page], pg[0, 0, :, cv:cv + hd].astype(BF16),
                                    preferred_element_type=F32)

        kw_new = pad_new(kvn[:, 512 + g * hd:512 + (g + 1) * hd])
        vw_new = pad_new(kvn[:, 768 + g * hd:768 + (g + 1) * hd])
        kw = win_ref[0, 0, :, g * hd:(g + 1) * hd].astype(BF16)
        vw = win_ref[0, 0, :, 256 + g * hd:256 + (g + 1) * hd].astype(BF16)
        sc = jnp.concatenate([_dot_t(q, kw), _dot_t(q, kw_new)], axis=-1) + bw_ref[g]
        j = lax.broadcasted_iota(jnp.int32, (rows, wb + LANES), 1)
        d = pos - (past - wb + j)
        p = _masked_softmax(sc, jnp.where(d <= WINDOW, d, -1) >= 0).astype(BF16)
        o_win = (jnp.dot(p[:, :wb], vw, preferred_element_type=F32)
                 + jnp.dot(p[:, wb:], vw_new, preferred_element_type=F32))

        o_ref[0, g] = gt[:, 0:1] * o_cmp + gt[:, 1:2] * o_slc + gt[:, 2:3] * o_win


def _nsa_sample(l, page_table, cache4, cache_win4, q_s, kvn_s, gates_s, lw, bias_c, bias_s, bias_w, *, tq):
    B, npages = page_table.shape
    page = cache4.shape[2]
    G = NSA_KV_HEADS
    rows = NSA_REP * tq
    wb = cache_win4.shape[2]
    page_specs = [pl.BlockSpec((1, 1, page, cache4.shape[3]), lambda b, pt, p=p: (l, pt[b, p], 0, 0))
                  for p in range(npages)]
    full = lambda a: pl.BlockSpec(a.shape, lambda b, pt, n=a.ndim: (0,) * n)
    wts = [lw["cmp_pe"], lw["cmp_w1"], lw["cmp_w2"], lw["k_norm_g"].reshape(1, -1), bias_c, bias_s, bias_w]
    grid_spec = pltpu.PrefetchScalarGridSpec(
        num_scalar_prefetch=1, grid=(B,),
        in_specs=page_specs + [
            pl.BlockSpec((1, 1, wb, cache_win4.shape[3]), lambda b, pt: (l, b, 0, 0)),
            pl.BlockSpec((1, G, rows, HEAD_DIM), lambda b, pt: (b, 0, 0, 0)),
            pl.BlockSpec((1, SUBLANES, kvn_s.shape[2]), lambda b, pt: (b, 0, 0)),
            pl.BlockSpec((1, G, rows, LANES), lambda b, pt: (b, 0, 0, 0))] + [full(w) for w in wts],
        out_specs=pl.BlockSpec((1, G, rows, HEAD_DIM), lambda b, pt: (b, 0, 0, 0)),
        scratch_shapes=[pltpu.VMEM((npages * page, HEAD_DIM), F32)])
    return pl.pallas_call(
        functools.partial(_nsa_sample_kernel, npages=npages, tq=tq),
        grid_spec=grid_spec,
        out_shape=jax.ShapeDtypeStruct((B, G, rows, HEAD_DIM), F32),
        compiler_params=_cp(("arbitrary",)), name="nsa_sample",
    )(page_table, *([cache4] * npages), cache_win4, q_s, kvn_s, gates_s, *wts)


def _gdn_prompt_kernel(gq_ref, ggb_ref, gz_ref, s0_ref, ng_ref, y_ref, so_ref, s_ref):
    n = pl.program_id(0)
    B = gq_ref.shape[0]
    C = gq_ref.shape[1]
    hd = HEAD_DIM
    W = GDN_HEADS * hd

    @pl.when(n == 0)
    def _():
        s_ref[...] = s0_ref[...]

    ri = lax.broadcasted_iota(jnp.int32, (C, C), 0)
    ci = lax.broadcasted_iota(jnp.int32, (C, C), 1)
    tri_incl = ri >= ci
    tri_strict = ri > ci
    tri_f = jnp.where(tri_incl, 1.0, 0.0).astype(F32)
    eye = jnp.where(ri == ci, 1.0, 0.0).astype(F32)
    ones = jnp.ones((C, C), F32)
    dot = functools.partial(jnp.dot, precision=HIGHEST, preferred_element_type=F32)
    ng = ng_ref[...]

    for b in range(B):
        for h in range(GDN_HEADS):
            q = gq_ref[b, :, h * hd:(h + 1) * hd]
            k = gq_ref[b, :, W + h * hd:W + (h + 1) * hd]
            v = gq_ref[b, :, 2 * W + h * hd:2 * W + (h + 1) * hd]
            g = ggb_ref[b, :, h * hd:(h + 1) * hd]
            beta = ggb_ref[b, :, W + h * hd:W + (h + 1) * hd]
            gc = dot(tri_f, g)
            gcol = gc[:, :C]
            grow = dot(ones, eye * gcol)
            decay = jnp.exp(jnp.where(tri_incl, gcol - grow, -jnp.inf))
            kb = k * beta
            low = jnp.where(tri_strict, _dot_t(kb, k, HIGHEST) * decay, 0.0)
            tinv = eye - low
            pw = low
            span = 2
            while span < C:
                pw = dot(pw, pw)
                tinv = dot(tinv, eye + pw)
                span *= 2
            egc = jnp.exp(gc)
            u = dot(tinv, v * beta)
            w = dot(tinv, kb * egc)
            qk = jnp.where(tri_incl, _dot_t(q, k, HIGHEST) * decay, 0.0)
            S = s_ref[b, h]
            v_new = u - dot(w, S)
            o = dot(q * egc, S) + dot(qk, v_new)
            g_last = gc[C - 1:C]
            kd = k * jnp.exp(g_last - gc)
            s_ref[b, h] = S * jnp.exp(g_last) + lax.dot_general(
                kd, v_new, (((0,), (0,)), ((), ())), precision=HIGHEST, preferred_element_type=F32)
            z = gz_ref[b, :, h * hd:(h + 1) * hd]
            y_ref[b, :, h * hd:(h + 1) * hd] = (_rms(o, ng) * (z * _sigmoid(z))).astype(y_ref.dtype)

    @pl.when(n == pl.num_programs(0) - 1)
    def _():
        so_ref[...] = s_ref[...]


def _gdn_prompt(gq3, ggb3, z3, s0, norm_g):
    B, T, _ = gq3.shape
    C = GDN_CHUNK
    W = GDN_HEADS * HEAD_DIM
    return pl.pallas_call(
        _gdn_prompt_kernel,
        grid=(T // C,),
        in_specs=[pl.BlockSpec((B, C, 3 * W), lambda n: (0, n, 0)),
                  pl.BlockSpec((B, C, 2 * W), lambda n: (0, n, 0)),
                  pl.BlockSpec((B, C, W), lambda n: (0, n, C_GZ // W)),
                  pl.BlockSpec(s0.shape, lambda n: (0, 0, 0, 0)),
                  pl.BlockSpec((1, HEAD_DIM), lambda n: (0, 0))],
        out_specs=[pl.BlockSpec((B, C, W), lambda n: (0, n, 0)),
                   pl.BlockSpec(s0.shape, lambda n: (0, 0, 0, 0))],
        out_shape=[jax.ShapeDtypeStruct((B, T, W), BF16), jax.ShapeDtypeStruct(s0.shape, F32)],
        scratch_shapes=[pltpu.VMEM(s0.shape, F32)],
        compiler_params=_cp(("arbitrary",)), name="gdn_prompt",
    )(gq3, ggb3, z3, s0, norm_g.reshape(1, -1))


def _gdn_sample_kernel(kq_ref, v_ref, gb_ref, gz_ref, s0_ref, ng_ref, y_ref, so_ref, *, tq):
    bt = kq_ref.shape[0]
    hd = HEAD_DIM
    ri = lax.broadcasted_iota(jnp.int32, (hd, hd), 0)
    ci = lax.broadcasted_iota(jnp.int32, (hd, hd), 1)
    eye = jnp.where(ri == ci, 1.0, 0.0).astype(F32)
    ng = ng_ref[...]

    def body(bi, carry):
        for h in range(GDN_HEADS):
            kq = kq_ref[bi, h]
            cols = _dot_t(eye, kq, HIGHEST)
            gb = gb_ref[bi, h]
            v = v_ref[bi, h]
            S = s0_ref[bi, h]
            outs = []
            for t in range(tq):
                a = jnp.exp(gb[t:t + 1])
                kc = cols[:, t:t + 1]
                qc = cols[:, tq + t:tq + t + 1]
                Sa = S * a
                stk = jnp.sum(Sa * kc, axis=0, keepdims=True)
                vn = gb[tq + t:tq + t + 1] * (v[t:t + 1] - stk)
                S = Sa + kc * vn
                outs.append(jnp.sum(S * qc, axis=0, keepdims=True))
            so_ref[bi, h] = S
            o = jnp.concatenate(outs + [jnp.zeros((SUBLANES - tq, hd), F32)], axis=0)
            z = gz_ref[bi, h]
            y_ref[bi, h] = _rms(o, ng) * (z * _sigmoid(z))
        return carry

    lax.fori_loop(0, bt, body, 0)


def _gdn_sample(kq_s, v_s, gb_s, gz_s, s0, norm_g, *, tq):
    B = kq_s.shape[0]
    bt = 8
    blk = lambda a: pl.BlockSpec((bt,) + a.shape[1:], lambda i: (i, 0, 0, 0))
    return pl.pallas_call(
        functools.partial(_gdn_sample_kernel, tq=tq),
        grid=(B // bt,),
        in_specs=[blk(kq_s), blk(v_s), blk(gb_s), blk(gz_s), blk(s0), pl.BlockSpec((1, HEAD_DIM), lambda i: (0, 0))],
        out_specs=[blk(v_s), blk(s0)],
        out_shape=[jax.ShapeDtypeStruct(v_s.shape, F32), jax.ShapeDtypeStruct(s0.shape, F32)],
        compiler_params=_cp(("parallel",)), name="gdn_sample",
    )(kq_s, v_s, gb_s, gz_s, s0, norm_g.reshape(1, -1))


def _mix_out_kernel(ya_ref, yb_ref, yc_ref, x_ref, onb_ref, w_ref, o_ref, mix_ref):
    @pl.when(pl.program_id(1) == 0)
    def _():
        mix_ref[:, 0:512] = ya_ref[...]
        mix_ref[:, 512:1536] = _rms(yb_ref[...], onb_ref[...]).astype(BF16)
        mix_ref[:, 1536:2048] = yc_ref[...]

    o_ref[...] = x_ref[...] + jnp.dot(mix_ref[...], w_ref[...], preferred_element_type=F32)


def _mix_out(ya, yb, yc, x, onb, w, *, tm=512, tn=512):
    M, N = x.shape
    tm = min(tm, M)
    K = w.shape[0]
    return pl.pallas_call(
        _mix_out_kernel,
        grid=(M // tm, N // tn),
        in_specs=[pl.BlockSpec((tm, 512), lambda i, j: (i, 0)),
                  pl.BlockSpec((tm, 1024), lambda i, j: (i, 0)),
                  pl.BlockSpec((tm, 512), lambda i, j: (i, 0)),
                  pl.BlockSpec((tm, tn), lambda i, j: (i, j)),
                  pl.BlockSpec((1, 1024), lambda i, j: (0, 0)),
                  pl.BlockSpec((K, tn), lambda i, j: (0, j))],
        out_specs=pl.BlockSpec((tm, tn), lambda i, j: (i, j)),
        out_shape=jax.ShapeDtypeStruct((M, N), F32),
        scratch_shapes=[pltpu.VMEM((tm, K), BF16)],
        compiler_params=_cp(("parallel", "arbitrary")), name="mix_out",
    )(ya, yb, yc, x, onb.reshape(1, -1), w)


def _ffn_mid_kernel(a_ref, v_ref, ap_ref, vp_ref, ha_ref, hv_ref, wa_ref, wv_ref, o_ref, *, tstride, tiles_per_seq):
    tc = wa_ref.shape[1]

    def conv(x_ref, prev_ref, hist_ref, w_ref):
        x = _ld(x_ref, 0, tc)
        if tstride == 1:
            start = (pl.program_id(0) % tiles_per_seq) == 0
            halo = jnp.where(start, hist_ref[0], prev_ref[...])
        else:
            halo = _ld(hist_ref, 0, tc)
        w = w_ref[...]
        return w[0:1] * _shift(x, halo, 2, tstride) + w[1:2] * _shift(x, halo, 1, tstride) + w[2:3] * x

    a = conv(a_ref, ap_ref, ha_ref, wa_ref)
    v = conv(v_ref, vp_ref, hv_ref, wv_ref)
    _st(o_ref, 0, tc, a * _sigmoid(a) * v)


def _ffn_mid(up, hist, w, *, sample, nseq, seq_len, tc=512):
    M, F2 = up.shape
    F = F2 // 2
    nj = F // tc
    wspec_a = pl.BlockSpec((3, tc), lambda i, j: (0, j))
    wspec_v = pl.BlockSpec((3, tc), lambda i, j: (0, j + nj))
    if not sample:
        tm = 512
        tps = seq_len // tm
        prev = lambda off: pl.BlockSpec(
            (SUBLANES, tc), lambda i, j, off=off: (jnp.maximum(i * (tm // SUBLANES) - 1, 0), j + off))
        hspec = lambda off: pl.BlockSpec((1, SUBLANES, tc), lambda i, j, off=off: (i // tps, 0, j + off))
        return pl.pallas_call(
            functools.partial(_ffn_mid_kernel, tstride=1, tiles_per_seq=tps),
            grid=(M // tm, nj),
            in_specs=[pl.BlockSpec((tm, tc), lambda i, j: (i, j)), pl.BlockSpec((tm, tc), lambda i, j: (i, j + nj)),
                      prev(0), prev(nj), hspec(0), hspec(nj), wspec_a, wspec_v],
            out_specs=pl.BlockSpec((tm, tc), lambda i, j: (i, j)),
            out_shape=jax.ShapeDtypeStruct((M, F), BF16),
            compiler_params=_cp(("parallel", "parallel")), name="ffn_mid_prompt",
        )(up, up, up, up, hist, hist, w, w)
    T = seq_len
    up3 = up.reshape(T, nseq, F2)
    hk = hist.shape[0]
    out = pl.pallas_call(
        functools.partial(_ffn_mid_kernel, tstride=nseq, tiles_per_seq=1),
        grid=(1, nj),
        in_specs=[pl.BlockSpec((T, nseq, tc), lambda i, j: (0, 0, j)),
                  pl.BlockSpec((T, nseq, tc), lambda i, j: (0, 0, j + nj)),
                  pl.BlockSpec((SUBLANES, tc), lambda i, j: (0, j)), pl.BlockSpec((SUBLANES, tc), lambda i, j: (0, j + nj)),
                  pl.BlockSpec((hk, nseq, tc), lambda i, j: (0, 0, j)),
                  pl.BlockSpec((hk, nseq, tc), lambda i, j: (0, 0, j + nj)), wspec_a, wspec_v],
        out_specs=pl.BlockSpec((T, nseq, tc), lambda i, j: (0, 0, j)),
        out_shape=jax.ShapeDtypeStruct((T, nseq, F), BF16),
        compiler_params=_cp(("parallel", "parallel")), name="ffn_mid_sample",
    )(up3, up3, up, up, hist, hist, w, w)
    return out.reshape(M, F)


def _matmul_res_kernel(a_ref, w_ref, r_ref, o_ref):
    o_ref[...] = r_ref[...] + jnp.dot(a_ref[...], w_ref[...], preferred_element_type=F32)


def _matmul_res(a, w, res, *, tm=512, tn=512):
    M, K = a.shape
    N = w.shape[1]
    tm = min(tm, M)
    return pl.pallas_call(
        _matmul_res_kernel,
        grid=(M // tm, N // tn),
        in_specs=[pl.BlockSpec((tm, K), lambda i, j: (i, 0)),
                  pl.BlockSpec((K, tn), lambda i, j: (0, j)),
                  pl.BlockSpec((tm, tn), lambda i, j: (i, j))],
        out_specs=pl.BlockSpec((tm, tn), lambda i, j: (i, j)),
        out_shape=jax.ShapeDtypeStruct((M, N), F32),
        compiler_params=_cp(("parallel", "parallel")), name="matmul_res",
    )(a, w, res)


def _layer_weights(l, p):
    w_in = p["w_in"][l]
    w_main = jnp.concatenate([w_in[:, :4096], w_in[:, 4120:5656], w_in[:, 5664:6176]], axis=1).astype(BF16)
    w_small = jnp.concatenate([w_in[:, 4096:4120], w_in[:, 5656:5664],
                               jnp.zeros((w_in.shape[0], LANES - 32), w_in.dtype)], axis=1).astype(BF16)
    lane_vec = lambda v: jnp.zeros((1, LANES), F32).at[0, S_GA:S_GA + GDN_HEADS].set(v)
    return {
        "norm_mix_g": p["norm_mix_g"][l], "w_main": w_main, "w_small": w_small,
        "conv_a_w": p["conv_a_w"][l], "q_norm_g": p["q_norm_g"][l], "k_norm_g": p["k_norm_g"][l],
        "cmp_pe": p["cmp_pe"][l],
        "cmp_w1": p["cmp_w1"][l].reshape(2, CMP_BLOCK, HEAD_DIM, CMP_HIDDEN).astype(BF16),
        "cmp_w2": p["cmp_w2"][l].astype(BF16),
        "gdn_conv_w": p["gdn_conv_w"][l], "alog_v": lane_vec(p["gdn_a_log"][l]), "dtb_v": lane_vec(p["gdn_dt_bias"][l]),
        "gdn_norm_g": p["gdn_norm_g"][l], "out_norm_a": p["out_norm_a"][l], "out_norm_b": p["out_norm_b"][l],
        "w_out": p["w_out"][l].astype(BF16), "norm_ffn_g": p["norm_ffn_g"][l],
        "ffn_up": p["ffn_up"][l].astype(BF16), "ffn_conv_w": p["ffn_conv_w"][l],
        "ffn_down": p["ffn_down"][l].astype(BF16),
    }


def _dense_tail(x2, ya, yb, yc, lw, hist_ffn, *, sample, nseq, seq_len):
    h = _mix_out(ya, yb, yc, x2, lw["out_norm_b"], lw["w_out"])
    up = _norm_matmul(h, lw["norm_ffn_g"], lw["ffn_up"])
    act = _ffn_mid(up, hist_ffn, lw["ffn_conv_w"], sample=sample, nseq=nseq, seq_len=seq_len)
    return _matmul_res(act, lw["ffn_down"], h), up


def _prompt_layer(x2, lw, bias_tiles, bias_cmp, *, B, T):
    M = B * T
    G, R = NSA_KV_HEADS, NSA_REP
    z, zs = _norm_matmul(x2, lw["norm_mix_g"], lw["w_main"], lw["w_small"])
    hu0 = jnp.zeros((B, SUBLANES, 512), F32)
    hq0 = jnp.zeros((B, SUBLANES, 1536), F32)
    ya, q, rows, win, kvb, gates, gq, ggb, hu = _prep(z, zs, hu0, hq0, lw, sample=False, nseq=B, seq_len=T)
    cmp_kv = _compress_prompt(rows.reshape(B, T, 1024), lw)
    gates_g = gates[:, :3 * G * R].reshape(M, 3, G, R).transpose(2, 0, 1, 3).reshape(G, M, 3 * R)
    gates_g = jnp.pad(gates_g, ((0, 0), (0, 0), (0, LANES - 3 * R)))
    yb = _nsa_prompt(q, cmp_kv, kvb.reshape(B, T, 1024), gates_g, bias_tiles, bias_cmp, B=B, T=T)
    s0 = jnp.zeros((B, GDN_HEADS, HEAD_DIM, HEAD_DIM), F32)
    yc, s_new = _gdn_prompt(gq.reshape(B, T, 1536), ggb.reshape(B, T, 1024), z.reshape(B, T, C_END), s0,
                            lw["gdn_norm_g"])
    hf0 = jnp.zeros((B, SUBLANES, lw["ffn_conv_w"].shape[1]), F32)
    out, up = _dense_tail(x2, ya, yb, yc.reshape(M, 512), lw, hf0, sample=False, nseq=B, seq_len=T)
    return out, rows, win, hu, z, s_new, up


def _sample_layer(l, x2, lw, st, bias_c, bias_s, bias_w, *, B, T):
    M = B * T
    G, R, H, hd = NSA_KV_HEADS, NSA_REP, GDN_HEADS, HEAD_DIM
    z, zs = _norm_matmul(x2, lw["norm_mix_g"], lw["w_main"], lw["w_small"])
    hu0 = jnp.swapaxes(st["conv_a"], 0, 1)
    hq0 = jnp.swapaxes(st["gdn_conv"], 0, 1)
    ya, q, rows, win, kvb, gates, gq, ggb, hu = _prep(z, zs, hu0, hq0, lw, sample=True, nseq=B, seq_len=T)
    tb = lambda a: a.reshape(T, B, -1)
    q_s = tb(q).reshape(T, B, G, R, hd).transpose(1, 2, 3, 0, 4).reshape(B, G, R * T, hd)
    gt = tb(gates)[:, :, :3 * G * R].reshape(T, B, 3, G, R).transpose(1, 3, 4, 0, 2).reshape(B, G, R * T, 3)
    gates_s = jnp.pad(gt, ((0, 0), (0, 0), (0, 0), (0, LANES - 3)))
    kvn_s = jnp.pad(jnp.swapaxes(tb(kvb), 0, 1), ((0, 0), (0, SUBLANES - T), (0, 0)))
    yb_s = _nsa_sample(l, st["page_table"], st["cache4"], st["cache_win4"], q_s, kvn_s, gates_s, lw,
                       bias_c, bias_s, bias_w, tq=T)
    yb = yb_s.reshape(B, G, R, T, hd).transpose(3, 0, 1, 2, 4).reshape(M, G * R * hd)
    gq4 = tb(gq).reshape(T, B, 3, H, hd)
    bh = lambda a: a.transpose(1, 2, 0, 3)
    padt = lambda a: jnp.pad(a, ((0, 0), (0, 0), (0, SUBLANES - T), (0, 0)))
    kq_s = jnp.concatenate([bh(gq4[:, :, 1]), bh(gq4[:, :, 0])], axis=2)
    v_s = padt(bh(gq4[:, :, 2]))
    ggb4 = tb(ggb).reshape(T, B, 2, H, hd)
    gb_s = jnp.concatenate([bh(ggb4[:, :, 0]), bh(ggb4[:, :, 1])], axis=2)
    gz_s = padt(bh(tb(z)[:, :, C_GZ:C_END].reshape(T, B, H, hd)))
    yc_s, s_new = _gdn_sample(kq_s, v_s, gb_s, gz_s, st["gdn"], lw["gdn_norm_g"], tq=T)
    yc = yc_s[:, :, :T].transpose(2, 0, 1, 3).reshape(M, H * hd).astype(BF16)
    hf0 = jnp.swapaxes(st["ffn_conv"], 0, 1)
    out, up = _dense_tail(x2, ya, yb, yc, lw, hf0, sample=True, nseq=B, seq_len=T)
    return out, rows, win, hu, z, s_new, up


def kernel(x_prompt, x_sample, cache_kv, cache_win, state_conv_a, state_gdn_conv, state_gdn, state_ffn_conv,
           page_table, rel_bias, norm_mix_g, w_in, conv_a_w, q_norm_g, k_norm_g, cmp_pe, cmp_w1, cmp_w2,
           gdn_conv_w, gdn_a_log, gdn_dt_bias, gdn_norm_g, out_norm_a, out_norm_b, w_out, norm_ffn_g,
           ffn_up, ffn_conv_w, ffn_down):
    params = dict(norm_mix_g=norm_mix_g, w_in=w_in, conv_a_w=conv_a_w, q_norm_g=q_norm_g, k_norm_g=k_norm_g,
                  cmp_pe=cmp_pe, cmp_w1=cmp_w1, cmp_w2=cmp_w2, gdn_conv_w=gdn_conv_w, gdn_a_log=gdn_a_log,
                  gdn_dt_bias=gdn_dt_bias, gdn_norm_g=gdn_norm_g, out_norm_a=out_norm_a, out_norm_b=out_norm_b,
                  w_out=w_out, norm_ffn_g=norm_ffn_g, ffn_up=ffn_up, ffn_conv_w=ffn_conv_w, ffn_down=ffn_down)
    depth = w_in.shape[0]
    Bp, T, D = x_prompt.shape
    Bs, Ts, _ = x_sample.shape
    G, R, hd = NSA_KV_HEADS, NSA_REP, HEAD_DIM
    n_pool, page = cache_kv.shape[1], cache_kv.shape[2]
    npages = page_table.shape[1]
    past = npages * page
    wb = cache_win.shape[2]
    L = past + Ts
    assert ((L - CMP_BLOCK) // CMP_STRIDE) * CMP_STRIDE + CMP_BLOCK <= past, "compressed blocks must lie in the cache"
    assert T % Q_BLOCK == 0 and T % 512 == 0 and 3 <= Ts <= SUBLANES // 2

    thr = _bucket_thresholds()
    rb_flat = rel_bias.reshape(-1)
    nqb = T // Q_BLOCK
    bias_tiles = _bias_table(thr, rb_flat, nqb, Q_BLOCK, Q_BLOCK, a0=0, an=Q_BLOCK, qs=1, ks=1)
    bias_cmp = _bias_table(thr, rb_flat, nqb, Q_BLOCK, T // CMP_STRIDE,
                           a0=-(CMP_BLOCK - 1), an=Q_BLOCK, qs=1, ks=CMP_STRIDE)

    def sample_bias(ncols, a0, ks):
        t = _bias_table(thr, rb_flat, 1, SUBLANES, ncols, a0=a0, an=0, qs=1, ks=ks)[0]
        return t[:, :Ts].reshape(G, R * Ts, ncols)

    bias_c = sample_bias(past // CMP_STRIDE, past - (CMP_BLOCK - 1), CMP_STRIDE)
    bias_s = sample_bias(past + LANES, past, 1)
    bias_w = sample_bias(wb + LANES, wb, 1)

    cache4 = cache_kv.reshape(depth, n_pool, page, 4 * G * hd)
    cache_win4 = cache_win.reshape(depth, Bs, wb, 2 * G * hd)

    xp = x_prompt.reshape(Bp * T, D)
    xs = jnp.swapaxes(x_sample, 0, 1).reshape(Ts * Bs, D)
    outs_p, outs_s = [], []
    for l in range(depth):
        lw = _layer_weights(l, params)
        xp, rows, win, hu, z, s_new, up = _prompt_layer(xp, lw, bias_tiles, bias_cmp, B=Bp, T=T)
        wl = min(WINDOW, T)
        outs_p.append((
            rows.reshape(Bp, T, 4, G, hd),
            win.reshape(Bp, T, 2, G, hd)[:, T - wl:],
            hu[:, SUBLANES - 2:],
            z.reshape(Bp, T, C_END)[:, T - 3:, C_GQKV:C_GZ],
            s_new,
            up.reshape(Bp, T, -1)[:, T - 2:]))
        st = dict(page_table=page_table, cache4=cache4, cache_win4=cache_win4, conv_a=state_conv_a[l],
                  gdn_conv=state_gdn_conv[l], gdn=state_gdn[l], ffn_conv=state_ffn_conv[l])
        xs, rows, win, hu, z, s_new, up = _sample_layer(l, xs, lw, st, bias_c, bias_s, bias_w, B=Bs, T=Ts)
        tb = lambda a: jnp.swapaxes(a.reshape(Ts, Bs, -1), 0, 1)
        win_all = jnp.concatenate([cache_win4[l], tb(win)], axis=1)
        wl = min(WINDOW, past + Ts)
        outs_s.append((
            tb(rows).reshape(Bs, Ts, 4, G, hd),
            win_all[:, win_all.shape[1] - wl:].reshape(Bs, wl, 2, G, hd),
            jnp.swapaxes(hu, 0, 1),
            tb(z)[:, Ts - 3:, C_GQKV:C_GZ],
            s_new,
            tb(up)[:, Ts - 2:]))
    y_p = xp.reshape(Bp, T, D)
    y_s = jnp.swapaxes(xs.reshape(Ts, Bs, D), 0, 1)
    stack = lambda outs, i: jnp.stack([o[i] for o in outs])
    return (y_p, y_s, stack(outs_p, 0), stack(outs_s, 0), stack(outs_p, 1), stack(outs_s, 1),
            stack(outs_p, 2), stack(outs_s, 2), stack(outs_p, 3), stack(outs_s, 3),
            stack(outs_p, 4), stack(outs_s, 4), stack(outs_p, 5), stack(outs_s, 5))
```

```python
import functools
import math

import jax
import jax.numpy as jnp
from jax import lax
from jax.experimental import pallas as pl
from jax.experimental.pallas import tpu as pltpu

F32 = jnp.float32
BF16 = jnp.bfloat16
HIGHEST = lax.Precision.HIGHEST

SUBLANES = 8
LANES = 128
VMEM_LIMIT_BYTES = 52 * 1024 * 1024

HEAD_DIM = 128
NSA_KV_HEADS = 2
NSA_REP = 4
NSA_HEADS = NSA_KV_HEADS * NSA_REP
GDN_HEADS = 4
CMP_BLOCK = 32
CMP_STRIDE = 16
CMP_HIDDEN = 256
SLC_BLOCK = 64
N_SELECT = 16
WINDOW = 512
Q_BLOCK = 128
GDN_CHUNK = 64
REL_BUCKETS = 32
REL_MAX_DIST = 1024
EPS = 1e-6
NEG_INF = -1e30
FORCE_SCORE = 1e4

C_AB, C_AC, C_AH, C_NQ, C_NKV, C_GQKV, C_GZ, C_END = 0, 512, 1024, 1536, 2560, 4096, 5632, 6144
S_GATE, S_GA, S_GB = 0, 24, 28


def _cp(sem):
    return pltpu.CompilerParams(dimension_semantics=sem, vmem_limit_bytes=VMEM_LIMIT_BYTES)


def _div_pow2(x, d):
    assert d & (d - 1) == 0
    return jnp.right_shift(x, d.bit_length() - 1)


def _mod_pow2(x, d):
    assert d & (d - 1) == 0
    return jnp.bitwise_and(x, d - 1)


def _sigmoid(x):
    return 1.0 / (1.0 + jnp.exp(-x))


def _rms(x, g):
    return x * lax.rsqrt(jnp.mean(x * x, axis=-1, keepdims=True) + EPS) * g


def _l2n(x):
    return x * lax.rsqrt(jnp.sum(x * x, axis=-1, keepdims=True) + EPS)


def _dot_t(a, b, precision=None):
    return lax.dot_general(a, b, (((1,), (1,)), ((), ())), precision=precision, preferred_element_type=F32)


def _ld(ref, c0, c1):
    if len(ref.shape) == 3:
        v = ref[:, :, c0:c1]
        return v.reshape(v.shape[0] * v.shape[1], v.shape[2])
    return ref[:, c0:c1]


def _st(ref, c0, c1, val):
    if len(ref.shape) == 3:
        ref[:, :, c0:c1] = val.reshape(ref.shape[0], ref.shape[1], c1 - c0).astype(ref.dtype)
    else:
        ref[:, c0:c1] = val.astype(ref.dtype)


def _shift(x, halo, s, tstride):
    n = s * tstride
    rows = x.shape[0]
    if tstride % SUBLANES == 0:
        hr = halo.shape[0]
        return jnp.concatenate([halo[hr - n:], x[:rows - n]], axis=0)
    xs = pltpu.roll(x, n, 0)
    hs = pltpu.roll(halo, n, 0)
    rid = lax.broadcasted_iota(jnp.int32, (SUBLANES, x.shape[1]), 0)
    head = jnp.where(rid < n, hs, xs[:SUBLANES])
    return jnp.concatenate([head, xs[SUBLANES:]], axis=0)


def _norm_matmul_kernel(x_ref, g_ref, w_ref, *rest, has_small):
    if has_small:
        ws_ref, o_ref, os_ref, xn_ref = rest
    else:
        o_ref, xn_ref = rest

    @pl.when(pl.program_id(1) == 0)
    def _():
        xn = _rms(x_ref[...], g_ref[...]).astype(BF16)
        xn_ref[...] = xn
        if has_small:
            os_ref[...] = jnp.dot(xn, ws_ref[...], preferred_element_type=F32)

    o_ref[...] = jnp.dot(xn_ref[...], w_ref[...], preferred_element_type=F32).astype(o_ref.dtype)


def _norm_matmul(x, g, w, w_small=None, *, tm=512, tn=512, out_dtype=F32):
    M, K = x.shape
    N = w.shape[1]
    tm = min(tm, M)
    has_small = w_small is not None
    in_specs = [pl.BlockSpec((tm, K), lambda i, j: (i, 0)),
                pl.BlockSpec((1, K), lambda i, j: (0, 0)),
                pl.BlockSpec((K, tn), lambda i, j: (0, j))]
    out_specs = [pl.BlockSpec((tm, tn), lambda i, j: (i, j))]
    out_shape = [jax.ShapeDtypeStruct((M, N), out_dtype)]
    args = [x, g.reshape(1, K), w]
    if has_small:
        in_specs.append(pl.BlockSpec((K, LANES), lambda i, j: (0, 0)))
        out_specs.append(pl.BlockSpec((tm, LANES), lambda i, j: (i, 0)))
        out_shape.append(jax.ShapeDtypeStruct((M, LANES), F32))
        args.append(w_small)
    res = pl.pallas_call(
        functools.partial(_norm_matmul_kernel, has_small=has_small),
        grid=(M // tm, N // tn),
        in_specs=in_specs, out_specs=out_specs, out_shape=out_shape,
        scratch_shapes=[pltpu.VMEM((tm, K), BF16)],
        compiler_params=_cp(("parallel", "arbitrary")),
        name="norm_matmul_small" if has_small else "norm_matmul",
    )(*args)
    return res if has_small else res[0]


def _prep_kernel(z_ref, zp_ref, zs_ref, hu_ref, hq_ref, caw_ref, gcw_ref, qg_ref, kg_ref, ona_ref,
                 alog_ref, dtb_ref,
                 ya_ref, q_ref, rows_ref, win_ref, kvb_ref, gates_ref, gq_ref, ggb_ref, huo_ref, *lin_refs,
                 tstride, tiles_per_seq):
    hd = HEAD_DIM

    def put_rows(grp, val):
        _st(rows_ref, grp * hd, (grp + 1) * hd, val)
        if lin_refs:
            lin_refs[0][pl.ds(grp, val.shape[0], stride=4 * NSA_KV_HEADS), :] = val
    u = _ld(z_ref, C_AC, C_AH) * _ld(z_ref, C_AH, C_NQ)
    if tstride == 1:
        start = (pl.program_id(0) % tiles_per_seq) == 0
        halo_u = jnp.where(start, hu_ref[0], zp_ref[:, C_AC:C_AH] * zp_ref[:, C_AH:C_NQ])
        halo_q = jnp.where(start, hq_ref[0], zp_ref[:, C_GQKV:C_GZ])
    else:
        halo_u = _ld(hu_ref, 0, 512)
        halo_q = _ld(hq_ref, 0, 1536)
    caw = caw_ref[...]
    conv = caw[0:1] * _shift(u, halo_u, 2, tstride) + caw[1:2] * _shift(u, halo_u, 1, tstride) + caw[2:3] * u
    y = _ld(z_ref, C_AB, C_AC) * conv
    _st(ya_ref, 0, 512, _rms(y, ona_ref[...]))
    hu_rows = huo_ref.shape[0] * huo_ref.shape[1] if tstride != 1 else SUBLANES
    if tstride == 1:
        huo_ref[0] = u[u.shape[0] - hu_rows:]
    else:
        _st(huo_ref, 0, 512, u[u.shape[0] - hu_rows:])

    qg = qg_ref[...]
    kg = kg_ref[...]
    for h in range(NSA_HEADS):
        c0 = C_NQ + h * hd
        _st(q_ref, h * hd, (h + 1) * hd, _rms(_ld(z_ref, c0, c0 + hd), qg) * (hd ** -0.5))
    for grp in range(4):
        put_rows(grp, _ld(z_ref, C_NKV + grp * hd, C_NKV + (grp + 1) * hd))
    for g in range(NSA_KV_HEADS):
        c0 = C_NKV + 512 + g * hd
        kn = _rms(_ld(z_ref, c0, c0 + hd), kg)
        put_rows(4 + g, kn)
        _st(kvb_ref, g * hd, (g + 1) * hd, kn)
    for g in range(NSA_KV_HEADS):
        c0 = C_NKV + 768 + g * hd
        vs = _ld(z_ref, c0, c0 + hd)
        put_rows(6 + g, vs)
        _st(kvb_ref, 256 + g * hd, 256 + (g + 1) * hd, vs)
    for g in range(NSA_KV_HEADS):
        c0 = C_NKV + 1024 + g * hd
        kn = _rms(_ld(z_ref, c0, c0 + hd), kg)
        _st(win_ref, g * hd, (g + 1) * hd, kn)
        _st(kvb_ref, 512 + g * hd, 512 + (g + 1) * hd, kn)
    vw = _ld(z_ref, C_NKV + 1280, C_NKV + 1536)
    _st(win_ref, 256, 512, vw)
    _st(kvb_ref, 768, 1024, vw)

    zs = _ld(zs_ref, 0, LANES)
    sg = _sigmoid(zs)
    _st(gates_ref, 0, LANES, sg)
    xs = zs + dtb_ref[...]
    softplus = jnp.maximum(xs, 0.0) + jnp.log(1.0 + jnp.exp(-jnp.abs(xs)))
    gdec = -jnp.exp(alog_ref[...]) * softplus
    rows = zs.shape[0]
    for h in range(GDN_HEADS):
        _st(ggb_ref, h * hd, (h + 1) * hd, jnp.broadcast_to(gdec[:, S_GA + h:S_GA + h + 1], (rows, hd)))
        _st(ggb_ref, 512 + h * hd, 512 + (h + 1) * hd, jnp.broadcast_to(sg[:, S_GB + h:S_GB + h + 1], (rows, hd)))

    gcw = gcw_ref[...]
    for part in range(3):
        c0 = C_GQKV + part * 512
        x = _ld(z_ref, c0, c0 + 512)
        hq = halo_q[:, part * 512:(part + 1) * 512]
        w = gcw[:, part * 512:(part + 1) * 512]
        c = (w[0:1] * _shift(x, hq, 3, tstride) + w[1:2] * _shift(x, hq, 2, tstride)
             + w[2:3] * _shift(x, hq, 1, tstride) + w[3:4] * x)
        c = c * _sigmoid(c)
        if part == 2:
            _st(gq_ref, 1024, 1536, c)
        else:
            for h in range(GDN_HEADS):
                v = _l2n(c[:, h * hd:(h + 1) * hd])
                if part == 0:
                    v = v * (hd ** -0.5)
                _st(gq_ref, part * 512 + h * hd, part * 512 + (h + 1) * hd, v)


def _prep(z, zs, hist_u, hist_q, lw, *, sample, nseq, seq_len):
    M = z.shape[0]
    wts = [lw["conv_a_w"], lw["gdn_conv_w"], lw["q_norm_g"].reshape(1, -1), lw["k_norm_g"].reshape(1, -1),
           lw["out_norm_a"].reshape(1, -1), lw["alog_v"], lw["dtb_v"]]
    wspecs2 = [pl.BlockSpec(w.shape, lambda i: (0, 0)) for w in wts]
    widths = [(512, BF16), (1024, BF16), (1024, F32), (512, F32), (1024, BF16), (LANES, F32), (1536, F32), (1024, F32)]
    if not sample:
        tm = 256
        tps = seq_len // tm
        in_specs = [pl.BlockSpec((tm, C_END), lambda i: (i, 0)),
                    pl.BlockSpec((SUBLANES, C_END), lambda i: (jnp.maximum(i * (tm // SUBLANES) - 1, 0), 0)),
                    pl.BlockSpec((tm, LANES), lambda i: (i, 0)),
                    pl.BlockSpec((1, SUBLANES, 512), lambda i: (i // tps, 0, 0)),
                    pl.BlockSpec((1, SUBLANES, 1536), lambda i: (i // tps, 0, 0))] + wspecs2
        out_specs = [pl.BlockSpec((tm, w), lambda i: (i, 0)) for w, _ in widths]
        out_specs.append(pl.BlockSpec((1, SUBLANES, 512), lambda i: (i // tps, 0, 0)))
        out_shape = [jax.ShapeDtypeStruct((M, w), d) for w, d in widths]
        out_shape.append(jax.ShapeDtypeStruct((nseq, SUBLANES, 512), F32))
        kvr = 4 * NSA_KV_HEADS
        out_specs.append(pl.BlockSpec((tm * kvr, HEAD_DIM), lambda i: (i, 0)))
        out_shape.append(jax.ShapeDtypeStruct((M * kvr, HEAD_DIM), F32))
        return pl.pallas_call(
            functools.partial(_prep_kernel, tstride=1, tiles_per_seq=tps),
            grid=(M // tm,), in_specs=in_specs, out_specs=out_specs, out_shape=out_shape,
            compiler_params=_cp(("arbitrary",)), name="prep_prompt",
        )(z, z, zs, hist_u, hist_q, *wts)
    T = seq_len
    bt = 64
    z3 = z.reshape(T, nseq, C_END)
    zs3 = zs.reshape(T, nseq, LANES)
    in_specs = [pl.BlockSpec((T, bt, C_END), lambda i: (0, i, 0)),
                pl.BlockSpec((SUBLANES, C_END), lambda i: (0, 0)),
                pl.BlockSpec((T, bt, LANES), lambda i: (0, i, 0)),
                pl.BlockSpec((hist_u.shape[0], bt, 512), lambda i: (0, i, 0)),
                pl.BlockSpec((hist_q.shape[0], bt, 1536), lambda i: (0, i, 0))] + wspecs2
    out_specs = [pl.BlockSpec((T, bt, w), lambda i: (0, i, 0)) for w, _ in widths]
    out_specs.append(pl.BlockSpec((hist_u.shape[0], bt, 512), lambda i: (0, i, 0)))
    out_shape = [jax.ShapeDtypeStruct((T, nseq, w), d) for w, d in widths]
    out_shape.append(jax.ShapeDtypeStruct((hist_u.shape[0], nseq, 512), F32))
    outs = pl.pallas_call(
        functools.partial(_prep_kernel, tstride=bt, tiles_per_seq=1),
        grid=(nseq // bt,), in_specs=in_specs, out_specs=out_specs, out_shape=out_shape,
        compiler_params=_cp(("arbitrary",)), name="prep_sample",
    )(z3, z, zs3, hist_u, hist_q, *wts)
    return [o.reshape(M, o.shape[-1]) for o in outs[:-1]] + [outs[-1]]


def _bias_kernel(thr_ref, rb_ref, o_ref, *, a0, an, qs, ks):
    n = pl.program_id(0)
    h = pl.program_id(1)
    R, C = o_ref.shape[-2:]
    dist = (a0 + an * n + qs * lax.broadcasted_iota(jnp.int32, (R, C), 0)
            - ks * lax.broadcasted_iota(jnp.int32, (R, C), 1))
    b = jnp.full((R, C), rb_ref[h], F32)
    for k in range(1, REL_BUCKETS):
        b = jnp.where(dist >= thr_ref[k], rb_ref[k * NSA_HEADS + h], b)
    o_ref[0, 0] = b


def _bias_table(thr, rb_flat, n, R, C, *, a0, an, qs, ks):
    return pl.pallas_call(
        functools.partial(_bias_kernel, a0=a0, an=an, qs=qs, ks=ks),
        grid=(n, NSA_HEADS),
        in_specs=[pl.BlockSpec(memory_space=pltpu.SMEM), pl.BlockSpec(memory_space=pltpu.SMEM)],
        out_specs=pl.BlockSpec((1, 1, R, C), lambda i, j: (i, j, 0, 0)),
        out_shape=jax.ShapeDtypeStruct((n, NSA_HEADS, R, C), F32),
        compiler_params=_cp(("parallel", "parallel")), name="bias_table",
    )(thr, rb_flat)


def _bucket_thresholds():
    n = jnp.arange(REL_MAX_DIST + 1)
    exact = REL_BUCKETS // 2
    nf = jnp.maximum(n, 1).astype(F32)
    far = exact + (jnp.log(nf / exact) / math.log(REL_MAX_DIST / exact) * (REL_BUCKETS - exact)).astype(jnp.int32)
    bucket = jnp.where(n < exact, n, jnp.minimum(far, REL_BUCKETS - 1))
    return jnp.sum(bucket[None, :] < jnp.arange(REL_BUCKETS)[:, None], axis=1).astype(jnp.int32)


def _gelu_tanh(x):
    return x * (0.5 * (1.0 + jnp.tanh(math.sqrt(2.0 / math.pi) * (x + 0.044715 * (x * x * x)))))


def _compress_math(get_rows, ng, pe, w1_ref, w2):
    half = CMP_BLOCK // 2
    top = jnp.zeros((ng, CMP_HIDDEN), F32)
    bot = jnp.zeros((ng, CMP_HIDDEN), F32)
    for r in range(half):
        s = get_rows(r)
        top = top + jnp.dot((s + pe[r:r + 1]).astype(BF16), w1_ref[r], preferred_element_type=F32)
        bot = bot + jnp.dot((s + pe[r + half:r + half + 1]).astype(BF16), w1_ref[r + half],
                            preferred_element_type=F32)
    h = top + pltpu.roll(bot, ng - 1, 0)
    return jnp.dot(_gelu_tanh(h).astype(BF16), w2, preferred_element_type=F32)


def _compress_kernel(x_ref, pe_ref, w1_ref, w2_ref, kg_ref, o_ref):
    ng = x_ref.shape[1] // CMP_STRIDE
    slot = pl.program_id(1)

    def get_rows(r):
        return x_ref[0, pl.ds(r, ng, stride=CMP_STRIDE), :]

    out = _compress_math(get_rows, ng, pe_ref[0], w1_ref.at[0], w2_ref[0])
    o_ref[0, 0, 0] = jnp.where(slot == 0, _rms(out, kg_ref[...]), out).astype(o_ref.dtype)


def _compress_prompt(rows3, lw):
    B, T, _ = rows3.shape
    ng = T // CMP_STRIDE
    G = NSA_KV_HEADS
    return pl.pallas_call(
        _compress_kernel,
        grid=(B, 2, G),
        in_specs=[pl.BlockSpec((1, T, HEAD_DIM), lambda b, s, g: (b, 0, s * G + g)),
                  pl.BlockSpec((1, CMP_BLOCK, HEAD_DIM), lambda b, s, g: (s, 0, 0)),
                  pl.BlockSpec((1, CMP_BLOCK, HEAD_DIM, CMP_HIDDEN), lambda b, s, g: (s, 0, 0, 0)),
                  pl.BlockSpec((1, CMP_HIDDEN, HEAD_DIM), lambda b, s, g: (s, 0, 0)),
                  pl.BlockSpec((1, HEAD_DIM), lambda b, s, g: (0, 0))],
        out_specs=pl.BlockSpec((1, 1, 1, ng, HEAD_DIM), lambda b, s, g: (b, s, g, 0, 0)),
        out_shape=jax.ShapeDtypeStruct((B, 2, G, ng, HEAD_DIM), BF16),
        compiler_params=_cp(("parallel", "parallel", "parallel")), name="compress_prompt",
    )(rows3, lw["cmp_pe"], lw["cmp_w1"], lw["cmp_w2"], lw["k_norm_g"].reshape(1, -1))


def _masked_softmax(s, mask):
    s = jnp.where(mask, s, NEG_INF)
    m = jnp.max(s, axis=-1, keepdims=True)
    e = jnp.where(mask, jnp.exp(s - m), 0.0)
    z = jnp.sum(e, axis=-1, keepdims=True)
    return e / jnp.where(z > 0, z, 1.0)


def _overlap_matrix(ncp):
    c = CMP_STRIDE * lax.broadcasted_iota(jnp.int32, (ncp, LANES), 0)
    s = SLC_BLOCK * lax.broadcasted_iota(jnp.int32, (ncp, LANES), 1)
    return jnp.where(c < s + SLC_BLOCK, jnp.where(c + CMP_BLOCK > s, 1.0, 0.0), 0.0).astype(F32)


def _select_mask(imp, cur, n_slc):
    R = imp.shape[0]
    blk = lax.broadcasted_iota(jnp.int32, (R, LANES), 1)
    for forced_blk in (0, cur, cur - 1):
        imp = jnp.where(blk == forced_blk, FORCE_SCORE, imp)
    imp = jnp.where(blk <= cur, imp, -1.0)
    imp = jnp.where(blk < n_slc, imp, -2.0)
    cnt = jnp.zeros((R, LANES), F32)
    for k in range(n_slc):
        col = imp[:, k:k + 1]
        cnt = cnt + jnp.where(col > imp, 1.0, jnp.where(col == imp, jnp.where(blk > k, 1.0, 0.0), 0.0))
    return jnp.where(cnt < float(min(N_SELECT, n_slc)), jnp.where(blk < n_slc, 1.0, 0.0), 0.0)


def _expand_matrix(nkeys):
    s = lax.broadcasted_iota(jnp.int32, (LANES, nkeys), 0)
    j = lax.broadcasted_iota(jnp.int32, (LANES, nkeys), 1)
    lo = s * SLC_BLOCK
    return jnp.where(j >= lo, jnp.where(j < lo + SLC_BLOCK, 1.0, 0.0), 0.0).astype(BF16)


def _nsa_prompt_kernel(q_ref, kc_ref, vc_ref, ks_ref, vs_ref, kw_ref, vw_ref, bt_ref, bc_ref, gates_ref,
                       o_ref, selk_ref, m_ref, l_ref, acc_ref, *, T):
    qb = pl.program_id(2)
    QB = Q_BLOCK
    R = NSA_REP
    nc = (T - CMP_BLOCK) // CMP_STRIDE + 1
    ncp = kc_ref.shape[-2]
    n_slc = -(-T // SLC_BLOCK)
    q = jnp.concatenate([q_ref[:, r * HEAD_DIM:(r + 1) * HEAD_DIM] for r in range(R)], axis=0)

    s = _dot_t(q, kc_ref[0, 0, 0]) + bc_ref[0].reshape(R * QB, ncp)
    t_row = lax.broadcasted_iota(jnp.int32, (R, QB, ncp), 1).reshape(R * QB, ncp)
    c_col = lax.broadcasted_iota(jnp.int32, (R * QB, ncp), 1)
    dist = qb * QB + t_row - (CMP_STRIDE * c_col + CMP_BLOCK - 1)
    p = _masked_softmax(s, jnp.where(c_col < nc, dist, -1) >= 0)
    o_cmp = jnp.dot(p.astype(BF16), vc_ref[0, 0, 0], preferred_element_type=F32)

    psum = p[0:QB]
    for r in range(1, R):
        psum = psum + p[r * QB:(r + 1) * QB]
    imp = jnp.dot(psum, _overlap_matrix(ncp), precision=HIGHEST, preferred_element_type=F32)
    cur = _div_pow2(qb * QB + lax.broadcasted_iota(jnp.int32, (QB, 1), 0), SLC_BLOCK)
    sel = _select_mask(imp, cur, n_slc)
    selk_ref[...] = jnp.dot(sel.astype(BF16), _expand_matrix(T), preferred_element_type=F32)

    ti = lax.broadcasted_iota(jnp.int32, (QB, QB), 0)
    kj = lax.broadcasted_iota(jnp.int32, (QB, QB), 1)
    rel = ti - kj

    def flash(k_ref, v_ref, lo, use_sel):
        m_ref[...] = jnp.full(m_ref.shape, NEG_INF, F32)
        l_ref[...] = jnp.zeros(l_ref.shape, F32)
        acc_ref[...] = jnp.zeros(acc_ref.shape, F32)

        def body(kb, carry):
            off = pl.multiple_of(kb * QB, QB)
            delta = qb - kb
            sc = _dot_t(q, k_ref[0, pl.ds(off, QB), :]) + bt_ref[delta].reshape(R * QB, QB)
            d = delta * QB + rel
            if use_sel:
                d = jnp.where(selk_ref[:, pl.ds(off, QB)] > 0.5, d, -1)
            else:
                d = jnp.where(d <= WINDOW, d, -1)
            mask = jnp.concatenate([d] * R, axis=0) >= 0
            sc = jnp.where(mask, sc, NEG_INF)
            m_old = m_ref[...]
            m_new = jnp.maximum(m_old, jnp.max(sc, axis=-1, keepdims=True))
            pe = jnp.where(mask, jnp.exp(sc - m_new), 0.0)
            alpha = jnp.exp(m_old - m_new)
            l_ref[...] = alpha * l_ref[...] + jnp.sum(pe, axis=-1, keepdims=True)
            acc_ref[...] = alpha * acc_ref[...] + jnp.dot(pe.astype(BF16), v_ref[0, pl.ds(off, QB), :],
                                                          preferred_element_type=F32)
            m_ref[...] = m_new
            return carry

        lax.fori_loop(lo, qb + 1, body, 0)
        l = l_ref[...]
        return acc_ref[...] / jnp.where(l > 0, l, 1.0)

    o_slc = flash(ks_ref, vs_ref, 0, True)
    o_win = flash(kw_ref, vw_ref, jnp.maximum(qb - WINDOW // QB, 0), False)

    gt = gates_ref[0]
    for r in range(R):
        sl = slice(r * QB, (r + 1) * QB)
        o = (gt[:, r:r + 1] * o_cmp[sl] + gt[:, R + r:R + r + 1] * o_slc[sl]
             + gt[:, 2 * R + r:2 * R + r + 1] * o_win[sl])
        o_ref[:, r * HEAD_DIM:(r + 1) * HEAD_DIM] = o


def _nsa_prompt(q, cmp_kv, kvb3, gates_g, bias_tiles, bias_cmp, *, B, T):
    G = NSA_KV_HEADS
    R = NSA_REP
    nqb = T // Q_BLOCK
    ncp = T // CMP_STRIDE
    kv_spec = lambda col: pl.BlockSpec((1, T, HEAD_DIM), lambda b, g, i, col=col: (b, 0, col + g))
    return pl.pallas_call(
        functools.partial(_nsa_prompt_kernel, T=T),
        grid=(B, G, nqb),
        in_specs=[pl.BlockSpec((Q_BLOCK, R * HEAD_DIM), lambda b, g, i: (b * nqb + i, g)),
                  pl.BlockSpec((1, 1, 1, ncp, HEAD_DIM), lambda b, g, i: (b, 0, g, 0, 0)),
                  pl.BlockSpec((1, 1, 1, ncp, HEAD_DIM), lambda b, g, i: (b, 1, g, 0, 0)),
                  kv_spec(0), kv_spec(2), kv_spec(4), kv_spec(6),
                  pl.BlockSpec((nqb, R, Q_BLOCK, Q_BLOCK), lambda b, g, i: (0, g, 0, 0)),
                  pl.BlockSpec((1, R, Q_BLOCK, ncp), lambda b, g, i: (i, g, 0, 0)),
                  pl.BlockSpec((1, Q_BLOCK, LANES), lambda b, g, i: (g, b * nqb + i, 0))],
        out_specs=pl.BlockSpec((Q_BLOCK, R * HEAD_DIM), lambda b, g, i: (b * nqb + i, g)),
        out_shape=jax.ShapeDtypeStruct((B * T, G * R * HEAD_DIM), F32),
        scratch_shapes=[pltpu.VMEM((Q_BLOCK, T), F32),
                        pltpu.VMEM((R * Q_BLOCK, 1), F32),
                        pltpu.VMEM((R * Q_BLOCK, 1), F32),
                        pltpu.VMEM((R * Q_BLOCK, HEAD_DIM), F32)],
        compiler_params=_cp(("parallel", "parallel", "arbitrary")), name="nsa_prompt",
    )(q, cmp_kv, cmp_kv, kvb3, kvb3, kvb3, kvb3, bias_tiles, bias_cmp, gates_g)


def _nsa_sample_kernel(pt_ref, *refs, npages, tq):
    pages = refs[:npages]
    (win_ref, wnew_ref, q_ref, kvn_ref, gates_ref, pe_ref, w1_ref, w2_ref, kg_ref, bc_ref, bs_ref, bw_ref,
     o_ref, wout_ref) = refs[npages:]
    G, R, hd = NSA_KV_HEADS, NSA_REP, HEAD_DIM
    KVR = 4 * G
    WR = 2 * G
    page = pages[0].shape[2] // KVR
    past = npages * page
    L = past + tq
    nc = (L - CMP_BLOCK) // CMP_STRIDE + 1
    ng = past // CMP_STRIDE
    n_slc = -(-L // SLC_BLOCK)
    rows = R * tq
    wb = win_ref.shape[2] // WR
    wout_ref[0, 0, 0:(wb - tq) * WR, :] = win_ref[0, 0, tq * WR:wb * WR, :]
    wout_ref[0, 0, (wb - tq) * WR:wb * WR, :] = wnew_ref[0]
    pad_new = lambda v: jnp.concatenate([v, jnp.zeros((LANES - v.shape[0], v.shape[1]), v.dtype)], axis=0)
    t_of_row = lax.broadcasted_iota(jnp.int32, (R, tq, 1), 1).reshape(rows, 1)
    pos = past + t_of_row
    ra = lax.broadcasted_iota(jnp.int32, (rows, rows), 0)
    rb = lax.broadcasted_iota(jnp.int32, (rows, rows), 1)
    same_t = jnp.where(_mod_pow2(ra, tq) == _mod_pow2(rb, tq), 1.0, 0.0).astype(F32)
    kg = kg_ref[...]
    kvn = kvn_ref[0]

    for g in range(G):
        q = q_ref[0, g]
        gt = gates_ref[0, g]

        def compress(slot):
            col = slot * G + g

            def get_rows(r):
                return jnp.concatenate(
                    [pg[0, 0, pl.ds(r * KVR + col, page // CMP_STRIDE, stride=CMP_STRIDE * KVR), :] for pg in pages],
                    axis=0)

            return _compress_math(get_rows, ng, pe_ref[slot], w1_ref.at[slot], w2_ref[slot])

        kc = _rms(compress(0), kg).astype(BF16)
        vc = compress(1).astype(BF16)

        c_col = lax.broadcasted_iota(jnp.int32, (rows, ng), 1)
        dist = pos - (CMP_STRIDE * c_col + CMP_BLOCK - 1)
        p = _masked_softmax(_dot_t(q, kc) + bc_ref[g], jnp.where(c_col < nc, dist, -1) >= 0)
        o_cmp = jnp.dot(p.astype(BF16), vc, preferred_element_type=F32)

        psum = jnp.dot(same_t, p, precision=HIGHEST, preferred_element_type=F32)
        imp = jnp.dot(psum, _overlap_matrix(ng), precision=HIGHEST, preferred_element_type=F32)
        sel = _select_mask(imp, _div_pow2(pos, SLC_BLOCK), n_slc)
        nk = past + LANES
        selk = jnp.dot(sel.astype(BF16), _expand_matrix(nk), preferred_element_type=F32)

        k_new = pad_new(kvn[:, g * hd:(g + 1) * hd])
        v_new = pad_new(kvn[:, 256 + g * hd:256 + (g + 1) * hd])
        page_rows = lambda pg, slot: pg[0, 0, pl.ds(slot * G + g, page, stride=KVR), :].astype(BF16)
        sc = jnp.concatenate([_dot_t(q, page_rows(pg, 2)) for pg in pages]
                             + [_dot_t(q, k_new)], axis=-1) + bs_ref[g]
        j = lax.broadcasted_iota(jnp.int32, (rows, nk), 1)
        p = _masked_softmax(sc, jnp.where(selk > 0.5, pos - j, -1) >= 0).astype(BF16)
        o_slc = jnp.dot(p[:, past:], v_new, preferred_element_type=F32)
        for i, pg in enumerate(pages):
            o_slc = o_slc + jnp.dot(p[:, i * page:(i + 1) * page], page_rows(pg, 3), preferred_element_type=F32)

        kw_new = pad_new(kvn[:, 512 + g * hd:512 + (g + 1) * hd])
        vw_new = pad_new(kvn[:, 768 + g * hd:768 + (g + 1) * hd])
        kw = win_ref[0, 0, pl.ds(g, wb, stride=WR), :].astype(BF16)
        vw = win_ref[0, 0, pl.ds(G + g, wb, stride=WR), :].astype(BF16)
        sc = jnp.concatenate([_dot_t(q, kw), _dot_t(q, kw_new)], axis=-1) + bw_ref[g]
        j = lax.broadcasted_iota(jnp.int32, (rows, wb + LANES), 1)
        d = pos - (past - wb + j)
        p = _masked_softmax(sc, jnp.where(d <= WINDOW, d, -1) >= 0).astype(BF16)
        o_win = (jnp.dot(p[:, :wb], vw, preferred_element_type=F32)
                 + jnp.dot(p[:, wb:], vw_new, preferred_element_type=F32))

        o_ref[0, g] = gt[:, 0:1] * o_cmp + gt[:, 1:2] * o_slc + gt[:, 2:3] * o_win


def _nsa_sample(l, page_table, cache_lin, win_lin, win_new, q_s, kvn_s, gates_s, lw, bias_c, bias_s, bias_w, *, tq):
    B, npages = page_table.shape
    G = NSA_KV_HEADS
    rows = NSA_REP * tq
    page_specs = [pl.BlockSpec((1, 1) + cache_lin.shape[2:], lambda b, pt, p=p: (l, pt[b, p], 0, 0))
                  for p in range(npages)]
    full = lambda a: pl.BlockSpec(a.shape, lambda b, pt, n=a.ndim: (0,) * n)
    wts = [lw["cmp_pe"], lw["cmp_w1"], lw["cmp_w2"], lw["k_norm_g"].reshape(1, -1), bias_c, bias_s, bias_w]
    grid_spec = pltpu.PrefetchScalarGridSpec(
        num_scalar_prefetch=1, grid=(B,),
        in_specs=page_specs + [
            pl.BlockSpec((1, 1) + win_lin.shape[2:], lambda b, pt: (l, b, 0, 0)),
            pl.BlockSpec((1,) + win_new.shape[1:], lambda b, pt: (b, 0, 0)),
            pl.BlockSpec((1, G, rows, HEAD_DIM), lambda b, pt: (b, 0, 0, 0)),
            pl.BlockSpec((1, SUBLANES, kvn_s.shape[2]), lambda b, pt: (b, 0, 0)),
            pl.BlockSpec((1, G, rows, LANES), lambda b, pt: (b, 0, 0, 0))] + [full(w) for w in wts],
        out_specs=[pl.BlockSpec((1, G, rows, HEAD_DIM), lambda b, pt: (b, 0, 0, 0)),
                   pl.BlockSpec((1, 1) + win_lin.shape[2:], lambda b, pt: (0, b, 0, 0))])
    return pl.pallas_call(
        functools.partial(_nsa_sample_kernel, npages=npages, tq=tq),
        grid_spec=grid_spec,
        out_shape=[jax.ShapeDtypeStruct((B, G, rows, HEAD_DIM), F32),
                   jax.ShapeDtypeStruct((1,) + win_lin.shape[1:], F32)],
        compiler_params=_cp(("arbitrary",)), name="nsa_sample",
    )(page_table, *([cache_lin] * npages), win_lin, win_new, q_s, kvn_s, gates_s, *wts)


def _gdn_prompt_kernel(gq_ref, ggb_ref, gz_ref, s0_ref, ng_ref, y_ref, so_ref, s_ref):
    n = pl.program_id(0)
    B = gq_ref.shape[0]
    C = gq_ref.shape[1]
    hd = HEAD_DIM
    W = GDN_HEADS * hd

    @pl.when(n == 0)
    def _():
        s_ref[...] = s0_ref[...]

    ri = lax.broadcasted_iota(jnp.int32, (C, C), 0)
    ci = lax.broadcasted_iota(jnp.int32, (C, C), 1)
    tri_incl = ri >= ci
    tri_strict = ri > ci
    tri_f = jnp.where(tri_incl, 1.0, 0.0).astype(F32)
    eye = jnp.where(ri == ci, 1.0, 0.0).astype(F32)
    ones = jnp.ones((C, C), F32)
    dot = functools.partial(jnp.dot, precision=HIGHEST, preferred_element_type=F32)
    ng = ng_ref[...]

    for b in range(B):
        for h in range(GDN_HEADS):
            q = gq_ref[b, :, h * hd:(h + 1) * hd]
            k = gq_ref[b, :, W + h * hd:W + (h + 1) * hd]
            v = gq_ref[b, :, 2 * W + h * hd:2 * W + (h + 1) * hd]
            g = ggb_ref[b, :, h * hd:(h + 1) * hd]
            beta = ggb_ref[b, :, W + h * hd:W + (h + 1) * hd]
            gc = dot(tri_f, g)
            gcol = gc[:, :C]
            grow = dot(ones, eye * gcol)
            decay = jnp.exp(jnp.where(tri_incl, gcol - grow, -jnp.inf))
            kb = k * beta
            low = jnp.where(tri_strict, _dot_t(kb, k, HIGHEST) * decay, 0.0)
            tinv = eye - low
            pw = low
            span = 2
            while span < C:
                pw = dot(pw, pw)
                tinv = dot(tinv, eye + pw)
                span *= 2
            egc = jnp.exp(gc)
            u = dot(tinv, v * beta)
            w = dot(tinv, kb * egc)
            qk = jnp.where(tri_incl, _dot_t(q, k, HIGHEST) * decay, 0.0)
            S = s_ref[b, h]
            v_new = u - dot(w, S)
            o = dot(q * egc, S) + dot(qk, v_new)
            g_last = gc[C - 1:C]
            kd = k * jnp.exp(g_last - gc)
            s_ref[b, h] = S * jnp.exp(g_last) + lax.dot_general(
                kd, v_new, (((0,), (0,)), ((), ())), precision=HIGHEST, preferred_element_type=F32)
            z = gz_ref[b, :, h * hd:(h + 1) * hd]
            y_ref[b, :, h * hd:(h + 1) * hd] = (_rms(o, ng) * (z * _sigmoid(z))).astype(y_ref.dtype)

    @pl.when(n == pl.num_programs(0) - 1)
    def _():
        so_ref[...] = s_ref[...]


def _gdn_prompt(gq3, ggb3, z3, s0, norm_g):
    B, T, _ = gq3.shape
    C = GDN_CHUNK
    W = GDN_HEADS * HEAD_DIM
    return pl.pallas_call(
        _gdn_prompt_kernel,
        grid=(T // C,),
        in_specs=[pl.BlockSpec((B, C, 3 * W), lambda n: (0, n, 0)),
                  pl.BlockSpec((B, C, 2 * W), lambda n: (0, n, 0)),
                  pl.BlockSpec((B, C, W), lambda n: (0, n, C_GZ // W)),
                  pl.BlockSpec(s0.shape, lambda n: (0, 0, 0, 0)),
                  pl.BlockSpec((1, HEAD_DIM), lambda n: (0, 0))],
        out_specs=[pl.BlockSpec((B, C, W), lambda n: (0, n, 0)),
                   pl.BlockSpec(s0.shape, lambda n: (0, 0, 0, 0))],
        out_shape=[jax.ShapeDtypeStruct((B, T, W), BF16), jax.ShapeDtypeStruct(s0.shape, F32)],
        scratch_shapes=[pltpu.VMEM(s0.shape, F32)],
        compiler_params=_cp(("arbitrary",)), name="gdn_prompt",
    )(gq3, ggb3, z3, s0, norm_g.reshape(1, -1))


def _gdn_sample_kernel(kq_ref, v_ref, gb_ref, gz_ref, s0_ref, ng_ref, y_ref, so_ref, *, tq):
    bt = kq_ref.shape[0]
    hd = HEAD_DIM
    ri = lax.broadcasted_iota(jnp.int32, (hd, hd), 0)
    ci = lax.broadcasted_iota(jnp.int32, (hd, hd), 1)
    eye = jnp.where(ri == ci, 1.0, 0.0).astype(F32)
    ng = ng_ref[...]

    def body(bi, carry):
        for h in range(GDN_HEADS):
            kq = kq_ref[bi, h]
            cols = _dot_t(eye, kq, HIGHEST)
            gb = gb_ref[bi, h]
            v = v_ref[bi, h]
            S = s0_ref[bi, h]
            outs = []
            for t in range(tq):
                a = jnp.exp(gb[t:t + 1])
                kc = cols[:, t:t + 1]
                qc = cols[:, tq + t:tq + t + 1]
                Sa = S * a
                stk = jnp.sum(Sa * kc, axis=0, keepdims=True)
                vn = gb[tq + t:tq + t + 1] * (v[t:t + 1] - stk)
                S = Sa + kc * vn
                outs.append(jnp.sum(S * qc, axis=0, keepdims=True))
            so_ref[bi, h] = S
            o = jnp.concatenate(outs + [jnp.zeros((SUBLANES - tq, hd), F32)], axis=0)
            z = gz_ref[bi, h]
            y_ref[bi, h] = _rms(o, ng) * (z * _sigmoid(z))
        return carry

    lax.fori_loop(0, bt, body, 0)


def _gdn_sample(kq_s, v_s, gb_s, gz_s, s0, norm_g, *, tq):
    B = kq_s.shape[0]
    bt = 8
    blk = lambda a: pl.BlockSpec((bt,) + a.shape[1:], lambda i: (i, 0, 0, 0))
    return pl.pallas_call(
        functools.partial(_gdn_sample_kernel, tq=tq),
        grid=(B // bt,),
        in_specs=[blk(kq_s), blk(v_s), blk(gb_s), blk(gz_s), blk(s0), pl.BlockSpec((1, HEAD_DIM), lambda i: (0, 0))],
        out_specs=[blk(v_s), blk(s0)],
        out_shape=[jax.ShapeDtypeStruct(v_s.shape, F32), jax.ShapeDtypeStruct(s0.shape, F32)],
        compiler_params=_cp(("parallel",)), name="gdn_sample",
    )(kq_s, v_s, gb_s, gz_s, s0, norm_g.reshape(1, -1))


def _mix_out_kernel(ya_ref, yb_ref, yc_ref, x_ref, onb_ref, w_ref, o_ref, mix_ref):
    @pl.when(pl.program_id(1) == 0)
    def _():
        mix_ref[:, 0:512] = ya_ref[...]
        mix_ref[:, 512:1536] = _rms(yb_ref[...], onb_ref[...]).astype(BF16)
        mix_ref[:, 1536:2048] = yc_ref[...]

    o_ref[...] = x_ref[...] + jnp.dot(mix_ref[...], w_ref[...], preferred_element_type=F32)


def _mix_out(ya, yb, yc, x, onb, w, *, tm=512, tn=512):
    M, N = x.shape
    tm = min(tm, M)
    K = w.shape[0]
    return pl.pallas_call(
        _mix_out_kernel,
        grid=(M // tm, N // tn),
        in_specs=[pl.BlockSpec((tm, 512), lambda i, j: (i, 0)),
                  pl.BlockSpec((tm, 1024), lambda i, j: (i, 0)),
                  pl.BlockSpec((tm, 512), lambda i, j: (i, 0)),
                  pl.BlockSpec((tm, tn), lambda i, j: (i, j)),
                  pl.BlockSpec((1, 1024), lambda i, j: (0, 0)),
                  pl.BlockSpec((K, tn), lambda i, j: (0, j))],
        out_specs=pl.BlockSpec((tm, tn), lambda i, j: (i, j)),
        out_shape=jax.ShapeDtypeStruct((M, N), F32),
        scratch_shapes=[pltpu.VMEM((tm, K), BF16)],
        compiler_params=_cp(("parallel", "arbitrary")), name="mix_out",
    )(ya, yb, yc, x, onb.reshape(1, -1), w)


def _ffn_mid_kernel(a_ref, v_ref, ap_ref, vp_ref, ha_ref, hv_ref, wa_ref, wv_ref, o_ref, *, tstride, tiles_per_seq):
    tc = wa_ref.shape[1]

    def conv(x_ref, prev_ref, hist_ref, w_ref):
        x = _ld(x_ref, 0, tc)
        if tstride == 1:
            start = (pl.program_id(0) % tiles_per_seq) == 0
            halo = jnp.where(start, hist_ref[0], prev_ref[...])
        else:
            halo = _ld(hist_ref, 0, tc)
        w = w_ref[...]
        return w[0:1] * _shift(x, halo, 2, tstride) + w[1:2] * _shift(x, halo, 1, tstride) + w[2:3] * x

    a = conv(a_ref, ap_ref, ha_ref, wa_ref)
    v = conv(v_ref, vp_ref, hv_ref, wv_ref)
    _st(o_ref, 0, tc, a * _sigmoid(a) * v)


def _ffn_mid(up, hist, w, *, sample, nseq, seq_len, tc=512):
    M, F2 = up.shape
    F = F2 // 2
    nj = F // tc
    wspec_a = pl.BlockSpec((3, tc), lambda i, j: (0, j))
    wspec_v = pl.BlockSpec((3, tc), lambda i, j: (0, j + nj))
    if not sample:
        tm = 512
        tps = seq_len // tm
        prev = lambda off: pl.BlockSpec(
            (SUBLANES, tc), lambda i, j, off=off: (jnp.maximum(i * (tm // SUBLANES) - 1, 0), j + off))
        hspec = lambda off: pl.BlockSpec((1, SUBLANES, tc), lambda i, j, off=off: (i // tps, 0, j + off))
        return pl.pallas_call(
            functools.partial(_ffn_mid_kernel, tstride=1, tiles_per_seq=tps),
            grid=(M // tm, nj),
            in_specs=[pl.BlockSpec((tm, tc), lambda i, j: (i, j)), pl.BlockSpec((tm, tc), lambda i, j: (i, j + nj)),
                      prev(0), prev(nj), hspec(0), hspec(nj), wspec_a, wspec_v],
            out_specs=pl.BlockSpec((tm, tc), lambda i, j: (i, j)),
            out_shape=jax.ShapeDtypeStruct((M, F), BF16),
            compiler_params=_cp(("parallel", "parallel")), name="ffn_mid_prompt",
        )(up, up, up, up, hist, hist, w, w)
    T = seq_len
    up3 = up.reshape(T, nseq, F2)
    hk = hist.shape[0]
    out = pl.pallas_call(
        functools.partial(_ffn_mid_kernel, tstride=nseq, tiles_per_seq=1),
        grid=(1, nj),
        in_specs=[pl.BlockSpec((T, nseq, tc), lambda i, j: (0, 0, j)),
                  pl.BlockSpec((T, nseq, tc), lambda i, j: (0, 0, j + nj)),
                  pl.BlockSpec((SUBLANES, tc), lambda i, j: (0, j)), pl.BlockSpec((SUBLANES, tc), lambda i, j: (0, j + nj)),
                  pl.BlockSpec((hk, nseq, tc), lambda i, j: (0, 0, j)),
                  pl.BlockSpec((hk, nseq, tc), lambda i, j: (0, 0, j + nj)), wspec_a, wspec_v],
        out_specs=pl.BlockSpec((T, nseq, tc), lambda i, j: (0, 0, j)),
        out_shape=jax.ShapeDtypeStruct((T, nseq, F), BF16),
        compiler_params=_cp(("parallel", "parallel")), name="ffn_mid_sample",
    )(up3, up3, up, up, hist, hist, w, w)
    return out.reshape(M, F)


def _matmul_res_kernel(a_ref, w_ref, r_ref, o_ref):
    o_ref[...] = r_ref[...] + jnp.dot(a_ref[...], w_ref[...], preferred_element_type=F32)


def _matmul_res(a, w, res, *, tm=512, tn=512):
    M, K = a.shape
    N = w.shape[1]
    tm = min(tm, M)
    return pl.pallas_call(
        _matmul_res_kernel,
        grid=(M // tm, N // tn),
        in_specs=[pl.BlockSpec((tm, K), lambda i, j: (i, 0)),
                  pl.BlockSpec((K, tn), lambda i, j: (0, j)),
                  pl.BlockSpec((tm, tn), lambda i, j: (i, j))],
        out_specs=pl.BlockSpec((tm, tn), lambda i, j: (i, j)),
        out_shape=jax.ShapeDtypeStruct((M, N), F32),
        compiler_params=_cp(("parallel", "parallel")), name="matmul_res",
    )(a, w, res)


def _layer_weights(l, p):
    w_in = p["w_in"][l]
    w_main = jnp.concatenate([w_in[:, :4096], w_in[:, 4120:5656], w_in[:, 5664:6176]], axis=1).astype(BF16)
    w_small = jnp.concatenate([w_in[:, 4096:4120], w_in[:, 5656:5664],
                               jnp.zeros((w_in.shape[0], LANES - 32), w_in.dtype)], axis=1).astype(BF16)
    lane_vec = lambda v: jnp.zeros((1, LANES), F32).at[0, S_GA:S_GA + GDN_HEADS].set(v)
    return {
        "norm_mix_g": p["norm_mix_g"][l], "w_main": w_main, "w_small": w_small,
        "conv_a_w": p["conv_a_w"][l], "q_norm_g": p["q_norm_g"][l], "k_norm_g": p["k_norm_g"][l],
        "cmp_pe": p["cmp_pe"][l],
        "cmp_w1": p["cmp_w1"][l].reshape(2, CMP_BLOCK, HEAD_DIM, CMP_HIDDEN).astype(BF16),
        "cmp_w2": p["cmp_w2"][l].astype(BF16),
        "gdn_conv_w": p["gdn_conv_w"][l], "alog_v": lane_vec(p["gdn_a_log"][l]), "dtb_v": lane_vec(p["gdn_dt_bias"][l]),
        "gdn_norm_g": p["gdn_norm_g"][l], "out_norm_a": p["out_norm_a"][l], "out_norm_b": p["out_norm_b"][l],
        "w_out": p["w_out"][l].astype(BF16), "norm_ffn_g": p["norm_ffn_g"][l],
        "ffn_up": p["ffn_up"][l].astype(BF16), "ffn_conv_w": p["ffn_conv_w"][l],
        "ffn_down": p["ffn_down"][l].astype(BF16),
    }


def _dense_tail(x2, ya, yb, yc, lw, hist_ffn, *, sample, nseq, seq_len):
    h = _mix_out(ya, yb, yc, x2, lw["out_norm_b"], lw["w_out"])
    up = _norm_matmul(h, lw["norm_ffn_g"], lw["ffn_up"])
    act = _ffn_mid(up, hist_ffn, lw["ffn_conv_w"], sample=sample, nseq=nseq, seq_len=seq_len)
    return _matmul_res(act, lw["ffn_down"], h), up


def _prompt_layer(x2, lw, bias_tiles, bias_cmp, *, B, T):
    M = B * T
    G, R = NSA_KV_HEADS, NSA_REP
    z, zs = _norm_matmul(x2, lw["norm_mix_g"], lw["w_main"], lw["w_small"])
    hu0 = jnp.zeros((B, SUBLANES, 512), F32)
    hq0 = jnp.zeros((B, SUBLANES, 1536), F32)
    ya, q, rows, win, kvb, gates, gq, ggb, hu, rows_lin = _prep(z, zs, hu0, hq0, lw, sample=False, nseq=B, seq_len=T)
    cmp_kv = _compress_prompt(rows.reshape(B, T, 1024), lw)
    gates_g = gates[:, :3 * G * R].reshape(M, 3, G, R).transpose(2, 0, 1, 3).reshape(G, M, 3 * R)
    gates_g = jnp.pad(gates_g, ((0, 0), (0, 0), (0, LANES - 3 * R)))
    yb = _nsa_prompt(q, cmp_kv, kvb.reshape(B, T, 1024), gates_g, bias_tiles, bias_cmp, B=B, T=T)
    s0 = jnp.zeros((B, GDN_HEADS, HEAD_DIM, HEAD_DIM), F32)
    yc, s_new = _gdn_prompt(gq.reshape(B, T, 1536), ggb.reshape(B, T, 1024), z.reshape(B, T, C_END), s0,
                            lw["gdn_norm_g"])
    hf0 = jnp.zeros((B, SUBLANES, lw["ffn_conv_w"].shape[1]), F32)
    out, up = _dense_tail(x2, ya, yb, yc.reshape(M, 512), lw, hf0, sample=False, nseq=B, seq_len=T)
    return out, rows_lin, win, hu, z, s_new, up


def _sample_layer(l, x2, lw, st, bias_c, bias_s, bias_w, *, B, T):
    M = B * T
    G, R, H, hd = NSA_KV_HEADS, NSA_REP, GDN_HEADS, HEAD_DIM
    z, zs = _norm_matmul(x2, lw["norm_mix_g"], lw["w_main"], lw["w_small"])
    hu0 = jnp.swapaxes(st["conv_a"], 0, 1)
    hq0 = jnp.swapaxes(st["gdn_conv"], 0, 1)
    ya, q, rows, win, kvb, gates, gq, ggb, hu = _prep(z, zs, hu0, hq0, lw, sample=True, nseq=B, seq_len=T)
    tb = lambda a: a.reshape(T, B, -1)
    q_s = tb(q).reshape(T, B, G, R, hd).transpose(1, 2, 3, 0, 4).reshape(B, G, R * T, hd)
    gt = tb(gates)[:, :, :3 * G * R].reshape(T, B, 3, G, R).transpose(1, 3, 4, 0, 2).reshape(B, G, R * T, 3)
    gates_s = jnp.pad(gt, ((0, 0), (0, 0), (0, 0), (0, LANES - 3)))
    kvn_s = jnp.pad(jnp.swapaxes(tb(kvb), 0, 1), ((0, 0), (0, SUBLANES - T), (0, 0)))
    win_new = jnp.swapaxes(tb(win), 0, 1).reshape(B, T * 2 * G, hd)
    yb_s, win = _nsa_sample(l, st["page_table"], st["cache_lin"], st["win_lin"], win_new, q_s, kvn_s, gates_s, lw,
                            bias_c, bias_s, bias_w, tq=T)
    yb = yb_s.reshape(B, G, R, T, hd).transpose(3, 0, 1, 2, 4).reshape(M, G * R * hd)
    gq4 = tb(gq).reshape(T, B, 3, H, hd)
    bh = lambda a: a.transpose(1, 2, 0, 3)
    padt = lambda a: jnp.pad(a, ((0, 0), (0, 0), (0, SUBLANES - T), (0, 0)))
    kq_s = jnp.concatenate([bh(gq4[:, :, 1]), bh(gq4[:, :, 0])], axis=2)
    v_s = padt(bh(gq4[:, :, 2]))
    ggb4 = tb(ggb).reshape(T, B, 2, H, hd)
    gb_s = jnp.concatenate([bh(ggb4[:, :, 0]), bh(ggb4[:, :, 1])], axis=2)
    gz_s = padt(bh(tb(z)[:, :, C_GZ:C_END].reshape(T, B, H, hd)))
    yc_s, s_new = _gdn_sample(kq_s, v_s, gb_s, gz_s, st["gdn"], lw["gdn_norm_g"], tq=T)
    yc = yc_s[:, :, :T].transpose(2, 0, 1, 3).reshape(M, H * hd).astype(BF16)
    hf0 = jnp.swapaxes(st["ffn_conv"], 0, 1)
    out, up = _dense_tail(x2, ya, yb, yc, lw, hf0, sample=True, nseq=B, seq_len=T)
    return out, rows, win, hu, z, s_new, up


def kernel(x_prompt, x_sample, cache_kv, cache_win, state_conv_a, state_gdn_conv, state_gdn, state_ffn_conv,
           page_table, rel_bias, norm_mix_g, w_in, conv_a_w, q_norm_g, k_norm_g, cmp_pe, cmp_w1, cmp_w2,
           gdn_conv_w, gdn_a_log, gdn_dt_bias, gdn_norm_g, out_norm_a, out_norm_b, w_out, norm_ffn_g,
           ffn_up, ffn_conv_w, ffn_down):
    params = dict(norm_mix_g=norm_mix_g, w_in=w_in, conv_a_w=conv_a_w, q_norm_g=q_norm_g, k_norm_g=k_norm_g,
                  cmp_pe=cmp_pe, cmp_w1=cmp_w1, cmp_w2=cmp_w2, gdn_conv_w=gdn_conv_w, gdn_a_log=gdn_a_log,
                  gdn_dt_bias=gdn_dt_bias, gdn_norm_g=gdn_norm_g, out_norm_a=out_norm_a, out_norm_b=out_norm_b,
                  w_out=w_out, norm_ffn_g=norm_ffn_g, ffn_up=ffn_up, ffn_conv_w=ffn_conv_w, ffn_down=ffn_down)
    depth = w_in.shape[0]
    Bp, T, D = x_prompt.shape
    Bs, Ts, _ = x_sample.shape
    G, R, hd = NSA_KV_HEADS, NSA_REP, HEAD_DIM
    n_pool, page = cache_kv.shape[1], cache_kv.shape[2]
    npages = page_table.shape[1]
    past = npages * page
    wb = cache_win.shape[2]
    L = past + Ts
    assert ((L - CMP_BLOCK) // CMP_STRIDE) * CMP_STRIDE + CMP_BLOCK <= past, "compressed blocks must lie in the cache"
    assert T % Q_BLOCK == 0 and T % 512 == 0 and 3 <= Ts <= SUBLANES // 2

    thr = _bucket_thresholds()
    rb_flat = rel_bias.reshape(-1)
    nqb = T // Q_BLOCK
    bias_tiles = _bias_table(thr, rb_flat, nqb, Q_BLOCK, Q_BLOCK, a0=0, an=Q_BLOCK, qs=1, ks=1)
    bias_cmp = _bias_table(thr, rb_flat, nqb, Q_BLOCK, T // CMP_STRIDE,
                           a0=-(CMP_BLOCK - 1), an=Q_BLOCK, qs=1, ks=CMP_STRIDE)

    def sample_bias(ncols, a0, ks):
        t = _bias_table(thr, rb_flat, 1, SUBLANES, ncols, a0=a0, an=0, qs=1, ks=ks)[0]
        return t[:, :Ts].reshape(G, R * Ts, ncols)

    bias_c = sample_bias(past // CMP_STRIDE, past - (CMP_BLOCK - 1), CMP_STRIDE)
    bias_s = sample_bias(past + LANES, past, 1)
    bias_w = sample_bias(wb + LANES, wb, 1)

    cache_lin = cache_kv.reshape(depth, n_pool, page * 4 * G, hd)
    win_lin = cache_win.reshape(depth, Bs, wb * 2 * G, hd)
    assert wb == WINDOW, "the new window buffer is the old one shifted by the new rows"

    xp = x_prompt.reshape(Bp * T, D)
    xs = jnp.swapaxes(x_sample, 0, 1).reshape(Ts * Bs, D)
    outs_p, outs_s = [], []
    for l in range(depth):
        lw = _layer_weights(l, params)
        xp, rows, win, hu, z, s_new, up = _prompt_layer(xp, lw, bias_tiles, bias_cmp, B=Bp, T=T)
        wl = min(WINDOW, T)
        outs_p.append((
            rows.reshape(Bp, T, 4, G, hd),
            win.reshape(Bp, T, 2, G, hd)[:, T - wl:],
            hu[:, SUBLANES - 2:],
            z.reshape(Bp, T, C_END)[:, T - 3:, C_GQKV:C_GZ],
            s_new,
            up.reshape(Bp, T, -1)[:, T - 2:]))
        st = dict(page_table=page_table, cache_lin=cache_lin, win_lin=win_lin, conv_a=state_conv_a[l],
                  gdn_conv=state_gdn_conv[l], gdn=state_gdn[l], ffn_conv=state_ffn_conv[l])
        xs, rows, win, hu, z, s_new, up = _sample_layer(l, xs, lw, st, bias_c, bias_s, bias_w, B=Bs, T=Ts)
        tb = lambda a: jnp.swapaxes(a.reshape(Ts, Bs, -1), 0, 1)
        outs_s.append((
            tb(rows).reshape(Bs, Ts, 4, G, hd),
            win[0].reshape(Bs, wb, 2, G, hd),
            jnp.swapaxes(hu, 0, 1),
            tb(z)[:, Ts - 3:, C_GQKV:C_GZ],
            s_new,
            tb(up)[:, Ts - 2:]))
    y_p = xp.reshape(Bp, T, D)
    y_s = jnp.swapaxes(xs.reshape(Ts, Bs, D), 0, 1)
    stack = lambda outs, i: jnp.stack([o[i] for o in outs])
    return (y_p, y_s, stack(outs_p, 0), stack(outs_s, 0), stack(outs_p, 1), stack(outs_s, 1),
            stack(outs_p, 2), stack(outs_s, 2), stack(outs_p, 3), stack(outs_s, 3),
            stack(outs_p, 4), stack(outs_s, 4), stack(outs_p, 5), stack(outs_s, 5))
```

```python
import functools
import math

import jax
import jax.numpy as jnp
from jax import lax
from jax.experimental import pallas as pl
from jax.experimental.pallas import tpu as pltpu

F32 = jnp.float32
BF16 = jnp.bfloat16
HIGHEST = lax.Precision.HIGHEST

SUBLANES = 8
LANES = 128
VMEM_LIMIT_BYTES = 52 * 1024 * 1024

HEAD_DIM = 128
NSA_KV_HEADS = 2
NSA_REP = 4
NSA_HEADS = NSA_KV_HEADS * NSA_REP
GDN_HEADS = 4
CMP_BLOCK = 32
CMP_STRIDE = 16
CMP_HIDDEN = 256
SLC_BLOCK = 64
N_SELECT = 16
WINDOW = 512
Q_BLOCK = 128
GDN_CHUNK = 64
REL_BUCKETS = 32
REL_MAX_DIST = 1024
EPS = 1e-6
NEG_INF = -1e30
FORCE_SCORE = 1e4

C_AB, C_AC, C_AH, C_NQ, C_NKV, C_GQKV, C_GZ, C_END = 0, 512, 1024, 1536, 2560, 4096, 5632, 6144
S_GATE, S_GA, S_GB = 0, 24, 28


def _cp(sem):
    return pltpu.CompilerParams(dimension_semantics=sem, vmem_limit_bytes=VMEM_LIMIT_BYTES)


def _div_pow2(x, d):
    assert d & (d - 1) == 0
    return jnp.right_shift(x, d.bit_length() - 1)


def _mod_pow2(x, d):
    assert d & (d - 1) == 0
    return jnp.bitwise_and(x, d - 1)


def _sigmoid(x):
    return 1.0 / (1.0 + jnp.exp(-x))


def _rms(x, g):
    return x * lax.rsqrt(jnp.mean(x * x, axis=-1, keepdims=True) + EPS) * g


def _l2n(x):
    return x * lax.rsqrt(jnp.sum(x * x, axis=-1, keepdims=True) + EPS)


def _dot_t(a, b, precision=None):
    return lax.dot_general(a, b, (((1,), (1,)), ((), ())), precision=precision, preferred_element_type=F32)


def _ld(ref, c0, c1):
    if len(ref.shape) == 3:
        v = ref[:, :, c0:c1]
        return v.reshape(v.shape[0] * v.shape[1], v.shape[2])
    return ref[:, c0:c1]


def _st(ref, c0, c1, val):
    if len(ref.shape) == 3:
        ref[:, :, c0:c1] = val.reshape(ref.shape[0], ref.shape[1], c1 - c0).astype(ref.dtype)
    else:
        ref[:, c0:c1] = val.astype(ref.dtype)


def _shift(x, halo, s, tstride):
    n = s * tstride
    rows = x.shape[0]
    if tstride % SUBLANES == 0:
        hr = halo.shape[0]
        return jnp.concatenate([halo[hr - n:], x[:rows - n]], axis=0)
    xs = pltpu.roll(x, n, 0)
    hs = pltpu.roll(halo, n, 0)
    rid = lax.broadcasted_iota(jnp.int32, (SUBLANES, x.shape[1]), 0)
    head = jnp.where(rid < n, hs, xs[:SUBLANES])
    return jnp.concatenate([head, xs[SUBLANES:]], axis=0)


def _norm_matmul_kernel(x_ref, g_ref, w_ref, *rest, has_small):
    if has_small:
        ws_ref, o_ref, os_ref, xn_ref = rest
    else:
        o_ref, xn_ref = rest

    @pl.when(pl.program_id(1) == 0)
    def _():
        xn = _rms(x_ref[...], g_ref[...]).astype(BF16)
        xn_ref[...] = xn
        if has_small:
            os_ref[...] = jnp.dot(xn, ws_ref[...], preferred_element_type=F32)

    o_ref[...] = jnp.dot(xn_ref[...], w_ref[...], preferred_element_type=F32).astype(o_ref.dtype)


def _norm_matmul(x, g, w, w_small=None, *, tm=512, tn=512, out_dtype=F32):
    M, K = x.shape
    N = w.shape[1]
    tm = min(tm, M)
    has_small = w_small is not None
    in_specs = [pl.BlockSpec((tm, K), lambda i, j: (i, 0)),
                pl.BlockSpec((1, K), lambda i, j: (0, 0)),
                pl.BlockSpec((K, tn), lambda i, j: (0, j))]
    out_specs = [pl.BlockSpec((tm, tn), lambda i, j: (i, j))]
    out_shape = [jax.ShapeDtypeStruct((M, N), out_dtype)]
    args = [x, g.reshape(1, K), w]
    if has_small:
        in_specs.append(pl.BlockSpec((K, LANES), lambda i, j: (0, 0)))
        out_specs.append(pl.BlockSpec((tm, LANES), lambda i, j: (i, 0)))
        out_shape.append(jax.ShapeDtypeStruct((M, LANES), F32))
        args.append(w_small)
    res = pl.pallas_call(
        functools.partial(_norm_matmul_kernel, has_small=has_small),
        grid=(M // tm, N // tn),
        in_specs=in_specs, out_specs=out_specs, out_shape=out_shape,
        scratch_shapes=[pltpu.VMEM((tm, K), BF16)],
        compiler_params=_cp(("parallel", "arbitrary")),
        name="norm_matmul_small" if has_small else "norm_matmul",
    )(*args)
    return res if has_small else res[0]


def _prep_kernel(z_ref, zp_ref, zs_ref, hu_ref, hq_ref, caw_ref, gcw_ref, qg_ref, kg_ref, ona_ref,
                 alog_ref, dtb_ref,
                 ya_ref, q_ref, rows_ref, win_ref, kvb_ref, gates_ref, gq_ref, ggb_ref, huo_ref, *lin_refs,
                 tstride, tiles_per_seq):
    hd = HEAD_DIM

    def put_rows(grp, val):
        _st(rows_ref, grp * hd, (grp + 1) * hd, val)
        if lin_refs:
            lin_refs[0][pl.ds(grp, val.shape[0], stride=4 * NSA_KV_HEADS), :] = val
    u = _ld(z_ref, C_AC, C_AH) * _ld(z_ref, C_AH, C_NQ)
    if tstride == 1:
        start = (pl.program_id(0) % tiles_per_seq) == 0
        halo_u = jnp.where(start, hu_ref[0], zp_ref[:, C_AC:C_AH] * zp_ref[:, C_AH:C_NQ])
        halo_q = jnp.where(start, hq_ref[0], zp_ref[:, C_GQKV:C_GZ])
    else:
        halo_u = _ld(hu_ref, 0, 512)
        halo_q = _ld(hq_ref, 0, 1536)
    caw = caw_ref[...]
    conv = caw[0:1] * _shift(u, halo_u, 2, tstride) + caw[1:2] * _shift(u, halo_u, 1, tstride) + caw[2:3] * u
    y = _ld(z_ref, C_AB, C_AC) * conv
    _st(ya_ref, 0, 512, _rms(y, ona_ref[...]))
    hu_rows = huo_ref.shape[0] * huo_ref.shape[1] if tstride != 1 else SUBLANES
    if tstride == 1:
        huo_ref[0] = u[u.shape[0] - hu_rows:]
    else:
        _st(huo_ref, 0, 512, u[u.shape[0] - hu_rows:])

    qg = qg_ref[...]
    kg = kg_ref[...]
    for h in range(NSA_HEADS):
        c0 = C_NQ + h * hd
        _st(q_ref, h * hd, (h + 1) * hd, _rms(_ld(z_ref, c0, c0 + hd), qg) * (hd ** -0.5))
    for grp in range(4):
        put_rows(grp, _ld(z_ref, C_NKV + grp * hd, C_NKV + (grp + 1) * hd))
    for g in range(NSA_KV_HEADS):
        c0 = C_NKV + 512 + g * hd
        kn = _rms(_ld(z_ref, c0, c0 + hd), kg)
        put_rows(4 + g, kn)
        _st(kvb_ref, g * hd, (g + 1) * hd, kn)
    for g in range(NSA_KV_HEADS):
        c0 = C_NKV + 768 + g * hd
        vs = _ld(z_ref, c0, c0 + hd)
        put_rows(6 + g, vs)
        _st(kvb_ref, 256 + g * hd, 256 + (g + 1) * hd, vs)
    for g in range(NSA_KV_HEADS):
        c0 = C_NKV + 1024 + g * hd
        kn = _rms(_ld(z_ref, c0, c0 + hd), kg)
        _st(win_ref, g * hd, (g + 1) * hd, kn)
        _st(kvb_ref, 512 + g * hd, 512 + (g + 1) * hd, kn)
    vw = _ld(z_ref, C_NKV + 1280, C_NKV + 1536)
    _st(win_ref, 256, 512, vw)
    _st(kvb_ref, 768, 1024, vw)

    zs = _ld(zs_ref, 0, LANES)
    sg = _sigmoid(zs)
    _st(gates_ref, 0, LANES, sg)
    xs = zs + dtb_ref[...]
    softplus = jnp.maximum(xs, 0.0) + jnp.log(1.0 + jnp.exp(-jnp.abs(xs)))
    gdec = -jnp.exp(alog_ref[...]) * softplus
    rows = zs.shape[0]
    if tstride == 1:
        in_chunk = _mod_pow2(lax.broadcasted_iota(jnp.int32, gdec.shape, 0), GDN_CHUNK)
        step = 1
        while step < GDN_CHUNK:
            gdec = gdec + jnp.where(in_chunk >= step, pltpu.roll(gdec, step, 0), 0.0)
            step *= 2
    for h in range(GDN_HEADS):
        _st(ggb_ref, h * hd, (h + 1) * hd, jnp.broadcast_to(gdec[:, S_GA + h:S_GA + h + 1], (rows, hd)))
        _st(ggb_ref, 512 + h * hd, 512 + (h + 1) * hd, jnp.broadcast_to(sg[:, S_GB + h:S_GB + h + 1], (rows, hd)))

    gcw = gcw_ref[...]
    for part in range(3):
        c0 = C_GQKV + part * 512
        x = _ld(z_ref, c0, c0 + 512)
        hq = halo_q[:, part * 512:(part + 1) * 512]
        w = gcw[:, part * 512:(part + 1) * 512]
        c = (w[0:1] * _shift(x, hq, 3, tstride) + w[1:2] * _shift(x, hq, 2, tstride)
             + w[2:3] * _shift(x, hq, 1, tstride) + w[3:4] * x)
        c = c * _sigmoid(c)
        if part == 2:
            _st(gq_ref, 1024, 1536, c)
        else:
            for h in range(GDN_HEADS):
                v = _l2n(c[:, h * hd:(h + 1) * hd])
                if part == 0:
                    v = v * (hd ** -0.5)
                _st(gq_ref, part * 512 + h * hd, part * 512 + (h + 1) * hd, v)


def _prep(z, zs, hist_u, hist_q, lw, *, sample, nseq, seq_len):
    M = z.shape[0]
    wts = [lw["conv_a_w"], lw["gdn_conv_w"], lw["q_norm_g"].reshape(1, -1), lw["k_norm_g"].reshape(1, -1),
           lw["out_norm_a"].reshape(1, -1), lw["alog_v"], lw["dtb_v"]]
    wspecs2 = [pl.BlockSpec(w.shape, lambda i: (0, 0)) for w in wts]
    widths = [(512, BF16), (1024, BF16), (1024, F32), (512, F32), (1024, BF16), (LANES, F32), (1536, F32), (1024, F32)]
    if not sample:
        tm = 256
        tps = seq_len // tm
        in_specs = [pl.BlockSpec((tm, C_END), lambda i: (i, 0)),
                    pl.BlockSpec((SUBLANES, C_END), lambda i: (jnp.maximum(i * (tm // SUBLANES) - 1, 0), 0)),
                    pl.BlockSpec((tm, LANES), lambda i: (i, 0)),
                    pl.BlockSpec((1, SUBLANES, 512), lambda i: (i // tps, 0, 0)),
                    pl.BlockSpec((1, SUBLANES, 1536), lambda i: (i // tps, 0, 0))] + wspecs2
        out_specs = [pl.BlockSpec((tm, w), lambda i: (i, 0)) for w, _ in widths]
        out_specs.append(pl.BlockSpec((1, SUBLANES, 512), lambda i: (i // tps, 0, 0)))
        out_shape = [jax.ShapeDtypeStruct((M, w), d) for w, d in widths]
        out_shape.append(jax.ShapeDtypeStruct((nseq, SUBLANES, 512), F32))
        kvr = 4 * NSA_KV_HEADS
        out_specs.append(pl.BlockSpec((tm * kvr, HEAD_DIM), lambda i: (i, 0)))
        out_shape.append(jax.ShapeDtypeStruct((M * kvr, HEAD_DIM), F32))
        return pl.pallas_call(
            functools.partial(_prep_kernel, tstride=1, tiles_per_seq=tps),
            grid=(M // tm,), in_specs=in_specs, out_specs=out_specs, out_shape=out_shape,
            compiler_params=_cp(("arbitrary",)), name="prep_prompt",
        )(z, z, zs, hist_u, hist_q, *wts)
    T = seq_len
    bt = 64
    z3 = z.reshape(T, nseq, C_END)
    zs3 = zs.reshape(T, nseq, LANES)
    in_specs = [pl.BlockSpec((T, bt, C_END), lambda i: (0, i, 0)),
                pl.BlockSpec((SUBLANES, C_END), lambda i: (0, 0)),
                pl.BlockSpec((T, bt, LANES), lambda i: (0, i, 0)),
                pl.BlockSpec((hist_u.shape[0], bt, 512), lambda i: (0, i, 0)),
                pl.BlockSpec((hist_q.shape[0], bt, 1536), lambda i: (0, i, 0))] + wspecs2
    out_specs = [pl.BlockSpec((T, bt, w), lambda i: (0, i, 0)) for w, _ in widths]
    out_specs.append(pl.BlockSpec((hist_u.shape[0], bt, 512), lambda i: (0, i, 0)))
    out_shape = [jax.ShapeDtypeStruct((T, nseq, w), d) for w, d in widths]
    out_shape.append(jax.ShapeDtypeStruct((hist_u.shape[0], nseq, 512), F32))
    outs = pl.pallas_call(
        functools.partial(_prep_kernel, tstride=bt, tiles_per_seq=1),
        grid=(nseq // bt,), in_specs=in_specs, out_specs=out_specs, out_shape=out_shape,
        compiler_params=_cp(("arbitrary",)), name="prep_sample",
    )(z3, z, zs3, hist_u, hist_q, *wts)
    return [o.reshape(M, o.shape[-1]) for o in outs[:-1]] + [outs[-1]]


def _bias_kernel(thr_ref, rb_ref, o_ref, *, a0, an, qs, ks):
    n = pl.program_id(0)
    h = pl.program_id(1)
    R, C = o_ref.shape[-2:]
    dist = (a0 + an * n + qs * lax.broadcasted_iota(jnp.int32, (R, C), 0)
            - ks * lax.broadcasted_iota(jnp.int32, (R, C), 1))
    b = jnp.full((R, C), rb_ref[h], F32)
    for k in range(1, REL_BUCKETS):
        b = jnp.where(dist >= thr_ref[k], rb_ref[k * NSA_HEADS + h], b)
    o_ref[0, 0] = b


def _bias_table(thr, rb_flat, n, R, C, *, a0, an, qs, ks):
    return pl.pallas_call(
        functools.partial(_bias_kernel, a0=a0, an=an, qs=qs, ks=ks),
        grid=(n, NSA_HEADS),
        in_specs=[pl.BlockSpec(memory_space=pltpu.SMEM), pl.BlockSpec(memory_space=pltpu.SMEM)],
        out_specs=pl.BlockSpec((1, 1, R, C), lambda i, j: (i, j, 0, 0)),
        out_shape=jax.ShapeDtypeStruct((n, NSA_HEADS, R, C), F32),
        compiler_params=_cp(("parallel", "parallel")), name="bias_table",
    )(thr, rb_flat)


def _bucket_thresholds():
    n = jnp.arange(REL_MAX_DIST + 1)
    exact = REL_BUCKETS // 2
    nf = jnp.maximum(n, 1).astype(F32)
    far = exact + (jnp.log(nf / exact) / math.log(REL_MAX_DIST / exact) * (REL_BUCKETS - exact)).astype(jnp.int32)
    bucket = jnp.where(n < exact, n, jnp.minimum(far, REL_BUCKETS - 1))
    return jnp.sum(bucket[None, :] < jnp.arange(REL_BUCKETS)[:, None], axis=1).astype(jnp.int32)


def _gelu_tanh(x):
    return x * (0.5 * (1.0 + jnp.tanh(math.sqrt(2.0 / math.pi) * (x + 0.044715 * (x * x * x)))))


def _compress_math(get_rows, ng, pe, w1_ref, w2):
    half = CMP_BLOCK // 2
    top = jnp.zeros((ng, CMP_HIDDEN), F32)
    bot = jnp.zeros((ng, CMP_HIDDEN), F32)
    for r in range(half):
        s = get_rows(r)
        top = top + jnp.dot((s + pe[r:r + 1]).astype(BF16), w1_ref[r], preferred_element_type=F32)
        bot = bot + jnp.dot((s + pe[r + half:r + half + 1]).astype(BF16), w1_ref[r + half],
                            preferred_element_type=F32)
    h = top + pltpu.roll(bot, ng - 1, 0)
    return jnp.dot(_gelu_tanh(h).astype(BF16), w2, preferred_element_type=F32)


def _compress_kernel(x_ref, pe_ref, w1_ref, w2_ref, kg_ref, o_ref):
    ng = x_ref.shape[1] // CMP_STRIDE
    slot = pl.program_id(1)

    def get_rows(r):
        return x_ref[0, pl.ds(r, ng, stride=CMP_STRIDE), :]

    out = _compress_math(get_rows, ng, pe_ref[0], w1_ref.at[0], w2_ref[0])
    o_ref[0, 0, 0] = jnp.where(slot == 0, _rms(out, kg_ref[...]), out).astype(o_ref.dtype)


def _compress_prompt(rows3, lw):
    B, T, _ = rows3.shape
    ng = T // CMP_STRIDE
    G = NSA_KV_HEADS
    return pl.pallas_call(
        _compress_kernel,
        grid=(B, 2, G),
        in_specs=[pl.BlockSpec((1, T, HEAD_DIM), lambda b, s, g: (b, 0, s * G + g)),
                  pl.BlockSpec((1, CMP_BLOCK, HEAD_DIM), lambda b, s, g: (s, 0, 0)),
                  pl.BlockSpec((1, CMP_BLOCK, HEAD_DIM, CMP_HIDDEN), lambda b, s, g: (s, 0, 0, 0)),
                  pl.BlockSpec((1, CMP_HIDDEN, HEAD_DIM), lambda b, s, g: (s, 0, 0)),
                  pl.BlockSpec((1, HEAD_DIM), lambda b, s, g: (0, 0))],
        out_specs=pl.BlockSpec((1, 1, 1, ng, HEAD_DIM), lambda b, s, g: (b, s, g, 0, 0)),
        out_shape=jax.ShapeDtypeStruct((B, 2, G, ng, HEAD_DIM), BF16),
        compiler_params=_cp(("parallel", "parallel", "parallel")), name="compress_prompt",
    )(rows3, lw["cmp_pe"], lw["cmp_w1"], lw["cmp_w2"], lw["k_norm_g"].reshape(1, -1))


def _masked_softmax(s, mask):
    s = jnp.where(mask, s, NEG_INF)
    m = jnp.max(s, axis=-1, keepdims=True)
    e = jnp.where(mask, jnp.exp(s - m), 0.0)
    z = jnp.sum(e, axis=-1, keepdims=True)
    return e / jnp.where(z > 0, z, 1.0)


def _overlap_matrix(ncp):
    c = CMP_STRIDE * lax.broadcasted_iota(jnp.int32, (ncp, LANES), 0)
    s = SLC_BLOCK * lax.broadcasted_iota(jnp.int32, (ncp, LANES), 1)
    return jnp.where(c < s + SLC_BLOCK, jnp.where(c + CMP_BLOCK > s, 1.0, 0.0), 0.0).astype(F32)


def _select_mask(imp, cur, n_slc):
    R = imp.shape[0]
    blk = lax.broadcasted_iota(jnp.int32, (R, LANES), 1)
    for forced_blk in (0, cur, cur - 1):
        imp = jnp.where(blk == forced_blk, FORCE_SCORE, imp)
    imp = jnp.where(blk <= cur, imp, -1.0)
    imp = jnp.where(blk < n_slc, imp, -2.0)
    cnt = jnp.zeros((R, LANES), F32)
    for k in range(n_slc):
        col = imp[:, k:k + 1]
        cnt = cnt + jnp.where(col > imp, 1.0, jnp.where(col == imp, jnp.where(blk > k, 1.0, 0.0), 0.0))
    return jnp.where(cnt < float(min(N_SELECT, n_slc)), jnp.where(blk < n_slc, 1.0, 0.0), 0.0)


def _expand_matrix(nkeys):
    s = lax.broadcasted_iota(jnp.int32, (LANES, nkeys), 0)
    j = lax.broadcasted_iota(jnp.int32, (LANES, nkeys), 1)
    lo = s * SLC_BLOCK
    return jnp.where(j >= lo, jnp.where(j < lo + SLC_BLOCK, 1.0, 0.0), 0.0).astype(BF16)


KEY_TILE = 512


def _select_mask_t(imp_t, cur, n_slc):
    blk = lax.broadcasted_iota(jnp.int32, imp_t.shape, 0)
    for forced_blk in (0, cur, cur - 1):
        imp_t = jnp.where(blk == forced_blk, FORCE_SCORE, imp_t)
    imp_t = jnp.where(blk <= cur, imp_t, -1.0)
    cnt = jnp.zeros(imp_t.shape, F32)
    for k in range(n_slc):
        row = imp_t[k:k + 1, :]
        cnt = cnt + jnp.where(row > imp_t, 1.0, jnp.where(row == imp_t, jnp.where(blk > k, 1.0, 0.0), 0.0))
    return jnp.where(cnt < float(min(N_SELECT, n_slc)), 1.0, 0.0)


def _nsa_prompt_kernel(q_ref, kc_ref, vc_ref, ks_ref, vs_ref, kw_ref, vw_ref, bt_ref, bc_ref, gates_ref, ex_ref,
                       o_ref, selk_ref, m_ref, acc_ref, *, T):
    qb = pl.program_id(2)
    QB = Q_BLOCK
    R = NSA_REP
    RQ = R * QB
    hd = HEAD_DIM
    nc = (T - CMP_BLOCK) // CMP_STRIDE + 1
    ncp = kc_ref.shape[-2]
    n_slc = ex_ref.shape[0]
    q = jnp.concatenate([q_ref[:, r * hd:(r + 1) * hd] for r in range(R)], axis=0)

    s = _dot_t(q, kc_ref[0, 0, 0]) + bc_ref[0].reshape(RQ, ncp)
    t_row = lax.broadcasted_iota(jnp.int32, (R, QB, ncp), 1).reshape(RQ, ncp)
    c_col = lax.broadcasted_iota(jnp.int32, (RQ, ncp), 1)
    dist = qb * QB + t_row - (CMP_STRIDE * c_col + CMP_BLOCK - 1)
    p = _masked_softmax(s, jnp.where(c_col < nc, dist, -1) >= 0)
    o_cmp = jnp.dot(p.astype(BF16), vc_ref[0, 0, 0], preferred_element_type=F32)

    psum = p[0:QB]
    for r in range(1, R):
        psum = psum + p[r * QB:(r + 1) * QB]
    s_lo = SLC_BLOCK * lax.broadcasted_iota(jnp.int32, (n_slc, ncp), 0)
    c_lo = CMP_STRIDE * lax.broadcasted_iota(jnp.int32, (n_slc, ncp), 1)
    overlap_t = jnp.where(c_lo < s_lo + SLC_BLOCK, jnp.where(c_lo + CMP_BLOCK > s_lo, 1.0, 0.0), 0.0).astype(F32)
    imp_t = _dot_t(overlap_t, psum, HIGHEST)
    cur = _div_pow2(qb * QB + lax.broadcasted_iota(jnp.int32, (1, QB), 1), SLC_BLOCK)
    sel_t = _select_mask_t(imp_t, cur, n_slc)
    selk_ref[...] = lax.dot_general(sel_t.astype(BF16), ex_ref[...], (((0,), (0,)), ((), ())),
                                    preferred_element_type=F32)

    def bias_row(first_blk, n):
        return jnp.concatenate([bt_ref[jnp.maximum(qb - (first_blk + j), 0)].reshape(RQ, QB) for j in range(n)],
                               axis=-1)

    def rel(first_key, nk):
        return (qb * QB - first_key + lax.broadcasted_iota(jnp.int32, (QB, nk), 0)
                - lax.broadcasted_iota(jnp.int32, (QB, nk), 1))

    def with_ones(v):
        return jnp.concatenate([v, jnp.ones(v.shape, BF16)], axis=-1)

    nkb = KEY_TILE // QB
    m_ref[...] = jnp.full(m_ref.shape, NEG_INF, F32)
    acc_ref[...] = jnp.zeros(acc_ref.shape, F32)

    def body(kt, carry):
        off = pl.multiple_of(kt * KEY_TILE, KEY_TILE)
        sc = _dot_t(q, ks_ref[0, pl.ds(off, KEY_TILE), :]) + bias_row(kt * nkb, nkb)
        d = jnp.where(selk_ref[:, pl.ds(off, KEY_TILE)] > 0.5, rel(off, KEY_TILE), -1)
        mask = jnp.concatenate([d] * R, axis=0) >= 0
        sc = jnp.where(mask, sc, NEG_INF)
        m_old = m_ref[...]
        m_new = jnp.maximum(m_old, jnp.max(sc, axis=-1, keepdims=True))
        pe = jnp.where(mask, jnp.exp(sc - m_new), 0.0).astype(BF16)
        acc_ref[...] = jnp.exp(m_old - m_new) * acc_ref[...] + jnp.dot(
            pe, with_ones(vs_ref[0, pl.ds(off, KEY_TILE), :]), preferred_element_type=F32)
        m_ref[...] = m_new
        return carry

    lax.fori_loop(0, qb // nkb + 1, body, 0)
    acc = acc_ref[...]
    den = acc[:, hd:]
    o_slc = acc[:, :hd] / jnp.where(den > 0, den, 1.0)

    wk = min(WINDOW + QB, T)
    w0 = jnp.maximum(qb - WINDOW // QB, 0)
    offw = pl.multiple_of(w0 * QB, QB)
    sc = _dot_t(q, kw_ref[0, pl.ds(offw, wk), :]) + bias_row(w0, wk // QB)
    d = rel(offw, wk)
    d = jnp.where(d <= WINDOW, d, -1)
    mask = jnp.concatenate([d] * R, axis=0) >= 0
    sc = jnp.where(mask, sc, NEG_INF)
    e = jnp.where(mask, jnp.exp(sc - jnp.max(sc, axis=-1, keepdims=True)), 0.0).astype(BF16)
    acc = jnp.dot(e, with_ones(vw_ref[0, pl.ds(offw, wk), :]), preferred_element_type=F32)
    den = acc[:, hd:]
    o_win = acc[:, :hd] / jnp.where(den > 0, den, 1.0)

    gt = gates_ref[0]
    for r in range(R):
        sl = slice(r * QB, (r + 1) * QB)
        o = (gt[:, r:r + 1] * o_cmp[sl] + gt[:, R + r:R + r + 1] * o_slc[sl]
             + gt[:, 2 * R + r:2 * R + r + 1] * o_win[sl])
        o_ref[:, r * hd:(r + 1) * hd] = o


def _nsa_prompt(q, cmp_kv, kvb3, gates_g, bias_tiles, bias_cmp, *, B, T):
    G = NSA_KV_HEADS
    R = NSA_REP
    nqb = T // Q_BLOCK
    ncp = T // CMP_STRIDE
    n_slc = -(-T // SLC_BLOCK)
    expand = (jnp.arange(T)[None, :] // SLC_BLOCK == jnp.arange(n_slc)[:, None]).astype(BF16)
    kv_spec = lambda col: pl.BlockSpec((1, T, HEAD_DIM), lambda b, g, i, col=col: (b, 0, col + g))
    return pl.pallas_call(
        functools.partial(_nsa_prompt_kernel, T=T),
        grid=(B, G, nqb),
        in_specs=[pl.BlockSpec((Q_BLOCK, R * HEAD_DIM), lambda b, g, i: (b * nqb + i, g)),
                  pl.BlockSpec((1, 1, 1, ncp, HEAD_DIM), lambda b, g, i: (b, 0, g, 0, 0)),
                  pl.BlockSpec((1, 1, 1, ncp, HEAD_DIM), lambda b, g, i: (b, 1, g, 0, 0)),
                  kv_spec(0), kv_spec(2), kv_spec(4), kv_spec(6),
                  pl.BlockSpec((nqb, R, Q_BLOCK, Q_BLOCK), lambda b, g, i: (0, g, 0, 0)),
                  pl.BlockSpec((1, R, Q_BLOCK, ncp), lambda b, g, i: (i, g, 0, 0)),
                  pl.BlockSpec((1, Q_BLOCK, LANES), lambda b, g, i: (g, b * nqb + i, 0)),
                  pl.BlockSpec((n_slc, T), lambda b, g, i: (0, 0))],
        out_specs=pl.BlockSpec((Q_BLOCK, R * HEAD_DIM), lambda b, g, i: (b * nqb + i, g)),
        out_shape=jax.ShapeDtypeStruct((B * T, G * R * HEAD_DIM), F32),
        scratch_shapes=[pltpu.VMEM((Q_BLOCK, T), F32),
                        pltpu.VMEM((R * Q_BLOCK, 1), F32),
                        pltpu.VMEM((R * Q_BLOCK, 2 * HEAD_DIM), F32)],
        compiler_params=_cp(("parallel", "parallel", "arbitrary")), name="nsa_prompt",
    )(q, cmp_kv, cmp_kv, kvb3, kvb3, kvb3, kvb3, bias_tiles, bias_cmp, gates_g, expand)


def _nsa_sample_kernel(pt_ref, *refs, npages, tq):
    pages = refs[:npages]
    (win_ref, wnew_ref, q_ref, kvn_ref, gates_ref, pe_ref, w1_ref, w2_ref, kg_ref, bc_ref, bs_ref, bw_ref,
     o_ref, wout_ref) = refs[npages:]
    G, R, hd = NSA_KV_HEADS, NSA_REP, HEAD_DIM
    KVR = 4 * G
    WR = 2 * G
    page = pages[0].shape[2] // KVR
    past = npages * page
    L = past + tq
    nc = (L - CMP_BLOCK) // CMP_STRIDE + 1
    ng = past // CMP_STRIDE
    n_slc = -(-L // SLC_BLOCK)
    rows = R * tq
    wb = win_ref.shape[2] // WR
    wout_ref[0, 0, 0:(wb - tq) * WR, :] = win_ref[0, 0, tq * WR:wb * WR, :]
    wout_ref[0, 0, (wb - tq) * WR:wb * WR, :] = wnew_ref[0]
    pad_new = lambda v: jnp.concatenate([v, jnp.zeros((LANES - v.shape[0], v.shape[1]), v.dtype)], axis=0)
    t_of_row = lax.broadcasted_iota(jnp.int32, (R, tq, 1), 1).reshape(rows, 1)
    pos = past + t_of_row
    ra = lax.broadcasted_iota(jnp.int32, (rows, rows), 0)
    rb = lax.broadcasted_iota(jnp.int32, (rows, rows), 1)
    same_t = jnp.where(_mod_pow2(ra, tq) == _mod_pow2(rb, tq), 1.0, 0.0).astype(F32)
    kg = kg_ref[...]
    kvn = kvn_ref[0]

    for g in range(G):
        q = q_ref[0, g]
        gt = gates_ref[0, g]

        def compress(slot):
            col = slot * G + g

            def get_rows(r):
                return jnp.concatenate(
                    [pg[0, 0, pl.ds(r * KVR + col, page // CMP_STRIDE, stride=CMP_STRIDE * KVR), :] for pg in pages],
                    axis=0)

            return _compress_math(get_rows, ng, pe_ref[slot], w1_ref.at[slot], w2_ref[slot])

        kc = _rms(compress(0), kg).astype(BF16)
        vc = compress(1).astype(BF16)

        c_col = lax.broadcasted_iota(jnp.int32, (rows, ng), 1)
        dist = pos - (CMP_STRIDE * c_col + CMP_BLOCK - 1)
        p = _masked_softmax(_dot_t(q, kc) + bc_ref[g], jnp.where(c_col < nc, dist, -1) >= 0)
        o_cmp = jnp.dot(p.astype(BF16), vc, preferred_element_type=F32)

        psum = jnp.dot(same_t, p, precision=HIGHEST, preferred_element_type=F32)
        imp = jnp.dot(psum, _overlap_matrix(ng), precision=HIGHEST, preferred_element_type=F32)
        sel = _select_mask(imp, _div_pow2(pos, SLC_BLOCK), n_slc)
        nk = past + LANES
        selk = jnp.dot(sel.astype(BF16), _expand_matrix(nk), preferred_element_type=F32)

        k_new = pad_new(kvn[:, g * hd:(g + 1) * hd])
        v_new = pad_new(kvn[:, 256 + g * hd:256 + (g + 1) * hd])
        page_rows = lambda pg, slot: pg[0, 0, pl.ds(slot * G + g, page, stride=KVR), :].astype(BF16)
        sc = jnp.concatenate([_dot_t(q, page_rows(pg, 2)) for pg in pages]
                             + [_dot_t(q, k_new)], axis=-1) + bs_ref[g]
        j = lax.broadcasted_iota(jnp.int32, (rows, nk), 1)
        p = _masked_softmax(sc, jnp.where(selk > 0.5, pos - j, -1) >= 0).astype(BF16)
        o_slc = jnp.dot(p[:, past:], v_new, preferred_element_type=F32)
        for i, pg in enumerate(pages):
            o_slc = o_slc + jnp.dot(p[:, i * page:(i + 1) * page], page_rows(pg, 3), preferred_element_type=F32)

        kw_new = pad_new(kvn[:, 512 + g * hd:512 + (g + 1) * hd])
        vw_new = pad_new(kvn[:, 768 + g * hd:768 + (g + 1) * hd])
        kw = win_ref[0, 0, pl.ds(g, wb, stride=WR), :].astype(BF16)
        vw = win_ref[0, 0, pl.ds(G + g, wb, stride=WR), :].astype(BF16)
        sc = jnp.concatenate([_dot_t(q, kw), _dot_t(q, kw_new)], axis=-1) + bw_ref[g]
        j = lax.broadcasted_iota(jnp.int32, (rows, wb + LANES), 1)
        d = pos - (past - wb + j)
        p = _masked_softmax(sc, jnp.where(d <= WINDOW, d, -1) >= 0).astype(BF16)
        o_win = (jnp.dot(p[:, :wb], vw, preferred_element_type=F32)
                 + jnp.dot(p[:, wb:], vw_new, preferred_element_type=F32))

        o_ref[0, g] = gt[:, 0:1] * o_cmp + gt[:, 1:2] * o_slc + gt[:, 2:3] * o_win


def _nsa_sample(l, page_table, cache_lin, win_lin, win_new, q_s, kvn_s, gates_s, lw, bias_c, bias_s, bias_w, *, tq):
    B, npages = page_table.shape
    G = NSA_KV_HEADS
    rows = NSA_REP * tq
    page_specs = [pl.BlockSpec((1, 1) + cache_lin.shape[2:], lambda b, pt, p=p: (l, pt[b, p], 0, 0))
                  for p in range(npages)]
    full = lambda a: pl.BlockSpec(a.shape, lambda b, pt, n=a.ndim: (0,) * n)
    wts = [lw["cmp_pe"], lw["cmp_w1"], lw["cmp_w2"], lw["k_norm_g"].reshape(1, -1), bias_c, bias_s, bias_w]
    grid_spec = pltpu.PrefetchScalarGridSpec(
        num_scalar_prefetch=1, grid=(B,),
        in_specs=page_specs + [
            pl.BlockSpec((1, 1) + win_lin.shape[2:], lambda b, pt: (l, b, 0, 0)),
            pl.BlockSpec((1,) + win_new.shape[1:], lambda b, pt: (b, 0, 0)),
            pl.BlockSpec((1, G, rows, HEAD_DIM), lambda b, pt: (b, 0, 0, 0)),
            pl.BlockSpec((1, SUBLANES, kvn_s.shape[2]), lambda b, pt: (b, 0, 0)),
            pl.BlockSpec((1, G, rows, LANES), lambda b, pt: (b, 0, 0, 0))] + [full(w) for w in wts],
        out_specs=[pl.BlockSpec((1, G, rows, HEAD_DIM), lambda b, pt: (b, 0, 0, 0)),
                   pl.BlockSpec((1, 1) + win_lin.shape[2:], lambda b, pt: (0, b, 0, 0))])
    return pl.pallas_call(
        functools.partial(_nsa_sample_kernel, npages=npages, tq=tq),
        grid_spec=grid_spec,
        out_shape=[jax.ShapeDtypeStruct((B, G, rows, HEAD_DIM), F32),
                   jax.ShapeDtypeStruct((1,) + win_lin.shape[1:], F32)],
        compiler_params=_cp(("arbitrary",)), name="nsa_sample",
    )(page_table, *([cache_lin] * npages), win_lin, win_new, q_s, kvn_s, gates_s, *wts)


def _dot3(a, b):
    ah = a.astype(BF16)
    bh = b.astype(BF16)
    al = (a - ah.astype(F32)).astype(BF16)
    bl = (b - bh.astype(F32)).astype(BF16)
    d = functools.partial(jnp.dot, preferred_element_type=F32)
    return d(ah, bh) + (d(al, bh) + d(ah, bl))


def _gdn_chunk_kernel(gq_ref, ggb_ref, n_ref, k2_ref, qe_ref, o0_ref, eg_ref):
    B, C = gq_ref.shape[:2]
    hd = HEAD_DIM
    W = GDN_HEADS * hd
    ri = lax.broadcasted_iota(jnp.int32, (C, C), 0)
    ci = lax.broadcasted_iota(jnp.int32, (C, C), 1)
    tri_incl = ri >= ci
    tri_strict = ri > ci
    eye = jnp.where(ri == ci, 1.0, 0.0).astype(F32)
    ones = jnp.ones((C, C), F32)

    for b in range(B):
        for h in range(GDN_HEADS):
            q = gq_ref[b, :, h * hd:(h + 1) * hd]
            k = gq_ref[b, :, W + h * hd:W + (h + 1) * hd]
            v = gq_ref[b, :, 2 * W + h * hd:2 * W + (h + 1) * hd]
            gc = ggb_ref[b, :, h * hd:(h + 1) * hd]
            beta = ggb_ref[b, :, W + h * hd:W + (h + 1) * hd]
            gcol = gc[:, :C]
            grow = jnp.dot(ones, eye * gcol, precision=HIGHEST, preferred_element_type=F32)
            decay = jnp.exp(jnp.where(tri_incl, gcol - grow, -jnp.inf))
            kb = k * beta
            k16 = k.astype(BF16)
            low = jnp.where(tri_strict, _dot_t(kb.astype(BF16), k16) * decay, 0.0)
            qk = jnp.where(tri_incl, _dot_t(q.astype(BF16), k16) * decay, 0.0)
            tinv = eye - low
            pw = low
            span = 2
            while span < C:
                pw = _dot3(pw, pw)
                tinv = _dot3(tinv, eye + pw)
                span *= 2
            egc = jnp.exp(gc)
            uw = _dot3(tinv, jnp.concatenate([v * beta, kb * egc], axis=-1)).astype(BF16)
            g_last = gc[C - 1:C]
            kd = (k * jnp.exp(g_last - gc)).astype(BF16)
            nk = lax.dot_general(kd, uw, (((0,), (0,)), ((), ())), preferred_element_type=F32)
            ow = jnp.dot(qk.astype(BF16), uw, preferred_element_type=F32)
            n_ref[b, h, 0] = nk[:, :hd]
            k2_ref[b, h, 0] = nk[:, hd:].astype(BF16)
            o0_ref[b, h, 0] = ow[:, :hd]
            qe_ref[b, h, 0] = (q * egc - ow[:, hd:]).astype(BF16)
            eg_ref[b, h, 0] = jnp.broadcast_to(jnp.exp(g_last), (SUBLANES, hd))


def _gdn_scan_kernel(n_ref, k2_ref, qe_ref, o0_ref, eg_ref, gz_ref, s0_ref, ng_ref, y_ref, so_ref, s_ref):
    n = pl.program_id(0)
    B = gz_ref.shape[0]
    hd = HEAD_DIM

    @pl.when(n == 0)
    def _():
        s_ref[...] = s0_ref[...]

    ng = ng_ref[...]
    for b in range(B):
        for h in range(GDN_HEADS):
            S = s_ref[b, h]
            s16 = S.astype(BF16)
            o = jnp.dot(qe_ref[b, h, 0], s16, preferred_element_type=F32) + o0_ref[b, h, 0]
            s_ref[b, h] = (eg_ref[b, h, 0][0:1] * S + n_ref[b, h, 0]
                           - jnp.dot(k2_ref[b, h, 0], s16, preferred_element_type=F32))
            z = gz_ref[b, :, h * hd:(h + 1) * hd]
            y_ref[b, :, h * hd:(h + 1) * hd] = (_rms(o, ng) * (z * _sigmoid(z))).astype(y_ref.dtype)

    @pl.when(n == pl.num_programs(0) - 1)
    def _():
        so_ref[...] = s_ref[...]


def _gdn_prompt(gq3, ggb3, z3, s0, norm_g):
    B, T, _ = gq3.shape
    C = GDN_CHUNK
    H = GDN_HEADS
    hd = HEAD_DIM
    W = H * hd
    nchunk = T // C
    item = lambda rows: pl.BlockSpec((B, H, 1, rows, hd), lambda n: (0, 0, n, 0, 0))
    shape = lambda rows, dt: jax.ShapeDtypeStruct((B, H, nchunk, rows, hd), dt)
    nn, k2, qe, o0, eg = pl.pallas_call(
        _gdn_chunk_kernel,
        grid=(nchunk,),
        in_specs=[pl.BlockSpec((B, C, 3 * W), lambda n: (0, n, 0)),
                  pl.BlockSpec((B, C, 2 * W), lambda n: (0, n, 0))],
        out_specs=[item(hd), item(hd), item(C), item(C), item(SUBLANES)],
        out_shape=[shape(hd, F32), shape(hd, BF16), shape(C, BF16), shape(C, F32), shape(SUBLANES, F32)],
        compiler_params=_cp(("parallel",)), name="gdn_chunk",
    )(gq3, ggb3)
    return pl.pallas_call(
        _gdn_scan_kernel,
        grid=(nchunk,),
        in_specs=[item(hd), item(hd), item(C), item(C), item(SUBLANES),
                  pl.BlockSpec((B, C, W), lambda n: (0, n, C_GZ // W)),
                  pl.BlockSpec(s0.shape, lambda n: (0, 0, 0, 0)),
                  pl.BlockSpec((1, hd), lambda n: (0, 0))],
        out_specs=[pl.BlockSpec((B, C, W), lambda n: (0, n, 0)),
                   pl.BlockSpec(s0.shape, lambda n: (0, 0, 0, 0))],
        out_shape=[jax.ShapeDtypeStruct((B, T, W), BF16), jax.ShapeDtypeStruct(s0.shape, F32)],
        scratch_shapes=[pltpu.VMEM(s0.shape, F32)],
        compiler_params=_cp(("arbitrary",)), name="gdn_scan",
    )(nn, k2, qe, o0, eg, z3, s0, norm_g.reshape(1, -1))


def _gdn_sample_kernel(kq_ref, v_ref, gb_ref, gz_ref, s0_ref, ng_ref, y_ref, so_ref, *, tq):
    bt = kq_ref.shape[0]
    hd = HEAD_DIM
    ri = lax.broadcasted_iota(jnp.int32, (hd, hd), 0)
    ci = lax.broadcasted_iota(jnp.int32, (hd, hd), 1)
    eye = jnp.where(ri == ci, 1.0, 0.0).astype(F32)
    ng = ng_ref[...]

    def body(bi, carry):
        for h in range(GDN_HEADS):
            kq = kq_ref[bi, h]
            cols = _dot_t(eye, kq, HIGHEST)
            gb = gb_ref[bi, h]
            v = v_ref[bi, h]
            S = s0_ref[bi, h]
            outs = []
            for t in range(tq):
                a = jnp.exp(gb[t:t + 1])
                kc = cols[:, t:t + 1]
                qc = cols[:, tq + t:tq + t + 1]
                Sa = S * a
                stk = jnp.sum(Sa * kc, axis=0, keepdims=True)
                vn = gb[tq + t:tq + t + 1] * (v[t:t + 1] - stk)
                S = Sa + kc * vn
                outs.append(jnp.sum(S * qc, axis=0, keepdims=True))
            so_ref[bi, h] = S
            o = jnp.concatenate(outs + [jnp.zeros((SUBLANES - tq, hd), F32)], axis=0)
            z = gz_ref[bi, h]
            y_ref[bi, h] = _rms(o, ng) * (z * _sigmoid(z))
        return carry

    lax.fori_loop(0, bt, body, 0)


def _gdn_sample(kq_s, v_s, gb_s, gz_s, s0, norm_g, *, tq):
    B = kq_s.shape[0]
    bt = 8
    blk = lambda a: pl.BlockSpec((bt,) + a.shape[1:], lambda i: (i, 0, 0, 0))
    return pl.pallas_call(
        functools.partial(_gdn_sample_kernel, tq=tq),
        grid=(B // bt,),
        in_specs=[blk(kq_s), blk(v_s), blk(gb_s), blk(gz_s), blk(s0), pl.BlockSpec((1, HEAD_DIM), lambda i: (0, 0))],
        out_specs=[blk(v_s), blk(s0)],
        out_shape=[jax.ShapeDtypeStruct(v_s.shape, F32), jax.ShapeDtypeStruct(s0.shape, F32)],
        compiler_params=_cp(("parallel",)), name="gdn_sample",
    )(kq_s, v_s, gb_s, gz_s, s0, norm_g.reshape(1, -1))


def _mix_out_kernel(ya_ref, yb_ref, yc_ref, x_ref, onb_ref, w_ref, o_ref, mix_ref):
    @pl.when(pl.program_id(1) == 0)
    def _():
        mix_ref[:, 0:512] = ya_ref[...]
        mix_ref[:, 512:1536] = _rms(yb_ref[...], onb_ref[...]).astype(BF16)
        mix_ref[:, 1536:2048] = yc_ref[...]

    o_ref[...] = x_ref[...] + jnp.dot(mix_ref[...], w_ref[...], preferred_element_type=F32)


def _mix_out(ya, yb, yc, x, onb, w, *, tm=512, tn=512):
    M, N = x.shape
    tm = min(tm, M)
    K = w.shape[0]
    return pl.pallas_call(
        _mix_out_kernel,
        grid=(M // tm, N // tn),
        in_specs=[pl.BlockSpec((tm, 512), lambda i, j: (i, 0)),
                  pl.BlockSpec((tm, 1024), lambda i, j: (i, 0)),
                  pl.BlockSpec((tm, 512), lambda i, j: (i, 0)),
                  pl.BlockSpec((tm, tn), lambda i, j: (i, j)),
                  pl.BlockSpec((1, 1024), lambda i, j: (0, 0)),
                  pl.BlockSpec((K, tn), lambda i, j: (0, j))],
        out_specs=pl.BlockSpec((tm, tn), lambda i, j: (i, j)),
        out_shape=jax.ShapeDtypeStruct((M, N), F32),
        scratch_shapes=[pltpu.VMEM((tm, K), BF16)],
        compiler_params=_cp(("parallel", "arbitrary")), name="mix_out",
    )(ya, yb, yc, x, onb.reshape(1, -1), w)


def _ffn_mid_kernel(a_ref, v_ref, ap_ref, vp_ref, ha_ref, hv_ref, wa_ref, wv_ref, o_ref, *, tstride, tiles_per_seq):
    tc = wa_ref.shape[1]

    def conv(x_ref, prev_ref, hist_ref, w_ref):
        x = _ld(x_ref, 0, tc)
        if tstride == 1:
            start = (pl.program_id(0) % tiles_per_seq) == 0
            halo = jnp.where(start, hist_ref[0], prev_ref[...])
        else:
            halo = _ld(hist_ref, 0, tc)
        w = w_ref[...]
        return w[0:1] * _shift(x, halo, 2, tstride) + w[1:2] * _shift(x, halo, 1, tstride) + w[2:3] * x

    a = conv(a_ref, ap_ref, ha_ref, wa_ref)
    v = conv(v_ref, vp_ref, hv_ref, wv_ref)
    _st(o_ref, 0, tc, a * _sigmoid(a) * v)


def _ffn_mid(up, hist, w, *, sample, nseq, seq_len, tc=512):
    M, F2 = up.shape
    F = F2 // 2
    nj = F // tc
    wspec_a = pl.BlockSpec((3, tc), lambda i, j: (0, j))
    wspec_v = pl.BlockSpec((3, tc), lambda i, j: (0, j + nj))
    if not sample:
        tm = 512
        tps = seq_len // tm
        prev = lambda off: pl.BlockSpec(
            (SUBLANES, tc), lambda i, j, off=off: (jnp.maximum(i * (tm // SUBLANES) - 1, 0), j + off))
        hspec = lambda off: pl.BlockSpec((1, SUBLANES, tc), lambda i, j, off=off: (i // tps, 0, j + off))
        return pl.pallas_call(
            functools.partial(_ffn_mid_kernel, tstride=1, tiles_per_seq=tps),
            grid=(M // tm, nj),
            in_specs=[pl.BlockSpec((tm, tc), lambda i, j: (i, j)), pl.BlockSpec((tm, tc), lambda i, j: (i, j + nj)),
                      prev(0), prev(nj), hspec(0), hspec(nj), wspec_a, wspec_v],
            out_specs=pl.BlockSpec((tm, tc), lambda i, j: (i, j)),
            out_shape=jax.ShapeDtypeStruct((M, F), BF16),
            compiler_params=_cp(("parallel", "parallel")), name="ffn_mid_prompt",
        )(up, up, up, up, hist, hist, w, w)
    T = seq_len
    up3 = up.reshape(T, nseq, F2)
    hk = hist.shape[0]
    out = pl.pallas_call(
        functools.partial(_ffn_mid_kernel, tstride=nseq, tiles_per_seq=1),
        grid=(1, nj),
        in_specs=[pl.BlockSpec((T, nseq, tc), lambda i, j: (0, 0, j)),
                  pl.BlockSpec((T, nseq, tc), lambda i, j: (0, 0, j + nj)),
                  pl.BlockSpec((SUBLANES, tc), lambda i, j: (0, j)), pl.BlockSpec((SUBLANES, tc), lambda i, j: (0, j + nj)),
                  pl.BlockSpec((hk, nseq, tc), lambda i, j: (0, 0, j)),
                  pl.BlockSpec((hk, nseq, tc), lambda i, j: (0, 0, j + nj)), wspec_a, wspec_v],
        out_specs=pl.BlockSpec((T, nseq, tc), lambda i, j: (0, 0, j)),
        out_shape=jax.ShapeDtypeStruct((T, nseq, F), BF16),
        compiler_params=_cp(("parallel", "parallel")), name="ffn_mid_sample",
    )(up3, up3, up, up, hist, hist, w, w)
    return out.reshape(M, F)


def _matmul_res_kernel(a_ref, w_ref, r_ref, o_ref):
    o_ref[...] = r_ref[...] + jnp.dot(a_ref[...], w_ref[...], preferred_element_type=F32)


def _matmul_res(a, w, res, *, tm=512, tn=512):
    M, K = a.shape
    N = w.shape[1]
    tm = min(tm, M)
    return pl.pallas_call(
        _matmul_res_kernel,
        grid=(M // tm, N // tn),
        in_specs=[pl.BlockSpec((tm, K), lambda i, j: (i, 0)),
                  pl.BlockSpec((K, tn), lambda i, j: (0, j)),
                  pl.BlockSpec((tm, tn), lambda i, j: (i, j))],
        out_specs=pl.BlockSpec((tm, tn), lambda i, j: (i, j)),
        out_shape=jax.ShapeDtypeStruct((M, N), F32),
        compiler_params=_cp(("parallel", "parallel")), name="matmul_res",
    )(a, w, res)


def _layer_weights(l, p):
    w_in = p["w_in"][l]
    w_main = jnp.concatenate([w_in[:, :4096], w_in[:, 4120:5656], w_in[:, 5664:6176]], axis=1).astype(BF16)
    w_small = jnp.concatenate([w_in[:, 4096:4120], w_in[:, 5656:5664],
                               jnp.zeros((w_in.shape[0], LANES - 32), w_in.dtype)], axis=1).astype(BF16)
    lane_vec = lambda v: jnp.zeros((1, LANES), F32).at[0, S_GA:S_GA + GDN_HEADS].set(v)
    return {
        "norm_mix_g": p["norm_mix_g"][l], "w_main": w_main, "w_small": w_small,
        "conv_a_w": p["conv_a_w"][l], "q_norm_g": p["q_norm_g"][l], "k_norm_g": p["k_norm_g"][l],
        "cmp_pe": p["cmp_pe"][l],
        "cmp_w1": p["cmp_w1"][l].reshape(2, CMP_BLOCK, HEAD_DIM, CMP_HIDDEN).astype(BF16),
        "cmp_w2": p["cmp_w2"][l].astype(BF16),
        "gdn_conv_w": p["gdn_conv_w"][l], "alog_v": lane_vec(p["gdn_a_log"][l]), "dtb_v": lane_vec(p["gdn_dt_bias"][l]),
        "gdn_norm_g": p["gdn_norm_g"][l], "out_norm_a": p["out_norm_a"][l], "out_norm_b": p["out_norm_b"][l],
        "w_out": p["w_out"][l].astype(BF16), "norm_ffn_g": p["norm_ffn_g"][l],
        "ffn_up": p["ffn_up"][l].astype(BF16), "ffn_conv_w": p["ffn_conv_w"][l],
        "ffn_down": p["ffn_down"][l].astype(BF16),
    }


HALO_ROWS = 16


def _ffn_up_kernel(x_ref, xp_ref, g_ref, wa_ref, wv_ref, ha_ref, hv_ref, ca_ref, cv_ref,
                   o_ref, hoa_ref, hov_ref, xn_ref, *, tstride, tiles_per_seq):
    tn = wa_ref.shape[1]
    prompt = tstride == 1

    @pl.when(pl.program_id(1) == 0)
    def _():
        g = g_ref[...]
        if prompt:
            xn_ref[0:HALO_ROWS] = _rms(xp_ref[...], g).astype(BF16)
            xn_ref[HALO_ROWS:] = _rms(x_ref[...], g).astype(BF16)
        else:
            xn_ref[...] = _rms(x_ref[...], g).astype(BF16)

    xn = xn_ref[...]

    def part(w_ref, hist_ref, cw_ref, ho_ref):
        up = jnp.dot(xn, w_ref[...], preferred_element_type=F32)
        if prompt:
            start = (pl.program_id(0) % tiles_per_seq) == 0
            halo = jnp.where(start, hist_ref[0], up[HALO_ROWS - SUBLANES:HALO_ROWS])
            x = up[HALO_ROWS:]
            ho_ref[0] = x[x.shape[0] - SUBLANES:]
        else:
            halo = _ld(hist_ref, 0, tn)
            x = up
            _st(ho_ref, 0, tn, x[x.shape[0] - halo.shape[0]:])
        w = cw_ref[...]
        return w[0:1] * _shift(x, halo, 2, tstride) + w[1:2] * _shift(x, halo, 1, tstride) + w[2:3] * x

    a = part(wa_ref, ha_ref, ca_ref, hoa_ref)
    v = part(wv_ref, hv_ref, cv_ref, hov_ref)
    o_ref[...] = (a * _sigmoid(a) * v).astype(o_ref.dtype)


def _ffn_up(h, g, w_up, conv_w, hist, *, sample, nseq, seq_len, tm=512, tn=512):
    M, K = h.shape
    F = w_up.shape[1] // 2
    nj = F // tn
    common = [pl.BlockSpec((1, K), lambda i, j: (0, 0)),
              pl.BlockSpec((K, tn), lambda i, j: (0, j)),
              pl.BlockSpec((K, tn), lambda i, j: (0, j + nj))]
    cw = [pl.BlockSpec((3, tn), lambda i, j: (0, j)), pl.BlockSpec((3, tn), lambda i, j: (0, j + nj))]
    if not sample:
        tps = seq_len // tm
        hspec = lambda off: pl.BlockSpec((1, SUBLANES, tn), lambda i, j, off=off: (i // tps, 0, j + off))
        hout = pl.BlockSpec((1, SUBLANES, tn), lambda i, j: (i, 0, j))
        act, tail_a, tail_v = pl.pallas_call(
            functools.partial(_ffn_up_kernel, tstride=1, tiles_per_seq=tps),
            grid=(M // tm, nj),
            in_specs=[pl.BlockSpec((tm, K), lambda i, j: (i, 0)),
                      pl.BlockSpec((HALO_ROWS, K), lambda i, j: (jnp.maximum(i * (tm // HALO_ROWS) - 1, 0), 0))]
            + common + [hspec(0), hspec(nj)] + cw,
            out_specs=[pl.BlockSpec((tm, tn), lambda i, j: (i, j)), hout, hout],
            out_shape=[jax.ShapeDtypeStruct((M, F), BF16)] + [jax.ShapeDtypeStruct((M // tm, SUBLANES, F), F32)] * 2,
            scratch_shapes=[pltpu.VMEM((tm + HALO_ROWS, K), BF16)],
            compiler_params=_cp(("arbitrary", "arbitrary")), name="ffn_up_prompt",
        )(h, h, g.reshape(1, K), w_up, w_up, hist, hist, conv_w, conv_w)
        return act, tail_a[tps - 1::tps], tail_v[tps - 1::tps]
    hk = hist.shape[0]
    hspec = lambda off: pl.BlockSpec((hk, nseq, tn), lambda i, j, off=off: (0, 0, j + off))
    hout = pl.BlockSpec((hk, nseq, tn), lambda i, j: (0, 0, j))
    return pl.pallas_call(
        functools.partial(_ffn_up_kernel, tstride=nseq, tiles_per_seq=1),
        grid=(1, nj),
        in_specs=[pl.BlockSpec((M, K), lambda i, j: (0, 0)), pl.BlockSpec((HALO_ROWS, K), lambda i, j: (0, 0))]
        + common + [hspec(0), hspec(nj)] + cw,
        out_specs=[pl.BlockSpec((M, tn), lambda i, j: (0, j)), hout, hout],
        out_shape=[jax.ShapeDtypeStruct((M, F), BF16)] + [jax.ShapeDtypeStruct((hk, nseq, F), F32)] * 2,
        scratch_shapes=[pltpu.VMEM((M, K), BF16)],
        compiler_params=_cp(("arbitrary", "arbitrary")), name="ffn_up_sample",
    )(h, h, g.reshape(1, K), w_up, w_up, hist, hist, conv_w, conv_w)


def _dense_tail(x2, ya, yb, yc, lw, hist_ffn, *, sample, nseq, seq_len):
    h = _mix_out(ya, yb, yc, x2, lw["out_norm_b"], lw["w_out"])
    act, hist_a, hist_v = _ffn_up(h, lw["norm_ffn_g"], lw["ffn_up"], lw["ffn_conv_w"], hist_ffn,
                                  sample=sample, nseq=nseq, seq_len=seq_len)
    return _matmul_res(act, lw["ffn_down"], h), jnp.concatenate([hist_a, hist_v], axis=-1)


def _prompt_layer(x2, lw, bias_tiles, bias_cmp, *, B, T):
    M = B * T
    G, R = NSA_KV_HEADS, NSA_REP
    z, zs = _norm_matmul(x2, lw["norm_mix_g"], lw["w_main"], lw["w_small"])
    hu0 = jnp.zeros((B, SUBLANES, 512), F32)
    hq0 = jnp.zeros((B, SUBLANES, 1536), F32)
    ya, q, rows, win, kvb, gates, gq, ggb, hu, rows_lin = _prep(z, zs, hu0, hq0, lw, sample=False, nseq=B, seq_len=T)
    cmp_kv = _compress_prompt(rows.reshape(B, T, 1024), lw)
    gates_g = gates[:, :3 * G * R].reshape(M, 3, G, R).transpose(2, 0, 1, 3).reshape(G, M, 3 * R)
    gates_g = jnp.pad(gates_g, ((0, 0), (0, 0), (0, LANES - 3 * R)))
    yb = _nsa_prompt(q, cmp_kv, kvb.reshape(B, T, 1024), gates_g, bias_tiles, bias_cmp, B=B, T=T)
    s0 = jnp.zeros((B, GDN_HEADS, HEAD_DIM, HEAD_DIM), F32)
    yc, s_new = _gdn_prompt(gq.reshape(B, T, 1536), ggb.reshape(B, T, 1024), z.reshape(B, T, C_END), s0,
                            lw["gdn_norm_g"])
    hf0 = jnp.zeros((B, SUBLANES, lw["ffn_conv_w"].shape[1]), F32)
    out, up = _dense_tail(x2, ya, yb, yc.reshape(M, 512), lw, hf0, sample=False, nseq=B, seq_len=T)
    return out, rows_lin, win, hu, z, s_new, up


def _sample_layer(l, x2, lw, st, bias_c, bias_s, bias_w, *, B, T):
    M = B * T
    G, R, H, hd = NSA_KV_HEADS, NSA_REP, GDN_HEADS, HEAD_DIM
    z, zs = _norm_matmul(x2, lw["norm_mix_g"], lw["w_main"], lw["w_small"])
    hu0 = jnp.swapaxes(st["conv_a"], 0, 1)
    hq0 = jnp.swapaxes(st["gdn_conv"], 0, 1)
    ya, q, rows, win, kvb, gates, gq, ggb, hu = _prep(z, zs, hu0, hq0, lw, sample=True, nseq=B, seq_len=T)
    tb = lambda a: a.reshape(T, B, -1)
    q_s = tb(q).reshape(T, B, G, R, hd).transpose(1, 2, 3, 0, 4).reshape(B, G, R * T, hd)
    gt = tb(gates)[:, :, :3 * G * R].reshape(T, B, 3, G, R).transpose(1, 3, 4, 0, 2).reshape(B, G, R * T, 3)
    gates_s = jnp.pad(gt, ((0, 0), (0, 0), (0, 0), (0, LANES - 3)))
    kvn_s = jnp.pad(jnp.swapaxes(tb(kvb), 0, 1), ((0, 0), (0, SUBLANES - T), (0, 0)))
    win_new = jnp.swapaxes(tb(win), 0, 1).reshape(B, T * 2 * G, hd)
    yb_s, win = _nsa_sample(l, st["page_table"], st["cache_lin"], st["win_lin"], win_new, q_s, kvn_s, gates_s, lw,
                            bias_c, bias_s, bias_w, tq=T)
    yb = yb_s.reshape(B, G, R, T, hd).transpose(3, 0, 1, 2, 4).reshape(M, G * R * hd)
    gq4 = tb(gq).reshape(T, B, 3, H, hd)
    bh = lambda a: a.transpose(1, 2, 0, 3)
    padt = lambda a: jnp.pad(a, ((0, 0), (0, 0), (0, SUBLANES - T), (0, 0)))
    kq_s = jnp.concatenate([bh(gq4[:, :, 1]), bh(gq4[:, :, 0])], axis=2)
    v_s = padt(bh(gq4[:, :, 2]))
    ggb4 = tb(ggb).reshape(T, B, 2, H, hd)
    gb_s = jnp.concatenate([bh(ggb4[:, :, 0]), bh(ggb4[:, :, 1])], axis=2)
    gz_s = padt(bh(tb(z)[:, :, C_GZ:C_END].reshape(T, B, H, hd)))
    yc_s, s_new = _gdn_sample(kq_s, v_s, gb_s, gz_s, st["gdn"], lw["gdn_norm_g"], tq=T)
    yc = yc_s[:, :, :T].transpose(2, 0, 1, 3).reshape(M, H * hd).astype(BF16)
    hf0 = jnp.swapaxes(st["ffn_conv"], 0, 1)
    out, up = _dense_tail(x2, ya, yb, yc, lw, hf0, sample=True, nseq=B, seq_len=T)
    return out, rows, win, hu, z, s_new, up


def kernel(x_prompt, x_sample, cache_kv, cache_win, state_conv_a, state_gdn_conv, state_gdn, state_ffn_conv,
           page_table, rel_bias, norm_mix_g, w_in, conv_a_w, q_norm_g, k_norm_g, cmp_pe, cmp_w1, cmp_w2,
           gdn_conv_w, gdn_a_log, gdn_dt_bias, gdn_norm_g, out_norm_a, out_norm_b, w_out, norm_ffn_g,
           ffn_up, ffn_conv_w, ffn_down):
    params = dict(norm_mix_g=norm_mix_g, w_in=w_in, conv_a_w=conv_a_w, q_norm_g=q_norm_g, k_norm_g=k_norm_g,
                  cmp_pe=cmp_pe, cmp_w1=cmp_w1, cmp_w2=cmp_w2, gdn_conv_w=gdn_conv_w, gdn_a_log=gdn_a_log,
                  gdn_dt_bias=gdn_dt_bias, gdn_norm_g=gdn_norm_g, out_norm_a=out_norm_a, out_norm_b=out_norm_b,
                  w_out=w_out, norm_ffn_g=norm_ffn_g, ffn_up=ffn_up, ffn_conv_w=ffn_conv_w, ffn_down=ffn_down)
    depth = w_in.shape[0]
    Bp, T, D = x_prompt.shape
    Bs, Ts, _ = x_sample.shape
    G, R, hd = NSA_KV_HEADS, NSA_REP, HEAD_DIM
    n_pool, page = cache_kv.shape[1], cache_kv.shape[2]
    npages = page_table.shape[1]
    past = npages * page
    wb = cache_win.shape[2]
    L = past + Ts
    assert ((L - CMP_BLOCK) // CMP_STRIDE) * CMP_STRIDE + CMP_BLOCK <= past, "compressed blocks must lie in the cache"
    assert T % Q_BLOCK == 0 and T % 512 == 0 and 3 <= Ts <= SUBLANES // 2

    thr = _bucket_thresholds()
    rb_flat = rel_bias.reshape(-1)
    nqb = T // Q_BLOCK
    bias_tiles = _bias_table(thr, rb_flat, nqb, Q_BLOCK, Q_BLOCK, a0=0, an=Q_BLOCK, qs=1, ks=1)
    bias_cmp = _bias_table(thr, rb_flat, nqb, Q_BLOCK, T // CMP_STRIDE,
                           a0=-(CMP_BLOCK - 1), an=Q_BLOCK, qs=1, ks=CMP_STRIDE)

    def sample_bias(ncols, a0, ks):
        t = _bias_table(thr, rb_flat, 1, SUBLANES, ncols, a0=a0, an=0, qs=1, ks=ks)[0]
        return t[:, :Ts].reshape(G, R * Ts, ncols)

    bias_c = sample_bias(past // CMP_STRIDE, past - (CMP_BLOCK - 1), CMP_STRIDE)
    bias_s = sample_bias(past + LANES, past, 1)
    bias_w = sample_bias(wb + LANES, wb, 1)

    cache_lin = cache_kv.reshape(depth, n_pool, page * 4 * G, hd)
    win_lin = cache_win.reshape(depth, Bs, wb * 2 * G, hd)
    assert wb == WINDOW, "the new window buffer is the old one shifted by the new rows"

    xp = x_prompt.reshape(Bp * T, D)
    xs = jnp.swapaxes(x_sample, 0, 1).reshape(Ts * Bs, D)
    outs_p, outs_s = [], []
    for l in range(depth):
        lw = _layer_weights(l, params)
        xp, rows, win, hu, z, s_new, up = _prompt_layer(xp, lw, bias_tiles, bias_cmp, B=Bp, T=T)
        wl = min(WINDOW, T)
        outs_p.append((
            rows.reshape(Bp, T, 4, G, hd),
            win.reshape(Bp, T, 2, G, hd)[:, T - wl:],
            hu[:, SUBLANES - 2:],
            z.reshape(Bp, T, C_END)[:, T - 3:, C_GQKV:C_GZ],
            s_new,
            up[:, SUBLANES - 2:]))
        st = dict(page_table=page_table, cache_lin=cache_lin, win_lin=win_lin, conv_a=state_conv_a[l],
                  gdn_conv=state_gdn_conv[l], gdn=state_gdn[l], ffn_conv=state_ffn_conv[l])
        xs, rows, win, hu, z, s_new, up = _sample_layer(l, xs, lw, st, bias_c, bias_s, bias_w, B=Bs, T=Ts)
        tb = lambda a: jnp.swapaxes(a.reshape(Ts, Bs, -1), 0, 1)
        outs_s.append((
            tb(rows).reshape(Bs, Ts, 4, G, hd),
            win[0].reshape(Bs, wb, 2, G, hd),
            jnp.swapaxes(hu, 0, 1),
            tb(z)[:, Ts - 3:, C_GQKV:C_GZ],
            s_new,
            jnp.swapaxes(up, 0, 1)))
    y_p = xp.reshape(Bp, T, D)
    y_s = jnp.swapaxes(xs.reshape(Ts, Bs, D), 0, 1)
    stack = lambda outs, i: jnp.stack([o[i] for o in outs])
    return (y_p, y_s, stack(outs_p, 0), stack(outs_s, 0), stack(outs_p, 1), stack(outs_s, 1),
            stack(outs_p, 2), stack(outs_s, 2), stack(outs_p, 3), stack(outs_s, 3),
            stack(outs_p, 4), stack(outs_s, 4), stack(outs_p, 5), stack(outs_s, 5))
```

```python
import functools
import math

import jax
import jax.numpy as jnp
from jax import lax
from jax.experimental import pallas as pl
from jax.experimental.pallas import tpu as pltpu

F32 = jnp.float32
BF16 = jnp.bfloat16
HIGHEST = lax.Precision.HIGHEST

SUBLANES = 8
LANES = 128
VMEM_LIMIT_BYTES = 52 * 1024 * 1024

HEAD_DIM = 128
NSA_KV_HEADS = 2
NSA_REP = 4
NSA_HEADS = NSA_KV_HEADS * NSA_REP
GDN_HEADS = 4
CMP_BLOCK = 32
CMP_STRIDE = 16
CMP_HIDDEN = 256
SLC_BLOCK = 64
N_SELECT = 16
WINDOW = 512
Q_BLOCK = 128
GDN_CHUNK = 64
REL_BUCKETS = 32
REL_MAX_DIST = 1024
EPS = 1e-6
NEG_INF = -1e30
FORCE_SCORE = 1e4

C_AB, C_AC, C_AH, C_NQ, C_NKV, C_GQKV, C_GZ, C_END = 0, 512, 1024, 1536, 2560, 4096, 5632, 6144
S_GATE, S_GA, S_GB = 0, 24, 28


def _cp(sem):
    return pltpu.CompilerParams(dimension_semantics=sem, vmem_limit_bytes=VMEM_LIMIT_BYTES)


def _div_pow2(x, d):
    assert d & (d - 1) == 0
    return jnp.right_shift(x, d.bit_length() - 1)


def _mod_pow2(x, d):
    assert d & (d - 1) == 0
    return jnp.bitwise_and(x, d - 1)


def _sigmoid(x):
    return 1.0 / (1.0 + jnp.exp(-x))


def _rms(x, g):
    return x * lax.rsqrt(jnp.mean(x * x, axis=-1, keepdims=True) + EPS) * g


def _l2n(x):
    return x * lax.rsqrt(jnp.sum(x * x, axis=-1, keepdims=True) + EPS)


def _dot_t(a, b, precision=None):
    return lax.dot_general(a, b, (((1,), (1,)), ((), ())), precision=precision, preferred_element_type=F32)


def _ld(ref, c0, c1):
    if len(ref.shape) == 3:
        v = ref[:, :, c0:c1]
        return v.reshape(v.shape[0] * v.shape[1], v.shape[2])
    return ref[:, c0:c1]


def _st(ref, c0, c1, val):
    if len(ref.shape) == 3:
        ref[:, :, c0:c1] = val.reshape(ref.shape[0], ref.shape[1], c1 - c0).astype(ref.dtype)
    else:
        ref[:, c0:c1] = val.astype(ref.dtype)


def _shift(x, halo, s, tstride):
    n = s * tstride
    rows = x.shape[0]
    if tstride % SUBLANES == 0:
        hr = halo.shape[0]
        return jnp.concatenate([halo[hr - n:], x[:rows - n]], axis=0)
    xs = pltpu.roll(x, n, 0)
    hs = pltpu.roll(halo, n, 0)
    rid = lax.broadcasted_iota(jnp.int32, (SUBLANES, x.shape[1]), 0)
    head = jnp.where(rid < n, hs, xs[:SUBLANES])
    return jnp.concatenate([head, xs[SUBLANES:]], axis=0)


def _norm_matmul_kernel(x_ref, g_ref, w_ref, *rest, has_small):
    if has_small:
        ws_ref, o_ref, os_ref, xn_ref = rest
    else:
        o_ref, xn_ref = rest

    @pl.when(pl.program_id(1) == 0)
    def _():
        xn = _rms(x_ref[...], g_ref[...]).astype(BF16)
        xn_ref[...] = xn
        if has_small:
            os_ref[...] = jnp.dot(xn, ws_ref[...], preferred_element_type=F32)

    o_ref[...] = jnp.dot(xn_ref[...], w_ref[...], preferred_element_type=F32).astype(o_ref.dtype)


def _row_tile(rows):
    for tm in (1024, 512):
        if rows % tm == 0:
            return tm
    return rows


def _norm_matmul(x, g, w, w_small=None, *, tn=512, out_dtype=F32):
    M, K = x.shape
    N = w.shape[1]
    tm = _row_tile(M)
    has_small = w_small is not None
    in_specs = [pl.BlockSpec((tm, K), lambda i, j: (i, 0)),
                pl.BlockSpec((1, K), lambda i, j: (0, 0)),
                pl.BlockSpec((K, tn), lambda i, j: (0, j))]
    out_specs = [pl.BlockSpec((tm, tn), lambda i, j: (i, j))]
    out_shape = [jax.ShapeDtypeStruct((M, N), out_dtype)]
    args = [x, g.reshape(1, K), w]
    if has_small:
        in_specs.append(pl.BlockSpec((K, LANES), lambda i, j: (0, 0)))
        out_specs.append(pl.BlockSpec((tm, LANES), lambda i, j: (i, 0)))
        out_shape.append(jax.ShapeDtypeStruct((M, LANES), F32))
        args.append(w_small)
    res = pl.pallas_call(
        functools.partial(_norm_matmul_kernel, has_small=has_small),
        grid=(M // tm, N // tn),
        in_specs=in_specs, out_specs=out_specs, out_shape=out_shape,
        scratch_shapes=[pltpu.VMEM((tm, K), BF16)],
        compiler_params=_cp(("parallel", "arbitrary")),
        name="norm_matmul_small" if has_small else "norm_matmul",
    )(*args)
    return res if has_small else res[0]


def _prep_kernel(z_ref, zp_ref, zs_ref, hu_ref, hq_ref, caw_ref, gcw_ref, qg_ref, kg_ref, ona_ref,
                 alog_ref, dtb_ref,
                 ya_ref, q_ref, rows_ref, win_ref, kvb_ref, gates_ref, gq_ref, ggb_ref, huo_ref, *lin_refs,
                 tstride, tiles_per_seq):
    hd = HEAD_DIM

    def put_rows(grp, val):
        _st(rows_ref, grp * hd, (grp + 1) * hd, val)
        if lin_refs:
            lin_refs[0][pl.ds(grp, val.shape[0], stride=4 * NSA_KV_HEADS), :] = val
    u = _ld(z_ref, C_AC, C_AH) * _ld(z_ref, C_AH, C_NQ)
    if tstride == 1:
        start = (pl.program_id(0) % tiles_per_seq) == 0
        halo_u = jnp.where(start, hu_ref[0], zp_ref[:, C_AC:C_AH] * zp_ref[:, C_AH:C_NQ])
        halo_q = jnp.where(start, hq_ref[0], zp_ref[:, C_GQKV:C_GZ])
    else:
        halo_u = _ld(hu_ref, 0, 512)
        halo_q = _ld(hq_ref, 0, 1536)
    caw = caw_ref[...]
    conv = caw[0:1] * _shift(u, halo_u, 2, tstride) + caw[1:2] * _shift(u, halo_u, 1, tstride) + caw[2:3] * u
    y = _ld(z_ref, C_AB, C_AC) * conv
    _st(ya_ref, 0, 512, _rms(y, ona_ref[...]))
    hu_rows = huo_ref.shape[0] * huo_ref.shape[1] if tstride != 1 else SUBLANES
    if tstride == 1:
        huo_ref[0] = u[u.shape[0] - hu_rows:]
    else:
        _st(huo_ref, 0, 512, u[u.shape[0] - hu_rows:])

    qg = qg_ref[...]
    kg = kg_ref[...]
    for h in range(NSA_HEADS):
        c0 = C_NQ + h * hd
        _st(q_ref, h * hd, (h + 1) * hd, _rms(_ld(z_ref, c0, c0 + hd), qg) * (hd ** -0.5))
    for grp in range(4):
        put_rows(grp, _ld(z_ref, C_NKV + grp * hd, C_NKV + (grp + 1) * hd))
    for g in range(NSA_KV_HEADS):
        c0 = C_NKV + 512 + g * hd
        kn = _rms(_ld(z_ref, c0, c0 + hd), kg)
        put_rows(4 + g, kn)
        _st(kvb_ref, g * hd, (g + 1) * hd, kn)
    for g in range(NSA_KV_HEADS):
        c0 = C_NKV + 768 + g * hd
        vs = _ld(z_ref, c0, c0 + hd)
        put_rows(6 + g, vs)
        _st(kvb_ref, 256 + g * hd, 256 + (g + 1) * hd, vs)
    for g in range(NSA_KV_HEADS):
        c0 = C_NKV + 1024 + g * hd
        kn = _rms(_ld(z_ref, c0, c0 + hd), kg)
        _st(win_ref, g * hd, (g + 1) * hd, kn)
        _st(kvb_ref, 512 + g * hd, 512 + (g + 1) * hd, kn)
    vw = _ld(z_ref, C_NKV + 1280, C_NKV + 1536)
    _st(win_ref, 256, 512, vw)
    _st(kvb_ref, 768, 1024, vw)

    zs = _ld(zs_ref, 0, LANES)
    sg = _sigmoid(zs)
    _st(gates_ref, 0, LANES, sg)
    xs = zs + dtb_ref[...]
    softplus = jnp.maximum(xs, 0.0) + jnp.log(1.0 + jnp.exp(-jnp.abs(xs)))
    gdec = -jnp.exp(alog_ref[...]) * softplus
    rows = zs.shape[0]
    if tstride == 1:
        in_chunk = _mod_pow2(lax.broadcasted_iota(jnp.int32, gdec.shape, 0), GDN_CHUNK)
        step = 1
        while step < GDN_CHUNK:
            gdec = gdec + jnp.where(in_chunk >= step, pltpu.roll(gdec, step, 0), 0.0)
            step *= 2
    for h in range(GDN_HEADS):
        _st(ggb_ref, h * hd, (h + 1) * hd, jnp.broadcast_to(gdec[:, S_GA + h:S_GA + h + 1], (rows, hd)))
        _st(ggb_ref, 512 + h * hd, 512 + (h + 1) * hd, jnp.broadcast_to(sg[:, S_GB + h:S_GB + h + 1], (rows, hd)))

    gcw = gcw_ref[...]
    for part in range(3):
        c0 = C_GQKV + part * 512
        x = _ld(z_ref, c0, c0 + 512)
        hq = halo_q[:, part * 512:(part + 1) * 512]
        w = gcw[:, part * 512:(part + 1) * 512]
        c = (w[0:1] * _shift(x, hq, 3, tstride) + w[1:2] * _shift(x, hq, 2, tstride)
             + w[2:3] * _shift(x, hq, 1, tstride) + w[3:4] * x)
        c = c * _sigmoid(c)
        if part == 2:
            _st(gq_ref, 1024, 1536, c)
        else:
            for h in range(GDN_HEADS):
                v = _l2n(c[:, h * hd:(h + 1) * hd])
                if part == 0:
                    v = v * (hd ** -0.5)
                _st(gq_ref, part * 512 + h * hd, part * 512 + (h + 1) * hd, v)


def _prep(z, zs, hist_u, hist_q, lw, *, sample, nseq, seq_len):
    M = z.shape[0]
    wts = [lw["conv_a_w"], lw["gdn_conv_w"], lw["q_norm_g"].reshape(1, -1), lw["k_norm_g"].reshape(1, -1),
           lw["out_norm_a"].reshape(1, -1), lw["alog_v"], lw["dtb_v"]]
    wspecs2 = [pl.BlockSpec(w.shape, lambda i: (0, 0)) for w in wts]
    widths = [(512, BF16), (1024, BF16), (1024, F32), (512, F32), (1024, BF16), (LANES, F32), (1536, F32), (1024, F32)]
    if not sample:
        tm = 256
        tps = seq_len // tm
        in_specs = [pl.BlockSpec((tm, C_END), lambda i: (i, 0)),
                    pl.BlockSpec((SUBLANES, C_END), lambda i: (jnp.maximum(i * (tm // SUBLANES) - 1, 0), 0)),
                    pl.BlockSpec((tm, LANES), lambda i: (i, 0)),
                    pl.BlockSpec((1, SUBLANES, 512), lambda i: (i // tps, 0, 0)),
                    pl.BlockSpec((1, SUBLANES, 1536), lambda i: (i // tps, 0, 0))] + wspecs2
        out_specs = [pl.BlockSpec((tm, w), lambda i: (i, 0)) for w, _ in widths]
        out_specs.append(pl.BlockSpec((1, SUBLANES, 512), lambda i: (i // tps, 0, 0)))
        out_shape = [jax.ShapeDtypeStruct((M, w), d) for w, d in widths]
        out_shape.append(jax.ShapeDtypeStruct((nseq, SUBLANES, 512), F32))
        kvr = 4 * NSA_KV_HEADS
        out_specs.append(pl.BlockSpec((tm * kvr, HEAD_DIM), lambda i: (i, 0)))
        out_shape.append(jax.ShapeDtypeStruct((M * kvr, HEAD_DIM), F32))
        return pl.pallas_call(
            functools.partial(_prep_kernel, tstride=1, tiles_per_seq=tps),
            grid=(M // tm,), in_specs=in_specs, out_specs=out_specs, out_shape=out_shape,
            compiler_params=_cp(("arbitrary",)), name="prep_prompt",
        )(z, z, zs, hist_u, hist_q, *wts)
    T = seq_len
    bt = 64
    z3 = z.reshape(T, nseq, C_END)
    zs3 = zs.reshape(T, nseq, LANES)
    in_specs = [pl.BlockSpec((T, bt, C_END), lambda i: (0, i, 0)),
                pl.BlockSpec((SUBLANES, C_END), lambda i: (0, 0)),
                pl.BlockSpec((T, bt, LANES), lambda i: (0, i, 0)),
                pl.BlockSpec((hist_u.shape[0], bt, 512), lambda i: (0, i, 0)),
                pl.BlockSpec((hist_q.shape[0], bt, 1536), lambda i: (0, i, 0))] + wspecs2
    out_specs = [pl.BlockSpec((T, bt, w), lambda i: (0, i, 0)) for w, _ in widths]
    out_specs.append(pl.BlockSpec((hist_u.shape[0], bt, 512), lambda i: (0, i, 0)))
    out_shape = [jax.ShapeDtypeStruct((T, nseq, w), d) for w, d in widths]
    out_shape.append(jax.ShapeDtypeStruct((hist_u.shape[0], nseq, 512), F32))
    outs = pl.pallas_call(
        functools.partial(_prep_kernel, tstride=bt, tiles_per_seq=1),
        grid=(nseq // bt,), in_specs=in_specs, out_specs=out_specs, out_shape=out_shape,
        compiler_params=_cp(("arbitrary",)), name="prep_sample",
    )(z3, z, zs3, hist_u, hist_q, *wts)
    return [o.reshape(M, o.shape[-1]) for o in outs[:-1]] + [outs[-1]]


def _bias_kernel(thr_ref, rb_ref, o_ref, *, a0, an, qs, ks):
    n = pl.program_id(0)
    R, C = o_ref.shape[-2:]

    def rows8(rc, carry):
        r0 = pl.multiple_of(rc * SUBLANES, SUBLANES)
        dist = (a0 + an * n + qs * (r0 + lax.broadcasted_iota(jnp.int32, (SUBLANES, C), 0))
                - ks * lax.broadcasted_iota(jnp.int32, (SUBLANES, C), 1))
        b = [jnp.full((SUBLANES, C), rb_ref[h], F32) for h in range(NSA_HEADS)]
        for k in range(1, REL_BUCKETS):
            reached = dist >= thr_ref[k]
            b = [jnp.where(reached, rb_ref[k * NSA_HEADS + h], b[h]) for h in range(NSA_HEADS)]
        for h in range(NSA_HEADS):
            o_ref[0, h, pl.ds(r0, SUBLANES), :] = b[h]
        return carry

    lax.fori_loop(0, R // SUBLANES, rows8, 0)


def _bias_table(thr, rb_flat, n, R, C, *, a0, an, qs, ks):
    return pl.pallas_call(
        functools.partial(_bias_kernel, a0=a0, an=an, qs=qs, ks=ks),
        grid=(n,),
        in_specs=[pl.BlockSpec(memory_space=pltpu.SMEM), pl.BlockSpec(memory_space=pltpu.SMEM)],
        out_specs=pl.BlockSpec((1, NSA_HEADS, R, C), lambda i: (i, 0, 0, 0)),
        out_shape=jax.ShapeDtypeStruct((n, NSA_HEADS, R, C), F32),
        compiler_params=_cp(("parallel",)), name="bias_table",
    )(thr, rb_flat)


def _bucket_thresholds():
    n = jnp.arange(REL_MAX_DIST + 1)
    exact = REL_BUCKETS // 2
    nf = jnp.maximum(n, 1).astype(F32)
    far = exact + (jnp.log(nf / exact) / math.log(REL_MAX_DIST / exact) * (REL_BUCKETS - exact)).astype(jnp.int32)
    bucket = jnp.where(n < exact, n, jnp.minimum(far, REL_BUCKETS - 1))
    return jnp.sum(bucket[None, :] < jnp.arange(REL_BUCKETS)[:, None], axis=1).astype(jnp.int32)


def _gelu_tanh(x):
    return x * (0.5 * (1.0 + jnp.tanh(math.sqrt(2.0 / math.pi) * (x + 0.044715 * (x * x * x)))))


def _compress_math(get_rows, ng, pe, w1_ref, w2):
    half = CMP_BLOCK // 2
    rows = [get_rows(r) for r in range(half)]
    top = jnp.dot(jnp.concatenate([(rows[r] + pe[r:r + 1]).astype(BF16) for r in range(half)], axis=-1),
                  w1_ref[0], preferred_element_type=F32)
    bot = jnp.dot(jnp.concatenate([(rows[r] + pe[r + half:r + half + 1]).astype(BF16) for r in range(half)], axis=-1),
                  w1_ref[1], preferred_element_type=F32)
    h = top + pltpu.roll(bot, ng - 1, 0)
    return jnp.dot(_gelu_tanh(h).astype(BF16), w2, preferred_element_type=F32)


def _compress_kernel(x_ref, pe_ref, w1_ref, w2_ref, kg_ref, o_ref):
    ng = x_ref.shape[1] // CMP_STRIDE
    slot = pl.program_id(1)

    def get_rows(r):
        return x_ref[0, pl.ds(r, ng, stride=CMP_STRIDE), :]

    out = _compress_math(get_rows, ng, pe_ref[0], w1_ref.at[0], w2_ref[0])
    o_ref[0, 0, 0] = jnp.where(slot == 0, _rms(out, kg_ref[...]), out).astype(o_ref.dtype)


def _compress_prompt(rows3, lw):
    B, T, _ = rows3.shape
    ng = T // CMP_STRIDE
    G = NSA_KV_HEADS
    return pl.pallas_call(
        _compress_kernel,
        grid=(B, 2, G),
        in_specs=[pl.BlockSpec((1, T, HEAD_DIM), lambda b, s, g: (b, 0, s * G + g)),
                  pl.BlockSpec((1, CMP_BLOCK, HEAD_DIM), lambda b, s, g: (s, 0, 0)),
                  pl.BlockSpec((1,) + lw["cmp_w1"].shape[1:], lambda b, s, g: (s, 0, 0, 0)),
                  pl.BlockSpec((1, CMP_HIDDEN, HEAD_DIM), lambda b, s, g: (s, 0, 0)),
                  pl.BlockSpec((1, HEAD_DIM), lambda b, s, g: (0, 0))],
        out_specs=pl.BlockSpec((1, 1, 1, ng, HEAD_DIM), lambda b, s, g: (b, s, g, 0, 0)),
        out_shape=jax.ShapeDtypeStruct((B, 2, G, ng, HEAD_DIM), BF16),
        compiler_params=_cp(("parallel", "parallel", "parallel")), name="compress_prompt",
    )(rows3, lw["cmp_pe"], lw["cmp_w1"], lw["cmp_w2"], lw["k_norm_g"].reshape(1, -1))


def _masked_softmax(s, mask):
    s = jnp.where(mask, s, NEG_INF)
    m = jnp.max(s, axis=-1, keepdims=True)
    e = jnp.where(mask, jnp.exp(s - m), 0.0)
    z = jnp.sum(e, axis=-1, keepdims=True)
    return e / jnp.where(z > 0, z, 1.0)


def _overlap_matrix(ncp):
    c = CMP_STRIDE * lax.broadcasted_iota(jnp.int32, (ncp, LANES), 0)
    s = SLC_BLOCK * lax.broadcasted_iota(jnp.int32, (ncp, LANES), 1)
    return jnp.where(c < s + SLC_BLOCK, jnp.where(c + CMP_BLOCK > s, 1.0, 0.0), 0.0).astype(F32)


def _select_mask(imp, cur, n_slc):
    R = imp.shape[0]
    blk = lax.broadcasted_iota(jnp.int32, (R, LANES), 1)
    for forced_blk in (0, cur, cur - 1):
        imp = jnp.where(blk == forced_blk, FORCE_SCORE, imp)
    imp = jnp.where(blk <= cur, imp, -1.0)
    imp = jnp.where(blk < n_slc, imp, -2.0)
    cnt = jnp.zeros((R, LANES), F32)
    for k in range(n_slc):
        col = imp[:, k:k + 1]
        cnt = cnt + jnp.where(col > imp, 1.0, jnp.where(col == imp, jnp.where(blk > k, 1.0, 0.0), 0.0))
    return jnp.where(cnt < float(min(N_SELECT, n_slc)), jnp.where(blk < n_slc, 1.0, 0.0), 0.0)


def _expand_matrix(nkeys):
    s = lax.broadcasted_iota(jnp.int32, (LANES, nkeys), 0)
    j = lax.broadcasted_iota(jnp.int32, (LANES, nkeys), 1)
    lo = s * SLC_BLOCK
    return jnp.where(j >= lo, jnp.where(j < lo + SLC_BLOCK, 1.0, 0.0), 0.0).astype(BF16)


KEY_TILE = 512


def _select_mask_t(imp_t, cur, n_slc):
    blk = lax.broadcasted_iota(jnp.int32, imp_t.shape, 0)
    for forced_blk in (0, cur, cur - 1):
        imp_t = jnp.where(blk == forced_blk, FORCE_SCORE, imp_t)
    imp_t = jnp.where(blk <= cur, imp_t, -1.0)
    cnt = jnp.zeros(imp_t.shape, F32)
    for k in range(n_slc):
        row = imp_t[k:k + 1, :]
        cnt = cnt + jnp.where(row > imp_t, 1.0, jnp.where(row == imp_t, jnp.where(blk > k, 1.0, 0.0), 0.0))
    return jnp.where(cnt < float(min(N_SELECT, n_slc)), 1.0, 0.0)


def _nsa_prompt_kernel(q_ref, kc_ref, vc_ref, ks_ref, vs_ref, kw_ref, vw_ref, bt_ref, bc_ref, gates_ref, ex_ref,
                       o_ref, selk_ref, m_ref, acc_ref, *, T):
    qb = pl.program_id(2)
    QB = Q_BLOCK
    R = NSA_REP
    RQ = R * QB
    hd = HEAD_DIM
    nc = (T - CMP_BLOCK) // CMP_STRIDE + 1
    ncp = kc_ref.shape[-2]
    n_slc = ex_ref.shape[0]
    q = jnp.concatenate([q_ref[:, r * hd:(r + 1) * hd] for r in range(R)], axis=0)

    s = _dot_t(q, kc_ref[0, 0, 0]) + bc_ref[0].reshape(RQ, ncp)
    t_row = lax.broadcasted_iota(jnp.int32, (R, QB, ncp), 1).reshape(RQ, ncp)
    c_col = lax.broadcasted_iota(jnp.int32, (RQ, ncp), 1)
    dist = qb * QB + t_row - (CMP_STRIDE * c_col + CMP_BLOCK - 1)
    p = _masked_softmax(s, jnp.where(c_col < nc, dist, -1) >= 0)
    o_cmp = jnp.dot(p.astype(BF16), vc_ref[0, 0, 0], preferred_element_type=F32)

    psum = p[0:QB]
    for r in range(1, R):
        psum = psum + p[r * QB:(r + 1) * QB]
    s_lo = SLC_BLOCK * lax.broadcasted_iota(jnp.int32, (n_slc, ncp), 0)
    c_lo = CMP_STRIDE * lax.broadcasted_iota(jnp.int32, (n_slc, ncp), 1)
    overlap_t = jnp.where(c_lo < s_lo + SLC_BLOCK, jnp.where(c_lo + CMP_BLOCK > s_lo, 1.0, 0.0), 0.0).astype(F32)
    imp_t = _dot_t(overlap_t, psum, HIGHEST)
    cur = _div_pow2(qb * QB + lax.broadcasted_iota(jnp.int32, (1, QB), 1), SLC_BLOCK)
    sel_t = _select_mask_t(imp_t, cur, n_slc)
    selk_ref[...] = lax.dot_general(sel_t.astype(BF16), ex_ref[...], (((0,), (0,)), ((), ())),
                                    preferred_element_type=F32)

    NCHAIN = 2
    hpc = R // NCHAIN
    crow = hpc * QB

    def bias_rows(c, first_blk, n):
        return jnp.concatenate(
            [bt_ref[jnp.maximum(qb - (first_blk + j), 0), c * hpc:(c + 1) * hpc].reshape(crow, QB) for j in range(n)],
            axis=-1)

    def rel(first_key, nk):
        return (qb * QB - first_key + lax.broadcasted_iota(jnp.int32, (QB, nk), 0)
                - lax.broadcasted_iota(jnp.int32, (QB, nk), 1))

    def with_ones(v):
        return jnp.concatenate([v, jnp.ones(v.shape, BF16)], axis=-1)

    nkb = KEY_TILE // QB
    m_ref[...] = jnp.full(m_ref.shape, NEG_INF, F32)
    acc_ref[...] = jnp.zeros(acc_ref.shape, F32)

    def body(kt, carry):
        off = pl.multiple_of(kt * KEY_TILE, KEY_TILE)
        k = ks_ref[0, pl.ds(off, KEY_TILE), :]
        vext = with_ones(vs_ref[0, pl.ds(off, KEY_TILE), :])
        d = jnp.where(selk_ref[:, pl.ds(off, KEY_TILE)] > 0.5, rel(off, KEY_TILE), -1)
        neg = jnp.concatenate([jnp.where(d >= 0, 0.0, NEG_INF)] * hpc, axis=0)
        for c in range(NCHAIN):
            rows = slice(c * crow, (c + 1) * crow)
            sc = _dot_t(q[rows], k) + bias_rows(c, kt * nkb, nkb) + neg
            m_old = m_ref[rows]
            m_new = jnp.maximum(m_old, jnp.max(sc, axis=-1, keepdims=True))
            pe = jnp.exp(sc - m_new).astype(BF16)
            acc_ref[rows] = jnp.exp(m_old - m_new) * acc_ref[rows] + jnp.dot(pe, vext, preferred_element_type=F32)
            m_ref[rows] = m_new
        return carry

    lax.fori_loop(0, qb // nkb + 1, body, 0)
    acc = acc_ref[...]
    den = acc[:, hd:]
    o_slc = acc[:, :hd] / jnp.where(den > 0, den, 1.0)

    wk = min(WINDOW + QB, T)
    w0 = jnp.maximum(qb - WINDOW // QB, 0)
    offw = pl.multiple_of(w0 * QB, QB)
    kwin = kw_ref[0, pl.ds(offw, wk), :]
    vext = with_ones(vw_ref[0, pl.ds(offw, wk), :])
    d = rel(offw, wk)
    d = jnp.where(d <= WINDOW, d, -1)
    neg = jnp.concatenate([jnp.where(d >= 0, 0.0, NEG_INF)] * hpc, axis=0)
    o_win = []
    for c in range(NCHAIN):
        sc = _dot_t(q[c * crow:(c + 1) * crow], kwin) + bias_rows(c, w0, wk // QB) + neg
        e = jnp.exp(sc - jnp.max(sc, axis=-1, keepdims=True)).astype(BF16)
        acc = jnp.dot(e, vext, preferred_element_type=F32)
        o_win.append(acc[:, :hd] / acc[:, hd:])
    o_win = jnp.concatenate(o_win, axis=0)

    gt = gates_ref[0]
    for r in range(R):
        sl = slice(r * QB, (r + 1) * QB)
        o = (gt[:, r:r + 1] * o_cmp[sl] + gt[:, R + r:R + r + 1] * o_slc[sl]
             + gt[:, 2 * R + r:2 * R + r + 1] * o_win[sl])
        o_ref[:, r * hd:(r + 1) * hd] = o


def _nsa_prompt(q, cmp_kv, kvb3, gates_g, bias_tiles, bias_cmp, *, B, T):
    G = NSA_KV_HEADS
    R = NSA_REP
    nqb = T // Q_BLOCK
    ncp = T // CMP_STRIDE
    n_slc = -(-T // SLC_BLOCK)
    expand = (jnp.arange(T)[None, :] // SLC_BLOCK == jnp.arange(n_slc)[:, None]).astype(BF16)
    kv_spec = lambda col: pl.BlockSpec((1, T, HEAD_DIM), lambda b, g, i, col=col: (b, 0, col + g))
    return pl.pallas_call(
        functools.partial(_nsa_prompt_kernel, T=T),
        grid=(B, G, nqb),
        in_specs=[pl.BlockSpec((Q_BLOCK, R * HEAD_DIM), lambda b, g, i: (b * nqb + i, g)),
                  pl.BlockSpec((1, 1, 1, ncp, HEAD_DIM), lambda b, g, i: (b, 0, g, 0, 0)),
                  pl.BlockSpec((1, 1, 1, ncp, HEAD_DIM), lambda b, g, i: (b, 1, g, 0, 0)),
                  kv_spec(0), kv_spec(2), kv_spec(4), kv_spec(6),
                  pl.BlockSpec((nqb, R, Q_BLOCK, Q_BLOCK), lambda b, g, i: (0, g, 0, 0)),
                  pl.BlockSpec((1, R, Q_BLOCK, ncp), lambda b, g, i: (i, g, 0, 0)),
                  pl.BlockSpec((1, Q_BLOCK, LANES), lambda b, g, i: (g, b * nqb + i, 0)),
                  pl.BlockSpec((n_slc, T), lambda b, g, i: (0, 0))],
        out_specs=pl.BlockSpec((Q_BLOCK, R * HEAD_DIM), lambda b, g, i: (b * nqb + i, g)),
        out_shape=jax.ShapeDtypeStruct((B * T, G * R * HEAD_DIM), F32),
        scratch_shapes=[pltpu.VMEM((Q_BLOCK, T), F32),
                        pltpu.VMEM((R * Q_BLOCK, 1), F32),
                        pltpu.VMEM((R * Q_BLOCK, 2 * HEAD_DIM), F32)],
        compiler_params=_cp(("parallel", "parallel", "arbitrary")), name="nsa_prompt",
    )(q, cmp_kv, cmp_kv, kvb3, kvb3, kvb3, kvb3, bias_tiles, bias_cmp, gates_g, expand)


def _nsa_sample_kernel(pt_ref, *refs, npages, tq):
    pages = refs[:npages]
    (win_ref, wnew_ref, q_ref, kvn_ref, gates_ref, pe_ref, w1_ref, w2_ref, kg_ref, bc_ref, bs_ref, bw_ref,
     o_ref, wout_ref) = refs[npages:]
    G, R, hd = NSA_KV_HEADS, NSA_REP, HEAD_DIM
    KVR = 4 * G
    WR = 2 * G
    page = pages[0].shape[2] // KVR
    past = npages * page
    L = past + tq
    nc = (L - CMP_BLOCK) // CMP_STRIDE + 1
    ng = past // CMP_STRIDE
    n_slc = -(-L // SLC_BLOCK)
    rows = R * tq
    wb = win_ref.shape[2] // WR
    wout_ref[0, 0, 0:(wb - tq) * WR, :] = win_ref[0, 0, tq * WR:wb * WR, :]
    wout_ref[0, 0, (wb - tq) * WR:wb * WR, :] = wnew_ref[0]
    pad_new = lambda v: jnp.concatenate([v, jnp.zeros((LANES - v.shape[0], v.shape[1]), v.dtype)], axis=0)
    t_of_row = lax.broadcasted_iota(jnp.int32, (R, tq, 1), 1).reshape(rows, 1)
    pos = past + t_of_row
    ra = lax.broadcasted_iota(jnp.int32, (rows, rows), 0)
    rb = lax.broadcasted_iota(jnp.int32, (rows, rows), 1)
    same_t = jnp.where(_mod_pow2(ra, tq) == _mod_pow2(rb, tq), 1.0, 0.0).astype(F32)
    kg = kg_ref[...]
    kvn = kvn_ref[0]

    for g in range(G):
        q = q_ref[0, g]
        gt = gates_ref[0, g]

        def compress(slot):
            col = slot * G + g

            def get_rows(r):
                return jnp.concatenate(
                    [pg[0, 0, pl.ds(r * KVR + col, page // CMP_STRIDE, stride=CMP_STRIDE * KVR), :] for pg in pages],
                    axis=0)

            return _compress_math(get_rows, ng, pe_ref[slot], w1_ref.at[slot], w2_ref[slot])

        kc = _rms(compress(0), kg).astype(BF16)
        vc = compress(1).astype(BF16)

        c_col = lax.broadcasted_iota(jnp.int32, (rows, ng), 1)
        dist = pos - (CMP_STRIDE * c_col + CMP_BLOCK - 1)
        p = _masked_softmax(_dot_t(q, kc) + bc_ref[g], jnp.where(c_col < nc, dist, -1) >= 0)
        o_cmp = jnp.dot(p.astype(BF16), vc, preferred_element_type=F32)

        psum = jnp.dot(same_t, p, precision=HIGHEST, preferred_element_type=F32)
        imp = jnp.dot(psum, _overlap_matrix(ng), precision=HIGHEST, preferred_element_type=F32)
        sel = _select_mask(imp, _div_pow2(pos, SLC_BLOCK), n_slc)
        nk = past + LANES
        selk = jnp.dot(sel.astype(BF16), _expand_matrix(nk), preferred_element_type=F32)

        k_new = pad_new(kvn[:, g * hd:(g + 1) * hd])
        v_new = pad_new(kvn[:, 256 + g * hd:256 + (g + 1) * hd])
        all_rows = lambda slot, new: jnp.concatenate(
            [pg[0, 0, pl.ds(slot * G + g, page, stride=KVR), :].astype(BF16) for pg in pages] + [new], axis=0)
        sc = _dot_t(q, all_rows(2, k_new)) + bs_ref[g]
        j = lax.broadcasted_iota(jnp.int32, (rows, nk), 1)
        p = _masked_softmax(sc, jnp.where(selk > 0.5, pos - j, -1) >= 0).astype(BF16)
        o_slc = jnp.dot(p, all_rows(3, v_new), preferred_element_type=F32)

        kw_new = pad_new(kvn[:, 512 + g * hd:512 + (g + 1) * hd])
        vw_new = pad_new(kvn[:, 768 + g * hd:768 + (g + 1) * hd])
        kw = win_ref[0, 0, pl.ds(g, wb, stride=WR), :].astype(BF16)
        vw = win_ref[0, 0, pl.ds(G + g, wb, stride=WR), :].astype(BF16)
        sc = _dot_t(q, jnp.concatenate([kw, kw_new], axis=0)) + bw_ref[g]
        j = lax.broadcasted_iota(jnp.int32, (rows, wb + LANES), 1)
        d = pos - (past - wb + j)
        p = _masked_softmax(sc, jnp.where(d <= WINDOW, d, -1) >= 0).astype(BF16)
        o_win = jnp.dot(p, jnp.concatenate([vw, vw_new], axis=0), preferred_element_type=F32)

        o_ref[0, g] = gt[:, 0:1] * o_cmp + gt[:, 1:2] * o_slc + gt[:, 2:3] * o_win


def _nsa_sample(l, page_table, cache_lin, win_lin, win_new, q_s, kvn_s, gates_s, lw, bias_c, bias_s, bias_w, *, tq):
    B, npages = page_table.shape
    G = NSA_KV_HEADS
    rows = NSA_REP * tq
    page_specs = [pl.BlockSpec((1, 1) + cache_lin.shape[2:], lambda b, pt, p=p: (l, pt[b, p], 0, 0))
                  for p in range(npages)]
    full = lambda a: pl.BlockSpec(a.shape, lambda b, pt, n=a.ndim: (0,) * n)
    wts = [lw["cmp_pe"], lw["cmp_w1"], lw["cmp_w2"], lw["k_norm_g"].reshape(1, -1), bias_c, bias_s, bias_w]
    grid_spec = pltpu.PrefetchScalarGridSpec(
        num_scalar_prefetch=1, grid=(B,),
        in_specs=page_specs + [
            pl.BlockSpec((1, 1) + win_lin.shape[2:], lambda b, pt: (l, b, 0, 0)),
            pl.BlockSpec((1,) + win_new.shape[1:], lambda b, pt: (b, 0, 0)),
            pl.BlockSpec((1, G, rows, HEAD_DIM), lambda b, pt: (b, 0, 0, 0)),
            pl.BlockSpec((1, SUBLANES, kvn_s.shape[2]), lambda b, pt: (b, 0, 0)),
            pl.BlockSpec((1, G, rows, LANES), lambda b, pt: (b, 0, 0, 0))] + [full(w) for w in wts],
        out_specs=[pl.BlockSpec((1, G, rows, HEAD_DIM), lambda b, pt: (b, 0, 0, 0)),
                   pl.BlockSpec((1, 1) + win_lin.shape[2:], lambda b, pt: (0, b, 0, 0))])
    return pl.pallas_call(
        functools.partial(_nsa_sample_kernel, npages=npages, tq=tq),
        grid_spec=grid_spec,
        out_shape=[jax.ShapeDtypeStruct((B, G, rows, HEAD_DIM), F32),
                   jax.ShapeDtypeStruct((1,) + win_lin.shape[1:], F32)],
        compiler_params=_cp(("arbitrary",)), name="nsa_sample",
    )(page_table, *([cache_lin] * npages), win_lin, win_new, q_s, kvn_s, gates_s, *wts)


def _dot3(a, b):
    ah = a.astype(BF16)
    bh = b.astype(BF16)
    al = (a - ah.astype(F32)).astype(BF16)
    bl = (b - bh.astype(F32)).astype(BF16)
    d = functools.partial(jnp.dot, preferred_element_type=F32)
    return d(ah, bh) + (d(al, bh) + d(ah, bl))


def _gdn_chunk_kernel(gq_ref, ggb_ref, n_ref, k2_ref, qe_ref, o0_ref, eg_ref):
    B, C = gq_ref.shape[:2]
    H = GDN_HEADS
    hd = HEAD_DIM
    W = H * hd
    HC = H * C
    ri = lax.broadcasted_iota(jnp.int32, (HC, HC), 0)
    ci = lax.broadcasted_iota(jnp.int32, (HC, HC), 1)
    tri = jnp.where(_div_pow2(ri, C) == _div_pow2(ci, C), ri - ci, -1)
    eye = jnp.where(ri == ci, 1.0, 0.0).astype(F32)
    ones = jnp.ones((HC, HC), F32)
    own_lanes = (_div_pow2(lax.broadcasted_iota(jnp.int32, (HC, W), 0), C)
                 == _div_pow2(lax.broadcasted_iota(jnp.int32, (HC, W), 1), hd))

    for b in range(B):
        stack = lambda ref, c0: jnp.concatenate([ref[b, :, c0 + h * hd:c0 + (h + 1) * hd] for h in range(H)], axis=0)
        q = stack(gq_ref, 0)
        k = stack(gq_ref, W)
        v = stack(gq_ref, 2 * W)
        gc = stack(ggb_ref, 0)
        beta = stack(ggb_ref, W)
        gcol = jnp.concatenate([gc] * (HC // hd), axis=-1)
        grow = jnp.dot(ones, eye * gcol, precision=HIGHEST, preferred_element_type=F32)
        decay = jnp.exp(jnp.where(tri >= 0, gcol - grow, -jnp.inf))
        kb = k * beta
        g2 = _dot_t(jnp.concatenate([kb, q], axis=0).astype(BF16), k.astype(BF16))
        low = jnp.where(tri > 0, g2[:HC] * decay, 0.0)
        qk = jnp.where(tri >= 0, g2[HC:] * decay, 0.0)
        egc = jnp.exp(gc)
        x = jnp.concatenate([v * beta, kb * egc], axis=-1)
        pw = low
        span = 1
        while span < C:
            last = span * 2 >= C
            rhs = x if last else jnp.concatenate([pw, x], axis=-1)
            r = _dot3(pw, rhs)
            px = r if last else r[:, HC:]
            x = x - px if span == 1 else x + px
            if not last:
                pw = r[:, :HC]
            span *= 2
        uw = x.astype(BF16)
        g_last = jnp.concatenate([jnp.broadcast_to(gc[(h + 1) * C - 1:(h + 1) * C], (C, hd)) for h in range(H)],
                                 axis=0)
        kd = k * jnp.exp(g_last - gc)
        kd_bd = jnp.where(own_lanes, jnp.concatenate([kd] * H, axis=-1), 0.0).astype(BF16)
        nk = lax.dot_general(kd_bd, uw, (((0,), (0,)), ((), ())), preferred_element_type=F32)
        ow = jnp.dot(qk.astype(BF16), uw, preferred_element_type=F32)
        qe = q * egc - ow[:, hd:]
        eg = jnp.exp(g_last)
        for h in range(H):
            n_ref[b, h, 0] = nk[h * hd:(h + 1) * hd, :hd]
            k2_ref[b, h, 0] = nk[h * hd:(h + 1) * hd, hd:].astype(BF16)
            o0_ref[b, h, 0] = ow[h * C:(h + 1) * C, :hd]
            qe_ref[b, h, 0] = qe[h * C:(h + 1) * C].astype(BF16)
            eg_ref[b, h, 0] = eg[h * C:h * C + SUBLANES]


def _gdn_scan_kernel(n_ref, k2_ref, qe_ref, o0_ref, eg_ref, gz_ref, s0_ref, ng_ref, y_ref, so_ref, s_ref):
    n = pl.program_id(0)
    B = gz_ref.shape[0]
    hd = HEAD_DIM

    @pl.when(n == 0)
    def _():
        s_ref[...] = s0_ref[...]

    ng = ng_ref[...]
    for b in range(B):
        for h in range(GDN_HEADS):
            S = s_ref[b, h]
            s16 = S.astype(BF16)
            o = jnp.dot(qe_ref[b, h, 0], s16, preferred_element_type=F32) + o0_ref[b, h, 0]
            s_ref[b, h] = (eg_ref[b, h, 0][0:1] * S + n_ref[b, h, 0]
                           - jnp.dot(k2_ref[b, h, 0], s16, preferred_element_type=F32))
            z = gz_ref[b, :, h * hd:(h + 1) * hd]
            y_ref[b, :, h * hd:(h + 1) * hd] = (_rms(o, ng) * (z * _sigmoid(z))).astype(y_ref.dtype)

    @pl.when(n == pl.num_programs(0) - 1)
    def _():
        so_ref[...] = s_ref[...]


def _gdn_prompt(gq3, ggb3, z3, s0, norm_g):
    B, T, _ = gq3.shape
    C = GDN_CHUNK
    H = GDN_HEADS
    hd = HEAD_DIM
    W = H * hd
    nchunk = T // C
    item = lambda rows: pl.BlockSpec((B, H, 1, rows, hd), lambda n: (0, 0, n, 0, 0))
    shape = lambda rows, dt: jax.ShapeDtypeStruct((B, H, nchunk, rows, hd), dt)
    nn, k2, qe, o0, eg = pl.pallas_call(
        _gdn_chunk_kernel,
        grid=(nchunk,),
        in_specs=[pl.BlockSpec((B, C, 3 * W), lambda n: (0, n, 0)),
                  pl.BlockSpec((B, C, 2 * W), lambda n: (0, n, 0))],
        out_specs=[item(hd), item(hd), item(C), item(C), item(SUBLANES)],
        out_shape=[shape(hd, F32), shape(hd, BF16), shape(C, BF16), shape(C, F32), shape(SUBLANES, F32)],
        compiler_params=_cp(("parallel",)), name="gdn_chunk",
    )(gq3, ggb3)
    return pl.pallas_call(
        _gdn_scan_kernel,
        grid=(nchunk,),
        in_specs=[item(hd), item(hd), item(C), item(C), item(SUBLANES),
                  pl.BlockSpec((B, C, W), lambda n: (0, n, C_GZ // W)),
                  pl.BlockSpec(s0.shape, lambda n: (0, 0, 0, 0)),
                  pl.BlockSpec((1, hd), lambda n: (0, 0))],
        out_specs=[pl.BlockSpec((B, C, W), lambda n: (0, n, 0)),
                   pl.BlockSpec(s0.shape, lambda n: (0, 0, 0, 0))],
        out_shape=[jax.ShapeDtypeStruct((B, T, W), BF16), jax.ShapeDtypeStruct(s0.shape, F32)],
        scratch_shapes=[pltpu.VMEM(s0.shape, F32)],
        compiler_params=_cp(("arbitrary",)), name="gdn_scan",
    )(nn, k2, qe, o0, eg, z3, s0, norm_g.reshape(1, -1))


def _gdn_sample_kernel(kq_ref, v_ref, gb_ref, gz_ref, s0_ref, ng_ref, y_ref, so_ref, *, tq):
    bt = kq_ref.shape[0]
    hd = HEAD_DIM
    ri = lax.broadcasted_iota(jnp.int32, (hd, hd), 0)
    ci = lax.broadcasted_iota(jnp.int32, (hd, hd), 1)
    eye = jnp.where(ri == ci, 1.0, 0.0).astype(F32)
    ng = ng_ref[...]

    def body(bi, carry):
        for h in range(GDN_HEADS):
            kq = kq_ref[bi, h]
            cols = _dot_t(eye, kq, HIGHEST)
            gb = gb_ref[bi, h]
            v = v_ref[bi, h]
            S = s0_ref[bi, h]
            outs = []
            for t in range(tq):
                a = jnp.exp(gb[t:t + 1])
                kc = cols[:, t:t + 1]
                qc = cols[:, tq + t:tq + t + 1]
                Sa = S * a
                stk = jnp.sum(Sa * kc, axis=0, keepdims=True)
                vn = gb[tq + t:tq + t + 1] * (v[t:t + 1] - stk)
                S = Sa + kc * vn
                outs.append(jnp.sum(S * qc, axis=0, keepdims=True))
            so_ref[bi, h] = S
            o = jnp.concatenate(outs + [jnp.zeros((SUBLANES - tq, hd), F32)], axis=0)
            z = gz_ref[bi, h]
            y_ref[bi, h] = _rms(o, ng) * (z * _sigmoid(z))
        return carry

    lax.fori_loop(0, bt, body, 0)


def _gdn_sample(kq_s, v_s, gb_s, gz_s, s0, norm_g, *, tq):
    B = kq_s.shape[0]
    bt = 8
    blk = lambda a: pl.BlockSpec((bt,) + a.shape[1:], lambda i: (i, 0, 0, 0))
    return pl.pallas_call(
        functools.partial(_gdn_sample_kernel, tq=tq),
        grid=(B // bt,),
        in_specs=[blk(kq_s), blk(v_s), blk(gb_s), blk(gz_s), blk(s0), pl.BlockSpec((1, HEAD_DIM), lambda i: (0, 0))],
        out_specs=[blk(v_s), blk(s0)],
        out_shape=[jax.ShapeDtypeStruct(v_s.shape, F32), jax.ShapeDtypeStruct(s0.shape, F32)],
        compiler_params=_cp(("parallel",)), name="gdn_sample",
    )(kq_s, v_s, gb_s, gz_s, s0, norm_g.reshape(1, -1))


def _mix_out_kernel(ya_ref, yb_ref, yc_ref, x_ref, onb_ref, w_ref, o_ref, mix_ref):
    @pl.when(pl.program_id(1) == 0)
    def _():
        mix_ref[:, 0:512] = ya_ref[...]
        mix_ref[:, 512:1536] = _rms(yb_ref[...], onb_ref[...]).astype(BF16)
        mix_ref[:, 1536:2048] = yc_ref[...]

    o_ref[...] = x_ref[...] + jnp.dot(mix_ref[...], w_ref[...], preferred_element_type=F32)


def _mix_out(ya, yb, yc, x, onb, w, *, tn=512):
    M, N = x.shape
    tm = _row_tile(M)
    K = w.shape[0]
    return pl.pallas_call(
        _mix_out_kernel,
        grid=(M // tm, N // tn),
        in_specs=[pl.BlockSpec((tm, 512), lambda i, j: (i, 0)),
                  pl.BlockSpec((tm, 1024), lambda i, j: (i, 0)),
                  pl.BlockSpec((tm, 512), lambda i, j: (i, 0)),
                  pl.BlockSpec((tm, tn), lambda i, j: (i, j)),
                  pl.BlockSpec((1, 1024), lambda i, j: (0, 0)),
                  pl.BlockSpec((K, tn), lambda i, j: (0, j))],
        out_specs=pl.BlockSpec((tm, tn), lambda i, j: (i, j)),
        out_shape=jax.ShapeDtypeStruct((M, N), F32),
        scratch_shapes=[pltpu.VMEM((tm, K), BF16)],
        compiler_params=_cp(("parallel", "arbitrary")), name="mix_out",
    )(ya, yb, yc, x, onb.reshape(1, -1), w)


def _ffn_mid_kernel(a_ref, v_ref, ap_ref, vp_ref, ha_ref, hv_ref, wa_ref, wv_ref, o_ref, *, tstride, tiles_per_seq):
    tc = wa_ref.shape[1]

    def conv(x_ref, prev_ref, hist_ref, w_ref):
        x = _ld(x_ref, 0, tc)
        if tstride == 1:
            start = (pl.program_id(0) % tiles_per_seq) == 0
            halo = jnp.where(start, hist_ref[0], prev_ref[...])
        else:
            halo = _ld(hist_ref, 0, tc)
        w = w_ref[...]
        return w[0:1] * _shift(x, halo, 2, tstride) + w[1:2] * _shift(x, halo, 1, tstride) + w[2:3] * x

    a = conv(a_ref, ap_ref, ha_ref, wa_ref)
    v = conv(v_ref, vp_ref, hv_ref, wv_ref)
    _st(o_ref, 0, tc, a * _sigmoid(a) * v)


def _ffn_mid(up, hist, w, *, sample, nseq, seq_len, tc=512):
    M, F2 = up.shape
    F = F2 // 2
    nj = F // tc
    wspec_a = pl.BlockSpec((3, tc), lambda i, j: (0, j))
    wspec_v = pl.BlockSpec((3, tc), lambda i, j: (0, j + nj))
    if not sample:
        tm = 512
        tps = seq_len // tm
        prev = lambda off: pl.BlockSpec(
            (SUBLANES, tc), lambda i, j, off=off: (jnp.maximum(i * (tm // SUBLANES) - 1, 0), j + off))
        hspec = lambda off: pl.BlockSpec((1, SUBLANES, tc), lambda i, j, off=off: (i // tps, 0, j + off))
        return pl.pallas_call(
            functools.partial(_ffn_mid_kernel, tstride=1, tiles_per_seq=tps),
            grid=(M // tm, nj),
            in_specs=[pl.BlockSpec((tm, tc), lambda i, j: (i, j)), pl.BlockSpec((tm, tc), lambda i, j: (i, j + nj)),
                      prev(0), prev(nj), hspec(0), hspec(nj), wspec_a, wspec_v],
            out_specs=pl.BlockSpec((tm, tc), lambda i, j: (i, j)),
            out_shape=jax.ShapeDtypeStruct((M, F), BF16),
            compiler_params=_cp(("parallel", "parallel")), name="ffn_mid_prompt",
        )(up, up, up, up, hist, hist, w, w)
    T = seq_len
    up3 = up.reshape(T, nseq, F2)
    hk = hist.shape[0]
    out = pl.pallas_call(
        functools.partial(_ffn_mid_kernel, tstride=nseq, tiles_per_seq=1),
        grid=(1, nj),
        in_specs=[pl.BlockSpec((T, nseq, tc), lambda i, j: (0, 0, j)),
                  pl.BlockSpec((T, nseq, tc), lambda i, j: (0, 0, j + nj)),
                  pl.BlockSpec((SUBLANES, tc), lambda i, j: (0, j)), pl.BlockSpec((SUBLANES, tc), lambda i, j: (0, j + nj)),
                  pl.BlockSpec((hk, nseq, tc), lambda i, j: (0, 0, j)),
                  pl.BlockSpec((hk, nseq, tc), lambda i, j: (0, 0, j + nj)), wspec_a, wspec_v],
        out_specs=pl.BlockSpec((T, nseq, tc), lambda i, j: (0, 0, j)),
        out_shape=jax.ShapeDtypeStruct((T, nseq, F), BF16),
        compiler_params=_cp(("parallel", "parallel")), name="ffn_mid_sample",
    )(up3, up3, up, up, hist, hist, w, w)
    return out.reshape(M, F)


def _matmul_res_kernel(a_ref, w_ref, r_ref, o_ref):
    o_ref[...] = r_ref[...] + jnp.dot(a_ref[...], w_ref[...], preferred_element_type=F32)


def _matmul_res(a, w, res, *, tm=512, tn=512):
    M, K = a.shape
    N = w.shape[1]
    tm = min(tm, M)
    return pl.pallas_call(
        _matmul_res_kernel,
        grid=(M // tm, N // tn),
        in_specs=[pl.BlockSpec((tm, K), lambda i, j: (i, 0)),
                  pl.BlockSpec((K, tn), lambda i, j: (0, j)),
                  pl.BlockSpec((tm, tn), lambda i, j: (i, j))],
        out_specs=pl.BlockSpec((tm, tn), lambda i, j: (i, j)),
        out_shape=jax.ShapeDtypeStruct((M, N), F32),
        compiler_params=_cp(("parallel", "parallel")), name="matmul_res",
    )(a, w, res)


def _layer_weights(l, p):
    w_in = p["w_in"][l]
    w_main = jnp.concatenate([w_in[:, :4096], w_in[:, 4120:5656], w_in[:, 5664:6176]], axis=1).astype(BF16)
    w_small = jnp.concatenate([w_in[:, 4096:4120], w_in[:, 5656:5664],
                               jnp.zeros((w_in.shape[0], LANES - 32), w_in.dtype)], axis=1).astype(BF16)
    lane_vec = lambda v: jnp.zeros((1, LANES), F32).at[0, S_GA:S_GA + GDN_HEADS].set(v)
    return {
        "norm_mix_g": p["norm_mix_g"][l], "w_main": w_main, "w_small": w_small,
        "conv_a_w": p["conv_a_w"][l], "q_norm_g": p["q_norm_g"][l], "k_norm_g": p["k_norm_g"][l],
        "cmp_pe": p["cmp_pe"][l],
        "cmp_w1": p["cmp_w1"][l].reshape(2, 2, CMP_BLOCK // 2 * HEAD_DIM, CMP_HIDDEN).astype(BF16),
        "cmp_w2": p["cmp_w2"][l].astype(BF16),
        "gdn_conv_w": p["gdn_conv_w"][l], "alog_v": lane_vec(p["gdn_a_log"][l]), "dtb_v": lane_vec(p["gdn_dt_bias"][l]),
        "gdn_norm_g": p["gdn_norm_g"][l], "out_norm_a": p["out_norm_a"][l], "out_norm_b": p["out_norm_b"][l],
        "w_out": p["w_out"][l].astype(BF16), "norm_ffn_g": p["norm_ffn_g"][l],
        "ffn_up": p["ffn_up"][l].astype(BF16), "ffn_conv_w": p["ffn_conv_w"][l],
        "ffn_down": p["ffn_down"][l].astype(BF16),
    }


HALO_ROWS = 16


def _ffn_up_kernel(x_ref, xp_ref, g_ref, wa_ref, wv_ref, ha_ref, hv_ref, ca_ref, cv_ref,
                   o_ref, hoa_ref, hov_ref, xn_ref, *, tstride, tiles_per_seq):
    tn = wa_ref.shape[1]
    prompt = tstride == 1

    @pl.when(pl.program_id(1) == 0)
    def _():
        g = g_ref[...]
        if prompt:
            xn_ref[0:HALO_ROWS] = _rms(xp_ref[...], g).astype(BF16)
            xn_ref[HALO_ROWS:] = _rms(x_ref[...], g).astype(BF16)
        else:
            xn_ref[...] = _rms(x_ref[...], g).astype(BF16)

    xn = xn_ref[...]

    def part(w_ref, hist_ref, cw_ref, ho_ref):
        up = jnp.dot(xn, w_ref[...], preferred_element_type=F32)
        if prompt:
            start = (pl.program_id(0) % tiles_per_seq) == 0
            halo = jnp.where(start, hist_ref[0], up[HALO_ROWS - SUBLANES:HALO_ROWS])
            x = up[HALO_ROWS:]
            ho_ref[0] = x[x.shape[0] - SUBLANES:]
        else:
            halo = _ld(hist_ref, 0, tn)
            x = up
            _st(ho_ref, 0, tn, x[x.shape[0] - halo.shape[0]:])
        w = cw_ref[...]
        return w[0:1] * _shift(x, halo, 2, tstride) + w[1:2] * _shift(x, halo, 1, tstride) + w[2:3] * x

    a = part(wa_ref, ha_ref, ca_ref, hoa_ref)
    v = part(wv_ref, hv_ref, cv_ref, hov_ref)
    o_ref[...] = (a * _sigmoid(a) * v).astype(o_ref.dtype)


def _ffn_up(h, g, w_up, conv_w, hist, *, sample, nseq, seq_len, tn=512):
    M, K = h.shape
    tm = _row_tile(seq_len)
    F = w_up.shape[1] // 2
    nj = F // tn
    common = [pl.BlockSpec((1, K), lambda i, j: (0, 0)),
              pl.BlockSpec((K, tn), lambda i, j: (0, j)),
              pl.BlockSpec((K, tn), lambda i, j: (0, j + nj))]
    cw = [pl.BlockSpec((3, tn), lambda i, j: (0, j)), pl.BlockSpec((3, tn), lambda i, j: (0, j + nj))]
    if not sample:
        tps = seq_len // tm
        hspec = lambda off: pl.BlockSpec((1, SUBLANES, tn), lambda i, j, off=off: (i // tps, 0, j + off))
        hout = pl.BlockSpec((1, SUBLANES, tn), lambda i, j: (i, 0, j))
        act, tail_a, tail_v = pl.pallas_call(
            functools.partial(_ffn_up_kernel, tstride=1, tiles_per_seq=tps),
            grid=(M // tm, nj),
            in_specs=[pl.BlockSpec((tm, K), lambda i, j: (i, 0)),
                      pl.BlockSpec((HALO_ROWS, K), lambda i, j: (jnp.maximum(i * (tm // HALO_ROWS) - 1, 0), 0))]
            + common + [hspec(0), hspec(nj)] + cw,
            out_specs=[pl.BlockSpec((tm, tn), lambda i, j: (i, j)), hout, hout],
            out_shape=[jax.ShapeDtypeStruct((M, F), BF16)] + [jax.ShapeDtypeStruct((M // tm, SUBLANES, F), F32)] * 2,
            scratch_shapes=[pltpu.VMEM((tm + HALO_ROWS, K), BF16)],
            compiler_params=_cp(("arbitrary", "arbitrary")), name="ffn_up_prompt",
        )(h, h, g.reshape(1, K), w_up, w_up, hist, hist, conv_w, conv_w)
        return act, tail_a[tps - 1::tps], tail_v[tps - 1::tps]
    hk = hist.shape[0]
    hspec = lambda off: pl.BlockSpec((hk, nseq, tn), lambda i, j, off=off: (0, 0, j + off))
    hout = pl.BlockSpec((hk, nseq, tn), lambda i, j: (0, 0, j))
    return pl.pallas_call(
        functools.partial(_ffn_up_kernel, tstride=nseq, tiles_per_seq=1),
        grid=(1, nj),
        in_specs=[pl.BlockSpec((M, K), lambda i, j: (0, 0)), pl.BlockSpec((HALO_ROWS, K), lambda i, j: (0, 0))]
        + common + [hspec(0), hspec(nj)] + cw,
        out_specs=[pl.BlockSpec((M, tn), lambda i, j: (0, j)), hout, hout],
        out_shape=[jax.ShapeDtypeStruct((M, F), BF16)] + [jax.ShapeDtypeStruct((hk, nseq, F), F32)] * 2,
        scratch_shapes=[pltpu.VMEM((M, K), BF16)],
        compiler_params=_cp(("arbitrary", "arbitrary")), name="ffn_up_sample",
    )(h, h, g.reshape(1, K), w_up, w_up, hist, hist, conv_w, conv_w)


def _dense_tail(x2, ya, yb, yc, lw, hist_ffn, *, sample, nseq, seq_len):
    h = _mix_out(ya, yb, yc, x2, lw["out_norm_b"], lw["w_out"])
    act, hist_a, hist_v = _ffn_up(h, lw["norm_ffn_g"], lw["ffn_up"], lw["ffn_conv_w"], hist_ffn,
                                  sample=sample, nseq=nseq, seq_len=seq_len)
    return _matmul_res(act, lw["ffn_down"], h), jnp.concatenate([hist_a, hist_v], axis=-1)


def _prompt_layer(x2, lw, bias_tiles, bias_cmp, *, B, T):
    M = B * T
    G, R = NSA_KV_HEADS, NSA_REP
    z, zs = _norm_matmul(x2, lw["norm_mix_g"], lw["w_main"], lw["w_small"])
    hu0 = jnp.zeros((B, SUBLANES, 512), F32)
    hq0 = jnp.zeros((B, SUBLANES, 1536), F32)
    ya, q, rows, win, kvb, gates, gq, ggb, hu, rows_lin = _prep(z, zs, hu0, hq0, lw, sample=False, nseq=B, seq_len=T)
    cmp_kv = _compress_prompt(rows.reshape(B, T, 1024), lw)
    gates_g = gates[:, :3 * G * R].reshape(M, 3, G, R).transpose(2, 0, 1, 3).reshape(G, M, 3 * R)
    gates_g = jnp.pad(gates_g, ((0, 0), (0, 0), (0, LANES - 3 * R)))
    yb = _nsa_prompt(q, cmp_kv, kvb.reshape(B, T, 1024), gates_g, bias_tiles, bias_cmp, B=B, T=T)
    s0 = jnp.zeros((B, GDN_HEADS, HEAD_DIM, HEAD_DIM), F32)
    yc, s_new = _gdn_prompt(gq.reshape(B, T, 1536), ggb.reshape(B, T, 1024), z.reshape(B, T, C_END), s0,
                            lw["gdn_norm_g"])
    hf0 = jnp.zeros((B, SUBLANES, lw["ffn_conv_w"].shape[1]), F32)
    out, up = _dense_tail(x2, ya, yb, yc.reshape(M, 512), lw, hf0, sample=False, nseq=B, seq_len=T)
    return out, rows_lin, win, hu, z, s_new, up


def _sample_layer(l, x2, lw, st, bias_c, bias_s, bias_w, *, B, T):
    M = B * T
    G, R, H, hd = NSA_KV_HEADS, NSA_REP, GDN_HEADS, HEAD_DIM
    z, zs = _norm_matmul(x2, lw["norm_mix_g"], lw["w_main"], lw["w_small"])
    hu0 = jnp.swapaxes(st["conv_a"], 0, 1)
    hq0 = jnp.swapaxes(st["gdn_conv"], 0, 1)
    ya, q, rows, win, kvb, gates, gq, ggb, hu = _prep(z, zs, hu0, hq0, lw, sample=True, nseq=B, seq_len=T)
    tb = lambda a: a.reshape(T, B, -1)
    q_s = tb(q).reshape(T, B, G, R, hd).transpose(1, 2, 3, 0, 4).reshape(B, G, R * T, hd)
    gt = tb(gates)[:, :, :3 * G * R].reshape(T, B, 3, G, R).transpose(1, 3, 4, 0, 2).reshape(B, G, R * T, 3)
    gates_s = jnp.pad(gt, ((0, 0), (0, 0), (0, 0), (0, LANES - 3)))
    kvn_s = jnp.pad(jnp.swapaxes(tb(kvb), 0, 1), ((0, 0), (0, SUBLANES - T), (0, 0)))
    win_new = jnp.swapaxes(tb(win), 0, 1).reshape(B, T * 2 * G, hd)
    yb_s, win = _nsa_sample(l, st["page_table"], st["cache_lin"], st["win_lin"], win_new, q_s, kvn_s, gates_s, lw,
                            bias_c, bias_s, bias_w, tq=T)
    yb = yb_s.reshape(B, G, R, T, hd).transpose(3, 0, 1, 2, 4).reshape(M, G * R * hd)
    gq4 = tb(gq).reshape(T, B, 3, H, hd)
    bh = lambda a: a.transpose(1, 2, 0, 3)
    padt = lambda a: jnp.pad(a, ((0, 0), (0, 0), (0, SUBLANES - T), (0, 0)))
    kq_s = jnp.concatenate([bh(gq4[:, :, 1]), bh(gq4[:, :, 0])], axis=2)
    v_s = padt(bh(gq4[:, :, 2]))
    ggb4 = tb(ggb).reshape(T, B, 2, H, hd)
    gb_s = jnp.concatenate([bh(ggb4[:, :, 0]), bh(ggb4[:, :, 1])], axis=2)
    gz_s = padt(bh(tb(z)[:, :, C_GZ:C_END].reshape(T, B, H, hd)))
    yc_s, s_new = _gdn_sample(kq_s, v_s, gb_s, gz_s, st["gdn"], lw["gdn_norm_g"], tq=T)
    yc = yc_s[:, :, :T].transpose(2, 0, 1, 3).reshape(M, H * hd).astype(BF16)
    hf0 = jnp.swapaxes(st["ffn_conv"], 0, 1)
    out, up = _dense_tail(x2, ya, yb, yc, lw, hf0, sample=True, nseq=B, seq_len=T)
    return out, rows, win, hu, z, s_new, up


def kernel(x_prompt, x_sample, cache_kv, cache_win, state_conv_a, state_gdn_conv, state_gdn, state_ffn_conv,
           page_table, rel_bias, norm_mix_g, w_in, conv_a_w, q_norm_g, k_norm_g, cmp_pe, cmp_w1, cmp_w2,
           gdn_conv_w, gdn_a_log, gdn_dt_bias, gdn_norm_g, out_norm_a, out_norm_b, w_out, norm_ffn_g,
           ffn_up, ffn_conv_w, ffn_down):
    params = dict(norm_mix_g=norm_mix_g, w_in=w_in, conv_a_w=conv_a_w, q_norm_g=q_norm_g, k_norm_g=k_norm_g,
                  cmp_pe=cmp_pe, cmp_w1=cmp_w1, cmp_w2=cmp_w2, gdn_conv_w=gdn_conv_w, gdn_a_log=gdn_a_log,
                  gdn_dt_bias=gdn_dt_bias, gdn_norm_g=gdn_norm_g, out_norm_a=out_norm_a, out_norm_b=out_norm_b,
                  w_out=w_out, norm_ffn_g=norm_ffn_g, ffn_up=ffn_up, ffn_conv_w=ffn_conv_w, ffn_down=ffn_down)
    depth = w_in.shape[0]
    Bp, T, D = x_prompt.shape
    Bs, Ts, _ = x_sample.shape
    G, R, hd = NSA_KV_HEADS, NSA_REP, HEAD_DIM
    n_pool, page = cache_kv.shape[1], cache_kv.shape[2]
    npages = page_table.shape[1]
    past = npages * page
    wb = cache_win.shape[2]
    L = past + Ts
    assert ((L - CMP_BLOCK) // CMP_STRIDE) * CMP_STRIDE + CMP_BLOCK <= past, "compressed blocks must lie in the cache"
    assert T % Q_BLOCK == 0 and T % 512 == 0 and 3 <= Ts <= SUBLANES // 2

    thr = _bucket_thresholds()
    rb_flat = rel_bias.reshape(-1)
    nqb = T // Q_BLOCK
    bias_tiles = _bias_table(thr, rb_flat, nqb, Q_BLOCK, Q_BLOCK, a0=0, an=Q_BLOCK, qs=1, ks=1)
    bias_cmp = _bias_table(thr, rb_flat, nqb, Q_BLOCK, T // CMP_STRIDE,
                           a0=-(CMP_BLOCK - 1), an=Q_BLOCK, qs=1, ks=CMP_STRIDE)

    def sample_bias(ncols, a0, ks):
        t = _bias_table(thr, rb_flat, 1, SUBLANES, ncols, a0=a0, an=0, qs=1, ks=ks)[0]
        return t[:, :Ts].reshape(G, R * Ts, ncols)

    bias_c = sample_bias(past // CMP_STRIDE, past - (CMP_BLOCK - 1), CMP_STRIDE)
    bias_s = sample_bias(past + LANES, past, 1)
    bias_w = sample_bias(wb + LANES, wb, 1)

    cache_lin = cache_kv.reshape(depth, n_pool, page * 4 * G, hd)
    win_lin = cache_win.reshape(depth, Bs, wb * 2 * G, hd)
    assert wb == WINDOW, "the new window buffer is the old one shifted by the new rows"

    xp = x_prompt.reshape(Bp * T, D)
    xs = jnp.swapaxes(x_sample, 0, 1).reshape(Ts * Bs, D)
    outs_p, outs_s = [], []
    for l in range(depth):
        lw = _layer_weights(l, params)
        xp, rows, win, hu, z, s_new, up = _prompt_layer(xp, lw, bias_tiles, bias_cmp, B=Bp, T=T)
        wl = min(WINDOW, T)
        outs_p.append((
            rows.reshape(Bp, T, 4, G, hd),
            win.reshape(Bp, T, 2, G, hd)[:, T - wl:],
            hu[:, SUBLANES - 2:],
            z.reshape(Bp, T, C_END)[:, T - 3:, C_GQKV:C_GZ],
            s_new,
            up[:, SUBLANES - 2:]))
        st = dict(page_table=page_table, cache_lin=cache_lin, win_lin=win_lin, conv_a=state_conv_a[l],
                  gdn_conv=state_gdn_conv[l], gdn=state_gdn[l], ffn_conv=state_ffn_conv[l])
        xs, rows, win, hu, z, s_new, up = _sample_layer(l, xs, lw, st, bias_c, bias_s, bias_w, B=Bs, T=Ts)
        tb = lambda a: jnp.swapaxes(a.reshape(Ts, Bs, -1), 0, 1)
        outs_s.append((
            tb(rows).reshape(Bs, Ts, 4, G, hd),
            win[0].reshape(Bs, wb, 2, G, hd),
            jnp.swapaxes(hu, 0, 1),
            tb(z)[:, Ts - 3:, C_GQKV:C_GZ],
            s_new,
            jnp.swapaxes(up, 0, 1)))
    y_p = xp.reshape(Bp, T, D)
    y_s = jnp.swapaxes(xs.reshape(Ts, Bs, D), 0, 1)
    stack = lambda outs, i: jnp.stack([o[i] for o in outs])
    return (y_p, y_s, stack(outs_p, 0), stack(outs_s, 0), stack(outs_p, 1), stack(outs_s, 1),
            stack(outs_p, 2), stack(outs_s, 2), stack(outs_p, 3), stack(outs_s, 3),
            stack(outs_p, 4), stack(outs_s, 4), stack(outs_p, 5), stack(outs_s, 5))
```

```python
import functools
import math

import jax
import jax.numpy as jnp
from jax import lax
from jax.experimental import pallas as pl
from jax.experimental.pallas import tpu as pltpu

F32 = jnp.float32
BF16 = jnp.bfloat16
HIGHEST = lax.Precision.HIGHEST

SUBLANES = 8
LANES = 128
VMEM_LIMIT_BYTES = 52 * 1024 * 1024

HEAD_DIM = 128
NSA_KV_HEADS = 2
NSA_REP = 4
NSA_HEADS = NSA_KV_HEADS * NSA_REP
GDN_HEADS = 4
CMP_BLOCK = 32
CMP_STRIDE = 16
CMP_HIDDEN = 256
SLC_BLOCK = 64
N_SELECT = 16
WINDOW = 512
Q_BLOCK = 128
GDN_CHUNK = 64
REL_BUCKETS = 32
REL_MAX_DIST = 1024
EPS = 1e-6
NEG_INF = -1e30
FORCE_SCORE = 1e4

C_AB, C_AC, C_AH, C_NQ, C_NKV, C_GQKV, C_GZ, C_END = 0, 512, 1024, 1536, 2560, 4096, 5632, 6144
S_GATE, S_GA, S_GB = 0, 24, 28


def _cp(sem):
    return pltpu.CompilerParams(dimension_semantics=sem, vmem_limit_bytes=VMEM_LIMIT_BYTES)


class _LayerSlot:
    def __init__(self, layer, depth, prev):
        self.layer, self.depth, self.prev = layer, depth, prev

    def in_specs(self):
        return [] if self.prev is None else [pl.BlockSpec(memory_space=pl.ANY)]

    def args(self):
        return [] if self.prev is None else [self.prev]

    def aliases(self, n_inputs_before, out_index):
        return {} if self.prev is None else {n_inputs_before: out_index}

    def wrap(self, kernel, n_refs_before):
        if self.prev is None:
            return kernel

        def wrapped(*refs):
            return kernel(*refs[:n_refs_before], *refs[n_refs_before + 1:])
        return wrapped


def _div_pow2(x, d):
    assert d & (d - 1) == 0
    return jnp.right_shift(x, d.bit_length() - 1)


def _mod_pow2(x, d):
    assert d & (d - 1) == 0
    return jnp.bitwise_and(x, d - 1)


def _sigmoid(x):
    return 1.0 / (1.0 + jnp.exp(-x))


def _rms(x, g):
    return x * lax.rsqrt(jnp.mean(x * x, axis=-1, keepdims=True) + EPS) * g


def _l2n(x):
    return x * lax.rsqrt(jnp.sum(x * x, axis=-1, keepdims=True) + EPS)


def _dot_t(a, b, precision=None):
    return lax.dot_general(a, b, (((1,), (1,)), ((), ())), precision=precision, preferred_element_type=F32)


def _ld(ref, c0, c1):
    if len(ref.shape) == 3:
        v = ref[:, :, c0:c1]
        return v.reshape(v.shape[0] * v.shape[1], v.shape[2])
    return ref[:, c0:c1]


def _st(ref, c0, c1, val):
    if len(ref.shape) == 3:
        ref[:, :, c0:c1] = val.reshape(ref.shape[0], ref.shape[1], c1 - c0).astype(ref.dtype)
    else:
        ref[:, c0:c1] = val.astype(ref.dtype)


def _shift(x, halo, s, tstride):
    n = s * tstride
    rows = x.shape[0]
    if tstride % SUBLANES == 0:
        hr = halo.shape[0]
        return jnp.concatenate([halo[hr - n:], x[:rows - n]], axis=0)
    xs = pltpu.roll(x, n, 0)
    hs = pltpu.roll(halo, n, 0)
    rid = lax.broadcasted_iota(jnp.int32, (SUBLANES, x.shape[1]), 0)
    head = jnp.where(rid < n, hs, xs[:SUBLANES])
    return jnp.concatenate([head, xs[SUBLANES:]], axis=0)


def _norm_matmul_kernel(x_ref, g_ref, w_ref, *rest, has_small):
    if has_small:
        ws_ref, o_ref, os_ref, xn_ref = rest
    else:
        o_ref, xn_ref = rest

    @pl.when(pl.program_id(1) == 0)
    def _():
        xn = _rms(x_ref[...], g_ref[...]).astype(BF16)
        xn_ref[...] = xn
        if has_small:
            os_ref[...] = jnp.dot(xn, ws_ref[...], preferred_element_type=F32)

    o_ref[...] = jnp.dot(xn_ref[...], w_ref[...], preferred_element_type=F32).astype(o_ref.dtype)


def _row_tile(rows):
    for tm in (1024, 512):
        if rows % tm == 0:
            return tm
    return rows


def _norm_matmul(x, g, w, w_small=None, *, tn=512, out_dtype=F32):
    M, K = x.shape
    N = w.shape[1]
    tm = _row_tile(M)
    has_small = w_small is not None
    in_specs = [pl.BlockSpec((tm, K), lambda i, j: (i, 0)),
                pl.BlockSpec((1, K), lambda i, j: (0, 0)),
                pl.BlockSpec((K, tn), lambda i, j: (0, j))]
    out_specs = [pl.BlockSpec((tm, tn), lambda i, j: (i, j))]
    out_shape = [jax.ShapeDtypeStruct((M, N), out_dtype)]
    args = [x, g.reshape(1, K), w]
    if has_small:
        in_specs.append(pl.BlockSpec((K, LANES), lambda i, j: (0, 0)))
        out_specs.append(pl.BlockSpec((tm, LANES), lambda i, j: (i, 0)))
        out_shape.append(jax.ShapeDtypeStruct((M, LANES), F32))
        args.append(w_small)
    res = pl.pallas_call(
        functools.partial(_norm_matmul_kernel, has_small=has_small),
        grid=(M // tm, N // tn),
        in_specs=in_specs, out_specs=out_specs, out_shape=out_shape,
        scratch_shapes=[pltpu.VMEM((tm, K), BF16)],
        compiler_params=_cp(("parallel", "arbitrary")),
        name="norm_matmul_small" if has_small else "norm_matmul",
    )(*args)
    return res if has_small else res[0]


def _prep_kernel(z_ref, zp_ref, zs_ref, hu_ref, hq_ref, caw_ref, gcw_ref, qg_ref, kg_ref, ona_ref,
                 alog_ref, dtb_ref,
                 ya_ref, q_ref, rows_ref, win_ref, kvb_ref, gates_ref, gq_ref, ggb_ref, huo_ref, *lin_refs,
                 tstride, tiles_per_seq):
    hd = HEAD_DIM

    def put_rows(grp, val):
        _st(rows_ref, grp * hd, (grp + 1) * hd, val)
        if lin_refs:
            lin_refs[0][0, pl.ds(grp, val.shape[0], stride=4 * NSA_KV_HEADS), :] = val
    u = _ld(z_ref, C_AC, C_AH) * _ld(z_ref, C_AH, C_NQ)
    if tstride == 1:
        start = (pl.program_id(0) % tiles_per_seq) == 0
        halo_u = jnp.where(start, hu_ref[0], zp_ref[:, C_AC:C_AH] * zp_ref[:, C_AH:C_NQ])
        halo_q = jnp.where(start, hq_ref[0], zp_ref[:, C_GQKV:C_GZ])
    else:
        halo_u = _ld(hu_ref, 0, 512)
        halo_q = _ld(hq_ref, 0, 1536)
    caw = caw_ref[...]
    conv = caw[0:1] * _shift(u, halo_u, 2, tstride) + caw[1:2] * _shift(u, halo_u, 1, tstride) + caw[2:3] * u
    y = _ld(z_ref, C_AB, C_AC) * conv
    _st(ya_ref, 0, 512, _rms(y, ona_ref[...]))
    hu_rows = huo_ref.shape[0] * huo_ref.shape[1] if tstride != 1 else SUBLANES
    if tstride == 1:
        huo_ref[0] = u[u.shape[0] - hu_rows:]
    else:
        _st(huo_ref, 0, 512, u[u.shape[0] - hu_rows:])

    qg = qg_ref[...]
    kg = kg_ref[...]
    for h in range(NSA_HEADS):
        c0 = C_NQ + h * hd
        _st(q_ref, h * hd, (h + 1) * hd, _rms(_ld(z_ref, c0, c0 + hd), qg) * (hd ** -0.5))
    for grp in range(4):
        put_rows(grp, _ld(z_ref, C_NKV + grp * hd, C_NKV + (grp + 1) * hd))
    for g in range(NSA_KV_HEADS):
        c0 = C_NKV + 512 + g * hd
        kn = _rms(_ld(z_ref, c0, c0 + hd), kg)
        put_rows(4 + g, kn)
        _st(kvb_ref, g * hd, (g + 1) * hd, kn)
    for g in range(NSA_KV_HEADS):
        c0 = C_NKV + 768 + g * hd
        vs = _ld(z_ref, c0, c0 + hd)
        put_rows(6 + g, vs)
        _st(kvb_ref, 256 + g * hd, 256 + (g + 1) * hd, vs)
    for g in range(NSA_KV_HEADS):
        c0 = C_NKV + 1024 + g * hd
        kn = _rms(_ld(z_ref, c0, c0 + hd), kg)
        _st(win_ref, g * hd, (g + 1) * hd, kn)
        _st(kvb_ref, 512 + g * hd, 512 + (g + 1) * hd, kn)
    vw = _ld(z_ref, C_NKV + 1280, C_NKV + 1536)
    _st(win_ref, 256, 512, vw)
    _st(kvb_ref, 768, 1024, vw)

    zs = _ld(zs_ref, 0, LANES)
    sg = _sigmoid(zs)
    _st(gates_ref, 0, LANES, sg)
    xs = zs + dtb_ref[...]
    softplus = jnp.maximum(xs, 0.0) + jnp.log(1.0 + jnp.exp(-jnp.abs(xs)))
    gdec = -jnp.exp(alog_ref[...]) * softplus
    rows = zs.shape[0]
    if tstride == 1:
        in_chunk = _mod_pow2(lax.broadcasted_iota(jnp.int32, gdec.shape, 0), GDN_CHUNK)
        step = 1
        while step < GDN_CHUNK:
            gdec = gdec + jnp.where(in_chunk >= step, pltpu.roll(gdec, step, 0), 0.0)
            step *= 2
    for h in range(GDN_HEADS):
        _st(ggb_ref, h * hd, (h + 1) * hd, jnp.broadcast_to(gdec[:, S_GA + h:S_GA + h + 1], (rows, hd)))
        _st(ggb_ref, 512 + h * hd, 512 + (h + 1) * hd, jnp.broadcast_to(sg[:, S_GB + h:S_GB + h + 1], (rows, hd)))

    gcw = gcw_ref[...]
    for part in range(3):
        c0 = C_GQKV + part * 512
        x = _ld(z_ref, c0, c0 + 512)
        hq = halo_q[:, part * 512:(part + 1) * 512]
        w = gcw[:, part * 512:(part + 1) * 512]
        c = (w[0:1] * _shift(x, hq, 3, tstride) + w[1:2] * _shift(x, hq, 2, tstride)
             + w[2:3] * _shift(x, hq, 1, tstride) + w[3:4] * x)
        c = c * _sigmoid(c)
        if part == 2:
            _st(gq_ref, 1024, 1536, c)
        else:
            for h in range(GDN_HEADS):
                v = _l2n(c[:, h * hd:(h + 1) * hd])
                if part == 0:
                    v = v * (hd ** -0.5)
                _st(gq_ref, part * 512 + h * hd, part * 512 + (h + 1) * hd, v)


def _prep(z, zs, hist_u, hist_q, lw, *, sample, nseq, seq_len, slot=None):
    M = z.shape[0]
    wts = [lw["conv_a_w"], lw["gdn_conv_w"], lw["q_norm_g"].reshape(1, -1), lw["k_norm_g"].reshape(1, -1),
           lw["out_norm_a"].reshape(1, -1), lw["alog_v"], lw["dtb_v"]]
    wspecs2 = [pl.BlockSpec(w.shape, lambda i: (0, 0)) for w in wts]
    widths = [(512, BF16), (1024, BF16), (1024, F32), (512, F32), (1024, BF16), (LANES, F32), (1536, F32), (1024, F32)]
    if not sample:
        tm = 256
        tps = seq_len // tm
        in_specs = [pl.BlockSpec((tm, C_END), lambda i: (i, 0)),
                    pl.BlockSpec((SUBLANES, C_END), lambda i: (jnp.maximum(i * (tm // SUBLANES) - 1, 0), 0)),
                    pl.BlockSpec((tm, LANES), lambda i: (i, 0)),
                    pl.BlockSpec((1, SUBLANES, 512), lambda i: (i // tps, 0, 0)),
                    pl.BlockSpec((1, SUBLANES, 1536), lambda i: (i // tps, 0, 0))] + wspecs2
        out_specs = [pl.BlockSpec((tm, w), lambda i: (i, 0)) for w, _ in widths]
        out_specs.append(pl.BlockSpec((1, SUBLANES, 512), lambda i: (i // tps, 0, 0)))
        out_shape = [jax.ShapeDtypeStruct((M, w), d) for w, d in widths]
        out_shape.append(jax.ShapeDtypeStruct((nseq, SUBLANES, 512), F32))
        kvr = 4 * NSA_KV_HEADS
        layer = slot.layer
        out_specs.append(pl.BlockSpec((1, tm * kvr, HEAD_DIM), lambda i: (layer, i, 0)))
        out_shape.append(jax.ShapeDtypeStruct((slot.depth, M * kvr, HEAD_DIM), F32))
        n_in = len(in_specs)
        return pl.pallas_call(
            slot.wrap(functools.partial(_prep_kernel, tstride=1, tiles_per_seq=tps), n_in),
            grid=(M // tm,), in_specs=in_specs + slot.in_specs(), out_specs=out_specs, out_shape=out_shape,
            input_output_aliases=slot.aliases(n_in, len(out_shape) - 1),
            compiler_params=_cp(("arbitrary",)), name="prep_prompt",
        )(z, z, zs, hist_u, hist_q, *wts, *slot.args())
    T = seq_len
    bt = 64
    z3 = z.reshape(T, nseq, C_END)
    zs3 = zs.reshape(T, nseq, LANES)
    in_specs = [pl.BlockSpec((T, bt, C_END), lambda i: (0, i, 0)),
                pl.BlockSpec((SUBLANES, C_END), lambda i: (0, 0)),
                pl.BlockSpec((T, bt, LANES), lambda i: (0, i, 0)),
                pl.BlockSpec((hist_u.shape[0], bt, 512), lambda i: (0, i, 0)),
                pl.BlockSpec((hist_q.shape[0], bt, 1536), lambda i: (0, i, 0))] + wspecs2
    out_specs = [pl.BlockSpec((T, bt, w), lambda i: (0, i, 0)) for w, _ in widths]
    out_specs.append(pl.BlockSpec((hist_u.shape[0], bt, 512), lambda i: (0, i, 0)))
    out_shape = [jax.ShapeDtypeStruct((T, nseq, w), d) for w, d in widths]
    out_shape.append(jax.ShapeDtypeStruct((hist_u.shape[0], nseq, 512), F32))
    outs = pl.pallas_call(
        functools.partial(_prep_kernel, tstride=bt, tiles_per_seq=1),
        grid=(nseq // bt,), in_specs=in_specs, out_specs=out_specs, out_shape=out_shape,
        compiler_params=_cp(("arbitrary",)), name="prep_sample",
    )(z3, z, zs3, hist_u, hist_q, *wts)
    return [o.reshape(M, o.shape[-1]) for o in outs[:-1]] + [outs[-1]]


def _bias_kernel(thr_ref, rb_ref, o_ref, *, a0, an, qs, ks):
    n = pl.program_id(0)
    R, C = o_ref.shape[-2:]

    def rows8(rc, carry):
        r0 = pl.multiple_of(rc * SUBLANES, SUBLANES)
        dist = (a0 + an * n + qs * (r0 + lax.broadcasted_iota(jnp.int32, (SUBLANES, C), 0))
                - ks * lax.broadcasted_iota(jnp.int32, (SUBLANES, C), 1))
        b = [jnp.full((SUBLANES, C), rb_ref[h], F32) for h in range(NSA_HEADS)]
        for k in range(1, REL_BUCKETS):
            reached = dist >= thr_ref[k]
            b = [jnp.where(reached, rb_ref[k * NSA_HEADS + h], b[h]) for h in range(NSA_HEADS)]
        for h in range(NSA_HEADS):
            o_ref[0, h, pl.ds(r0, SUBLANES), :] = b[h]
        return carry

    lax.fori_loop(0, R // SUBLANES, rows8, 0)


def _bias_table(thr, rb_flat, n, R, C, *, a0, an, qs, ks):
    return pl.pallas_call(
        functools.partial(_bias_kernel, a0=a0, an=an, qs=qs, ks=ks),
        grid=(n,),
        in_specs=[pl.BlockSpec(memory_space=pltpu.SMEM), pl.BlockSpec(memory_space=pltpu.SMEM)],
        out_specs=pl.BlockSpec((1, NSA_HEADS, R, C), lambda i: (i, 0, 0, 0)),
        out_shape=jax.ShapeDtypeStruct((n, NSA_HEADS, R, C), F32),
        compiler_params=_cp(("parallel",)), name="bias_table",
    )(thr, rb_flat)


def _bucket_thresholds():
    n = jnp.arange(REL_MAX_DIST + 1)
    exact = REL_BUCKETS // 2
    nf = jnp.maximum(n, 1).astype(F32)
    far = exact + (jnp.log(nf / exact) / math.log(REL_MAX_DIST / exact) * (REL_BUCKETS - exact)).astype(jnp.int32)
    bucket = jnp.where(n < exact, n, jnp.minimum(far, REL_BUCKETS - 1))
    return jnp.sum(bucket[None, :] < jnp.arange(REL_BUCKETS)[:, None], axis=1).astype(jnp.int32)


def _gelu_tanh(x):
    return x * (0.5 * (1.0 + jnp.tanh(math.sqrt(2.0 / math.pi) * (x + 0.044715 * (x * x * x)))))


def _compress_math(get_rows, ng, pe, w1_ref, w2):
    half = CMP_BLOCK // 2
    rows = [get_rows(r) for r in range(half)]
    top = jnp.dot(jnp.concatenate([(rows[r] + pe[r:r + 1]).astype(BF16) for r in range(half)], axis=-1),
                  w1_ref[0], preferred_element_type=F32)
    bot = jnp.dot(jnp.concatenate([(rows[r] + pe[r + half:r + half + 1]).astype(BF16) for r in range(half)], axis=-1),
                  w1_ref[1], preferred_element_type=F32)
    h = top + pltpu.roll(bot, ng - 1, 0)
    return jnp.dot(_gelu_tanh(h).astype(BF16), w2, preferred_element_type=F32)


def _compress_kernel(x_ref, pe_ref, w1_ref, w2_ref, kg_ref, o_ref):
    ng = x_ref.shape[1] // CMP_STRIDE
    slot = pl.program_id(1)

    def get_rows(r):
        return x_ref[0, pl.ds(r, ng, stride=CMP_STRIDE), :]

    out = _compress_math(get_rows, ng, pe_ref[0], w1_ref.at[0], w2_ref[0])
    o_ref[0, 0, 0] = jnp.where(slot == 0, _rms(out, kg_ref[...]), out).astype(o_ref.dtype)


def _compress_prompt(rows3, lw):
    B, T, _ = rows3.shape
    ng = T // CMP_STRIDE
    G = NSA_KV_HEADS
    return pl.pallas_call(
        _compress_kernel,
        grid=(B, 2, G),
        in_specs=[pl.BlockSpec((1, T, HEAD_DIM), lambda b, s, g: (b, 0, s * G + g)),
                  pl.BlockSpec((1, CMP_BLOCK, HEAD_DIM), lambda b, s, g: (s, 0, 0)),
                  pl.BlockSpec((1,) + lw["cmp_w1"].shape[1:], lambda b, s, g: (s, 0, 0, 0)),
                  pl.BlockSpec((1, CMP_HIDDEN, HEAD_DIM), lambda b, s, g: (s, 0, 0)),
                  pl.BlockSpec((1, HEAD_DIM), lambda b, s, g: (0, 0))],
        out_specs=pl.BlockSpec((1, 1, 1, ng, HEAD_DIM), lambda b, s, g: (b, s, g, 0, 0)),
        out_shape=jax.ShapeDtypeStruct((B, 2, G, ng, HEAD_DIM), BF16),
        compiler_params=_cp(("parallel", "parallel", "parallel")), name="compress_prompt",
    )(rows3, lw["cmp_pe"], lw["cmp_w1"], lw["cmp_w2"], lw["k_norm_g"].reshape(1, -1))


def _masked_softmax(s, mask):
    s = jnp.where(mask, s, NEG_INF)
    m = jnp.max(s, axis=-1, keepdims=True)
    e = jnp.where(mask, jnp.exp(s - m), 0.0)
    z = jnp.sum(e, axis=-1, keepdims=True)
    return e / jnp.where(z > 0, z, 1.0)


def _overlap_matrix(ncp):
    c = CMP_STRIDE * lax.broadcasted_iota(jnp.int32, (ncp, LANES), 0)
    s = SLC_BLOCK * lax.broadcasted_iota(jnp.int32, (ncp, LANES), 1)
    return jnp.where(c < s + SLC_BLOCK, jnp.where(c + CMP_BLOCK > s, 1.0, 0.0), 0.0).astype(F32)


def _select_mask(imp, cur, n_slc):
    R = imp.shape[0]
    blk = lax.broadcasted_iota(jnp.int32, (R, LANES), 1)
    for forced_blk in (0, cur, cur - 1):
        imp = jnp.where(blk == forced_blk, FORCE_SCORE, imp)
    imp = jnp.where(blk <= cur, imp, -1.0)
    imp = jnp.where(blk < n_slc, imp, -2.0)
    cnt = jnp.zeros((R, LANES), F32)
    for k in range(n_slc):
        col = imp[:, k:k + 1]
        cnt = cnt + jnp.where(col > imp, 1.0, jnp.where(col == imp, jnp.where(blk > k, 1.0, 0.0), 0.0))
    return jnp.where(cnt < float(min(N_SELECT, n_slc)), jnp.where(blk < n_slc, 1.0, 0.0), 0.0)


def _expand_matrix(nkeys):
    s = lax.broadcasted_iota(jnp.int32, (LANES, nkeys), 0)
    j = lax.broadcasted_iota(jnp.int32, (LANES, nkeys), 1)
    lo = s * SLC_BLOCK
    return jnp.where(j >= lo, jnp.where(j < lo + SLC_BLOCK, 1.0, 0.0), 0.0).astype(BF16)


KEY_TILE = 512


def _select_mask_t(imp_t, cur, n_slc):
    blk = lax.broadcasted_iota(jnp.int32, imp_t.shape, 0)
    for forced_blk in (0, cur, cur - 1):
        imp_t = jnp.where(blk == forced_blk, FORCE_SCORE, imp_t)
    imp_t = jnp.where(blk <= cur, imp_t, -1.0)
    cnt = jnp.zeros(imp_t.shape, F32)
    for k in range(n_slc):
        row = imp_t[k:k + 1, :]
        cnt = cnt + jnp.where(row > imp_t, 1.0, jnp.where(row == imp_t, jnp.where(blk > k, 1.0, 0.0), 0.0))
    return jnp.where(cnt < float(min(N_SELECT, n_slc)), 1.0, 0.0)


def _nsa_prompt_kernel(q_ref, kc_ref, vc_ref, ks_ref, vs_ref, kw_ref, vw_ref, bt_ref, bc_ref, gates_ref, ex_ref,
                       o_ref, m_ref, acc_ref, *, T):
    qb = pl.program_id(2)
    QB = Q_BLOCK
    R = NSA_REP
    RQ = R * QB
    hd = HEAD_DIM
    nc = (T - CMP_BLOCK) // CMP_STRIDE + 1
    ncp = kc_ref.shape[-2]
    n_slc = ex_ref.shape[0]
    q = jnp.concatenate([q_ref[:, r * hd:(r + 1) * hd] for r in range(R)], axis=0)

    s = _dot_t(q, kc_ref[0, 0, 0]) + bc_ref[0].reshape(RQ, ncp)
    t_row = lax.broadcasted_iota(jnp.int32, (R, QB, ncp), 1).reshape(RQ, ncp)
    c_col = lax.broadcasted_iota(jnp.int32, (RQ, ncp), 1)
    dist = qb * QB + t_row - (CMP_STRIDE * c_col + CMP_BLOCK - 1)
    p = _masked_softmax(s, jnp.where(c_col < nc, dist, -1) >= 0)
    o_cmp = jnp.dot(p.astype(BF16), vc_ref[0, 0, 0], preferred_element_type=F32)

    psum = p[0:QB]
    for r in range(1, R):
        psum = psum + p[r * QB:(r + 1) * QB]
    s_lo = SLC_BLOCK * lax.broadcasted_iota(jnp.int32, (n_slc, ncp), 0)
    c_lo = CMP_STRIDE * lax.broadcasted_iota(jnp.int32, (n_slc, ncp), 1)
    overlap_t = jnp.where(c_lo < s_lo + SLC_BLOCK, jnp.where(c_lo + CMP_BLOCK > s_lo, 1.0, 0.0), 0.0).astype(F32)
    imp_t = _dot_t(overlap_t, psum, HIGHEST)
    cur = _div_pow2(qb * QB + lax.broadcasted_iota(jnp.int32, (1, QB), 1), SLC_BLOCK)
    sel_t = _select_mask_t(imp_t, cur, n_slc).astype(BF16)

    NCHAIN = 2
    hpc = R // NCHAIN
    crow = hpc * QB

    def bias_rows(c, first_blk, n):
        return jnp.concatenate(
            [bt_ref[jnp.maximum(qb - (first_blk + j), 0), c * hpc:(c + 1) * hpc].reshape(crow, QB) for j in range(n)],
            axis=-1)

    def rel(first_key, nk):
        return (qb * QB - first_key + lax.broadcasted_iota(jnp.int32, (QB, nk), 0)
                - lax.broadcasted_iota(jnp.int32, (QB, nk), 1))

    def with_ones(v):
        return jnp.concatenate([v, jnp.ones(v.shape, BF16)], axis=-1)

    nkb = KEY_TILE // QB
    m_ref[...] = jnp.full(m_ref.shape, NEG_INF, F32)
    acc_ref[...] = jnp.zeros(acc_ref.shape, F32)

    def body(kt, carry):
        off = pl.multiple_of(kt * KEY_TILE, KEY_TILE)
        k = ks_ref[0, pl.ds(off, KEY_TILE), :]
        vext = with_ones(vs_ref[0, pl.ds(off, KEY_TILE), :])
        selk = lax.dot_general(sel_t, ex_ref[:, pl.ds(off, KEY_TILE)], (((0,), (0,)), ((), ())),
                               preferred_element_type=F32)
        d = jnp.where(selk > 0.5, rel(off, KEY_TILE), -1)
        neg = jnp.concatenate([jnp.where(d >= 0, 0.0, NEG_INF)] * hpc, axis=0)
        for c in range(NCHAIN):
            rows = slice(c * crow, (c + 1) * crow)
            sc = _dot_t(q[rows], k) + bias_rows(c, kt * nkb, nkb) + neg
            m_old = m_ref[rows]
            m_new = jnp.maximum(m_old, jnp.max(sc, axis=-1, keepdims=True))
            pe = jnp.exp(sc - m_new).astype(BF16)
            acc_ref[rows] = jnp.exp(m_old - m_new) * acc_ref[rows] + jnp.dot(pe, vext, preferred_element_type=F32)
            m_ref[rows] = m_new
        return carry

    lax.fori_loop(0, qb // nkb + 1, body, 0)
    acc = acc_ref[...]
    den = acc[:, hd:]
    o_slc = acc[:, :hd] / jnp.where(den > 0, den, 1.0)

    wk = min(WINDOW + QB, T)
    w0 = jnp.maximum(qb - WINDOW // QB, 0)
    offw = pl.multiple_of(w0 * QB, QB)
    kwin = kw_ref[0, pl.ds(offw, wk), :]
    vext = with_ones(vw_ref[0, pl.ds(offw, wk), :])
    d = rel(offw, wk)
    d = jnp.where(d <= WINDOW, d, -1)
    neg = jnp.concatenate([jnp.where(d >= 0, 0.0, NEG_INF)] * hpc, axis=0)
    o_win = []
    for c in range(NCHAIN):
        sc = _dot_t(q[c * crow:(c + 1) * crow], kwin) + bias_rows(c, w0, wk // QB) + neg
        e = jnp.exp(sc - jnp.max(sc, axis=-1, keepdims=True)).astype(BF16)
        acc = jnp.dot(e, vext, preferred_element_type=F32)
        o_win.append(acc[:, :hd] / acc[:, hd:])
    o_win = jnp.concatenate(o_win, axis=0)

    gt = gates_ref[0]
    for r in range(R):
        sl = slice(r * QB, (r + 1) * QB)
        o = (gt[:, r:r + 1] * o_cmp[sl] + gt[:, R + r:R + r + 1] * o_slc[sl]
             + gt[:, 2 * R + r:2 * R + r + 1] * o_win[sl])
        o_ref[:, r * hd:(r + 1) * hd] = o


def _nsa_prompt(q, cmp_kv, kvb3, gates_g, bias_tiles, bias_cmp, *, B, T):
    G = NSA_KV_HEADS
    R = NSA_REP
    nqb = T // Q_BLOCK
    ncp = T // CMP_STRIDE
    n_slc = -(-T // SLC_BLOCK)
    expand = (jnp.arange(T)[None, :] // SLC_BLOCK == jnp.arange(n_slc)[:, None]).astype(BF16)
    kv_spec = lambda col: pl.BlockSpec((1, T, HEAD_DIM), lambda b, g, i, col=col: (b, 0, col + g))
    return pl.pallas_call(
        functools.partial(_nsa_prompt_kernel, T=T),
        grid=(B, G, nqb),
        in_specs=[pl.BlockSpec((Q_BLOCK, R * HEAD_DIM), lambda b, g, i: (b * nqb + i, g)),
                  pl.BlockSpec((1, 1, 1, ncp, HEAD_DIM), lambda b, g, i: (b, 0, g, 0, 0)),
                  pl.BlockSpec((1, 1, 1, ncp, HEAD_DIM), lambda b, g, i: (b, 1, g, 0, 0)),
                  kv_spec(0), kv_spec(2), kv_spec(4), kv_spec(6),
                  pl.BlockSpec((nqb, R, Q_BLOCK, Q_BLOCK), lambda b, g, i: (0, g, 0, 0)),
                  pl.BlockSpec((1, R, Q_BLOCK, ncp), lambda b, g, i: (i, g, 0, 0)),
                  pl.BlockSpec((1, Q_BLOCK, LANES), lambda b, g, i: (g, b * nqb + i, 0)),
                  pl.BlockSpec((n_slc, T), lambda b, g, i: (0, 0))],
        out_specs=pl.BlockSpec((Q_BLOCK, R * HEAD_DIM), lambda b, g, i: (b * nqb + i, g)),
        out_shape=jax.ShapeDtypeStruct((B * T, G * R * HEAD_DIM), F32),
        scratch_shapes=[pltpu.VMEM((R * Q_BLOCK, 1), F32),
                        pltpu.VMEM((R * Q_BLOCK, 2 * HEAD_DIM), F32)],
        compiler_params=_cp(("parallel", "parallel", "arbitrary")), name="nsa_prompt",
    )(q, cmp_kv, cmp_kv, kvb3, kvb3, kvb3, kvb3, bias_tiles, bias_cmp, gates_g, expand)


def _nsa_sample_kernel(pt_ref, *refs, npages, tq):
    pages = refs[:npages]
    (win_ref, wnew_ref, q_ref, kvn_ref, gates_ref, pe_ref, w1_ref, w2_ref, kg_ref, bc_ref, bs_ref, bw_ref,
     o_ref, wout_ref) = refs[npages:]
    G, R, hd = NSA_KV_HEADS, NSA_REP, HEAD_DIM
    KVR = 4 * G
    WR = 2 * G
    page = pages[0].shape[2] // KVR
    past = npages * page
    L = past + tq
    nc = (L - CMP_BLOCK) // CMP_STRIDE + 1
    ng = past // CMP_STRIDE
    n_slc = -(-L // SLC_BLOCK)
    rows = R * tq
    wb = win_ref.shape[2] // WR
    wout_ref[0, 0, 0:(wb - tq) * WR, :] = win_ref[0, 0, tq * WR:wb * WR, :]
    wout_ref[0, 0, (wb - tq) * WR:wb * WR, :] = wnew_ref[0]
    pad_new = lambda v: jnp.concatenate([v, jnp.zeros((LANES - v.shape[0], v.shape[1]), v.dtype)], axis=0)
    t_of_row = lax.broadcasted_iota(jnp.int32, (R, tq, 1), 1).reshape(rows, 1)
    pos = past + t_of_row
    ra = lax.broadcasted_iota(jnp.int32, (rows, rows), 0)
    rb = lax.broadcasted_iota(jnp.int32, (rows, rows), 1)
    same_t = jnp.where(_mod_pow2(ra, tq) == _mod_pow2(rb, tq), 1.0, 0.0).astype(F32)
    kg = kg_ref[...]
    kvn = kvn_ref[0]

    for g in range(G):
        q = q_ref[0, g]
        gt = gates_ref[0, g]

        def compress(slot):
            col = slot * G + g

            def get_rows(r):
                return jnp.concatenate(
                    [pg[0, 0, pl.ds(r * KVR + col, page // CMP_STRIDE, stride=CMP_STRIDE * KVR), :] for pg in pages],
                    axis=0)

            return _compress_math(get_rows, ng, pe_ref[slot], w1_ref.at[slot], w2_ref[slot])

        kc = _rms(compress(0), kg).astype(BF16)
        vc = compress(1).astype(BF16)

        c_col = lax.broadcasted_iota(jnp.int32, (rows, ng), 1)
        dist = pos - (CMP_STRIDE * c_col + CMP_BLOCK - 1)
        p = _masked_softmax(_dot_t(q, kc) + bc_ref[g], jnp.where(c_col < nc, dist, -1) >= 0)
        o_cmp = jnp.dot(p.astype(BF16), vc, preferred_element_type=F32)

        psum = jnp.dot(same_t, p, precision=HIGHEST, preferred_element_type=F32)
        imp = jnp.dot(psum, _overlap_matrix(ng), precision=HIGHEST, preferred_element_type=F32)
        sel = _select_mask(imp, _div_pow2(pos, SLC_BLOCK), n_slc)
        nk = past + LANES
        selk = jnp.dot(sel.astype(BF16), _expand_matrix(nk), preferred_element_type=F32)

        k_new = pad_new(kvn[:, g * hd:(g + 1) * hd])
        v_new = pad_new(kvn[:, 256 + g * hd:256 + (g + 1) * hd])
        all_rows = lambda slot, new: jnp.concatenate(
            [pg[0, 0, pl.ds(slot * G + g, page, stride=KVR), :].astype(BF16) for pg in pages] + [new], axis=0)
        sc = _dot_t(q, all_rows(2, k_new)) + bs_ref[g]
        j = lax.broadcasted_iota(jnp.int32, (rows, nk), 1)
        p = _masked_softmax(sc, jnp.where(selk > 0.5, pos - j, -1) >= 0).astype(BF16)
        o_slc = jnp.dot(p, all_rows(3, v_new), preferred_element_type=F32)

        kw_new = pad_new(kvn[:, 512 + g * hd:512 + (g + 1) * hd])
        vw_new = pad_new(kvn[:, 768 + g * hd:768 + (g + 1) * hd])
        kw = win_ref[0, 0, pl.ds(g, wb, stride=WR), :].astype(BF16)
        vw = win_ref[0, 0, pl.ds(G + g, wb, stride=WR), :].astype(BF16)
        sc = _dot_t(q, jnp.concatenate([kw, kw_new], axis=0)) + bw_ref[g]
        j = lax.broadcasted_iota(jnp.int32, (rows, wb + LANES), 1)
        d = pos - (past - wb + j)
        p = _masked_softmax(sc, jnp.where(d <= WINDOW, d, -1) >= 0).astype(BF16)
        o_win = jnp.dot(p, jnp.concatenate([vw, vw_new], axis=0), preferred_element_type=F32)

        o_ref[0, g] = gt[:, 0:1] * o_cmp + gt[:, 1:2] * o_slc + gt[:, 2:3] * o_win


def _nsa_sample(slot, page_table, cache_lin, win_lin, win_new, q_s, kvn_s, gates_s, lw, bias_c, bias_s, bias_w, *, tq):
    B, npages = page_table.shape
    G = NSA_KV_HEADS
    rows = NSA_REP * tq
    l = slot.layer
    page_specs = [pl.BlockSpec((1, 1) + cache_lin.shape[2:], lambda b, pt, p=p: (l, pt[b, p], 0, 0))
                  for p in range(npages)]
    full = lambda a: pl.BlockSpec(a.shape, lambda b, pt, n=a.ndim: (0,) * n)
    wts = [lw["cmp_pe"], lw["cmp_w1"], lw["cmp_w2"], lw["k_norm_g"].reshape(1, -1), bias_c, bias_s, bias_w]
    grid_spec = pltpu.PrefetchScalarGridSpec(
        num_scalar_prefetch=1, grid=(B,),
        in_specs=page_specs + [
            pl.BlockSpec((1, 1) + win_lin.shape[2:], lambda b, pt: (l, b, 0, 0)),
            pl.BlockSpec((1,) + win_new.shape[1:], lambda b, pt: (b, 0, 0)),
            pl.BlockSpec((1, G, rows, HEAD_DIM), lambda b, pt: (b, 0, 0, 0)),
            pl.BlockSpec((1, SUBLANES, kvn_s.shape[2]), lambda b, pt: (b, 0, 0)),
            pl.BlockSpec((1, G, rows, LANES), lambda b, pt: (b, 0, 0, 0))] + [full(w) for w in wts]
        + slot.in_specs(),
        out_specs=[pl.BlockSpec((1, G, rows, HEAD_DIM), lambda b, pt: (b, 0, 0, 0)),
                   pl.BlockSpec((1, 1) + win_lin.shape[2:], lambda b, pt: (l, b, 0, 0))])
    n_in = 1 + npages + 5 + len(wts)
    return pl.pallas_call(
        slot.wrap(functools.partial(_nsa_sample_kernel, npages=npages, tq=tq), n_in),
        grid_spec=grid_spec,
        out_shape=[jax.ShapeDtypeStruct((B, G, rows, HEAD_DIM), F32),
                   jax.ShapeDtypeStruct((slot.depth,) + win_lin.shape[1:], F32)],
        input_output_aliases=slot.aliases(n_in, 1),
        compiler_params=_cp(("arbitrary",)), name="nsa_sample",
    )(page_table, *([cache_lin] * npages), win_lin, win_new, q_s, kvn_s, gates_s, *wts, *slot.args())


def _dot3(a, b):
    ah = a.astype(BF16)
    bh = b.astype(BF16)
    al = (a - ah.astype(F32)).astype(BF16)
    bl = (b - bh.astype(F32)).astype(BF16)
    d = functools.partial(jnp.dot, preferred_element_type=F32)
    return d(ah, bh) + (d(al, bh) + d(ah, bl))


def _gdn_chunk_kernel(gq_ref, ggb_ref, n_ref, k2_ref, qe_ref, o0_ref, eg_ref):
    B, C = gq_ref.shape[:2]
    H = GDN_HEADS
    hd = HEAD_DIM
    W = H * hd
    HC = H * C
    ri = lax.broadcasted_iota(jnp.int32, (HC, HC), 0)
    ci = lax.broadcasted_iota(jnp.int32, (HC, HC), 1)
    tri = jnp.where(_div_pow2(ri, C) == _div_pow2(ci, C), ri - ci, -1)
    eye = jnp.where(ri == ci, 1.0, 0.0).astype(F32)
    ones = jnp.ones((HC, HC), F32)
    own_lanes = (_div_pow2(lax.broadcasted_iota(jnp.int32, (HC, W), 0), C)
                 == _div_pow2(lax.broadcasted_iota(jnp.int32, (HC, W), 1), hd))

    for b in range(B):
        stack = lambda ref, c0: jnp.concatenate([ref[b, :, c0 + h * hd:c0 + (h + 1) * hd] for h in range(H)], axis=0)
        q = stack(gq_ref, 0)
        k = stack(gq_ref, W)
        v = stack(gq_ref, 2 * W)
        gc = stack(ggb_ref, 0)
        beta = stack(ggb_ref, W)
        gcol = jnp.concatenate([gc] * (HC // hd), axis=-1)
        grow = jnp.dot(ones, eye * gcol, precision=HIGHEST, preferred_element_type=F32)
        decay = jnp.exp(jnp.where(tri >= 0, gcol - grow, -jnp.inf))
        kb = k * beta
        g2 = _dot_t(jnp.concatenate([kb, q], axis=0).astype(BF16), k.astype(BF16))
        low = jnp.where(tri > 0, g2[:HC] * decay, 0.0)
        qk = jnp.where(tri >= 0, g2[HC:] * decay, 0.0)
        egc = jnp.exp(gc)
        x = jnp.concatenate([v * beta, kb * egc], axis=-1)
        pw = low
        span = 1
        while span < C:
            last = span * 2 >= C
            rhs = x if last else jnp.concatenate([pw, x], axis=-1)
            r = _dot3(pw, rhs)
            px = r if last else r[:, HC:]
            x = x - px if span == 1 else x + px
            if not last:
                pw = r[:, :HC]
            span *= 2
        uw = x.astype(BF16)
        g_last = jnp.concatenate([jnp.broadcast_to(gc[(h + 1) * C - 1:(h + 1) * C], (C, hd)) for h in range(H)],
                                 axis=0)
        kd = k * jnp.exp(g_last - gc)
        kd_bd = jnp.where(own_lanes, jnp.concatenate([kd] * H, axis=-1), 0.0).astype(BF16)
        nk = lax.dot_general(kd_bd, uw, (((0,), (0,)), ((), ())), preferred_element_type=F32)
        ow = jnp.dot(qk.astype(BF16), uw, preferred_element_type=F32)
        qe = q * egc - ow[:, hd:]
        eg = jnp.exp(g_last)
        for h in range(H):
            n_ref[b, h, 0] = nk[h * hd:(h + 1) * hd, :hd]
            k2_ref[b, h, 0] = nk[h * hd:(h + 1) * hd, hd:].astype(BF16)
            o0_ref[b, h, 0] = ow[h * C:(h + 1) * C, :hd]
            qe_ref[b, h, 0] = qe[h * C:(h + 1) * C].astype(BF16)
            eg_ref[b, h, 0] = eg[h * C:h * C + SUBLANES]


def _gdn_scan_kernel(n_ref, k2_ref, qe_ref, o0_ref, eg_ref, gz_ref, s0_ref, ng_ref, y_ref, so_ref, s_ref):
    n = pl.program_id(0)
    B = gz_ref.shape[0]
    hd = HEAD_DIM

    @pl.when(n == 0)
    def _():
        s_ref[...] = s0_ref[...]

    ng = ng_ref[...]
    for b in range(B):
        for h in range(GDN_HEADS):
            S = s_ref[b, h]
            s16 = S.astype(BF16)
            o = jnp.dot(qe_ref[b, h, 0], s16, preferred_element_type=F32) + o0_ref[b, h, 0]
            s_ref[b, h] = (eg_ref[b, h, 0][0:1] * S + n_ref[b, h, 0]
                           - jnp.dot(k2_ref[b, h, 0], s16, preferred_element_type=F32))
            z = gz_ref[b, :, h * hd:(h + 1) * hd]
            y_ref[b, :, h * hd:(h + 1) * hd] = (_rms(o, ng) * (z * _sigmoid(z))).astype(y_ref.dtype)

    @pl.when(n == pl.num_programs(0) - 1)
    def _():
        so_ref[...] = s_ref[...]


def _gdn_prompt(gq3, ggb3, z3, s0, norm_g):
    B, T, _ = gq3.shape
    C = GDN_CHUNK
    H = GDN_HEADS
    hd = HEAD_DIM
    W = H * hd
    nchunk = T // C
    item = lambda rows: pl.BlockSpec((B, H, 1, rows, hd), lambda n: (0, 0, n, 0, 0))
    shape = lambda rows, dt: jax.ShapeDtypeStruct((B, H, nchunk, rows, hd), dt)
    nn, k2, qe, o0, eg = pl.pallas_call(
        _gdn_chunk_kernel,
        grid=(nchunk,),
        in_specs=[pl.BlockSpec((B, C, 3 * W), lambda n: (0, n, 0)),
                  pl.BlockSpec((B, C, 2 * W), lambda n: (0, n, 0))],
        out_specs=[item(hd), item(hd), item(C), item(C), item(SUBLANES)],
        out_shape=[shape(hd, F32), shape(hd, BF16), shape(C, BF16), shape(C, F32), shape(SUBLANES, F32)],
        compiler_params=_cp(("parallel",)), name="gdn_chunk",
    )(gq3, ggb3)
    return pl.pallas_call(
        _gdn_scan_kernel,
        grid=(nchunk,),
        in_specs=[item(hd), item(hd), item(C), item(C), item(SUBLANES),
                  pl.BlockSpec((B, C, W), lambda n: (0, n, C_GZ // W)),
                  pl.BlockSpec(s0.shape, lambda n: (0, 0, 0, 0)),
                  pl.BlockSpec((1, hd), lambda n: (0, 0))],
        out_specs=[pl.BlockSpec((B, C, W), lambda n: (0, n, 0)),
                   pl.BlockSpec(s0.shape, lambda n: (0, 0, 0, 0))],
        out_shape=[jax.ShapeDtypeStruct((B, T, W), BF16), jax.ShapeDtypeStruct(s0.shape, F32)],
        scratch_shapes=[pltpu.VMEM(s0.shape, F32)],
        compiler_params=_cp(("arbitrary",)), name="gdn_scan",
    )(nn, k2, qe, o0, eg, z3, s0, norm_g.reshape(1, -1))


def _gdn_sample_kernel(kq_ref, v_ref, gb_ref, gz_ref, s0_ref, ng_ref, y_ref, so_ref, *, tq):
    bt = kq_ref.shape[0]
    hd = HEAD_DIM
    ri = lax.broadcasted_iota(jnp.int32, (hd, hd), 0)
    ci = lax.broadcasted_iota(jnp.int32, (hd, hd), 1)
    eye = jnp.where(ri == ci, 1.0, 0.0).astype(F32)
    ng = ng_ref[...]

    def body(bi, carry):
        for h in range(GDN_HEADS):
            kq = kq_ref[bi, h]
            cols = _dot_t(eye, kq, HIGHEST)
            gb = gb_ref[bi, h]
            v = v_ref[bi, h]
            S = s0_ref[bi, h]
            outs = []
            for t in range(tq):
                a = jnp.exp(gb[t:t + 1])
                kc = cols[:, t:t + 1]
                qc = cols[:, tq + t:tq + t + 1]
                Sa = S * a
                stk = jnp.sum(Sa * kc, axis=0, keepdims=True)
                vn = gb[tq + t:tq + t + 1] * (v[t:t + 1] - stk)
                S = Sa + kc * vn
                outs.append(jnp.sum(S * qc, axis=0, keepdims=True))
            so_ref[0, bi, h] = S
            o = jnp.concatenate(outs + [jnp.zeros((SUBLANES - tq, hd), F32)], axis=0)
            z = gz_ref[bi, h]
            y_ref[bi, h] = _rms(o, ng) * (z * _sigmoid(z))
        return carry

    lax.fori_loop(0, bt, body, 0)


def _gdn_sample(slot, kq_s, v_s, gb_s, gz_s, s0, norm_g, *, tq):
    B = kq_s.shape[0]
    bt = 8
    layer = slot.layer
    blk = lambda a: pl.BlockSpec((bt,) + a.shape[1:], lambda i: (i, 0, 0, 0))
    in_specs = [blk(kq_s), blk(v_s), blk(gb_s), blk(gz_s), blk(s0), pl.BlockSpec((1, HEAD_DIM), lambda i: (0, 0))]
    return pl.pallas_call(
        slot.wrap(functools.partial(_gdn_sample_kernel, tq=tq), len(in_specs)),
        grid=(B // bt,),
        in_specs=in_specs + slot.in_specs(),
        out_specs=[blk(v_s), pl.BlockSpec((1, bt) + s0.shape[1:], lambda i: (layer, i, 0, 0, 0))],
        out_shape=[jax.ShapeDtypeStruct(v_s.shape, F32), jax.ShapeDtypeStruct((slot.depth,) + s0.shape, F32)],
        input_output_aliases=slot.aliases(len(in_specs), 1),
        compiler_params=_cp(("parallel",)), name="gdn_sample",
    )(kq_s, v_s, gb_s, gz_s, s0, norm_g.reshape(1, -1), *slot.args())


def _mix_out_kernel(ya_ref, yb_ref, yc_ref, x_ref, onb_ref, w_ref, o_ref, mix_ref):
    @pl.when(pl.program_id(1) == 0)
    def _():
        mix_ref[:, 0:512] = ya_ref[...]
        mix_ref[:, 512:1536] = _rms(yb_ref[...], onb_ref[...]).astype(BF16)
        mix_ref[:, 1536:2048] = yc_ref[...]

    o_ref[...] = x_ref[...] + jnp.dot(mix_ref[...], w_ref[...], preferred_element_type=F32)


def _mix_out(ya, yb, yc, x, onb, w, *, tn=512):
    M, N = x.shape
    tm = _row_tile(M)
    K = w.shape[0]
    return pl.pallas_call(
        _mix_out_kernel,
        grid=(M // tm, N // tn),
        in_specs=[pl.BlockSpec((tm, 512), lambda i, j: (i, 0)),
                  pl.BlockSpec((tm, 1024), lambda i, j: (i, 0)),
                  pl.BlockSpec((tm, 512), lambda i, j: (i, 0)),
                  pl.BlockSpec((tm, tn), lambda i, j: (i, j)),
                  pl.BlockSpec((1, 1024), lambda i, j: (0, 0)),
                  pl.BlockSpec((K, tn), lambda i, j: (0, j))],
        out_specs=pl.BlockSpec((tm, tn), lambda i, j: (i, j)),
        out_shape=jax.ShapeDtypeStruct((M, N), F32),
        scratch_shapes=[pltpu.VMEM((tm, K), BF16)],
        compiler_params=_cp(("parallel", "arbitrary")), name="mix_out",
    )(ya, yb, yc, x, onb.reshape(1, -1), w)


def _ffn_mid_kernel(a_ref, v_ref, ap_ref, vp_ref, ha_ref, hv_ref, wa_ref, wv_ref, o_ref, *, tstride, tiles_per_seq):
    tc = wa_ref.shape[1]

    def conv(x_ref, prev_ref, hist_ref, w_ref):
        x = _ld(x_ref, 0, tc)
        if tstride == 1:
            start = (pl.program_id(0) % tiles_per_seq) == 0
            halo = jnp.where(start, hist_ref[0], prev_ref[...])
        else:
            halo = _ld(hist_ref, 0, tc)
        w = w_ref[...]
        return w[0:1] * _shift(x, halo, 2, tstride) + w[1:2] * _shift(x, halo, 1, tstride) + w[2:3] * x

    a = conv(a_ref, ap_ref, ha_ref, wa_ref)
    v = conv(v_ref, vp_ref, hv_ref, wv_ref)
    _st(o_ref, 0, tc, a * _sigmoid(a) * v)


def _ffn_mid(up, hist, w, *, sample, nseq, seq_len, tc=512):
    M, F2 = up.shape
    F = F2 // 2
    nj = F // tc
    wspec_a = pl.BlockSpec((3, tc), lambda i, j: (0, j))
    wspec_v = pl.BlockSpec((3, tc), lambda i, j: (0, j + nj))
    if not sample:
        tm = 512
        tps = seq_len // tm
        prev = lambda off: pl.BlockSpec(
            (SUBLANES, tc), lambda i, j, off=off: (jnp.maximum(i * (tm // SUBLANES) - 1, 0), j + off))
        hspec = lambda off: pl.BlockSpec((1, SUBLANES, tc), lambda i, j, off=off: (i // tps, 0, j + off))
        return pl.pallas_call(
            functools.partial(_ffn_mid_kernel, tstride=1, tiles_per_seq=tps),
            grid=(M // tm, nj),
            in_specs=[pl.BlockSpec((tm, tc), lambda i, j: (i, j)), pl.BlockSpec((tm, tc), lambda i, j: (i, j + nj)),
                      prev(0), prev(nj), hspec(0), hspec(nj), wspec_a, wspec_v],
            out_specs=pl.BlockSpec((tm, tc), lambda i, j: (i, j)),
            out_shape=jax.ShapeDtypeStruct((M, F), BF16),
            compiler_params=_cp(("parallel", "parallel")), name="ffn_mid_prompt",
        )(up, up, up, up, hist, hist, w, w)
    T = seq_len
    up3 = up.reshape(T, nseq, F2)
    hk = hist.shape[0]
    out = pl.pallas_call(
        functools.partial(_ffn_mid_kernel, tstride=nseq, tiles_per_seq=1),
        grid=(1, nj),
        in_specs=[pl.BlockSpec((T, nseq, tc), lambda i, j: (0, 0, j)),
                  pl.BlockSpec((T, nseq, tc), lambda i, j: (0, 0, j + nj)),
                  pl.BlockSpec((SUBLANES, tc), lambda i, j: (0, j)), pl.BlockSpec((SUBLANES, tc), lambda i, j: (0, j + nj)),
                  pl.BlockSpec((hk, nseq, tc), lambda i, j: (0, 0, j)),
                  pl.BlockSpec((hk, nseq, tc), lambda i, j: (0, 0, j + nj)), wspec_a, wspec_v],
        out_specs=pl.BlockSpec((T, nseq, tc), lambda i, j: (0, 0, j)),
        out_shape=jax.ShapeDtypeStruct((T, nseq, F), BF16),
        compiler_params=_cp(("parallel", "parallel")), name="ffn_mid_sample",
    )(up3, up3, up, up, hist, hist, w, w)
    return out.reshape(M, F)


def _matmul_res_kernel(a_ref, w_ref, r_ref, o_ref):
    o_ref[...] = r_ref[...] + jnp.dot(a_ref[...], w_ref[...], preferred_element_type=F32)


def _matmul_res(a, w, res, *, tm=512, tn=512):
    M, K = a.shape
    N = w.shape[1]
    tm = min(tm, M)
    return pl.pallas_call(
        _matmul_res_kernel,
        grid=(M // tm, N // tn),
        in_specs=[pl.BlockSpec((tm, K), lambda i, j: (i, 0)),
                  pl.BlockSpec((K, tn), lambda i, j: (0, j)),
                  pl.BlockSpec((tm, tn), lambda i, j: (i, j))],
        out_specs=pl.BlockSpec((tm, tn), lambda i, j: (i, j)),
        out_shape=jax.ShapeDtypeStruct((M, N), F32),
        compiler_params=_cp(("parallel", "parallel")), name="matmul_res",
    )(a, w, res)


def _layer_weights(l, p):
    w_in = p["w_in"][l]
    w_main = jnp.concatenate([w_in[:, :4096], w_in[:, 4120:5656], w_in[:, 5664:6176]], axis=1).astype(BF16)
    w_small = jnp.concatenate([w_in[:, 4096:4120], w_in[:, 5656:5664],
                               jnp.zeros((w_in.shape[0], LANES - 32), w_in.dtype)], axis=1).astype(BF16)
    lane_vec = lambda v: jnp.zeros((1, LANES), F32).at[0, S_GA:S_GA + GDN_HEADS].set(v)
    return {
        "norm_mix_g": p["norm_mix_g"][l], "w_main": w_main, "w_small": w_small,
        "conv_a_w": p["conv_a_w"][l], "q_norm_g": p["q_norm_g"][l], "k_norm_g": p["k_norm_g"][l],
        "cmp_pe": p["cmp_pe"][l],
        "cmp_w1": p["cmp_w1"][l].reshape(2, 2, CMP_BLOCK // 2 * HEAD_DIM, CMP_HIDDEN).astype(BF16),
        "cmp_w2": p["cmp_w2"][l].astype(BF16),
        "gdn_conv_w": p["gdn_conv_w"][l], "alog_v": lane_vec(p["gdn_a_log"][l]), "dtb_v": lane_vec(p["gdn_dt_bias"][l]),
        "gdn_norm_g": p["gdn_norm_g"][l], "out_norm_a": p["out_norm_a"][l], "out_norm_b": p["out_norm_b"][l],
        "w_out": p["w_out"][l].astype(BF16), "norm_ffn_g": p["norm_ffn_g"][l],
        "ffn_up": p["ffn_up"][l].astype(BF16), "ffn_conv_w": p["ffn_conv_w"][l],
        "ffn_down": p["ffn_down"][l].astype(BF16),
    }


HALO_ROWS = 16


def _ffn_up_kernel(x_ref, xp_ref, g_ref, wa_ref, wv_ref, ha_ref, hv_ref, ca_ref, cv_ref,
                   o_ref, hoa_ref, hov_ref, xn_ref, *, tstride, tiles_per_seq):
    tn = wa_ref.shape[1]
    prompt = tstride == 1

    @pl.when(pl.program_id(1) == 0)
    def _():
        g = g_ref[...]
        if prompt:
            xn_ref[0:HALO_ROWS] = _rms(xp_ref[...], g).astype(BF16)
            xn_ref[HALO_ROWS:] = _rms(x_ref[...], g).astype(BF16)
        else:
            xn_ref[...] = _rms(x_ref[...], g).astype(BF16)

    xn = xn_ref[...]

    def part(w_ref, hist_ref, cw_ref, ho_ref):
        up = jnp.dot(xn, w_ref[...], preferred_element_type=F32)
        if prompt:
            start = (pl.program_id(0) % tiles_per_seq) == 0
            halo = jnp.where(start, hist_ref[0], up[HALO_ROWS - SUBLANES:HALO_ROWS])
            x = up[HALO_ROWS:]
            ho_ref[0] = x[x.shape[0] - SUBLANES:]
        else:
            halo = _ld(hist_ref, 0, tn)
            x = up
            _st(ho_ref, 0, tn, x[x.shape[0] - halo.shape[0]:])
        w = cw_ref[...]
        return w[0:1] * _shift(x, halo, 2, tstride) + w[1:2] * _shift(x, halo, 1, tstride) + w[2:3] * x

    a = part(wa_ref, ha_ref, ca_ref, hoa_ref)
    v = part(wv_ref, hv_ref, cv_ref, hov_ref)
    o_ref[...] = (a * _sigmoid(a) * v).astype(o_ref.dtype)


def _ffn_up(h, g, w_up, conv_w, hist, *, sample, nseq, seq_len, tn=512):
    M, K = h.shape
    tm = _row_tile(seq_len)
    F = w_up.shape[1] // 2
    nj = F // tn
    common = [pl.BlockSpec((1, K), lambda i, j: (0, 0)),
              pl.BlockSpec((K, tn), lambda i, j: (0, j)),
              pl.BlockSpec((K, tn), lambda i, j: (0, j + nj))]
    cw = [pl.BlockSpec((3, tn), lambda i, j: (0, j)), pl.BlockSpec((3, tn), lambda i, j: (0, j + nj))]
    if not sample:
        tps = seq_len // tm
        hspec = lambda off: pl.BlockSpec((1, SUBLANES, tn), lambda i, j, off=off: (i // tps, 0, j + off))
        hout = pl.BlockSpec((1, SUBLANES, tn), lambda i, j: (i, 0, j))
        act, tail_a, tail_v = pl.pallas_call(
            functools.partial(_ffn_up_kernel, tstride=1, tiles_per_seq=tps),
            grid=(M // tm, nj),
            in_specs=[pl.BlockSpec((tm, K), lambda i, j: (i, 0)),
                      pl.BlockSpec((HALO_ROWS, K), lambda i, j: (jnp.maximum(i * (tm // HALO_ROWS) - 1, 0), 0))]
            + common + [hspec(0), hspec(nj)] + cw,
            out_specs=[pl.BlockSpec((tm, tn), lambda i, j: (i, j)), hout, hout],
            out_shape=[jax.ShapeDtypeStruct((M, F), BF16)] + [jax.ShapeDtypeStruct((M // tm, SUBLANES, F), F32)] * 2,
            scratch_shapes=[pltpu.VMEM((tm + HALO_ROWS, K), BF16)],
            compiler_params=_cp(("arbitrary", "arbitrary")), name="ffn_up_prompt",
        )(h, h, g.reshape(1, K), w_up, w_up, hist, hist, conv_w, conv_w)
        return act, tail_a[tps - 1::tps], tail_v[tps - 1::tps]
    hk = hist.shape[0]
    hspec = lambda off: pl.BlockSpec((hk, nseq, tn), lambda i, j, off=off: (0, 0, j + off))
    hout = pl.BlockSpec((hk, nseq, tn), lambda i, j: (0, 0, j))
    return pl.pallas_call(
        functools.partial(_ffn_up_kernel, tstride=nseq, tiles_per_seq=1),
        grid=(1, nj),
        in_specs=[pl.BlockSpec((M, K), lambda i, j: (0, 0)), pl.BlockSpec((HALO_ROWS, K), lambda i, j: (0, 0))]
        + common + [hspec(0), hspec(nj)] + cw,
        out_specs=[pl.BlockSpec((M, tn), lambda i, j: (0, j)), hout, hout],
        out_shape=[jax.ShapeDtypeStruct((M, F), BF16)] + [jax.ShapeDtypeStruct((hk, nseq, F), F32)] * 2,
        scratch_shapes=[pltpu.VMEM((M, K), BF16)],
        compiler_params=_cp(("arbitrary", "arbitrary")), name="ffn_up_sample",
    )(h, h, g.reshape(1, K), w_up, w_up, hist, hist, conv_w, conv_w)


def _dense_tail(x2, ya, yb, yc, lw, hist_ffn, *, sample, nseq, seq_len):
    h = _mix_out(ya, yb, yc, x2, lw["out_norm_b"], lw["w_out"])
    act, hist_a, hist_v = _ffn_up(h, lw["norm_ffn_g"], lw["ffn_up"], lw["ffn_conv_w"], hist_ffn,
                                  sample=sample, nseq=nseq, seq_len=seq_len)
    return _matmul_res(act, lw["ffn_down"], h), jnp.concatenate([hist_a, hist_v], axis=-1)


def _prompt_layer(x2, lw, bias_tiles, bias_cmp, rows_slot, *, B, T):
    M = B * T
    G, R = NSA_KV_HEADS, NSA_REP
    z, zs = _norm_matmul(x2, lw["norm_mix_g"], lw["w_main"], lw["w_small"])
    hu0 = jnp.zeros((B, SUBLANES, 512), F32)
    hq0 = jnp.zeros((B, SUBLANES, 1536), F32)
    ya, q, rows, win, kvb, gates, gq, ggb, hu, rows_lin = _prep(z, zs, hu0, hq0, lw, sample=False, nseq=B, seq_len=T,
                                                                slot=rows_slot)
    cmp_kv = _compress_prompt(rows.reshape(B, T, 1024), lw)
    gates_g = gates[:, :3 * G * R].reshape(M, 3, G, R).transpose(2, 0, 1, 3).reshape(G, M, 3 * R)
    gates_g = jnp.pad(gates_g, ((0, 0), (0, 0), (0, LANES - 3 * R)))
    yb = _nsa_prompt(q, cmp_kv, kvb.reshape(B, T, 1024), gates_g, bias_tiles, bias_cmp, B=B, T=T)
    s0 = jnp.zeros((B, GDN_HEADS, HEAD_DIM, HEAD_DIM), F32)
    yc, s_new = _gdn_prompt(gq.reshape(B, T, 1536), ggb.reshape(B, T, 1024), z.reshape(B, T, C_END), s0,
                            lw["gdn_norm_g"])
    hf0 = jnp.zeros((B, SUBLANES, lw["ffn_conv_w"].shape[1]), F32)
    out, up = _dense_tail(x2, ya, yb, yc.reshape(M, 512), lw, hf0, sample=False, nseq=B, seq_len=T)
    return out, rows_lin, win, hu, z, s_new, up


def _sample_layer(x2, lw, st, bias_c, bias_s, bias_w, win_slot, gdn_slot, *, B, T):
    M = B * T
    G, R, H, hd = NSA_KV_HEADS, NSA_REP, GDN_HEADS, HEAD_DIM
    z, zs = _norm_matmul(x2, lw["norm_mix_g"], lw["w_main"], lw["w_small"])
    hu0 = jnp.swapaxes(st["conv_a"], 0, 1)
    hq0 = jnp.swapaxes(st["gdn_conv"], 0, 1)
    ya, q, rows, win, kvb, gates, gq, ggb, hu = _prep(z, zs, hu0, hq0, lw, sample=True, nseq=B, seq_len=T)
    tb = lambda a: a.reshape(T, B, -1)
    q_s = tb(q).reshape(T, B, G, R, hd).transpose(1, 2, 3, 0, 4).reshape(B, G, R * T, hd)
    gt = tb(gates)[:, :, :3 * G * R].reshape(T, B, 3, G, R).transpose(1, 3, 4, 0, 2).reshape(B, G, R * T, 3)
    gates_s = jnp.pad(gt, ((0, 0), (0, 0), (0, 0), (0, LANES - 3)))
    kvn_s = jnp.pad(jnp.swapaxes(tb(kvb), 0, 1), ((0, 0), (0, SUBLANES - T), (0, 0)))
    win_new = jnp.swapaxes(tb(win), 0, 1).reshape(B, T * 2 * G, hd)
    yb_s, win = _nsa_sample(win_slot, st["page_table"], st["cache_lin"], st["win_lin"], win_new, q_s, kvn_s, gates_s, lw,
                            bias_c, bias_s, bias_w, tq=T)
    yb = yb_s.reshape(B, G, R, T, hd).transpose(3, 0, 1, 2, 4).reshape(M, G * R * hd)
    gq4 = tb(gq).reshape(T, B, 3, H, hd)
    bh = lambda a: a.transpose(1, 2, 0, 3)
    padt = lambda a: jnp.pad(a, ((0, 0), (0, 0), (0, SUBLANES - T), (0, 0)))
    kq_s = jnp.concatenate([bh(gq4[:, :, 1]), bh(gq4[:, :, 0])], axis=2)
    v_s = padt(bh(gq4[:, :, 2]))
    ggb4 = tb(ggb).reshape(T, B, 2, H, hd)
    gb_s = jnp.concatenate([bh(ggb4[:, :, 0]), bh(ggb4[:, :, 1])], axis=2)
    gz_s = padt(bh(tb(z)[:, :, C_GZ:C_END].reshape(T, B, H, hd)))
    yc_s, s_new = _gdn_sample(gdn_slot, kq_s, v_s, gb_s, gz_s, st["gdn"], lw["gdn_norm_g"], tq=T)
    yc = yc_s[:, :, :T].transpose(2, 0, 1, 3).reshape(M, H * hd).astype(BF16)
    hf0 = jnp.swapaxes(st["ffn_conv"], 0, 1)
    out, up = _dense_tail(x2, ya, yb, yc, lw, hf0, sample=True, nseq=B, seq_len=T)
    return out, rows, win, hu, z, s_new, up


def kernel(x_prompt, x_sample, cache_kv, cache_win, state_conv_a, state_gdn_conv, state_gdn, state_ffn_conv,
           page_table, rel_bias, norm_mix_g, w_in, conv_a_w, q_norm_g, k_norm_g, cmp_pe, cmp_w1, cmp_w2,
           gdn_conv_w, gdn_a_log, gdn_dt_bias, gdn_norm_g, out_norm_a, out_norm_b, w_out, norm_ffn_g,
           ffn_up, ffn_conv_w, ffn_down):
    params = dict(norm_mix_g=norm_mix_g, w_in=w_in, conv_a_w=conv_a_w, q_norm_g=q_norm_g, k_norm_g=k_norm_g,
                  cmp_pe=cmp_pe, cmp_w1=cmp_w1, cmp_w2=cmp_w2, gdn_conv_w=gdn_conv_w, gdn_a_log=gdn_a_log,
                  gdn_dt_bias=gdn_dt_bias, gdn_norm_g=gdn_norm_g, out_norm_a=out_norm_a, out_norm_b=out_norm_b,
                  w_out=w_out, norm_ffn_g=norm_ffn_g, ffn_up=ffn_up, ffn_conv_w=ffn_conv_w, ffn_down=ffn_down)
    depth = w_in.shape[0]
    Bp, T, D = x_prompt.shape
    Bs, Ts, _ = x_sample.shape
    G, R, hd = NSA_KV_HEADS, NSA_REP, HEAD_DIM
    n_pool, page = cache_kv.shape[1], cache_kv.shape[2]
    npages = page_table.shape[1]
    past = npages * page
    wb = cache_win.shape[2]
    L = past + Ts
    assert ((L - CMP_BLOCK) // CMP_STRIDE) * CMP_STRIDE + CMP_BLOCK <= past, "compressed blocks must lie in the cache"
    assert T % Q_BLOCK == 0 and T % 512 == 0 and 3 <= Ts <= SUBLANES // 2

    thr = _bucket_thresholds()
    rb_flat = rel_bias.reshape(-1)
    nqb = T // Q_BLOCK
    bias_tiles = _bias_table(thr, rb_flat, nqb, Q_BLOCK, Q_BLOCK, a0=0, an=Q_BLOCK, qs=1, ks=1)
    bias_cmp = _bias_table(thr, rb_flat, nqb, Q_BLOCK, T // CMP_STRIDE,
                           a0=-(CMP_BLOCK - 1), an=Q_BLOCK, qs=1, ks=CMP_STRIDE)

    def sample_bias(ncols, a0, ks):
        t = _bias_table(thr, rb_flat, 1, SUBLANES, ncols, a0=a0, an=0, qs=1, ks=ks)[0]
        return t[:, :Ts].reshape(G, R * Ts, ncols)

    bias_c = sample_bias(past // CMP_STRIDE, past - (CMP_BLOCK - 1), CMP_STRIDE)
    bias_s = sample_bias(past + LANES, past, 1)
    bias_w = sample_bias(wb + LANES, wb, 1)

    cache_lin = cache_kv.reshape(depth, n_pool, page * 4 * G, hd)
    win_lin = cache_win.reshape(depth, Bs, wb * 2 * G, hd)
    assert wb == WINDOW, "the new window buffer is the old one shifted by the new rows"

    xp = x_prompt.reshape(Bp * T, D)
    xs = jnp.swapaxes(x_sample, 0, 1).reshape(Ts * Bs, D)
    outs_p, outs_s = [], []
    rows_p_all = win_s_all = gdn_s_all = None
    for l in range(depth):
        lw = _layer_weights(l, params)
        xp, rows_p_all, win, hu, z, s_new, up = _prompt_layer(xp, lw, bias_tiles, bias_cmp,
                                                              _LayerSlot(l, depth, rows_p_all), B=Bp, T=T)
        wl = min(WINDOW, T)
        outs_p.append((
            win.reshape(Bp, T, 2, G, hd)[:, T - wl:],
            hu[:, SUBLANES - 2:],
            z.reshape(Bp, T, C_END)[:, T - 3:, C_GQKV:C_GZ],
            s_new,
            up[:, SUBLANES - 2:]))
        st = dict(page_table=page_table, cache_lin=cache_lin, win_lin=win_lin, conv_a=state_conv_a[l],
                  gdn_conv=state_gdn_conv[l], gdn=state_gdn[l], ffn_conv=state_ffn_conv[l])
        xs, rows, win_s_all, hu, z, gdn_s_all, up = _sample_layer(
            xs, lw, st, bias_c, bias_s, bias_w, _LayerSlot(l, depth, win_s_all), _LayerSlot(l, depth, gdn_s_all),
            B=Bs, T=Ts)
        tb = lambda a: jnp.swapaxes(a.reshape(Ts, Bs, -1), 0, 1)
        outs_s.append((
            tb(rows).reshape(Bs, Ts, 4, G, hd),
            jnp.swapaxes(hu, 0, 1),
            tb(z)[:, Ts - 3:, C_GQKV:C_GZ],
            jnp.swapaxes(up, 0, 1)))
    y_p = xp.reshape(Bp, T, D)
    y_s = jnp.swapaxes(xs.reshape(Ts, Bs, D), 0, 1)
    stack = lambda outs, i: jnp.stack([o[i] for o in outs])
    return (y_p, y_s,
            rows_p_all.reshape(depth, Bp, T, 4, G, hd), stack(outs_s, 0),
            stack(outs_p, 0), win_s_all.reshape(depth, Bs, wb, 2, G, hd),
            stack(outs_p, 1), stack(outs_s, 1), stack(outs_p, 2), stack(outs_s, 2),
            stack(outs_p, 3), gdn_s_all, stack(outs_p, 4), stack(outs_s, 3))
```

```python
import functools
import math

import jax
import jax.numpy as jnp
from jax import lax
from jax.experimental import pallas as pl
from jax.experimental.pallas import tpu as pltpu

F32 = jnp.float32
BF16 = jnp.bfloat16
HIGHEST = lax.Precision.HIGHEST

SUBLANES = 8
LANES = 128
VMEM_LIMIT_BYTES = 52 * 1024 * 1024

HEAD_DIM = 128
NSA_KV_HEADS = 2
NSA_REP = 4
NSA_HEADS = NSA_KV_HEADS * NSA_REP
GDN_HEADS = 4
CMP_BLOCK = 32
CMP_STRIDE = 16
CMP_HIDDEN = 256
SLC_BLOCK = 64
N_SELECT = 16
WINDOW = 512
Q_BLOCK = 128
GDN_CHUNK = 64
REL_BUCKETS = 32
REL_MAX_DIST = 1024
EPS = 1e-6
NEG_INF = -1e30
FORCE_SCORE = 1e4

C_AB, C_AC, C_AH, C_NQ, C_NKV, C_GQKV, C_GZ, C_END = 0, 512, 1024, 1536, 2560, 4096, 5632, 6144
S_GATE, S_GA, S_GB = 0, 24, 28


def _cp(sem):
    return pltpu.CompilerParams(dimension_semantics=sem, vmem_limit_bytes=VMEM_LIMIT_BYTES)


class _LayerOf:
    def __init__(self, stacked, layer):
        self.array, self.layer, self.shape = stacked, layer, stacked.shape[1:]

    def spec(self, block, index_map):
        layer = self.layer
        return pl.BlockSpec((None,) + block, lambda *ij: (layer,) + tuple(index_map(*ij)))


class _LayerSlot:
    def __init__(self, layer, depth, prev):
        self.layer, self.depth, self.prev = layer, depth, prev

    def in_specs(self):
        return [] if self.prev is None else [pl.BlockSpec(memory_space=pl.ANY)]

    def args(self):
        return [] if self.prev is None else [self.prev]

    def aliases(self, n_inputs_before, out_index):
        return {} if self.prev is None else {n_inputs_before: out_index}

    def wrap(self, kernel, n_refs_before):
        if self.prev is None:
            return kernel

        def wrapped(*refs):
            return kernel(*refs[:n_refs_before], *refs[n_refs_before + 1:])
        return wrapped


def _div_pow2(x, d):
    assert d & (d - 1) == 0
    return jnp.right_shift(x, d.bit_length() - 1)


def _mod_pow2(x, d):
    assert d & (d - 1) == 0
    return jnp.bitwise_and(x, d - 1)


def _sigmoid(x):
    return 1.0 / (1.0 + jnp.exp(-x))


def _rms(x, g):
    return x * lax.rsqrt(jnp.mean(x * x, axis=-1, keepdims=True) + EPS) * g


def _l2n(x):
    return x * lax.rsqrt(jnp.sum(x * x, axis=-1, keepdims=True) + EPS)


def _dot_t(a, b, precision=None):
    return lax.dot_general(a, b, (((1,), (1,)), ((), ())), precision=precision, preferred_element_type=F32)


def _ld(ref, c0, c1):
    if len(ref.shape) == 3:
        v = ref[:, :, c0:c1]
        return v.reshape(v.shape[0] * v.shape[1], v.shape[2])
    return ref[:, c0:c1]


def _st(ref, c0, c1, val):
    if len(ref.shape) == 3:
        ref[:, :, c0:c1] = val.reshape(ref.shape[0], ref.shape[1], c1 - c0).astype(ref.dtype)
    else:
        ref[:, c0:c1] = val.astype(ref.dtype)


def _shift(x, halo, s, tstride):
    n = s * tstride
    rows = x.shape[0]
    if tstride % SUBLANES == 0:
        hr = halo.shape[0]
        return jnp.concatenate([halo[hr - n:], x[:rows - n]], axis=0)
    xs = pltpu.roll(x, n, 0)
    hs = pltpu.roll(halo, n, 0)
    rid = lax.broadcasted_iota(jnp.int32, (SUBLANES, x.shape[1]), 0)
    head = jnp.where(rid < n, hs, xs[:SUBLANES])
    return jnp.concatenate([head, xs[SUBLANES:]], axis=0)


def _norm_matmul_kernel(x_ref, g_ref, w_ref, *rest, has_small):
    if has_small:
        ws_ref, o_ref, os_ref, xn_ref = rest
    else:
        o_ref, xn_ref = rest

    @pl.when(pl.program_id(1) == 0)
    def _():
        xn = _rms(x_ref[...], g_ref[...]).astype(BF16)
        xn_ref[...] = xn
        if has_small:
            os_ref[...] = jnp.dot(xn, ws_ref[...], preferred_element_type=F32)

    o_ref[...] = jnp.dot(xn_ref[...], w_ref[...], preferred_element_type=F32).astype(o_ref.dtype)


def _row_tile(rows):
    for tm in (1024, 512):
        if rows % tm == 0:
            return tm
    return rows


def _norm_matmul(x, g, w, w_small=None, *, tn=512, out_dtype=F32):
    M, K = x.shape
    N = w.shape[1]
    tm = _row_tile(M)
    has_small = w_small is not None
    in_specs = [pl.BlockSpec((tm, K), lambda i, j: (i, 0)),
                pl.BlockSpec((1, K), lambda i, j: (0, 0)),
                pl.BlockSpec((K, tn), lambda i, j: (0, j))]
    out_specs = [pl.BlockSpec((tm, tn), lambda i, j: (i, j))]
    out_shape = [jax.ShapeDtypeStruct((M, N), out_dtype)]
    args = [x, g.reshape(1, K), w]
    if has_small:
        in_specs.append(pl.BlockSpec((K, LANES), lambda i, j: (0, 0)))
        out_specs.append(pl.BlockSpec((tm, LANES), lambda i, j: (i, 0)))
        out_shape.append(jax.ShapeDtypeStruct((M, LANES), F32))
        args.append(w_small)
    res = pl.pallas_call(
        functools.partial(_norm_matmul_kernel, has_small=has_small),
        grid=(M // tm, N // tn),
        in_specs=in_specs, out_specs=out_specs, out_shape=out_shape,
        scratch_shapes=[pltpu.VMEM((tm, K), BF16)],
        compiler_params=_cp(("parallel", "arbitrary")),
        name="norm_matmul_small" if has_small else "norm_matmul",
    )(*args)
    return res if has_small else res[0]


def _prep_kernel(z_ref, zp_ref, zs_ref, hu_ref, hq_ref, caw_ref, gcw_ref, qg_ref, kg_ref, ona_ref,
                 alog_ref, dtb_ref,
                 ya_ref, q_ref, rows_ref, win_ref, kvb_ref, gates_ref, gq_ref, ggb_ref, huo_ref, *lin_refs,
                 tstride, tiles_per_seq):
    hd = HEAD_DIM

    def put_rows(grp, val):
        _st(rows_ref, grp * hd, (grp + 1) * hd, val)
        if lin_refs:
            lin_refs[0][0, pl.ds(grp, val.shape[0], stride=4 * NSA_KV_HEADS), :] = val
    u = _ld(z_ref, C_AC, C_AH) * _ld(z_ref, C_AH, C_NQ)
    if tstride == 1:
        start = (pl.program_id(0) % tiles_per_seq) == 0
        halo_u = jnp.where(start, hu_ref[0], zp_ref[:, C_AC:C_AH] * zp_ref[:, C_AH:C_NQ])
        halo_q = jnp.where(start, hq_ref[0], zp_ref[:, C_GQKV:C_GZ])
    else:
        halo_u = _ld(hu_ref, 0, 512)
        halo_q = _ld(hq_ref, 0, 1536)
    caw = caw_ref[...]
    conv = caw[0:1] * _shift(u, halo_u, 2, tstride) + caw[1:2] * _shift(u, halo_u, 1, tstride) + caw[2:3] * u
    y = _ld(z_ref, C_AB, C_AC) * conv
    _st(ya_ref, 0, 512, _rms(y, ona_ref[...]))
    hu_rows = huo_ref.shape[0] * huo_ref.shape[1] if tstride != 1 else SUBLANES
    if tstride == 1:
        huo_ref[0] = u[u.shape[0] - hu_rows:]
    else:
        _st(huo_ref, 0, 512, u[u.shape[0] - hu_rows:])

    qg = qg_ref[...]
    kg = kg_ref[...]
    for h in range(NSA_HEADS):
        c0 = C_NQ + h * hd
        _st(q_ref, h * hd, (h + 1) * hd, _rms(_ld(z_ref, c0, c0 + hd), qg) * (hd ** -0.5))
    for grp in range(4):
        put_rows(grp, _ld(z_ref, C_NKV + grp * hd, C_NKV + (grp + 1) * hd))
    for g in range(NSA_KV_HEADS):
        c0 = C_NKV + 512 + g * hd
        kn = _rms(_ld(z_ref, c0, c0 + hd), kg)
        put_rows(4 + g, kn)
        _st(kvb_ref, g * hd, (g + 1) * hd, kn)
    for g in range(NSA_KV_HEADS):
        c0 = C_NKV + 768 + g * hd
        vs = _ld(z_ref, c0, c0 + hd)
        put_rows(6 + g, vs)
        _st(kvb_ref, 256 + g * hd, 256 + (g + 1) * hd, vs)
    for g in range(NSA_KV_HEADS):
        c0 = C_NKV + 1024 + g * hd
        kn = _rms(_ld(z_ref, c0, c0 + hd), kg)
        _st(win_ref, g * hd, (g + 1) * hd, kn)
        _st(kvb_ref, 512 + g * hd, 512 + (g + 1) * hd, kn)
    vw = _ld(z_ref, C_NKV + 1280, C_NKV + 1536)
    _st(win_ref, 256, 512, vw)
    _st(kvb_ref, 768, 1024, vw)

    zs = _ld(zs_ref, 0, LANES)
    sg = _sigmoid(zs)
    _st(gates_ref, 0, LANES, sg)
    xs = zs + dtb_ref[...]
    e = jnp.exp(-jnp.abs(xs))
    u1 = 1.0 + e
    log1p_e = jnp.where(u1 == 1.0, e, jnp.log(u1) * (e / jnp.where(u1 == 1.0, 1.0, u1 - 1.0)))
    softplus = jnp.maximum(xs, 0.0) + log1p_e
    gdec = -jnp.exp(alog_ref[...]) * softplus
    rows = zs.shape[0]
    if tstride == 1:
        in_chunk = _mod_pow2(lax.broadcasted_iota(jnp.int32, gdec.shape, 0), GDN_CHUNK)
        step = 1
        while step < GDN_CHUNK:
            gdec = gdec + jnp.where(in_chunk >= step, pltpu.roll(gdec, step, 0), 0.0)
            step *= 2
    for h in range(GDN_HEADS):
        _st(ggb_ref, h * hd, (h + 1) * hd, jnp.broadcast_to(gdec[:, S_GA + h:S_GA + h + 1], (rows, hd)))
        _st(ggb_ref, 512 + h * hd, 512 + (h + 1) * hd, jnp.broadcast_to(sg[:, S_GB + h:S_GB + h + 1], (rows, hd)))

    gcw = gcw_ref[...]
    for part in range(3):
        c0 = C_GQKV + part * 512
        x = _ld(z_ref, c0, c0 + 512)
        hq = halo_q[:, part * 512:(part + 1) * 512]
        w = gcw[:, part * 512:(part + 1) * 512]
        c = (w[0:1] * _shift(x, hq, 3, tstride) + w[1:2] * _shift(x, hq, 2, tstride)
             + w[2:3] * _shift(x, hq, 1, tstride) + w[3:4] * x)
        c = c * _sigmoid(c)
        if part == 2:
            _st(gq_ref, 1024, 1536, c)
        else:
            for h in range(GDN_HEADS):
                v = _l2n(c[:, h * hd:(h + 1) * hd])
                if part == 0:
                    v = v * (hd ** -0.5)
                _st(gq_ref, part * 512 + h * hd, part * 512 + (h + 1) * hd, v)


def _prep(z, zs, hist_u, hist_q, lw, *, sample, nseq, seq_len, slot=None):
    M = z.shape[0]
    wts = [lw["conv_a_w"], lw["gdn_conv_w"], lw["q_norm_g"].reshape(1, -1), lw["k_norm_g"].reshape(1, -1),
           lw["out_norm_a"].reshape(1, -1), lw["alog_v"], lw["dtb_v"]]
    wspecs2 = [pl.BlockSpec(w.shape, lambda i: (0, 0)) for w in wts]
    widths = [(512, BF16), (1024, BF16), (1024, F32), (512, F32), (1024, BF16), (LANES, F32), (1536, F32), (1024, F32)]
    if not sample:
        tm = 256
        tps = seq_len // tm
        in_specs = [pl.BlockSpec((tm, C_END), lambda i: (i, 0)),
                    pl.BlockSpec((SUBLANES, C_END), lambda i: (jnp.maximum(i * (tm // SUBLANES) - 1, 0), 0)),
                    pl.BlockSpec((tm, LANES), lambda i: (i, 0)),
                    pl.BlockSpec((1, SUBLANES, 512), lambda i: (i // tps, 0, 0)),
                    pl.BlockSpec((1, SUBLANES, 1536), lambda i: (i // tps, 0, 0))] + wspecs2
        out_specs = [pl.BlockSpec((tm, w), lambda i: (i, 0)) for w, _ in widths]
        out_specs.append(pl.BlockSpec((1, SUBLANES, 512), lambda i: (i // tps, 0, 0)))
        out_shape = [jax.ShapeDtypeStruct((M, w), d) for w, d in widths]
        out_shape.append(jax.ShapeDtypeStruct((nseq, SUBLANES, 512), F32))
        kvr = 4 * NSA_KV_HEADS
        layer = slot.layer
        out_specs.append(pl.BlockSpec((1, tm * kvr, HEAD_DIM), lambda i: (layer, i, 0)))
        out_shape.append(jax.ShapeDtypeStruct((slot.depth, M * kvr, HEAD_DIM), F32))
        n_in = len(in_specs)
        return pl.pallas_call(
            slot.wrap(functools.partial(_prep_kernel, tstride=1, tiles_per_seq=tps), n_in),
            grid=(M // tm,), in_specs=in_specs + slot.in_specs(), out_specs=out_specs, out_shape=out_shape,
            input_output_aliases=slot.aliases(n_in, len(out_shape) - 1),
            compiler_params=_cp(("arbitrary",)), name="prep_prompt",
        )(z, z, zs, hist_u, hist_q, *wts, *slot.args())
    T = seq_len
    bt = 64
    z3 = z.reshape(T, nseq, C_END)
    zs3 = zs.reshape(T, nseq, LANES)
    in_specs = [pl.BlockSpec((T, bt, C_END), lambda i: (0, i, 0)),
                pl.BlockSpec((SUBLANES, C_END), lambda i: (0, 0)),
                pl.BlockSpec((T, bt, LANES), lambda i: (0, i, 0)),
                pl.BlockSpec((hist_u.shape[0], bt, 512), lambda i: (0, i, 0)),
                pl.BlockSpec((hist_q.shape[0], bt, 1536), lambda i: (0, i, 0))] + wspecs2
    out_specs = [pl.BlockSpec((T, bt, w), lambda i: (0, i, 0)) for w, _ in widths]
    out_specs.append(pl.BlockSpec((hist_u.shape[0], bt, 512), lambda i: (0, i, 0)))
    out_shape = [jax.ShapeDtypeStruct((T, nseq, w), d) for w, d in widths]
    out_shape.append(jax.ShapeDtypeStruct((hist_u.shape[0], nseq, 512), F32))
    outs = pl.pallas_call(
        functools.partial(_prep_kernel, tstride=bt, tiles_per_seq=1),
        grid=(nseq // bt,), in_specs=in_specs, out_specs=out_specs, out_shape=out_shape,
        compiler_params=_cp(("arbitrary",)), name="prep_sample",
    )(z3, z, zs3, hist_u, hist_q, *wts)
    return [o.reshape(M, o.shape[-1]) for o in outs[:-1]] + [outs[-1]]


def _bias_kernel(thr_ref, rb_ref, o_ref, *, a0, an, qs, ks):
    n = pl.program_id(0)
    R, C = o_ref.shape[-2:]

    def rows8(rc, carry):
        r0 = pl.multiple_of(rc * SUBLANES, SUBLANES)
        dist = (a0 + an * n + qs * (r0 + lax.broadcasted_iota(jnp.int32, (SUBLANES, C), 0))
                - ks * lax.broadcasted_iota(jnp.int32, (SUBLANES, C), 1))
        b = [jnp.full((SUBLANES, C), rb_ref[h], F32) for h in range(NSA_HEADS)]
        for k in range(1, REL_BUCKETS):
            reached = dist >= thr_ref[k]
            b = [jnp.where(reached, rb_ref[k * NSA_HEADS + h], b[h]) for h in range(NSA_HEADS)]
        for h in range(NSA_HEADS):
            o_ref[0, h, pl.ds(r0, SUBLANES), :] = b[h]
        return carry

    lax.fori_loop(0, R // SUBLANES, rows8, 0)


def _bias_table(thr, rb_flat, n, R, C, *, a0, an, qs, ks):
    return pl.pallas_call(
        functools.partial(_bias_kernel, a0=a0, an=an, qs=qs, ks=ks),
        grid=(n,),
        in_specs=[pl.BlockSpec(memory_space=pltpu.SMEM), pl.BlockSpec(memory_space=pltpu.SMEM)],
        out_specs=pl.BlockSpec((1, NSA_HEADS, R, C), lambda i: (i, 0, 0, 0)),
        out_shape=jax.ShapeDtypeStruct((n, NSA_HEADS, R, C), F32),
        compiler_params=_cp(("parallel",)), name="bias_table",
    )(thr, rb_flat)


def _bucket_thresholds():
    n = jnp.arange(REL_MAX_DIST + 1)
    exact = REL_BUCKETS // 2
    nf = jnp.maximum(n, 1).astype(F32)
    far = exact + (jnp.log(nf / exact) / math.log(REL_MAX_DIST / exact) * (REL_BUCKETS - exact)).astype(jnp.int32)
    bucket = jnp.where(n < exact, n, jnp.minimum(far, REL_BUCKETS - 1))
    return jnp.sum(bucket[None, :] < jnp.arange(REL_BUCKETS)[:, None], axis=1).astype(jnp.int32)


def _gelu_tanh(x):
    return x * (0.5 * (1.0 + jnp.tanh(math.sqrt(2.0 / math.pi) * (x + 0.044715 * (x * x * x)))))


def _compress_math(get_rows, ng, pe, w1_ref, w2):
    half = CMP_BLOCK // 2
    rows = [get_rows(r) for r in range(half)]
    top = jnp.dot(jnp.concatenate([(rows[r] + pe[r:r + 1]).astype(BF16) for r in range(half)], axis=-1),
                  w1_ref[0], preferred_element_type=F32)
    bot = jnp.dot(jnp.concatenate([(rows[r] + pe[r + half:r + half + 1]).astype(BF16) for r in range(half)], axis=-1),
                  w1_ref[1], preferred_element_type=F32)
    h = top + pltpu.roll(bot, ng - 1, 0)
    return jnp.dot(_gelu_tanh(h).astype(BF16), w2, preferred_element_type=F32)


def _compress_kernel(x_ref, pe_ref, w1_ref, w2_ref, kg_ref, o_ref):
    ng = x_ref.shape[1] // CMP_STRIDE
    slot = pl.program_id(1)

    def get_rows(r):
        return x_ref[0, pl.ds(r, ng, stride=CMP_STRIDE), :]

    out = _compress_math(get_rows, ng, pe_ref[0], w1_ref.at[0], w2_ref[0])
    o_ref[0, 0, 0] = jnp.where(slot == 0, _rms(out, kg_ref[...]), out).astype(o_ref.dtype)


def _compress_prompt(rows3, lw):
    B, T, _ = rows3.shape
    ng = T // CMP_STRIDE
    G = NSA_KV_HEADS
    return pl.pallas_call(
        _compress_kernel,
        grid=(B, 2, G),
        in_specs=[pl.BlockSpec((1, T, HEAD_DIM), lambda b, s, g: (b, 0, s * G + g)),
                  pl.BlockSpec((1, CMP_BLOCK, HEAD_DIM), lambda b, s, g: (s, 0, 0)),
                  pl.BlockSpec((1,) + lw["cmp_w1"].shape[1:], lambda b, s, g: (s, 0, 0, 0)),
                  pl.BlockSpec((1, CMP_HIDDEN, HEAD_DIM), lambda b, s, g: (s, 0, 0)),
                  pl.BlockSpec((1, HEAD_DIM), lambda b, s, g: (0, 0))],
        out_specs=pl.BlockSpec((1, 1, 1, ng, HEAD_DIM), lambda b, s, g: (b, s, g, 0, 0)),
        out_shape=jax.ShapeDtypeStruct((B, 2, G, ng, HEAD_DIM), BF16),
        compiler_params=_cp(("parallel", "parallel", "parallel")), name="compress_prompt",
    )(rows3, lw["cmp_pe"], lw["cmp_w1"], lw["cmp_w2"], lw["k_norm_g"].reshape(1, -1))


def _masked_softmax(s, mask):
    s = jnp.where(mask, s, NEG_INF)
    m = jnp.max(s, axis=-1, keepdims=True)
    e = jnp.where(mask, jnp.exp(s - m), 0.0)
    z = jnp.sum(e, axis=-1, keepdims=True)
    return e / jnp.where(z > 0, z, 1.0)


def _overlap_matrix(ncp):
    c = CMP_STRIDE * lax.broadcasted_iota(jnp.int32, (ncp, LANES), 0)
    s = SLC_BLOCK * lax.broadcasted_iota(jnp.int32, (ncp, LANES), 1)
    return jnp.where(c < s + SLC_BLOCK, jnp.where(c + CMP_BLOCK > s, 1.0, 0.0), 0.0).astype(F32)


def _select_mask(imp, cur, n_slc):
    R = imp.shape[0]
    blk = lax.broadcasted_iota(jnp.int32, (R, LANES), 1)
    for forced_blk in (0, cur, cur - 1):
        imp = jnp.where(blk == forced_blk, FORCE_SCORE, imp)
    imp = jnp.where(blk <= cur, imp, -1.0)
    imp = jnp.where(blk < n_slc, imp, -2.0)
    cnt = jnp.zeros((R, LANES), F32)
    for k in range(n_slc):
        col = imp[:, k:k + 1]
        cnt = cnt + jnp.where(col > imp, 1.0, jnp.where(col == imp, jnp.where(blk > k, 1.0, 0.0), 0.0))
    return jnp.where(cnt < float(min(N_SELECT, n_slc)), jnp.where(blk < n_slc, 1.0, 0.0), 0.0)


def _expand_matrix(nkeys):
    s = lax.broadcasted_iota(jnp.int32, (LANES, nkeys), 0)
    j = lax.broadcasted_iota(jnp.int32, (LANES, nkeys), 1)
    lo = s * SLC_BLOCK
    return jnp.where(j >= lo, jnp.where(j < lo + SLC_BLOCK, 1.0, 0.0), 0.0).astype(BF16)


KEY_TILE = 512


def _select_mask_t(imp_t, cur, n_slc):
    blk = lax.broadcasted_iota(jnp.int32, imp_t.shape, 0)
    for forced_blk in (0, cur, cur - 1):
        imp_t = jnp.where(blk == forced_blk, FORCE_SCORE, imp_t)
    imp_t = jnp.where(blk <= cur, imp_t, -1.0)
    cnt = jnp.zeros(imp_t.shape, F32)
    for k in range(n_slc):
        row = imp_t[k:k + 1, :]
        cnt = cnt + jnp.where(row > imp_t, 1.0, jnp.where(row == imp_t, jnp.where(blk > k, 1.0, 0.0), 0.0))
    return jnp.where(cnt < float(min(N_SELECT, n_slc)), 1.0, 0.0)


def _nsa_prompt_kernel(q_ref, kc_ref, vc_ref, ks_ref, vs_ref, kw_ref, vw_ref, bt_ref, bc_ref, gates_ref, ex_ref,
                       o_ref, m_ref, acc_ref, *, T):
    qb = pl.program_id(2)
    QB = Q_BLOCK
    R = NSA_REP
    RQ = R * QB
    hd = HEAD_DIM
    nc = (T - CMP_BLOCK) // CMP_STRIDE + 1
    ncp = kc_ref.shape[-2]
    n_slc = ex_ref.shape[0]
    q = jnp.concatenate([q_ref[:, r * hd:(r + 1) * hd] for r in range(R)], axis=0)

    s = _dot_t(q, kc_ref[0, 0, 0]) + bc_ref[0].reshape(RQ, ncp)
    t_row = lax.broadcasted_iota(jnp.int32, (R, QB, ncp), 1).reshape(RQ, ncp)
    c_col = lax.broadcasted_iota(jnp.int32, (RQ, ncp), 1)
    dist = qb * QB + t_row - (CMP_STRIDE * c_col + CMP_BLOCK - 1)
    p = _masked_softmax(s, jnp.where(c_col < nc, dist, -1) >= 0)
    o_cmp = jnp.dot(p.astype(BF16), vc_ref[0, 0, 0], preferred_element_type=F32)

    psum = p[0:QB]
    for r in range(1, R):
        psum = psum + p[r * QB:(r + 1) * QB]
    s_lo = SLC_BLOCK * lax.broadcasted_iota(jnp.int32, (n_slc, ncp), 0)
    c_lo = CMP_STRIDE * lax.broadcasted_iota(jnp.int32, (n_slc, ncp), 1)
    overlap_t = jnp.where(c_lo < s_lo + SLC_BLOCK, jnp.where(c_lo + CMP_BLOCK > s_lo, 1.0, 0.0), 0.0).astype(F32)
    imp_t = _dot_t(overlap_t, psum, HIGHEST)
    cur = _div_pow2(qb * QB + lax.broadcasted_iota(jnp.int32, (1, QB), 1), SLC_BLOCK)
    sel_t = _select_mask_t(imp_t, cur, n_slc).astype(BF16)

    NCHAIN = 2
    hpc = R // NCHAIN
    crow = hpc * QB

    def bias_rows(c, first_blk, n):
        return jnp.concatenate(
            [bt_ref[jnp.maximum(qb - (first_blk + j), 0), c * hpc:(c + 1) * hpc].reshape(crow, QB) for j in range(n)],
            axis=-1)

    def rel(first_key, nk):
        return (qb * QB - first_key + lax.broadcasted_iota(jnp.int32, (QB, nk), 0)
                - lax.broadcasted_iota(jnp.int32, (QB, nk), 1))

    def with_ones(v):
        return jnp.concatenate([v, jnp.ones(v.shape, BF16)], axis=-1)

    nkb = KEY_TILE // QB
    m_ref[...] = jnp.full(m_ref.shape, NEG_INF, F32)
    acc_ref[...] = jnp.zeros(acc_ref.shape, F32)

    def body(kt, carry):
        off = pl.multiple_of(kt * KEY_TILE, KEY_TILE)
        k = ks_ref[0, pl.ds(off, KEY_TILE), :]
        vext = with_ones(vs_ref[0, pl.ds(off, KEY_TILE), :])
        selk = lax.dot_general(sel_t, ex_ref[:, pl.ds(off, KEY_TILE)], (((0,), (0,)), ((), ())),
                               preferred_element_type=F32)
        d = jnp.where(selk > 0.5, rel(off, KEY_TILE), -1)
        neg = jnp.concatenate([jnp.where(d >= 0, 0.0, NEG_INF)] * hpc, axis=0)
        for c in range(NCHAIN):
            rows = slice(c * crow, (c + 1) * crow)
            sc = _dot_t(q[rows], k) + bias_rows(c, kt * nkb, nkb) + neg
            m_old = m_ref[rows]
            m_new = jnp.maximum(m_old, jnp.max(sc, axis=-1, keepdims=True))
            pe = jnp.exp(sc - m_new).astype(BF16)
            acc_ref[rows] = jnp.exp(m_old - m_new) * acc_ref[rows] + jnp.dot(pe, vext, preferred_element_type=F32)
            m_ref[rows] = m_new
        return carry

    lax.fori_loop(0, qb // nkb + 1, body, 0)
    acc = acc_ref[...]
    den = acc[:, hd:]
    o_slc = acc[:, :hd] / jnp.where(den > 0, den, 1.0)

    wk = min(WINDOW + QB, T)
    w0 = jnp.maximum(qb - WINDOW // QB, 0)
    offw = pl.multiple_of(w0 * QB, QB)
    kwin = kw_ref[0, pl.ds(offw, wk), :]
    vext = with_ones(vw_ref[0, pl.ds(offw, wk), :])
    d = rel(offw, wk)
    d = jnp.where(d <= WINDOW, d, -1)
    neg = jnp.concatenate([jnp.where(d >= 0, 0.0, NEG_INF)] * hpc, axis=0)
    o_win = []
    for c in range(NCHAIN):
        sc = _dot_t(q[c * crow:(c + 1) * crow], kwin) + bias_rows(c, w0, wk // QB) + neg
        e = jnp.exp(sc - jnp.max(sc, axis=-1, keepdims=True)).astype(BF16)
        acc = jnp.dot(e, vext, preferred_element_type=F32)
        o_win.append(acc[:, :hd] / acc[:, hd:])
    o_win = jnp.concatenate(o_win, axis=0)

    gt = gates_ref[0]
    for r in range(R):
        sl = slice(r * QB, (r + 1) * QB)
        o = (gt[:, r:r + 1] * o_cmp[sl] + gt[:, R + r:R + r + 1] * o_slc[sl]
             + gt[:, 2 * R + r:2 * R + r + 1] * o_win[sl])
        o_ref[:, r * hd:(r + 1) * hd] = o


def _nsa_prompt(q, cmp_kv, kvb3, gates_g, bias_tiles, bias_cmp, *, B, T):
    G = NSA_KV_HEADS
    R = NSA_REP
    nqb = T // Q_BLOCK
    ncp = T // CMP_STRIDE
    n_slc = -(-T // SLC_BLOCK)
    expand = (jnp.arange(T)[None, :] // SLC_BLOCK == jnp.arange(n_slc)[:, None]).astype(BF16)
    kv_spec = lambda col: pl.BlockSpec((1, T, HEAD_DIM), lambda b, g, i, col=col: (b, 0, col + g))
    return pl.pallas_call(
        functools.partial(_nsa_prompt_kernel, T=T),
        grid=(B, G, nqb),
        in_specs=[pl.BlockSpec((Q_BLOCK, R * HEAD_DIM), lambda b, g, i: (b * nqb + i, g)),
                  pl.BlockSpec((1, 1, 1, ncp, HEAD_DIM), lambda b, g, i: (b, 0, g, 0, 0)),
                  pl.BlockSpec((1, 1, 1, ncp, HEAD_DIM), lambda b, g, i: (b, 1, g, 0, 0)),
                  kv_spec(0), kv_spec(2), kv_spec(4), kv_spec(6),
                  pl.BlockSpec((nqb, R, Q_BLOCK, Q_BLOCK), lambda b, g, i: (0, g, 0, 0)),
                  pl.BlockSpec((1, R, Q_BLOCK, ncp), lambda b, g, i: (i, g, 0, 0)),
                  pl.BlockSpec((1, Q_BLOCK, LANES), lambda b, g, i: (g, b * nqb + i, 0)),
                  pl.BlockSpec((n_slc, T), lambda b, g, i: (0, 0))],
        out_specs=pl.BlockSpec((Q_BLOCK, R * HEAD_DIM), lambda b, g, i: (b * nqb + i, g)),
        out_shape=jax.ShapeDtypeStruct((B * T, G * R * HEAD_DIM), F32),
        scratch_shapes=[pltpu.VMEM((R * Q_BLOCK, 1), F32),
                        pltpu.VMEM((R * Q_BLOCK, 2 * HEAD_DIM), F32)],
        compiler_params=_cp(("parallel", "parallel", "arbitrary")), name="nsa_prompt",
    )(q, cmp_kv, cmp_kv, kvb3, kvb3, kvb3, kvb3, bias_tiles, bias_cmp, gates_g, expand)


def _nsa_sample_kernel(pt_ref, *refs, npages, tq):
    pages = refs[:npages]
    (win_ref, wnew_ref, q_ref, kvn_ref, gates_ref, pe_ref, w1_ref, w2_ref, kg_ref, bc_ref, bs_ref, bw_ref,
     o_ref, wout_ref) = refs[npages:]
    G, R, hd = NSA_KV_HEADS, NSA_REP, HEAD_DIM
    KVR = 4 * G
    WR = 2 * G
    page = pages[0].shape[2] // KVR
    past = npages * page
    L = past + tq
    nc = (L - CMP_BLOCK) // CMP_STRIDE + 1
    ng = past // CMP_STRIDE
    n_slc = -(-L // SLC_BLOCK)
    rows = R * tq
    wb = win_ref.shape[2] // WR
    wout_ref[0, 0, 0:(wb - tq) * WR, :] = win_ref[0, 0, tq * WR:wb * WR, :]
    wout_ref[0, 0, (wb - tq) * WR:wb * WR, :] = wnew_ref[0]
    pad_new = lambda v: jnp.concatenate([v, jnp.zeros((LANES - v.shape[0], v.shape[1]), v.dtype)], axis=0)
    t_of_row = lax.broadcasted_iota(jnp.int32, (R, tq, 1), 1).reshape(rows, 1)
    pos = past + t_of_row
    ra = lax.broadcasted_iota(jnp.int32, (rows, rows), 0)
    rb = lax.broadcasted_iota(jnp.int32, (rows, rows), 1)
    same_t = jnp.where(_mod_pow2(ra, tq) == _mod_pow2(rb, tq), 1.0, 0.0).astype(F32)
    kg = kg_ref[...]
    kvn = kvn_ref[0]

    for g in range(G):
        q = q_ref[0, g]
        gt = gates_ref[0, g]

        def compress(slot):
            col = slot * G + g

            def get_rows(r):
                return jnp.concatenate(
                    [pg[0, 0, pl.ds(r * KVR + col, page // CMP_STRIDE, stride=CMP_STRIDE * KVR), :] for pg in pages],
                    axis=0)

            return _compress_math(get_rows, ng, pe_ref[slot], w1_ref.at[slot], w2_ref[slot])

        kc = _rms(compress(0), kg).astype(BF16)
        vc = compress(1).astype(BF16)

        c_col = lax.broadcasted_iota(jnp.int32, (rows, ng), 1)
        dist = pos - (CMP_STRIDE * c_col + CMP_BLOCK - 1)
        p = _masked_softmax(_dot_t(q, kc) + bc_ref[g], jnp.where(c_col < nc, dist, -1) >= 0)
        o_cmp = jnp.dot(p.astype(BF16), vc, preferred_element_type=F32)

        psum = jnp.dot(same_t, p, precision=HIGHEST, preferred_element_type=F32)
        imp = jnp.dot(psum, _overlap_matrix(ng), precision=HIGHEST, preferred_element_type=F32)
        sel = _select_mask(imp, _div_pow2(pos, SLC_BLOCK), n_slc)
        nk = past + LANES
        selk = jnp.dot(sel.astype(BF16), _expand_matrix(nk), preferred_element_type=F32)

        k_new = pad_new(kvn[:, g * hd:(g + 1) * hd])
        v_new = pad_new(kvn[:, 256 + g * hd:256 + (g + 1) * hd])
        all_rows = lambda slot, new: jnp.concatenate(
            [pg[0, 0, pl.ds(slot * G + g, page, stride=KVR), :].astype(BF16) for pg in pages] + [new], axis=0)
        sc = _dot_t(q, all_rows(2, k_new)) + bs_ref[g]
        j = lax.broadcasted_iota(jnp.int32, (rows, nk), 1)
        p = _masked_softmax(sc, jnp.where(selk > 0.5, pos - j, -1) >= 0).astype(BF16)
        o_slc = jnp.dot(p, all_rows(3, v_new), preferred_element_type=F32)

        kw_new = pad_new(kvn[:, 512 + g * hd:512 + (g + 1) * hd])
        vw_new = pad_new(kvn[:, 768 + g * hd:768 + (g + 1) * hd])
        kw = win_ref[0, 0, pl.ds(g, wb, stride=WR), :].astype(BF16)
        vw = win_ref[0, 0, pl.ds(G + g, wb, stride=WR), :].astype(BF16)
        sc = _dot_t(q, jnp.concatenate([kw, kw_new], axis=0)) + bw_ref[g]
        j = lax.broadcasted_iota(jnp.int32, (rows, wb + LANES), 1)
        d = pos - (past - wb + j)
        p = _masked_softmax(sc, jnp.where(d <= WINDOW, d, -1) >= 0).astype(BF16)
        o_win = jnp.dot(p, jnp.concatenate([vw, vw_new], axis=0), preferred_element_type=F32)

        o_ref[0, g] = gt[:, 0:1] * o_cmp + gt[:, 1:2] * o_slc + gt[:, 2:3] * o_win


def _nsa_sample(slot, page_table, cache_lin, win_lin, win_new, q_s, kvn_s, gates_s, lw, bias_c, bias_s, bias_w, *, tq):
    B, npages = page_table.shape
    G = NSA_KV_HEADS
    rows = NSA_REP * tq
    l = slot.layer
    page_specs = [pl.BlockSpec((1, 1) + cache_lin.shape[2:], lambda b, pt, p=p: (l, pt[b, p], 0, 0))
                  for p in range(npages)]
    full = lambda a: pl.BlockSpec(a.shape, lambda b, pt, n=a.ndim: (0,) * n)
    wts = [lw["cmp_pe"], lw["cmp_w1"], lw["cmp_w2"], lw["k_norm_g"].reshape(1, -1), bias_c, bias_s, bias_w]
    grid_spec = pltpu.PrefetchScalarGridSpec(
        num_scalar_prefetch=1, grid=(B,),
        in_specs=page_specs + [
            pl.BlockSpec((1, 1) + win_lin.shape[2:], lambda b, pt: (l, b, 0, 0)),
            pl.BlockSpec((1,) + win_new.shape[1:], lambda b, pt: (b, 0, 0)),
            pl.BlockSpec((1, G, rows, HEAD_DIM), lambda b, pt: (b, 0, 0, 0)),
            pl.BlockSpec((1, SUBLANES, kvn_s.shape[2]), lambda b, pt: (b, 0, 0)),
            pl.BlockSpec((1, G, rows, LANES), lambda b, pt: (b, 0, 0, 0))] + [full(w) for w in wts]
        + slot.in_specs(),
        out_specs=[pl.BlockSpec((1, G, rows, HEAD_DIM), lambda b, pt: (b, 0, 0, 0)),
                   pl.BlockSpec((1, 1) + win_lin.shape[2:], lambda b, pt: (l, b, 0, 0))])
    n_in = 1 + npages + 5 + len(wts)
    return pl.pallas_call(
        slot.wrap(functools.partial(_nsa_sample_kernel, npages=npages, tq=tq), n_in),
        grid_spec=grid_spec,
        out_shape=[jax.ShapeDtypeStruct((B, G, rows, HEAD_DIM), F32),
                   jax.ShapeDtypeStruct((slot.depth,) + win_lin.shape[1:], F32)],
        input_output_aliases=slot.aliases(n_in, 1),
        compiler_params=_cp(("arbitrary",)), name="nsa_sample",
    )(page_table, *([cache_lin] * npages), win_lin, win_new, q_s, kvn_s, gates_s, *wts, *slot.args())


def _dot3(a, b):
    ah = a.astype(BF16)
    bh = b.astype(BF16)
    al = (a - ah.astype(F32)).astype(BF16)
    bl = (b - bh.astype(F32)).astype(BF16)
    d = functools.partial(jnp.dot, preferred_element_type=F32)
    return d(ah, bh) + (d(al, bh) + d(ah, bl))


def _gdn_chunk_kernel(gq_ref, ggb_ref, n_ref, k2_ref, qe_ref, o0_ref, eg_ref):
    B, C = gq_ref.shape[:2]
    H = GDN_HEADS
    hd = HEAD_DIM
    W = H * hd
    HC = H * C
    ri = lax.broadcasted_iota(jnp.int32, (HC, HC), 0)
    ci = lax.broadcasted_iota(jnp.int32, (HC, HC), 1)
    tri = jnp.where(_div_pow2(ri, C) == _div_pow2(ci, C), ri - ci, -1)
    own_lanes = (_div_pow2(lax.broadcasted_iota(jnp.int32, (HC, W), 0), C)
                 == _div_pow2(lax.broadcasted_iota(jnp.int32, (HC, W), 1), hd))

    for b in range(B):
        stack = lambda ref, c0: jnp.concatenate([ref[b, :, c0 + h * hd:c0 + (h + 1) * hd] for h in range(H)], axis=0)
        q = stack(gq_ref, 0)
        k = stack(gq_ref, W)
        v = stack(gq_ref, 2 * W)
        gc = stack(ggb_ref, 0)
        beta = stack(ggb_ref, W)
        gcol = jnp.concatenate([gc] * (HC // hd), axis=-1)
        grow = gcol.T
        decay = jnp.exp(jnp.where(tri >= 0, gcol - grow, -jnp.inf))
        kb = k * beta
        g2 = _dot_t(jnp.concatenate([kb, q], axis=0).astype(BF16), k.astype(BF16))
        low = jnp.where(tri > 0, g2[:HC] * decay, 0.0)
        qk = jnp.where(tri >= 0, g2[HC:] * decay, 0.0)
        egc = jnp.exp(gc)
        x = jnp.concatenate([v * beta, kb * egc], axis=-1)
        pw = low
        span = 1
        while span < C:
            last = span * 2 >= C
            rhs = x if last else jnp.concatenate([pw, x], axis=-1)
            r = _dot3(pw, rhs)
            px = r if last else r[:, HC:]
            x = x - px if span == 1 else x + px
            if not last:
                pw = r[:, :HC]
            span *= 2
        uw = x.astype(BF16)
        g_last = jnp.concatenate([jnp.broadcast_to(gc[(h + 1) * C - 1:(h + 1) * C], (C, hd)) for h in range(H)],
                                 axis=0)
        kd = k * jnp.exp(g_last - gc)
        kd_bd = jnp.where(own_lanes, jnp.concatenate([kd] * H, axis=-1), 0.0).astype(BF16)
        nk = lax.dot_general(kd_bd, uw, (((0,), (0,)), ((), ())), preferred_element_type=F32)
        ow = jnp.dot(qk.astype(BF16), uw, preferred_element_type=F32)
        qe = q * egc - ow[:, hd:]
        eg = jnp.exp(g_last)
        for h in range(H):
            n_ref[b, h, 0] = nk[h * hd:(h + 1) * hd, :hd]
            k2_ref[b, h, 0] = nk[h * hd:(h + 1) * hd, hd:].astype(BF16)
            o0_ref[b, h, 0] = ow[h * C:(h + 1) * C, :hd]
            qe_ref[b, h, 0] = qe[h * C:(h + 1) * C].astype(BF16)
            eg_ref[b, h, 0] = eg[h * C:h * C + SUBLANES]


def _gdn_scan_kernel(n_ref, k2_ref, qe_ref, o0_ref, eg_ref, gz_ref, s0_ref, ng_ref, y_ref, so_ref, s_ref):
    n = pl.program_id(0)
    B = gz_ref.shape[0]
    hd = HEAD_DIM

    @pl.when(n == 0)
    def _():
        s_ref[...] = s0_ref[...]

    ng = ng_ref[...]
    for b in range(B):
        for h in range(GDN_HEADS):
            S = s_ref[b, h]
            s16 = S.astype(BF16)
            o = jnp.dot(qe_ref[b, h, 0], s16, preferred_element_type=F32) + o0_ref[b, h, 0]
            s_ref[b, h] = (eg_ref[b, h, 0][0:1] * S + n_ref[b, h, 0]
                           - jnp.dot(k2_ref[b, h, 0], s16, preferred_element_type=F32))
            z = gz_ref[b, :, h * hd:(h + 1) * hd]
            y_ref[b, :, h * hd:(h + 1) * hd] = (_rms(o, ng) * (z * _sigmoid(z))).astype(y_ref.dtype)

    @pl.when(n == pl.num_programs(0) - 1)
    def _():
        so_ref[...] = s_ref[...]


def _gdn_prompt(gq3, ggb3, z3, s0, norm_g):
    B, T, _ = gq3.shape
    C = GDN_CHUNK
    H = GDN_HEADS
    hd = HEAD_DIM
    W = H * hd
    nchunk = T // C
    item = lambda rows: pl.BlockSpec((B, H, 1, rows, hd), lambda n: (0, 0, n, 0, 0))
    shape = lambda rows, dt: jax.ShapeDtypeStruct((B, H, nchunk, rows, hd), dt)
    nn, k2, qe, o0, eg = pl.pallas_call(
        _gdn_chunk_kernel,
        grid=(nchunk,),
        in_specs=[pl.BlockSpec((B, C, 3 * W), lambda n: (0, n, 0)),
                  pl.BlockSpec((B, C, 2 * W), lambda n: (0, n, 0))],
        out_specs=[item(hd), item(hd), item(C), item(C), item(SUBLANES)],
        out_shape=[shape(hd, F32), shape(hd, BF16), shape(C, BF16), shape(C, F32), shape(SUBLANES, F32)],
        compiler_params=_cp(("parallel",)), name="gdn_chunk",
    )(gq3, ggb3)
    return pl.pallas_call(
        _gdn_scan_kernel,
        grid=(nchunk,),
        in_specs=[item(hd), item(hd), item(C), item(C), item(SUBLANES),
                  pl.BlockSpec((B, C, W), lambda n: (0, n, C_GZ // W)),
                  pl.BlockSpec(s0.shape, lambda n: (0, 0, 0, 0)),
                  pl.BlockSpec((1, hd), lambda n: (0, 0))],
        out_specs=[pl.BlockSpec((B, C, W), lambda n: (0, n, 0)),
                   pl.BlockSpec(s0.shape, lambda n: (0, 0, 0, 0))],
        out_shape=[jax.ShapeDtypeStruct((B, T, W), BF16), jax.ShapeDtypeStruct(s0.shape, F32)],
        scratch_shapes=[pltpu.VMEM(s0.shape, F32)],
        compiler_params=_cp(("arbitrary",)), name="gdn_scan",
    )(nn, k2, qe, o0, eg, z3, s0, norm_g.reshape(1, -1))


def _gdn_sample_kernel(kq_ref, v_ref, gb_ref, gz_ref, s0_ref, ng_ref, y_ref, so_ref, *, tq):
    bt = kq_ref.shape[0]
    hd = HEAD_DIM
    ri = lax.broadcasted_iota(jnp.int32, (hd, hd), 0)
    ci = lax.broadcasted_iota(jnp.int32, (hd, hd), 1)
    eye = jnp.where(ri == ci, 1.0, 0.0).astype(F32)
    ng = ng_ref[...]

    def body(bi, carry):
        for h in range(GDN_HEADS):
            kq = kq_ref[bi, h]
            cols = _dot_t(eye, kq, HIGHEST)
            gb = gb_ref[bi, h]
            v = v_ref[bi, h]
            S = s0_ref[bi, h]
            outs = []
            for t in range(tq):
                a = jnp.exp(gb[t:t + 1])
                kc = cols[:, t:t + 1]
                qc = cols[:, tq + t:tq + t + 1]
                Sa = S * a
                stk = jnp.sum(Sa * kc, axis=0, keepdims=True)
                vn = gb[tq + t:tq + t + 1] * (v[t:t + 1] - stk)
                S = Sa + kc * vn
                outs.append(jnp.sum(S * qc, axis=0, keepdims=True))
            so_ref[0, bi, h] = S
            o = jnp.concatenate(outs + [jnp.zeros((SUBLANES - tq, hd), F32)], axis=0)
            z = gz_ref[bi, h]
            y_ref[bi, h] = _rms(o, ng) * (z * _sigmoid(z))
        return carry

    lax.fori_loop(0, bt, body, 0)


def _gdn_sample(slot, kq_s, v_s, gb_s, gz_s, s0, norm_g, *, tq):
    B = kq_s.shape[0]
    bt = 8
    layer = slot.layer
    blk = lambda a: pl.BlockSpec((bt,) + a.shape[1:], lambda i: (i, 0, 0, 0))
    in_specs = [blk(kq_s), blk(v_s), blk(gb_s), blk(gz_s), blk(s0), pl.BlockSpec((1, HEAD_DIM), lambda i: (0, 0))]
    return pl.pallas_call(
        slot.wrap(functools.partial(_gdn_sample_kernel, tq=tq), len(in_specs)),
        grid=(B // bt,),
        in_specs=in_specs + slot.in_specs(),
        out_specs=[blk(v_s), pl.BlockSpec((1, bt) + s0.shape[1:], lambda i: (layer, i, 0, 0, 0))],
        out_shape=[jax.ShapeDtypeStruct(v_s.shape, F32), jax.ShapeDtypeStruct((slot.depth,) + s0.shape, F32)],
        input_output_aliases=slot.aliases(len(in_specs), 1),
        compiler_params=_cp(("parallel",)), name="gdn_sample",
    )(kq_s, v_s, gb_s, gz_s, s0, norm_g.reshape(1, -1), *slot.args())


def _mix_out_kernel(ya_ref, yb_ref, yc_ref, x_ref, onb_ref, w_ref, o_ref, mix_ref):
    @pl.when(pl.program_id(1) == 0)
    def _():
        mix_ref[:, 0:512] = ya_ref[...]
        mix_ref[:, 512:1536] = _rms(yb_ref[...], onb_ref[...]).astype(BF16)
        mix_ref[:, 1536:2048] = yc_ref[...]

    o_ref[...] = x_ref[...] + jnp.dot(mix_ref[...], w_ref[...], preferred_element_type=F32)


def _mix_out(ya, yb, yc, x, onb, w, *, tn=512):
    M, N = x.shape
    tm = _row_tile(M)
    K = w.shape[0]
    return pl.pallas_call(
        _mix_out_kernel,
        grid=(M // tm, N // tn),
        in_specs=[pl.BlockSpec((tm, 512), lambda i, j: (i, 0)),
                  pl.BlockSpec((tm, 1024), lambda i, j: (i, 0)),
                  pl.BlockSpec((tm, 512), lambda i, j: (i, 0)),
                  pl.BlockSpec((tm, tn), lambda i, j: (i, j)),
                  pl.BlockSpec((1, 1024), lambda i, j: (0, 0)),
                  w.spec((K, tn), lambda i, j: (0, j))],
        out_specs=pl.BlockSpec((tm, tn), lambda i, j: (i, j)),
        out_shape=jax.ShapeDtypeStruct((M, N), F32),
        scratch_shapes=[pltpu.VMEM((tm, K), BF16)],
        compiler_params=_cp(("parallel", "arbitrary")), name="mix_out",
    )(ya, yb, yc, x, onb.reshape(1, -1), w.array)


def _matmul_res_kernel(a_ref, w_ref, r_ref, o_ref):
    o_ref[...] = r_ref[...] + jnp.dot(a_ref[...], w_ref[...], preferred_element_type=F32)


def _matmul_res(a, w, res, *, tm=512, tn=512):
    M, K = a.shape
    N = w.shape[1]
    tm = min(tm, M)
    return pl.pallas_call(
        _matmul_res_kernel,
        grid=(M // tm, N // tn),
        in_specs=[pl.BlockSpec((tm, K), lambda i, j: (i, 0)),
                  w.spec((K, tn), lambda i, j: (0, j)),
                  pl.BlockSpec((tm, tn), lambda i, j: (i, j))],
        out_specs=pl.BlockSpec((tm, tn), lambda i, j: (i, j)),
        out_shape=jax.ShapeDtypeStruct((M, N), F32),
        compiler_params=_cp(("parallel", "parallel")), name="matmul_res",
    )(a, w.array, res)


def _layer_weights(l, p):
    w_in = p["w_in"][l]
    w_main = jnp.concatenate([w_in[:, :4096], w_in[:, 4120:5656], w_in[:, 5664:6176]], axis=1).astype(BF16)
    w_small = jnp.concatenate([w_in[:, 4096:4120], w_in[:, 5656:5664],
                               jnp.zeros((w_in.shape[0], LANES - 32), w_in.dtype)], axis=1).astype(BF16)
    lane_vec = lambda v: jnp.zeros((1, LANES), F32).at[0, S_GA:S_GA + GDN_HEADS].set(v)
    return {
        "norm_mix_g": p["norm_mix_g"][l], "w_main": w_main, "w_small": w_small,
        "conv_a_w": p["conv_a_w"][l], "q_norm_g": p["q_norm_g"][l], "k_norm_g": p["k_norm_g"][l],
        "cmp_pe": p["cmp_pe"][l],
        "cmp_w1": p["cmp_w1"][l].reshape(2, 2, CMP_BLOCK // 2 * HEAD_DIM, CMP_HIDDEN).astype(BF16),
        "cmp_w2": p["cmp_w2"][l].astype(BF16),
        "gdn_conv_w": p["gdn_conv_w"][l], "alog_v": lane_vec(p["gdn_a_log"][l]), "dtb_v": lane_vec(p["gdn_dt_bias"][l]),
        "gdn_norm_g": p["gdn_norm_g"][l], "out_norm_a": p["out_norm_a"][l], "out_norm_b": p["out_norm_b"][l],
        "w_out": _LayerOf(p["w_out_bf16"], l), "norm_ffn_g": p["norm_ffn_g"][l],
        "ffn_up": _LayerOf(p["ffn_up_bf16"], l), "ffn_conv_w": p["ffn_conv_w"][l],
        "ffn_down": _LayerOf(p["ffn_down_bf16"], l),
    }


HALO_ROWS = 16


def _ffn_up_kernel(x_ref, xp_ref, g_ref, wa_ref, wv_ref, ha_ref, hv_ref, ca_ref, cv_ref,
                   o_ref, hoa_ref, hov_ref, xn_ref, *, tstride, tiles_per_seq):
    tn = wa_ref.shape[1]
    prompt = tstride == 1

    @pl.when(pl.program_id(1) == 0)
    def _():
        g = g_ref[...]
        if prompt:
            xn_ref[0:HALO_ROWS] = _rms(xp_ref[...], g).astype(BF16)
            xn_ref[HALO_ROWS:] = _rms(x_ref[...], g).astype(BF16)
        else:
            xn_ref[...] = _rms(x_ref[...], g).astype(BF16)

    xn = xn_ref[...]

    def part(w_ref, hist_ref, cw_ref, ho_ref):
        up = jnp.dot(xn, w_ref[...], preferred_element_type=F32)
        if prompt:
            start = (pl.program_id(0) % tiles_per_seq) == 0
            halo = jnp.where(start, hist_ref[0], up[HALO_ROWS - SUBLANES:HALO_ROWS])
            x = up[HALO_ROWS:]
            ho_ref[0] = x[x.shape[0] - SUBLANES:]
        else:
            halo = _ld(hist_ref, 0, tn)
            x = up
            _st(ho_ref, 0, tn, x[x.shape[0] - halo.shape[0]:])
        w = cw_ref[...]
        return w[0:1] * _shift(x, halo, 2, tstride) + w[1:2] * _shift(x, halo, 1, tstride) + w[2:3] * x

    a = part(wa_ref, ha_ref, ca_ref, hoa_ref)
    v = part(wv_ref, hv_ref, cv_ref, hov_ref)
    o_ref[...] = (a * _sigmoid(a) * v).astype(o_ref.dtype)


def _ffn_up(h, g, w_up, conv_w, hist, *, sample, nseq, seq_len, tn=512):
    M, K = h.shape
    tm = _row_tile(seq_len)
    F = w_up.shape[1] // 2
    nj = F // tn
    common = [pl.BlockSpec((1, K), lambda i, j: (0, 0)),
              w_up.spec((K, tn), lambda i, j: (0, j)),
              w_up.spec((K, tn), lambda i, j: (0, j + nj))]
    cw = [pl.BlockSpec((3, tn), lambda i, j: (0, j)), pl.BlockSpec((3, tn), lambda i, j: (0, j + nj))]
    if not sample:
        tps = seq_len // tm
        hspec = lambda off: pl.BlockSpec((1, SUBLANES, tn), lambda i, j, off=off: (i // tps, 0, j + off))
        hout = pl.BlockSpec((1, SUBLANES, tn), lambda i, j: (i, 0, j))
        act, tail_a, tail_v = pl.pallas_call(
            functools.partial(_ffn_up_kernel, tstride=1, tiles_per_seq=tps),
            grid=(M // tm, nj),
            in_specs=[pl.BlockSpec((tm, K), lambda i, j: (i, 0)),
                      pl.BlockSpec((HALO_ROWS, K), lambda i, j: (jnp.maximum(i * (tm // HALO_ROWS) - 1, 0), 0))]
            + common + [hspec(0), hspec(nj)] + cw,
            out_specs=[pl.BlockSpec((tm, tn), lambda i, j: (i, j)), hout, hout],
            out_shape=[jax.ShapeDtypeStruct((M, F), BF16)] + [jax.ShapeDtypeStruct((M // tm, SUBLANES, F), F32)] * 2,
            scratch_shapes=[pltpu.VMEM((tm + HALO_ROWS, K), BF16)],
            compiler_params=_cp(("arbitrary", "arbitrary")), name="ffn_up_prompt",
        )(h, h, g.reshape(1, K), w_up.array, w_up.array, hist, hist, conv_w, conv_w)
        return act, tail_a[tps - 1::tps], tail_v[tps - 1::tps]
    hk = hist.shape[0]
    hspec = lambda off: pl.BlockSpec((hk, nseq, tn), lambda i, j, off=off: (0, 0, j + off))
    hout = pl.BlockSpec((hk, nseq, tn), lambda i, j: (0, 0, j))
    return pl.pallas_call(
        functools.partial(_ffn_up_kernel, tstride=nseq, tiles_per_seq=1),
        grid=(1, nj),
        in_specs=[pl.BlockSpec((M, K), lambda i, j: (0, 0)), pl.BlockSpec((HALO_ROWS, K), lambda i, j: (0, 0))]
        + common + [hspec(0), hspec(nj)] + cw,
        out_specs=[pl.BlockSpec((M, tn), lambda i, j: (0, j)), hout, hout],
        out_shape=[jax.ShapeDtypeStruct((M, F), BF16)] + [jax.ShapeDtypeStruct((hk, nseq, F), F32)] * 2,
        scratch_shapes=[pltpu.VMEM((M, K), BF16)],
        compiler_params=_cp(("arbitrary", "arbitrary")), name="ffn_up_sample",
    )(h, h, g.reshape(1, K), w_up.array, w_up.array, hist, hist, conv_w, conv_w)


def _dense_tail(x2, ya, yb, yc, lw, hist_ffn, *, sample, nseq, seq_len):
    h = _mix_out(ya, yb, yc, x2, lw["out_norm_b"], lw["w_out"])
    act, hist_a, hist_v = _ffn_up(h, lw["norm_ffn_g"], lw["ffn_up"], lw["ffn_conv_w"], hist_ffn,
                                  sample=sample, nseq=nseq, seq_len=seq_len)
    return _matmul_res(act, lw["ffn_down"], h), jnp.concatenate([hist_a, hist_v], axis=-1)


def _prompt_layer(x2, lw, bias_tiles, bias_cmp, rows_slot, *, B, T):
    M = B * T
    G, R = NSA_KV_HEADS, NSA_REP
    z, zs = _norm_matmul(x2, lw["norm_mix_g"], lw["w_main"], lw["w_small"])
    hu0 = jnp.zeros((B, SUBLANES, 512), F32)
    hq0 = jnp.zeros((B, SUBLANES, 1536), F32)
    ya, q, rows, win, kvb, gates, gq, ggb, hu, rows_lin = _prep(z, zs, hu0, hq0, lw, sample=False, nseq=B, seq_len=T,
                                                                slot=rows_slot)
    cmp_kv = _compress_prompt(rows.reshape(B, T, 1024), lw)
    gates_g = gates[:, :3 * G * R].reshape(M, 3, G, R).transpose(2, 0, 1, 3).reshape(G, M, 3 * R)
    gates_g = jnp.pad(gates_g, ((0, 0), (0, 0), (0, LANES - 3 * R)))
    yb = _nsa_prompt(q, cmp_kv, kvb.reshape(B, T, 1024), gates_g, bias_tiles, bias_cmp, B=B, T=T)
    s0 = jnp.zeros((B, GDN_HEADS, HEAD_DIM, HEAD_DIM), F32)
    yc, s_new = _gdn_prompt(gq.reshape(B, T, 1536), ggb.reshape(B, T, 1024), z.reshape(B, T, C_END), s0,
                            lw["gdn_norm_g"])
    hf0 = jnp.zeros((B, SUBLANES, lw["ffn_conv_w"].shape[1]), F32)
    out, up = _dense_tail(x2, ya, yb, yc.reshape(M, 512), lw, hf0, sample=False, nseq=B, seq_len=T)
    return out, rows_lin, win, hu, z, s_new, up


def _sample_layer(x2, lw, st, bias_c, bias_s, bias_w, win_slot, gdn_slot, *, B, T):
    M = B * T
    G, R, H, hd = NSA_KV_HEADS, NSA_REP, GDN_HEADS, HEAD_DIM
    z, zs = _norm_matmul(x2, lw["norm_mix_g"], lw["w_main"], lw["w_small"])
    hu0 = jnp.swapaxes(st["conv_a"], 0, 1)
    hq0 = jnp.swapaxes(st["gdn_conv"], 0, 1)
    ya, q, rows, win, kvb, gates, gq, ggb, hu = _prep(z, zs, hu0, hq0, lw, sample=True, nseq=B, seq_len=T)
    tb = lambda a: a.reshape(T, B, -1)
    q_s = tb(q).reshape(T, B, G, R, hd).transpose(1, 2, 3, 0, 4).reshape(B, G, R * T, hd)
    gt = tb(gates)[:, :, :3 * G * R].reshape(T, B, 3, G, R).transpose(1, 3, 4, 0, 2).reshape(B, G, R * T, 3)
    gates_s = jnp.pad(gt, ((0, 0), (0, 0), (0, 0), (0, LANES - 3)))
    kvn_s = jnp.pad(jnp.swapaxes(tb(kvb), 0, 1), ((0, 0), (0, SUBLANES - T), (0, 0)))
    win_new = jnp.swapaxes(tb(win), 0, 1).reshape(B, T * 2 * G, hd)
    yb_s, win = _nsa_sample(win_slot, st["page_table"], st["cache_lin"], st["win_lin"], win_new, q_s, kvn_s, gates_s, lw,
                            bias_c, bias_s, bias_w, tq=T)
    yb = yb_s.reshape(B, G, R, T, hd).transpose(3, 0, 1, 2, 4).reshape(M, G * R * hd)
    gq4 = tb(gq).reshape(T, B, 3, H, hd)
    bh = lambda a: a.transpose(1, 2, 0, 3)
    padt = lambda a: jnp.pad(a, ((0, 0), (0, 0), (0, SUBLANES - T), (0, 0)))
    kq_s = jnp.concatenate([bh(gq4[:, :, 1]), bh(gq4[:, :, 0])], axis=2)
    v_s = padt(bh(gq4[:, :, 2]))
    ggb4 = tb(ggb).reshape(T, B, 2, H, hd)
    gb_s = jnp.concatenate([bh(ggb4[:, :, 0]), bh(ggb4[:, :, 1])], axis=2)
    gz_s = padt(bh(tb(z)[:, :, C_GZ:C_END].reshape(T, B, H, hd)))
    yc_s, s_new = _gdn_sample(gdn_slot, kq_s, v_s, gb_s, gz_s, st["gdn"], lw["gdn_norm_g"], tq=T)
    yc = yc_s[:, :, :T].transpose(2, 0, 1, 3).reshape(M, H * hd).astype(BF16)
    hf0 = jnp.swapaxes(st["ffn_conv"], 0, 1)
    out, up = _dense_tail(x2, ya, yb, yc, lw, hf0, sample=True, nseq=B, seq_len=T)
    return out, rows, win, hu, z, s_new, up


def kernel(x_prompt, x_sample, cache_kv, cache_win, state_conv_a, state_gdn_conv, state_gdn, state_ffn_conv,
           page_table, rel_bias, norm_mix_g, w_in, conv_a_w, q_norm_g, k_norm_g, cmp_pe, cmp_w1, cmp_w2,
           gdn_conv_w, gdn_a_log, gdn_dt_bias, gdn_norm_g, out_norm_a, out_norm_b, w_out, norm_ffn_g,
           ffn_up, ffn_conv_w, ffn_down):
    params = dict(norm_mix_g=norm_mix_g, w_in=w_in, conv_a_w=conv_a_w, q_norm_g=q_norm_g, k_norm_g=k_norm_g,
                  cmp_pe=cmp_pe, cmp_w1=cmp_w1, cmp_w2=cmp_w2, gdn_conv_w=gdn_conv_w, gdn_a_log=gdn_a_log,
                  gdn_dt_bias=gdn_dt_bias, gdn_norm_g=gdn_norm_g, out_norm_a=out_norm_a, out_norm_b=out_norm_b,
                  w_out=w_out, norm_ffn_g=norm_ffn_g, ffn_up=ffn_up, ffn_conv_w=ffn_conv_w, ffn_down=ffn_down,
                  w_out_bf16=w_out.astype(BF16), ffn_up_bf16=ffn_up.astype(BF16), ffn_down_bf16=ffn_down.astype(BF16))
    depth = w_in.shape[0]
    Bp, T, D = x_prompt.shape
    Bs, Ts, _ = x_sample.shape
    G, R, hd = NSA_KV_HEADS, NSA_REP, HEAD_DIM
    n_pool, page = cache_kv.shape[1], cache_kv.shape[2]
    npages = page_table.shape[1]
    past = npages * page
    wb = cache_win.shape[2]
    L = past + Ts
    assert ((L - CMP_BLOCK) // CMP_STRIDE) * CMP_STRIDE + CMP_BLOCK <= past, "compressed blocks must lie in the cache"
    assert T % Q_BLOCK == 0 and T % 512 == 0 and 3 <= Ts <= SUBLANES // 2

    thr = _bucket_thresholds()
    rb_flat = rel_bias.reshape(-1)
    nqb = T // Q_BLOCK
    bias_tiles = _bias_table(thr, rb_flat, nqb, Q_BLOCK, Q_BLOCK, a0=0, an=Q_BLOCK, qs=1, ks=1)
    bias_cmp = _bias_table(thr, rb_flat, nqb, Q_BLOCK, T // CMP_STRIDE,
                           a0=-(CMP_BLOCK - 1), an=Q_BLOCK, qs=1, ks=CMP_STRIDE)

    def sample_bias(ncols, a0, ks):
        t = _bias_table(thr, rb_flat, 1, SUBLANES, ncols, a0=a0, an=0, qs=1, ks=ks)[0]
        return t[:, :Ts].reshape(G, R * Ts, ncols)

    bias_c = sample_bias(past // CMP_STRIDE, past - (CMP_BLOCK - 1), CMP_STRIDE)
    bias_s = sample_bias(past + LANES, past, 1)
    bias_w = sample_bias(wb + LANES, wb, 1)

    cache_lin = cache_kv.reshape(depth, n_pool, page * 4 * G, hd)
    win_lin = cache_win.reshape(depth, Bs, wb * 2 * G, hd)
    assert wb == WINDOW, "the new window buffer is the old one shifted by the new rows"

    xp = x_prompt.reshape(Bp * T, D)
    xs = jnp.swapaxes(x_sample, 0, 1).reshape(Ts * Bs, D)
    outs_p, outs_s = [], []
    rows_p_all = win_s_all = gdn_s_all = None
    for l in range(depth):
        lw = _layer_weights(l, params)
        xp, rows_p_all, win, hu, z, s_new, up = _prompt_layer(xp, lw, bias_tiles, bias_cmp,
                                                              _LayerSlot(l, depth, rows_p_all), B=Bp, T=T)
        wl = min(WINDOW, T)
        outs_p.append((
            win.reshape(Bp, T, 2, G, hd)[:, T - wl:],
            hu[:, SUBLANES - 2:],
            z.reshape(Bp, T, C_END)[:, T - 3:, C_GQKV:C_GZ],
            s_new,
            up[:, SUBLANES - 2:]))
        st = dict(page_table=page_table, cache_lin=cache_lin, win_lin=win_lin, conv_a=state_conv_a[l],
                  gdn_conv=state_gdn_conv[l], gdn=state_gdn[l], ffn_conv=state_ffn_conv[l])
        xs, rows, win_s_all, hu, z, gdn_s_all, up = _sample_layer(
            xs, lw, st, bias_c, bias_s, bias_w, _LayerSlot(l, depth, win_s_all), _LayerSlot(l, depth, gdn_s_all),
            B=Bs, T=Ts)
        tb = lambda a: jnp.swapaxes(a.reshape(Ts, Bs, -1), 0, 1)
        outs_s.append((
            tb(rows).reshape(Bs, Ts, 4, G, hd),
            jnp.swapaxes(hu, 0, 1),
            tb(z)[:, Ts - 3:, C_GQKV:C_GZ],
            jnp.swapaxes(up, 0, 1)))
    y_p = xp.reshape(Bp, T, D)
    y_s = jnp.swapaxes(xs.reshape(Ts, Bs, D), 0, 1)
    stack = lambda outs, i: jnp.stack([o[i] for o in outs])
    return (y_p, y_s,
            rows_p_all.reshape(depth, Bp, T, 4, G, hd), stack(outs_s, 0),
            stack(outs_p, 0), win_s_all.reshape(depth, Bs, wb, 2, G, hd),
            stack(outs_p, 1), stack(outs_s, 1), stack(outs_p, 2), stack(outs_s, 2),
            stack(outs_p, 3), gdn_s_all, stack(outs_p, 4), stack(outs_s, 3))
```

```python
import functools
import math

import jax
import jax.numpy as jnp
from jax import lax
from jax.experimental import pallas as pl
from jax.experimental.pallas import tpu as pltpu

F32 = jnp.float32
BF16 = jnp.bfloat16
HIGHEST = lax.Precision.HIGHEST

SUBLANES = 8
LANES = 128
VMEM_LIMIT_BYTES = 52 * 1024 * 1024

HEAD_DIM = 128
NSA_KV_HEADS = 2
NSA_REP = 4
NSA_HEADS = NSA_KV_HEADS * NSA_REP
GDN_HEADS = 4
CMP_BLOCK = 32
CMP_STRIDE = 16
CMP_HIDDEN = 256
SLC_BLOCK = 64
N_SELECT = 16
WINDOW = 512
Q_BLOCK = 128
GDN_CHUNK = 64
REL_BUCKETS = 32
REL_MAX_DIST = 1024
EPS = 1e-6
NEG_INF = -1e30
FORCE_SCORE = 1e4

C_AB, C_AC, C_AH, C_NQ, C_NKV, C_GQKV, C_GZ, C_END = 0, 512, 1024, 1536, 2560, 4096, 5632, 6144
S_GATE, S_GA, S_GB = 0, 24, 28


def _cp(sem):
    return pltpu.CompilerParams(dimension_semantics=sem, vmem_limit_bytes=VMEM_LIMIT_BYTES)


class _LayerOf:
    def __init__(self, stacked, layer):
        self.array, self.layer, self.shape = stacked, layer, stacked.shape[1:]

    def spec(self, block, index_map):
        layer = self.layer
        return pl.BlockSpec((None,) + block, lambda *ij: (layer,) + tuple(index_map(*ij)))


class _LayerSlot:
    def __init__(self, layer, depth, prev):
        self.layer, self.depth, self.prev = layer, depth, prev

    def in_specs(self):
        return [] if self.prev is None else [pl.BlockSpec(memory_space=pl.ANY)]

    def args(self):
        return [] if self.prev is None else [self.prev]

    def aliases(self, n_inputs_before, out_index):
        return {} if self.prev is None else {n_inputs_before: out_index}

    def wrap(self, kernel, n_refs_before):
        if self.prev is None:
            return kernel

        def wrapped(*refs):
            return kernel(*refs[:n_refs_before], *refs[n_refs_before + 1:])
        return wrapped


def _div_pow2(x, d):
    assert d & (d - 1) == 0
    return jnp.right_shift(x, d.bit_length() - 1)


def _mod_pow2(x, d):
    assert d & (d - 1) == 0
    return jnp.bitwise_and(x, d - 1)


def _sigmoid(x):
    return 1.0 / (1.0 + jnp.exp(-x))


def _rms(x, g):
    return x * lax.rsqrt(jnp.mean(x * x, axis=-1, keepdims=True) + EPS) * g


def _l2n(x):
    return x * lax.rsqrt(jnp.sum(x * x, axis=-1, keepdims=True) + EPS)


def _dot_t(a, b, precision=None):
    return lax.dot_general(a, b, (((1,), (1,)), ((), ())), precision=precision, preferred_element_type=F32)


def _ld(ref, c0, c1):
    if len(ref.shape) == 3:
        v = ref[:, :, c0:c1]
        return v.reshape(v.shape[0] * v.shape[1], v.shape[2])
    return ref[:, c0:c1]


def _st(ref, c0, c1, val):
    if len(ref.shape) == 3:
        ref[:, :, c0:c1] = val.reshape(ref.shape[0], ref.shape[1], c1 - c0).astype(ref.dtype)
    else:
        ref[:, c0:c1] = val.astype(ref.dtype)


def _shift(x, halo, s, tstride):
    n = s * tstride
    rows = x.shape[0]
    if tstride % SUBLANES == 0:
        hr = halo.shape[0]
        return jnp.concatenate([halo[hr - n:], x[:rows - n]], axis=0)
    xs = pltpu.roll(x, n, 0)
    hs = pltpu.roll(halo, n, 0)
    rid = lax.broadcasted_iota(jnp.int32, (SUBLANES, x.shape[1]), 0)
    head = jnp.where(rid < n, hs, xs[:SUBLANES])
    return jnp.concatenate([head, xs[SUBLANES:]], axis=0)


def _norm_matmul_kernel(x_ref, g_ref, w_ref, ws_ref, o_ref, os_ref, xn_ref):
    @pl.when(pl.program_id(1) == 0)
    def _():
        xn = _rms(x_ref[...], g_ref[...]).astype(BF16)
        xn_ref[...] = xn
        os_ref[...] = jnp.dot(xn, ws_ref[...], preferred_element_type=F32)

    o_ref[...] = jnp.dot(xn_ref[...], w_ref[...], preferred_element_type=F32)


def _row_tile(rows):
    for tm in (1024, 512):
        if rows % tm == 0:
            return tm
    return rows


def _norm_matmul(x, g, w, w_small, *, tn=512):
    M, K = x.shape
    N = w.shape[1]
    tm = _row_tile(M)
    return pl.pallas_call(
        _norm_matmul_kernel,
        grid=(M // tm, N // tn),
        in_specs=[pl.BlockSpec((tm, K), lambda i, j: (i, 0)),
                  pl.BlockSpec((1, K), lambda i, j: (0, 0)),
                  pl.BlockSpec((K, tn), lambda i, j: (0, j)),
                  pl.BlockSpec((K, LANES), lambda i, j: (0, 0))],
        out_specs=[pl.BlockSpec((tm, tn), lambda i, j: (i, j)), pl.BlockSpec((tm, LANES), lambda i, j: (i, 0))],
        out_shape=[jax.ShapeDtypeStruct((M, N), F32), jax.ShapeDtypeStruct((M, LANES), F32)],
        scratch_shapes=[pltpu.VMEM((tm, K), BF16)],
        compiler_params=_cp(("parallel", "arbitrary")), name="norm_matmul",
    )(x, g.reshape(1, K), w, w_small)


def _prep_kernel(z_ref, zp_ref, zs_ref, hu_ref, hq_ref, caw_ref, gcw_ref, qg_ref, kg_ref, ona_ref,
                 alog_ref, dtb_ref,
                 ya_ref, q_ref, rows_ref, win_ref, kvb_ref, gates_ref, gq_ref, ggb_ref, huo_ref, *lin_refs,
                 tstride, tiles_per_seq):
    hd = HEAD_DIM

    def put_rows(grp, val):
        _st(rows_ref, grp * hd, (grp + 1) * hd, val)
        if lin_refs:
            lin_refs[0][0, pl.ds(grp, val.shape[0], stride=4 * NSA_KV_HEADS), :] = val
    u = _ld(z_ref, C_AC, C_AH) * _ld(z_ref, C_AH, C_NQ)
    if tstride == 1:
        start = (pl.program_id(0) % tiles_per_seq) == 0
        halo_u = jnp.where(start, hu_ref[0], zp_ref[:, C_AC:C_AH] * zp_ref[:, C_AH:C_NQ])
        halo_q = jnp.where(start, hq_ref[0], zp_ref[:, C_GQKV:C_GZ])
    else:
        halo_u = _ld(hu_ref, 0, 512)
        halo_q = _ld(hq_ref, 0, 1536)
    caw = caw_ref[...]
    conv = caw[0:1] * _shift(u, halo_u, 2, tstride) + caw[1:2] * _shift(u, halo_u, 1, tstride) + caw[2:3] * u
    y = _ld(z_ref, C_AB, C_AC) * conv
    _st(ya_ref, 0, 512, _rms(y, ona_ref[...]))
    hu_rows = huo_ref.shape[0] * huo_ref.shape[1] if tstride != 1 else SUBLANES
    if tstride == 1:
        huo_ref[0] = u[u.shape[0] - hu_rows:]
    else:
        _st(huo_ref, 0, 512, u[u.shape[0] - hu_rows:])

    qg = qg_ref[...]
    kg = kg_ref[...]
    for h in range(NSA_HEADS):
        c0 = C_NQ + h * hd
        _st(q_ref, h * hd, (h + 1) * hd, _rms(_ld(z_ref, c0, c0 + hd), qg) * (hd ** -0.5))
    for grp in range(4):
        put_rows(grp, _ld(z_ref, C_NKV + grp * hd, C_NKV + (grp + 1) * hd))
    for g in range(NSA_KV_HEADS):
        c0 = C_NKV + 512 + g * hd
        kn = _rms(_ld(z_ref, c0, c0 + hd), kg)
        put_rows(4 + g, kn)
        _st(kvb_ref, g * hd, (g + 1) * hd, kn)
    for g in range(NSA_KV_HEADS):
        c0 = C_NKV + 768 + g * hd
        vs = _ld(z_ref, c0, c0 + hd)
        put_rows(6 + g, vs)
        _st(kvb_ref, 256 + g * hd, 256 + (g + 1) * hd, vs)
    for g in range(NSA_KV_HEADS):
        c0 = C_NKV + 1024 + g * hd
        kn = _rms(_ld(z_ref, c0, c0 + hd), kg)
        _st(win_ref, g * hd, (g + 1) * hd, kn)
        _st(kvb_ref, 512 + g * hd, 512 + (g + 1) * hd, kn)
    vw = _ld(z_ref, C_NKV + 1280, C_NKV + 1536)
    _st(win_ref, 256, 512, vw)
    _st(kvb_ref, 768, 1024, vw)

    zs = _ld(zs_ref, 0, LANES)
    sg = _sigmoid(zs)
    _st(gates_ref, 0, LANES, sg)
    xs = zs + dtb_ref[...]
    e = jnp.exp(-jnp.abs(xs))
    u1 = 1.0 + e
    log1p_e = jnp.where(u1 == 1.0, e, jnp.log(u1) * (e / jnp.where(u1 == 1.0, 1.0, u1 - 1.0)))
    softplus = jnp.maximum(xs, 0.0) + log1p_e
    gdec = -jnp.exp(alog_ref[...]) * softplus
    rows = zs.shape[0]
    if tstride == 1:
        in_chunk = _mod_pow2(lax.broadcasted_iota(jnp.int32, gdec.shape, 0), GDN_CHUNK)
        step = 1
        while step < GDN_CHUNK:
            gdec = gdec + jnp.where(in_chunk >= step, pltpu.roll(gdec, step, 0), 0.0)
            step *= 2
    for h in range(GDN_HEADS):
        _st(ggb_ref, h * hd, (h + 1) * hd, jnp.broadcast_to(gdec[:, S_GA + h:S_GA + h + 1], (rows, hd)))
        _st(ggb_ref, 512 + h * hd, 512 + (h + 1) * hd, jnp.broadcast_to(sg[:, S_GB + h:S_GB + h + 1], (rows, hd)))

    gcw = gcw_ref[...]
    for part in range(3):
        c0 = C_GQKV + part * 512
        x = _ld(z_ref, c0, c0 + 512)
        hq = halo_q[:, part * 512:(part + 1) * 512]
        w = gcw[:, part * 512:(part + 1) * 512]
        c = (w[0:1] * _shift(x, hq, 3, tstride) + w[1:2] * _shift(x, hq, 2, tstride)
             + w[2:3] * _shift(x, hq, 1, tstride) + w[3:4] * x)
        c = c * _sigmoid(c)
        if part == 2:
            _st(gq_ref, 1024, 1536, c)
        else:
            for h in range(GDN_HEADS):
                v = _l2n(c[:, h * hd:(h + 1) * hd])
                if part == 0:
                    v = v * (hd ** -0.5)
                _st(gq_ref, part * 512 + h * hd, part * 512 + (h + 1) * hd, v)


def _prep(z, zs, hist_u, hist_q, lw, *, sample, nseq, seq_len, slot=None):
    M = z.shape[0]
    wts = [lw["conv_a_w"], lw["gdn_conv_w"], lw["q_norm_g"].reshape(1, -1), lw["k_norm_g"].reshape(1, -1),
           lw["out_norm_a"].reshape(1, -1), lw["alog_v"], lw["dtb_v"]]
    wspecs2 = [pl.BlockSpec(w.shape, lambda i: (0, 0)) for w in wts]
    widths = [(512, BF16), (1024, BF16), (1024, F32), (512, F32), (1024, BF16), (LANES, F32), (1536, F32), (1024, F32)]
    if not sample:
        tm = 256
        tps = seq_len // tm
        in_specs = [pl.BlockSpec((tm, C_END), lambda i: (i, 0)),
                    pl.BlockSpec((SUBLANES, C_END), lambda i: (jnp.maximum(i * (tm // SUBLANES) - 1, 0), 0)),
                    pl.BlockSpec((tm, LANES), lambda i: (i, 0)),
                    pl.BlockSpec((1, SUBLANES, 512), lambda i: (i // tps, 0, 0)),
                    pl.BlockSpec((1, SUBLANES, 1536), lambda i: (i // tps, 0, 0))] + wspecs2
        out_specs = [pl.BlockSpec((tm, w), lambda i: (i, 0)) for w, _ in widths]
        out_specs.append(pl.BlockSpec((1, SUBLANES, 512), lambda i: (i // tps, 0, 0)))
        out_shape = [jax.ShapeDtypeStruct((M, w), d) for w, d in widths]
        out_shape.append(jax.ShapeDtypeStruct((nseq, SUBLANES, 512), F32))
        kvr = 4 * NSA_KV_HEADS
        layer = slot.layer
        out_specs.append(pl.BlockSpec((1, tm * kvr, HEAD_DIM), lambda i: (layer, i, 0)))
        out_shape.append(jax.ShapeDtypeStruct((slot.depth, M * kvr, HEAD_DIM), F32))
        n_in = len(in_specs)
        return pl.pallas_call(
            slot.wrap(functools.partial(_prep_kernel, tstride=1, tiles_per_seq=tps), n_in),
            grid=(M // tm,), in_specs=in_specs + slot.in_specs(), out_specs=out_specs, out_shape=out_shape,
            input_output_aliases=slot.aliases(n_in, len(out_shape) - 1),
            compiler_params=_cp(("arbitrary",)), name="prep_prompt",
        )(z, z, zs, hist_u, hist_q, *wts, *slot.args())
    T = seq_len
    bt = 64
    z3 = z.reshape(T, nseq, C_END)
    zs3 = zs.reshape(T, nseq, LANES)
    in_specs = [pl.BlockSpec((T, bt, C_END), lambda i: (0, i, 0)),
                pl.BlockSpec((SUBLANES, C_END), lambda i: (0, 0)),
                pl.BlockSpec((T, bt, LANES), lambda i: (0, i, 0)),
                pl.BlockSpec((hist_u.shape[0], bt, 512), lambda i: (0, i, 0)),
                pl.BlockSpec((hist_q.shape[0], bt, 1536), lambda i: (0, i, 0))] + wspecs2
    out_specs = [pl.BlockSpec((T, bt, w), lambda i: (0, i, 0)) for w, _ in widths]
    out_specs.append(pl.BlockSpec((hist_u.shape[0], bt, 512), lambda i: (0, i, 0)))
    out_shape = [jax.ShapeDtypeStruct((T, nseq, w), d) for w, d in widths]
    out_shape.append(jax.ShapeDtypeStruct((hist_u.shape[0], nseq, 512), F32))
    outs = pl.pallas_call(
        functools.partial(_prep_kernel, tstride=bt, tiles_per_seq=1),
        grid=(nseq // bt,), in_specs=in_specs, out_specs=out_specs, out_shape=out_shape,
        compiler_params=_cp(("arbitrary",)), name="prep_sample",
    )(z3, z, zs3, hist_u, hist_q, *wts)
    return [o.reshape(M, o.shape[-1]) for o in outs[:-1]] + [outs[-1]]


def _bias_kernel(thr_ref, rb_ref, o_ref, *, a0, an, qs, ks):
    n = pl.program_id(0)
    R, C = o_ref.shape[-2:]

    def rows8(rc, carry):
        r0 = pl.multiple_of(rc * SUBLANES, SUBLANES)
        dist = (a0 + an * n + qs * (r0 + lax.broadcasted_iota(jnp.int32, (SUBLANES, C), 0))
                - ks * lax.broadcasted_iota(jnp.int32, (SUBLANES, C), 1))
        b = [jnp.full((SUBLANES, C), rb_ref[h], F32) for h in range(NSA_HEADS)]
        for k in range(1, REL_BUCKETS):
            reached = dist >= thr_ref[k]
            b = [jnp.where(reached, rb_ref[k * NSA_HEADS + h], b[h]) for h in range(NSA_HEADS)]
        for h in range(NSA_HEADS):
            o_ref[0, h, pl.ds(r0, SUBLANES), :] = b[h]
        return carry

    lax.fori_loop(0, R // SUBLANES, rows8, 0)


def _bias_table(thr, rb_flat, n, R, C, *, a0, an, qs, ks):
    return pl.pallas_call(
        functools.partial(_bias_kernel, a0=a0, an=an, qs=qs, ks=ks),
        grid=(n,),
        in_specs=[pl.BlockSpec(memory_space=pltpu.SMEM), pl.BlockSpec(memory_space=pltpu.SMEM)],
        out_specs=pl.BlockSpec((1, NSA_HEADS, R, C), lambda i: (i, 0, 0, 0)),
        out_shape=jax.ShapeDtypeStruct((n, NSA_HEADS, R, C), F32),
        compiler_params=_cp(("parallel",)), name="bias_table",
    )(thr, rb_flat)


def _bucket_thresholds():
    n = jnp.arange(REL_MAX_DIST + 1)
    exact = REL_BUCKETS // 2
    nf = jnp.maximum(n, 1).astype(F32)
    far = exact + (jnp.log(nf / exact) / math.log(REL_MAX_DIST / exact) * (REL_BUCKETS - exact)).astype(jnp.int32)
    bucket = jnp.where(n < exact, n, jnp.minimum(far, REL_BUCKETS - 1))
    return jnp.sum(bucket[None, :] < jnp.arange(REL_BUCKETS)[:, None], axis=1).astype(jnp.int32)


def _gelu_tanh(x):
    return x * (0.5 * (1.0 + jnp.tanh(math.sqrt(2.0 / math.pi) * (x + 0.044715 * (x * x * x)))))


def _compress_math(get_rows, ng, pe, w1_ref, w2):
    half = CMP_BLOCK // 2
    rows = [get_rows(r) for r in range(half)]
    top = jnp.dot(jnp.concatenate([(rows[r] + pe[r:r + 1]).astype(BF16) for r in range(half)], axis=-1),
                  w1_ref[0], preferred_element_type=F32)
    bot = jnp.dot(jnp.concatenate([(rows[r] + pe[r + half:r + half + 1]).astype(BF16) for r in range(half)], axis=-1),
                  w1_ref[1], preferred_element_type=F32)
    h = top + pltpu.roll(bot, ng - 1, 0)
    return jnp.dot(_gelu_tanh(h).astype(BF16), w2, preferred_element_type=F32)


def _compress_kernel(x_ref, pe_ref, w1_ref, w2_ref, kg_ref, o_ref):
    ng = x_ref.shape[1] // CMP_STRIDE
    slot = pl.program_id(1)

    def get_rows(r):
        return x_ref[0, pl.ds(r, ng, stride=CMP_STRIDE), :]

    out = _compress_math(get_rows, ng, pe_ref[0], w1_ref.at[0], w2_ref[0])
    o_ref[0, 0, 0] = jnp.where(slot == 0, _rms(out, kg_ref[...]), out).astype(o_ref.dtype)


def _compress_prompt(rows3, lw):
    B, T, _ = rows3.shape
    ng = T // CMP_STRIDE
    G = NSA_KV_HEADS
    return pl.pallas_call(
        _compress_kernel,
        grid=(B, 2, G),
        in_specs=[pl.BlockSpec((1, T, HEAD_DIM), lambda b, s, g: (b, 0, s * G + g)),
                  pl.BlockSpec((1, CMP_BLOCK, HEAD_DIM), lambda b, s, g: (s, 0, 0)),
                  pl.BlockSpec((1,) + lw["cmp_w1"].shape[1:], lambda b, s, g: (s, 0, 0, 0)),
                  pl.BlockSpec((1, CMP_HIDDEN, HEAD_DIM), lambda b, s, g: (s, 0, 0)),
                  pl.BlockSpec((1, HEAD_DIM), lambda b, s, g: (0, 0))],
        out_specs=pl.BlockSpec((1, 1, 1, ng, HEAD_DIM), lambda b, s, g: (b, s, g, 0, 0)),
        out_shape=jax.ShapeDtypeStruct((B, 2, G, ng, HEAD_DIM), BF16),
        compiler_params=_cp(("parallel", "parallel", "parallel")), name="compress_prompt",
    )(rows3, lw["cmp_pe"], lw["cmp_w1"], lw["cmp_w2"], lw["k_norm_g"].reshape(1, -1))


def _masked_softmax(s, mask):
    s = jnp.where(mask, s, NEG_INF)
    m = jnp.max(s, axis=-1, keepdims=True)
    e = jnp.where(mask, jnp.exp(s - m), 0.0)
    z = jnp.sum(e, axis=-1, keepdims=True)
    return e / jnp.where(z > 0, z, 1.0)


def _overlap_matrix(ncp):
    c = CMP_STRIDE * lax.broadcasted_iota(jnp.int32, (ncp, LANES), 0)
    s = SLC_BLOCK * lax.broadcasted_iota(jnp.int32, (ncp, LANES), 1)
    return jnp.where(c < s + SLC_BLOCK, jnp.where(c + CMP_BLOCK > s, 1.0, 0.0), 0.0).astype(F32)


def _select_mask(imp, cur, n_slc):
    R = imp.shape[0]
    blk = lax.broadcasted_iota(jnp.int32, (R, LANES), 1)
    for forced_blk in (0, cur, cur - 1):
        imp = jnp.where(blk == forced_blk, FORCE_SCORE, imp)
    imp = jnp.where(blk <= cur, imp, -1.0)
    imp = jnp.where(blk < n_slc, imp, -2.0)
    cnt = jnp.zeros((R, LANES), F32)
    for k in range(n_slc):
        col = imp[:, k:k + 1]
        cnt = cnt + jnp.where(col > imp, 1.0, jnp.where(col == imp, jnp.where(blk > k, 1.0, 0.0), 0.0))
    return jnp.where(cnt < float(min(N_SELECT, n_slc)), jnp.where(blk < n_slc, 1.0, 0.0), 0.0)


def _expand_matrix(nkeys):
    s = lax.broadcasted_iota(jnp.int32, (LANES, nkeys), 0)
    j = lax.broadcasted_iota(jnp.int32, (LANES, nkeys), 1)
    lo = s * SLC_BLOCK
    return jnp.where(j >= lo, jnp.where(j < lo + SLC_BLOCK, 1.0, 0.0), 0.0).astype(BF16)


KEY_TILE = 1024


def _select_mask_t(imp_t, cur, n_slc):
    blk = lax.broadcasted_iota(jnp.int32, imp_t.shape, 0)
    for forced_blk in (0, cur, cur - 1):
        imp_t = jnp.where(blk == forced_blk, FORCE_SCORE, imp_t)
    imp_t = jnp.where(blk <= cur, imp_t, -1.0)
    cnt = jnp.zeros(imp_t.shape, F32)
    for k in range(n_slc):
        row = imp_t[k:k + 1, :]
        cnt = cnt + jnp.where(row > imp_t, 1.0, jnp.where(row == imp_t, jnp.where(blk > k, 1.0, 0.0), 0.0))
    return jnp.where(cnt < float(min(N_SELECT, n_slc)), 1.0, 0.0)


def _nsa_prompt_kernel(q_ref, kc_ref, vc_ref, ks_ref, vs_ref, kw_ref, vw_ref, bt_ref, bc_ref, gates_ref, ex_ref,
                       o_ref, m_ref, acc_ref, *, T):
    qb = pl.program_id(2)
    QB = Q_BLOCK
    R = NSA_REP
    RQ = R * QB
    hd = HEAD_DIM
    nc = (T - CMP_BLOCK) // CMP_STRIDE + 1
    ncp = kc_ref.shape[-2]
    n_slc = ex_ref.shape[0]
    q = jnp.concatenate([q_ref[:, r * hd:(r + 1) * hd] for r in range(R)], axis=0)

    s = _dot_t(q, kc_ref[0, 0, 0]) + bc_ref[0].reshape(RQ, ncp)
    t_row = lax.broadcasted_iota(jnp.int32, (R, QB, ncp), 1).reshape(RQ, ncp)
    c_col = lax.broadcasted_iota(jnp.int32, (RQ, ncp), 1)
    dist = qb * QB + t_row - (CMP_STRIDE * c_col + CMP_BLOCK - 1)
    p = _masked_softmax(s, jnp.where(c_col < nc, dist, -1) >= 0)
    o_cmp = jnp.dot(p.astype(BF16), vc_ref[0, 0, 0], preferred_element_type=F32)

    psum = p[0:QB]
    for r in range(1, R):
        psum = psum + p[r * QB:(r + 1) * QB]
    s_lo = SLC_BLOCK * lax.broadcasted_iota(jnp.int32, (n_slc, ncp), 0)
    c_lo = CMP_STRIDE * lax.broadcasted_iota(jnp.int32, (n_slc, ncp), 1)
    overlap_t = jnp.where(c_lo < s_lo + SLC_BLOCK, jnp.where(c_lo + CMP_BLOCK > s_lo, 1.0, 0.0), 0.0).astype(F32)
    imp_t = _dot_t(overlap_t, psum, HIGHEST)
    cur = _div_pow2(qb * QB + lax.broadcasted_iota(jnp.int32, (1, QB), 1), SLC_BLOCK)
    sel_t = _select_mask_t(imp_t, cur, n_slc).astype(BF16)

    NCHAIN = 2
    hpc = R // NCHAIN
    crow = hpc * QB

    def bias_rows(c, first_blk, n):
        return jnp.concatenate(
            [bt_ref[jnp.maximum(qb - (first_blk + j), 0), c * hpc:(c + 1) * hpc].reshape(crow, QB) for j in range(n)],
            axis=-1)

    def rel(first_key, nk):
        return (qb * QB - first_key + lax.broadcasted_iota(jnp.int32, (QB, nk), 0)
                - lax.broadcasted_iota(jnp.int32, (QB, nk), 1))

    def with_ones(v):
        return jnp.concatenate([v, jnp.ones(v.shape, BF16)], axis=-1)

    nkb = KEY_TILE // QB
    m_ref[...] = jnp.full(m_ref.shape, NEG_INF, F32)
    acc_ref[...] = jnp.zeros(acc_ref.shape, F32)

    def body(kt, carry):
        off = pl.multiple_of(kt * KEY_TILE, KEY_TILE)
        k = ks_ref[0, pl.ds(off, KEY_TILE), :]
        vext = with_ones(vs_ref[0, pl.ds(off, KEY_TILE), :])
        selk = lax.dot_general(sel_t, ex_ref[:, pl.ds(off, KEY_TILE)], (((0,), (0,)), ((), ())),
                               preferred_element_type=F32)
        d = jnp.where(selk > 0.5, rel(off, KEY_TILE), -1)
        neg = jnp.concatenate([jnp.where(d >= 0, 0.0, NEG_INF)] * hpc, axis=0)
        for c in range(NCHAIN):
            rows = slice(c * crow, (c + 1) * crow)
            sc = _dot_t(q[rows], k) + bias_rows(c, kt * nkb, nkb) + neg
            m_old = m_ref[rows]
            m_new = jnp.maximum(m_old, jnp.max(sc, axis=-1, keepdims=True))
            pe = jnp.exp(sc - m_new).astype(BF16)
            acc_ref[rows] = jnp.exp(m_old - m_new) * acc_ref[rows] + jnp.dot(pe, vext, preferred_element_type=F32)
            m_ref[rows] = m_new
        return carry

    lax.fori_loop(0, qb // nkb + 1, body, 0)
    acc = acc_ref[...]
    den = acc[:, hd:]
    o_slc = acc[:, :hd] / jnp.where(den > 0, den, 1.0)

    wk = min(WINDOW + QB, T)
    w0 = jnp.maximum(qb - WINDOW // QB, 0)
    offw = pl.multiple_of(w0 * QB, QB)
    kwin = kw_ref[0, pl.ds(offw, wk), :]
    vext = with_ones(vw_ref[0, pl.ds(offw, wk), :])
    d = rel(offw, wk)
    d = jnp.where(d <= WINDOW, d, -1)
    neg = jnp.concatenate([jnp.where(d >= 0, 0.0, NEG_INF)] * hpc, axis=0)
    o_win = []
    for c in range(NCHAIN):
        sc = _dot_t(q[c * crow:(c + 1) * crow], kwin) + bias_rows(c, w0, wk // QB) + neg
        e = jnp.exp(sc - jnp.max(sc, axis=-1, keepdims=True)).astype(BF16)
        acc = jnp.dot(e, vext, preferred_element_type=F32)
        o_win.append(acc[:, :hd] / acc[:, hd:])
    o_win = jnp.concatenate(o_win, axis=0)

    gt = gates_ref[0]
    for r in range(R):
        sl = slice(r * QB, (r + 1) * QB)
        o = (gt[:, r:r + 1] * o_cmp[sl] + gt[:, R + r:R + r + 1] * o_slc[sl]
             + gt[:, 2 * R + r:2 * R + r + 1] * o_win[sl])
        o_ref[:, r * hd:(r + 1) * hd] = o


def _nsa_prompt(q, cmp_kv, kvb3, gates_g, bias_tiles, bias_cmp, *, B, T):
    G = NSA_KV_HEADS
    R = NSA_REP
    nqb = T // Q_BLOCK
    ncp = T // CMP_STRIDE
    n_slc = -(-T // SLC_BLOCK)
    expand = (jnp.arange(T)[None, :] // SLC_BLOCK == jnp.arange(n_slc)[:, None]).astype(BF16)
    kv_spec = lambda col: pl.BlockSpec((1, T, HEAD_DIM), lambda b, g, i, col=col: (b, 0, col + g))
    return pl.pallas_call(
        functools.partial(_nsa_prompt_kernel, T=T),
        grid=(B, G, nqb),
        in_specs=[pl.BlockSpec((Q_BLOCK, R * HEAD_DIM), lambda b, g, i: (b * nqb + i, g)),
                  pl.BlockSpec((1, 1, 1, ncp, HEAD_DIM), lambda b, g, i: (b, 0, g, 0, 0)),
                  pl.BlockSpec((1, 1, 1, ncp, HEAD_DIM), lambda b, g, i: (b, 1, g, 0, 0)),
                  kv_spec(0), kv_spec(2), kv_spec(4), kv_spec(6),
                  pl.BlockSpec((nqb, R, Q_BLOCK, Q_BLOCK), lambda b, g, i: (0, g, 0, 0)),
                  pl.BlockSpec((1, R, Q_BLOCK, ncp), lambda b, g, i: (i, g, 0, 0)),
                  pl.BlockSpec((1, Q_BLOCK, LANES), lambda b, g, i: (g, b * nqb + i, 0)),
                  pl.BlockSpec((n_slc, T), lambda b, g, i: (0, 0))],
        out_specs=pl.BlockSpec((Q_BLOCK, R * HEAD_DIM), lambda b, g, i: (b * nqb + i, g)),
        out_shape=jax.ShapeDtypeStruct((B * T, G * R * HEAD_DIM), F32),
        scratch_shapes=[pltpu.VMEM((R * Q_BLOCK, 1), F32),
                        pltpu.VMEM((R * Q_BLOCK, 2 * HEAD_DIM), F32)],
        compiler_params=_cp(("parallel", "parallel", "arbitrary")), name="nsa_prompt",
    )(q, cmp_kv, cmp_kv, kvb3, kvb3, kvb3, kvb3, bias_tiles, bias_cmp, gates_g, expand)


def _nsa_sample_kernel(pt_ref, *refs, npages, tq):
    pages = refs[:npages]
    (win_ref, wnew_ref, q_ref, kvn_ref, gates_ref, pe_ref, w1_ref, w2_ref, kg_ref, bc_ref, bs_ref, bw_ref,
     o_ref, wout_ref) = refs[npages:]
    G, R, hd = NSA_KV_HEADS, NSA_REP, HEAD_DIM
    KVR = 4 * G
    WR = 2 * G
    page = pages[0].shape[2] // KVR
    past = npages * page
    L = past + tq
    nc = (L - CMP_BLOCK) // CMP_STRIDE + 1
    ng = past // CMP_STRIDE
    n_slc = -(-L // SLC_BLOCK)
    rows = R * tq
    wb = win_ref.shape[2] // WR
    wout_ref[0, 0, 0:(wb - tq) * WR, :] = win_ref[0, 0, tq * WR:wb * WR, :]
    wout_ref[0, 0, (wb - tq) * WR:wb * WR, :] = wnew_ref[0]
    pad_new = lambda v: jnp.concatenate([v, jnp.zeros((LANES - v.shape[0], v.shape[1]), v.dtype)], axis=0)
    t_of_row = lax.broadcasted_iota(jnp.int32, (R, tq, 1), 1).reshape(rows, 1)
    pos = past + t_of_row
    ra = lax.broadcasted_iota(jnp.int32, (rows, rows), 0)
    rb = lax.broadcasted_iota(jnp.int32, (rows, rows), 1)
    same_t = jnp.where(_mod_pow2(ra, tq) == _mod_pow2(rb, tq), 1.0, 0.0).astype(F32)
    kg = kg_ref[...]
    kvn = kvn_ref[0]

    for g in range(G):
        q = q_ref[0, g]
        gt = gates_ref[0, g]

        def compress(slot):
            col = slot * G + g

            def get_rows(r):
                return jnp.concatenate(
                    [pg[0, 0, pl.ds(r * KVR + col, page // CMP_STRIDE, stride=CMP_STRIDE * KVR), :] for pg in pages],
                    axis=0)

            return _compress_math(get_rows, ng, pe_ref[slot], w1_ref.at[slot], w2_ref[slot])

        kc = _rms(compress(0), kg).astype(BF16)
        vc = compress(1).astype(BF16)

        c_col = lax.broadcasted_iota(jnp.int32, (rows, ng), 1)
        dist = pos - (CMP_STRIDE * c_col + CMP_BLOCK - 1)
        p = _masked_softmax(_dot_t(q, kc) + bc_ref[g], jnp.where(c_col < nc, dist, -1) >= 0)
        o_cmp = jnp.dot(p.astype(BF16), vc, preferred_element_type=F32)

        psum = jnp.dot(same_t, p, precision=HIGHEST, preferred_element_type=F32)
        imp = jnp.dot(psum, _overlap_matrix(ng), precision=HIGHEST, preferred_element_type=F32)
        sel = _select_mask(imp, _div_pow2(pos, SLC_BLOCK), n_slc)
        nk = past + LANES
        selk = jnp.dot(sel.astype(BF16), _expand_matrix(nk), preferred_element_type=F32)

        k_new = pad_new(kvn[:, g * hd:(g + 1) * hd])
        v_new = pad_new(kvn[:, 256 + g * hd:256 + (g + 1) * hd])
        all_rows = lambda slot, new: jnp.concatenate(
            [pg[0, 0, pl.ds(slot * G + g, page, stride=KVR), :].astype(BF16) for pg in pages] + [new], axis=0)
        sc = _dot_t(q, all_rows(2, k_new)) + bs_ref[g]
        j = lax.broadcasted_iota(jnp.int32, (rows, nk), 1)
        p = _masked_softmax(sc, jnp.where(selk > 0.5, pos - j, -1) >= 0).astype(BF16)
        o_slc = jnp.dot(p, all_rows(3, v_new), preferred_element_type=F32)

        kw_new = pad_new(kvn[:, 512 + g * hd:512 + (g + 1) * hd])
        vw_new = pad_new(kvn[:, 768 + g * hd:768 + (g + 1) * hd])
        kw = win_ref[0, 0, pl.ds(g, wb, stride=WR), :].astype(BF16)
        vw = win_ref[0, 0, pl.ds(G + g, wb, stride=WR), :].astype(BF16)
        sc = _dot_t(q, jnp.concatenate([kw, kw_new], axis=0)) + bw_ref[g]
        j = lax.broadcasted_iota(jnp.int32, (rows, wb + LANES), 1)
        d = pos - (past - wb + j)
        p = _masked_softmax(sc, jnp.where(d <= WINDOW, d, -1) >= 0).astype(BF16)
        o_win = jnp.dot(p, jnp.concatenate([vw, vw_new], axis=0), preferred_element_type=F32)

        o_ref[0, g] = gt[:, 0:1] * o_cmp + gt[:, 1:2] * o_slc + gt[:, 2:3] * o_win


def _nsa_sample(slot, page_table, cache_lin, win_lin, win_new, q_s, kvn_s, gates_s, lw, bias_c, bias_s, bias_w, *, tq):
    B, npages = page_table.shape
    G = NSA_KV_HEADS
    rows = NSA_REP * tq
    l = slot.layer
    page_specs = [pl.BlockSpec((1, 1) + cache_lin.shape[2:], lambda b, pt, p=p: (l, pt[b, p], 0, 0))
                  for p in range(npages)]
    full = lambda a: pl.BlockSpec(a.shape, lambda b, pt, n=a.ndim: (0,) * n)
    wts = [lw["cmp_pe"], lw["cmp_w1"], lw["cmp_w2"], lw["k_norm_g"].reshape(1, -1), bias_c, bias_s, bias_w]
    grid_spec = pltpu.PrefetchScalarGridSpec(
        num_scalar_prefetch=1, grid=(B,),
        in_specs=page_specs + [
            pl.BlockSpec((1, 1) + win_lin.shape[2:], lambda b, pt: (l, b, 0, 0)),
            pl.BlockSpec((1,) + win_new.shape[1:], lambda b, pt: (b, 0, 0)),
            pl.BlockSpec((1, G, rows, HEAD_DIM), lambda b, pt: (b, 0, 0, 0)),
            pl.BlockSpec((1, SUBLANES, kvn_s.shape[2]), lambda b, pt: (b, 0, 0)),
            pl.BlockSpec((1, G, rows, LANES), lambda b, pt: (b, 0, 0, 0))] + [full(w) for w in wts]
        + slot.in_specs(),
        out_specs=[pl.BlockSpec((1, G, rows, HEAD_DIM), lambda b, pt: (b, 0, 0, 0)),
                   pl.BlockSpec((1, 1) + win_lin.shape[2:], lambda b, pt: (l, b, 0, 0))])
    n_in = 1 + npages + 5 + len(wts)
    return pl.pallas_call(
        slot.wrap(functools.partial(_nsa_sample_kernel, npages=npages, tq=tq), n_in),
        grid_spec=grid_spec,
        out_shape=[jax.ShapeDtypeStruct((B, G, rows, HEAD_DIM), F32),
                   jax.ShapeDtypeStruct((slot.depth,) + win_lin.shape[1:], F32)],
        input_output_aliases=slot.aliases(n_in, 1),
        compiler_params=_cp(("arbitrary",)), name="nsa_sample",
    )(page_table, *([cache_lin] * npages), win_lin, win_new, q_s, kvn_s, gates_s, *wts, *slot.args())


def _dot3(a, b):
    ah = a.astype(BF16)
    bh = b.astype(BF16)
    al = (a - ah.astype(F32)).astype(BF16)
    bl = (b - bh.astype(F32)).astype(BF16)
    d = functools.partial(jnp.dot, preferred_element_type=F32)
    return d(ah, bh) + (d(al, bh) + d(ah, bl))


def _gdn_chunk_kernel(gq_ref, ggb_ref, n_ref, k2_ref, qe_ref, o0_ref, eg_ref):
    B, C = gq_ref.shape[:2]
    H = GDN_HEADS
    hd = HEAD_DIM
    W = H * hd
    HC = H * C
    ri = lax.broadcasted_iota(jnp.int32, (HC, HC), 0)
    ci = lax.broadcasted_iota(jnp.int32, (HC, HC), 1)
    tri = jnp.where(_div_pow2(ri, C) == _div_pow2(ci, C), ri - ci, -1)
    own_lanes = (_div_pow2(lax.broadcasted_iota(jnp.int32, (HC, W), 0), C)
                 == _div_pow2(lax.broadcasted_iota(jnp.int32, (HC, W), 1), hd))

    for b in range(B):
        stack = lambda ref, c0: jnp.concatenate([ref[b, :, c0 + h * hd:c0 + (h + 1) * hd] for h in range(H)], axis=0)
        q = stack(gq_ref, 0)
        k = stack(gq_ref, W)
        v = stack(gq_ref, 2 * W)
        gc = stack(ggb_ref, 0)
        beta = stack(ggb_ref, W)
        gcol = jnp.concatenate([gc] * (HC // hd), axis=-1)
        grow = gcol.T
        decay = jnp.exp(jnp.where(tri >= 0, gcol - grow, -jnp.inf))
        kb = k * beta
        g2 = _dot_t(jnp.concatenate([kb, q], axis=0).astype(BF16), k.astype(BF16))
        low = jnp.where(tri > 0, g2[:HC] * decay, 0.0)
        qk = jnp.where(tri >= 0, g2[HC:] * decay, 0.0)
        egc = jnp.exp(gc)
        x = jnp.concatenate([v * beta, kb * egc], axis=-1)
        pw = low
        span = 1
        while span < C:
            last = span * 2 >= C
            rhs = x if last else jnp.concatenate([pw, x], axis=-1)
            r = _dot3(pw, rhs)
            px = r if last else r[:, HC:]
            x = x - px if span == 1 else x + px
            if not last:
                pw = r[:, :HC]
            span *= 2
        uw = x.astype(BF16)
        g_last = jnp.concatenate([jnp.broadcast_to(gc[(h + 1) * C - 1:(h + 1) * C], (C, hd)) for h in range(H)],
                                 axis=0)
        kd = k * jnp.exp(g_last - gc)
        kd_bd = jnp.where(own_lanes, jnp.concatenate([kd] * H, axis=-1), 0.0).astype(BF16)
        nk = lax.dot_general(kd_bd, uw, (((0,), (0,)), ((), ())), preferred_element_type=F32)
        ow = jnp.dot(qk.astype(BF16), uw, preferred_element_type=F32)
        qe = q * egc - ow[:, hd:]
        eg = jnp.exp(g_last)
        for h in range(H):
            n_ref[b, h, 0] = nk[h * hd:(h + 1) * hd, :hd]
            k2_ref[b, h, 0] = nk[h * hd:(h + 1) * hd, hd:].astype(BF16)
            o0_ref[b, h, 0] = ow[h * C:(h + 1) * C, :hd]
            qe_ref[b, h, 0] = qe[h * C:(h + 1) * C].astype(BF16)
            eg_ref[b, h, 0] = eg[h * C:h * C + SUBLANES]


def _gdn_scan_kernel(n_ref, k2_ref, qe_ref, o0_ref, eg_ref, gz_ref, s0_ref, ng_ref, y_ref, so_ref, s_ref):
    n = pl.program_id(0)
    B = gz_ref.shape[0]
    hd = HEAD_DIM

    @pl.when(n == 0)
    def _():
        s_ref[...] = s0_ref[...]

    ng = ng_ref[...]
    for b in range(B):
        for h in range(GDN_HEADS):
            S = s_ref[b, h]
            s16 = S.astype(BF16)
            o = jnp.dot(qe_ref[b, h, 0], s16, preferred_element_type=F32) + o0_ref[b, h, 0]
            s_ref[b, h] = (eg_ref[b, h, 0][0:1] * S + n_ref[b, h, 0]
                           - jnp.dot(k2_ref[b, h, 0], s16, preferred_element_type=F32))
            z = gz_ref[b, :, h * hd:(h + 1) * hd]
            y_ref[b, :, h * hd:(h + 1) * hd] = (_rms(o, ng) * (z * _sigmoid(z))).astype(y_ref.dtype)

    @pl.when(n == pl.num_programs(0) - 1)
    def _():
        so_ref[...] = s_ref[...]


def _gdn_prompt(gq3, ggb3, z3, s0, norm_g):
    B, T, _ = gq3.shape
    C = GDN_CHUNK
    H = GDN_HEADS
    hd = HEAD_DIM
    W = H * hd
    nchunk = T // C
    item = lambda rows: pl.BlockSpec((B, H, 1, rows, hd), lambda n: (0, 0, n, 0, 0))
    shape = lambda rows, dt: jax.ShapeDtypeStruct((B, H, nchunk, rows, hd), dt)
    nn, k2, qe, o0, eg = pl.pallas_call(
        _gdn_chunk_kernel,
        grid=(nchunk,),
        in_specs=[pl.BlockSpec((B, C, 3 * W), lambda n: (0, n, 0)),
                  pl.BlockSpec((B, C, 2 * W), lambda n: (0, n, 0))],
        out_specs=[item(hd), item(hd), item(C), item(C), item(SUBLANES)],
        out_shape=[shape(hd, F32), shape(hd, BF16), shape(C, BF16), shape(C, F32), shape(SUBLANES, F32)],
        compiler_params=_cp(("parallel",)), name="gdn_chunk",
    )(gq3, ggb3)
    return pl.pallas_call(
        _gdn_scan_kernel,
        grid=(nchunk,),
        in_specs=[item(hd), item(hd), item(C), item(C), item(SUBLANES),
                  pl.BlockSpec((B, C, W), lambda n: (0, n, C_GZ // W)),
                  pl.BlockSpec(s0.shape, lambda n: (0, 0, 0, 0)),
                  pl.BlockSpec((1, hd), lambda n: (0, 0))],
        out_specs=[pl.BlockSpec((B, C, W), lambda n: (0, n, 0)),
                   pl.BlockSpec(s0.shape, lambda n: (0, 0, 0, 0))],
        out_shape=[jax.ShapeDtypeStruct((B, T, W), BF16), jax.ShapeDtypeStruct(s0.shape, F32)],
        scratch_shapes=[pltpu.VMEM(s0.shape, F32)],
        compiler_params=_cp(("arbitrary",)), name="gdn_scan",
    )(nn, k2, qe, o0, eg, z3, s0, norm_g.reshape(1, -1))


def _gdn_sample_kernel(kq_ref, v_ref, gb_ref, gz_ref, s0_ref, ng_ref, y_ref, so_ref, *, tq):
    bt = kq_ref.shape[0]
    hd = HEAD_DIM
    ri = lax.broadcasted_iota(jnp.int32, (hd, hd), 0)
    ci = lax.broadcasted_iota(jnp.int32, (hd, hd), 1)
    eye = jnp.where(ri == ci, 1.0, 0.0).astype(F32)
    ng = ng_ref[...]

    def body(bi, carry):
        for h in range(GDN_HEADS):
            kq = kq_ref[bi, h]
            cols = _dot_t(eye, kq, HIGHEST)
            gb = gb_ref[bi, h]
            v = v_ref[bi, h]
            S = s0_ref[bi, h]
            outs = []
            for t in range(tq):
                a = jnp.exp(gb[t:t + 1])
                kc = cols[:, t:t + 1]
                qc = cols[:, tq + t:tq + t + 1]
                Sa = S * a
                stk = jnp.sum(Sa * kc, axis=0, keepdims=True)
                vn = gb[tq + t:tq + t + 1] * (v[t:t + 1] - stk)
                S = Sa + kc * vn
                outs.append(jnp.sum(S * qc, axis=0, keepdims=True))
            so_ref[0, bi, h] = S
            o = jnp.concatenate(outs + [jnp.zeros((SUBLANES - tq, hd), F32)], axis=0)
            z = gz_ref[bi, h]
            y_ref[bi, h] = _rms(o, ng) * (z * _sigmoid(z))
        return carry

    lax.fori_loop(0, bt, body, 0)


def _gdn_sample(slot, kq_s, v_s, gb_s, gz_s, s0, norm_g, *, tq):
    B = kq_s.shape[0]
    bt = 8
    layer = slot.layer
    blk = lambda a: pl.BlockSpec((bt,) + a.shape[1:], lambda i: (i, 0, 0, 0))
    in_specs = [blk(kq_s), blk(v_s), blk(gb_s), blk(gz_s), blk(s0), pl.BlockSpec((1, HEAD_DIM), lambda i: (0, 0))]
    return pl.pallas_call(
        slot.wrap(functools.partial(_gdn_sample_kernel, tq=tq), len(in_specs)),
        grid=(B // bt,),
        in_specs=in_specs + slot.in_specs(),
        out_specs=[blk(v_s), pl.BlockSpec((1, bt) + s0.shape[1:], lambda i: (layer, i, 0, 0, 0))],
        out_shape=[jax.ShapeDtypeStruct(v_s.shape, F32), jax.ShapeDtypeStruct((slot.depth,) + s0.shape, F32)],
        input_output_aliases=slot.aliases(len(in_specs), 1),
        compiler_params=_cp(("parallel",)), name="gdn_sample",
    )(kq_s, v_s, gb_s, gz_s, s0, norm_g.reshape(1, -1), *slot.args())


def _mix_out_kernel(ya_ref, yb_ref, yc_ref, x_ref, onb_ref, w_ref, o_ref, mix_ref):
    @pl.when(pl.program_id(1) == 0)
    def _():
        mix_ref[:, 0:512] = ya_ref[...]
        mix_ref[:, 512:1536] = _rms(yb_ref[...], onb_ref[...]).astype(BF16)
        mix_ref[:, 1536:2048] = yc_ref[...]

    o_ref[...] = x_ref[...] + jnp.dot(mix_ref[...], w_ref[...], preferred_element_type=F32)


def _mix_out(ya, yb, yc, x, onb, w, *, tn=512):
    M, N = x.shape
    tm = _row_tile(M)
    K = w.shape[0]
    return pl.pallas_call(
        _mix_out_kernel,
        grid=(M // tm, N // tn),
        in_specs=[pl.BlockSpec((tm, 512), lambda i, j: (i, 0)),
                  pl.BlockSpec((tm, 1024), lambda i, j: (i, 0)),
                  pl.BlockSpec((tm, 512), lambda i, j: (i, 0)),
                  pl.BlockSpec((tm, tn), lambda i, j: (i, j)),
                  pl.BlockSpec((1, 1024), lambda i, j: (0, 0)),
                  w.spec((K, tn), lambda i, j: (0, j))],
        out_specs=pl.BlockSpec((tm, tn), lambda i, j: (i, j)),
        out_shape=jax.ShapeDtypeStruct((M, N), F32),
        scratch_shapes=[pltpu.VMEM((tm, K), BF16)],
        compiler_params=_cp(("parallel", "arbitrary")), name="mix_out",
    )(ya, yb, yc, x, onb.reshape(1, -1), w.array)


def _matmul_res_kernel(a_ref, w_ref, r_ref, o_ref):
    o_ref[...] = r_ref[...] + jnp.dot(a_ref[...], w_ref[...], preferred_element_type=F32)


def _matmul_res(a, w, res, *, tn=512):
    M, K = a.shape
    N = w.shape[1]
    tm = _row_tile(M)
    return pl.pallas_call(
        _matmul_res_kernel,
        grid=(M // tm, N // tn),
        in_specs=[pl.BlockSpec((tm, K), lambda i, j: (i, 0)),
                  w.spec((K, tn), lambda i, j: (0, j)),
                  pl.BlockSpec((tm, tn), lambda i, j: (i, j))],
        out_specs=pl.BlockSpec((tm, tn), lambda i, j: (i, j)),
        out_shape=jax.ShapeDtypeStruct((M, N), F32),
        compiler_params=_cp(("parallel", "parallel")), name="matmul_res",
    )(a, w.array, res)


def _layer_weights(l, p):
    w_in = p["w_in"][l]
    w_main = jnp.concatenate([w_in[:, :4096], w_in[:, 4120:5656], w_in[:, 5664:6176]], axis=1).astype(BF16)
    w_small = jnp.concatenate([w_in[:, 4096:4120], w_in[:, 5656:5664],
                               jnp.zeros((w_in.shape[0], LANES - 32), w_in.dtype)], axis=1).astype(BF16)
    lane_vec = lambda v: jnp.zeros((1, LANES), F32).at[0, S_GA:S_GA + GDN_HEADS].set(v)
    return {
        "norm_mix_g": p["norm_mix_g"][l], "w_main": w_main, "w_small": w_small,
        "conv_a_w": p["conv_a_w"][l], "q_norm_g": p["q_norm_g"][l], "k_norm_g": p["k_norm_g"][l],
        "cmp_pe": p["cmp_pe"][l],
        "cmp_w1": p["cmp_w1"][l].reshape(2, 2, CMP_BLOCK // 2 * HEAD_DIM, CMP_HIDDEN).astype(BF16),
        "cmp_w2": p["cmp_w2"][l].astype(BF16),
        "gdn_conv_w": p["gdn_conv_w"][l], "alog_v": lane_vec(p["gdn_a_log"][l]), "dtb_v": lane_vec(p["gdn_dt_bias"][l]),
        "gdn_norm_g": p["gdn_norm_g"][l], "out_norm_a": p["out_norm_a"][l], "out_norm_b": p["out_norm_b"][l],
        "w_out": _LayerOf(p["w_out_bf16"], l), "norm_ffn_g": p["norm_ffn_g"][l],
        "ffn_up": _LayerOf(p["ffn_up_bf16"], l), "ffn_conv_w": p["ffn_conv_w"][l],
        "ffn_down": _LayerOf(p["ffn_down_bf16"], l),
    }


HALO_ROWS = 16


def _ffn_up_kernel(x_ref, xp_ref, g_ref, wa_ref, wv_ref, ha_ref, hv_ref, ca_ref, cv_ref,
                   o_ref, hoa_ref, hov_ref, xn_ref, *, tstride, tiles_per_seq):
    tn = wa_ref.shape[1]
    prompt = tstride == 1

    @pl.when(pl.program_id(1) == 0)
    def _():
        g = g_ref[...]
        if prompt:
            xn_ref[0:HALO_ROWS] = _rms(xp_ref[...], g).astype(BF16)
            xn_ref[HALO_ROWS:] = _rms(x_ref[...], g).astype(BF16)
        else:
            xn_ref[...] = _rms(x_ref[...], g).astype(BF16)

    xn = xn_ref[...]

    def part(w_ref, hist_ref, cw_ref, ho_ref):
        up = jnp.dot(xn, w_ref[...], preferred_element_type=F32)
        if prompt:
            start = (pl.program_id(0) % tiles_per_seq) == 0
            halo = jnp.where(start, hist_ref[0], up[HALO_ROWS - SUBLANES:HALO_ROWS])
            x = up[HALO_ROWS:]
            ho_ref[0] = x[x.shape[0] - SUBLANES:]
        else:
            halo = _ld(hist_ref, 0, tn)
            x = up
            _st(ho_ref, 0, tn, x[x.shape[0] - halo.shape[0]:])
        w = cw_ref[...]
        return w[0:1] * _shift(x, halo, 2, tstride) + w[1:2] * _shift(x, halo, 1, tstride) + w[2:3] * x

    a = part(wa_ref, ha_ref, ca_ref, hoa_ref)
    v = part(wv_ref, hv_ref, cv_ref, hov_ref)
    o_ref[...] = (a * _sigmoid(a) * v).astype(o_ref.dtype)


def _ffn_up(h, g, w_up, conv_w, hist, *, sample, nseq, seq_len, tn=512):
    M, K = h.shape
    tm = _row_tile(seq_len)
    F = w_up.shape[1] // 2
    nj = F // tn
    common = [pl.BlockSpec((1, K), lambda i, j: (0, 0)),
              w_up.spec((K, tn), lambda i, j: (0, j)),
              w_up.spec((K, tn), lambda i, j: (0, j + nj))]
    cw = [pl.BlockSpec((3, tn), lambda i, j: (0, j)), pl.BlockSpec((3, tn), lambda i, j: (0, j + nj))]
    if not sample:
        tps = seq_len // tm
        hspec = lambda off: pl.BlockSpec((1, SUBLANES, tn), lambda i, j, off=off: (i // tps, 0, j + off))
        hout = pl.BlockSpec((1, SUBLANES, tn), lambda i, j: (i, 0, j))
        act, tail_a, tail_v = pl.pallas_call(
            functools.partial(_ffn_up_kernel, tstride=1, tiles_per_seq=tps),
            grid=(M // tm, nj),
            in_specs=[pl.BlockSpec((tm, K), lambda i, j: (i, 0)),
                      pl.BlockSpec((HALO_ROWS, K), lambda i, j: (jnp.maximum(i * (tm // HALO_ROWS) - 1, 0), 0))]
            + common + [hspec(0), hspec(nj)] + cw,
            out_specs=[pl.BlockSpec((tm, tn), lambda i, j: (i, j)), hout, hout],
            out_shape=[jax.ShapeDtypeStruct((M, F), BF16)] + [jax.ShapeDtypeStruct((M // tm, SUBLANES, F), F32)] * 2,
            scratch_shapes=[pltpu.VMEM((tm + HALO_ROWS, K), BF16)],
            compiler_params=_cp(("arbitrary", "arbitrary")), name="ffn_up_prompt",
        )(h, h, g.reshape(1, K), w_up.array, w_up.array, hist, hist, conv_w, conv_w)
        return act, tail_a[tps - 1::tps], tail_v[tps - 1::tps]
    hk = hist.shape[0]
    hspec = lambda off: pl.BlockSpec((hk, nseq, tn), lambda i, j, off=off: (0, 0, j + off))
    hout = pl.BlockSpec((hk, nseq, tn), lambda i, j: (0, 0, j))
    return pl.pallas_call(
        functools.partial(_ffn_up_kernel, tstride=nseq, tiles_per_seq=1),
        grid=(1, nj),
        in_specs=[pl.BlockSpec((M, K), lambda i, j: (0, 0)), pl.BlockSpec((HALO_ROWS, K), lambda i, j: (0, 0))]
        + common + [hspec(0), hspec(nj)] + cw,
        out_specs=[pl.BlockSpec((M, tn), lambda i, j: (0, j)), hout, hout],
        out_shape=[jax.ShapeDtypeStruct((M, F), BF16)] + [jax.ShapeDtypeStruct((hk, nseq, F), F32)] * 2,
        scratch_shapes=[pltpu.VMEM((M, K), BF16)],
        compiler_params=_cp(("arbitrary", "arbitrary")), name="ffn_up_sample",
    )(h, h, g.reshape(1, K), w_up.array, w_up.array, hist, hist, conv_w, conv_w)


def _dense_tail(x2, ya, yb, yc, lw, hist_ffn, *, sample, nseq, seq_len):
    h = _mix_out(ya, yb, yc, x2, lw["out_norm_b"], lw["w_out"])
    act, hist_a, hist_v = _ffn_up(h, lw["norm_ffn_g"], lw["ffn_up"], lw["ffn_conv_w"], hist_ffn,
                                  sample=sample, nseq=nseq, seq_len=seq_len)
    return _matmul_res(act, lw["ffn_down"], h), jnp.concatenate([hist_a, hist_v], axis=-1)


def _prompt_layer(x2, lw, bias_tiles, bias_cmp, rows_slot, *, B, T):
    M = B * T
    G, R = NSA_KV_HEADS, NSA_REP
    z, zs = _norm_matmul(x2, lw["norm_mix_g"], lw["w_main"], lw["w_small"])
    hu0 = jnp.zeros((B, SUBLANES, 512), F32)
    hq0 = jnp.zeros((B, SUBLANES, 1536), F32)
    ya, q, rows, win, kvb, gates, gq, ggb, hu, rows_lin = _prep(z, zs, hu0, hq0, lw, sample=False, nseq=B, seq_len=T,
                                                                slot=rows_slot)
    cmp_kv = _compress_prompt(rows.reshape(B, T, 1024), lw)
    gates_g = gates[:, :3 * G * R].reshape(M, 3, G, R).transpose(2, 0, 1, 3).reshape(G, M, 3 * R)
    gates_g = jnp.pad(gates_g, ((0, 0), (0, 0), (0, LANES - 3 * R)))
    yb = _nsa_prompt(q, cmp_kv, kvb.reshape(B, T, 1024), gates_g, bias_tiles, bias_cmp, B=B, T=T)
    s0 = jnp.zeros((B, GDN_HEADS, HEAD_DIM, HEAD_DIM), F32)
    yc, s_new = _gdn_prompt(gq.reshape(B, T, 1536), ggb.reshape(B, T, 1024), z.reshape(B, T, C_END), s0,
                            lw["gdn_norm_g"])
    hf0 = jnp.zeros((B, SUBLANES, lw["ffn_conv_w"].shape[1]), F32)
    out, up = _dense_tail(x2, ya, yb, yc.reshape(M, 512), lw, hf0, sample=False, nseq=B, seq_len=T)
    return out, rows_lin, win, hu, z, s_new, up


def _sample_layer(x2, lw, st, bias_c, bias_s, bias_w, win_slot, gdn_slot, *, B, T):
    M = B * T
    G, R, H, hd = NSA_KV_HEADS, NSA_REP, GDN_HEADS, HEAD_DIM
    z, zs = _norm_matmul(x2, lw["norm_mix_g"], lw["w_main"], lw["w_small"])
    hu0 = jnp.swapaxes(st["conv_a"], 0, 1)
    hq0 = jnp.swapaxes(st["gdn_conv"], 0, 1)
    ya, q, rows, win, kvb, gates, gq, ggb, hu = _prep(z, zs, hu0, hq0, lw, sample=True, nseq=B, seq_len=T)
    tb = lambda a: a.reshape(T, B, -1)
    q_s = tb(q).reshape(T, B, G, R, hd).transpose(1, 2, 3, 0, 4).reshape(B, G, R * T, hd)
    gt = tb(gates)[:, :, :3 * G * R].reshape(T, B, 3, G, R).transpose(1, 3, 4, 0, 2).reshape(B, G, R * T, 3)
    gates_s = jnp.pad(gt, ((0, 0), (0, 0), (0, 0), (0, LANES - 3)))
    kvn_s = jnp.pad(jnp.swapaxes(tb(kvb), 0, 1), ((0, 0), (0, SUBLANES - T), (0, 0)))
    win_new = jnp.swapaxes(tb(win), 0, 1).reshape(B, T * 2 * G, hd)
    yb_s, win = _nsa_sample(win_slot, st["page_table"], st["cache_lin"], st["win_lin"], win_new, q_s, kvn_s, gates_s, lw,
                            bias_c, bias_s, bias_w, tq=T)
    yb = yb_s.reshape(B, G, R, T, hd).transpose(3, 0, 1, 2, 4).reshape(M, G * R * hd)
    gq4 = tb(gq).reshape(T, B, 3, H, hd)
    bh = lambda a: a.transpose(1, 2, 0, 3)
    padt = lambda a: jnp.pad(a, ((0, 0), (0, 0), (0, SUBLANES - T), (0, 0)))
    kq_s = jnp.concatenate([bh(gq4[:, :, 1]), bh(gq4[:, :, 0])], axis=2)
    v_s = padt(bh(gq4[:, :, 2]))
    ggb4 = tb(ggb).reshape(T, B, 2, H, hd)
    gb_s = jnp.concatenate([bh(ggb4[:, :, 0]), bh(ggb4[:, :, 1])], axis=2)
    gz_s = padt(bh(tb(z)[:, :, C_GZ:C_END].reshape(T, B, H, hd)))
    yc_s, s_new = _gdn_sample(gdn_slot, kq_s, v_s, gb_s, gz_s, st["gdn"], lw["gdn_norm_g"], tq=T)
    yc = yc_s[:, :, :T].transpose(2, 0, 1, 3).reshape(M, H * hd).astype(BF16)
    hf0 = jnp.swapaxes(st["ffn_conv"], 0, 1)
    out, up = _dense_tail(x2, ya, yb, yc, lw, hf0, sample=True, nseq=B, seq_len=T)
    return out, rows, win, hu, z, s_new, up


def kernel(x_prompt, x_sample, cache_kv, cache_win, state_conv_a, state_gdn_conv, state_gdn, state_ffn_conv,
           page_table, rel_bias, norm_mix_g, w_in, conv_a_w, q_norm_g, k_norm_g, cmp_pe, cmp_w1, cmp_w2,
           gdn_conv_w, gdn_a_log, gdn_dt_bias, gdn_norm_g, out_norm_a, out_norm_b, w_out, norm_ffn_g,
           ffn_up, ffn_conv_w, ffn_down):
    params = dict(norm_mix_g=norm_mix_g, w_in=w_in, conv_a_w=conv_a_w, q_norm_g=q_norm_g, k_norm_g=k_norm_g,
                  cmp_pe=cmp_pe, cmp_w1=cmp_w1, cmp_w2=cmp_w2, gdn_conv_w=gdn_conv_w, gdn_a_log=gdn_a_log,
                  gdn_dt_bias=gdn_dt_bias, gdn_norm_g=gdn_norm_g, out_norm_a=out_norm_a, out_norm_b=out_norm_b,
                  w_out=w_out, norm_ffn_g=norm_ffn_g, ffn_up=ffn_up, ffn_conv_w=ffn_conv_w, ffn_down=ffn_down,
                  w_out_bf16=w_out.astype(BF16), ffn_up_bf16=ffn_up.astype(BF16), ffn_down_bf16=ffn_down.astype(BF16))
    depth = w_in.shape[0]
    Bp, T, D = x_prompt.shape
    Bs, Ts, _ = x_sample.shape
    G, R, hd = NSA_KV_HEADS, NSA_REP, HEAD_DIM
    n_pool, page = cache_kv.shape[1], cache_kv.shape[2]
    npages = page_table.shape[1]
    past = npages * page
    wb = cache_win.shape[2]
    L = past + Ts
    assert ((L - CMP_BLOCK) // CMP_STRIDE) * CMP_STRIDE + CMP_BLOCK <= past, "compressed blocks must lie in the cache"
    assert T % KEY_TILE == 0 and T >= WINDOW + Q_BLOCK and 3 <= Ts <= SUBLANES // 2

    thr = _bucket_thresholds()
    rb_flat = rel_bias.reshape(-1)
    nqb = T // Q_BLOCK
    bias_tiles = _bias_table(thr, rb_flat, nqb, Q_BLOCK, Q_BLOCK, a0=0, an=Q_BLOCK, qs=1, ks=1)
    bias_cmp = _bias_table(thr, rb_flat, nqb, Q_BLOCK, T // CMP_STRIDE,
                           a0=-(CMP_BLOCK - 1), an=Q_BLOCK, qs=1, ks=CMP_STRIDE)

    def sample_bias(ncols, a0, ks):
        t = _bias_table(thr, rb_flat, 1, SUBLANES, ncols, a0=a0, an=0, qs=1, ks=ks)[0]
        return t[:, :Ts].reshape(G, R * Ts, ncols)

    bias_c = sample_bias(past // CMP_STRIDE, past - (CMP_BLOCK - 1), CMP_STRIDE)
    bias_s = sample_bias(past + LANES, past, 1)
    bias_w = sample_bias(wb + LANES, wb, 1)

    cache_lin = cache_kv.reshape(depth, n_pool, page * 4 * G, hd)
    win_lin = cache_win.reshape(depth, Bs, wb * 2 * G, hd)
    assert wb == WINDOW, "the new window buffer is the old one shifted by the new rows"

    xp = x_prompt.reshape(Bp * T, D)
    xs = jnp.swapaxes(x_sample, 0, 1).reshape(Ts * Bs, D)
    outs_p, outs_s = [], []
    rows_p_all = win_s_all = gdn_s_all = None
    for l in range(depth):
        lw = _layer_weights(l, params)
        xp, rows_p_all, win, hu, z, s_new, up = _prompt_layer(xp, lw, bias_tiles, bias_cmp,
                                                              _LayerSlot(l, depth, rows_p_all), B=Bp, T=T)
        wl = min(WINDOW, T)
        outs_p.append((
            win.reshape(Bp, T, 2, G, hd)[:, T - wl:],
            hu[:, SUBLANES - 2:],
            z.reshape(Bp, T, C_END)[:, T - 3:, C_GQKV:C_GZ],
            s_new,
            up[:, SUBLANES - 2:]))
        st = dict(page_table=page_table, cache_lin=cache_lin, win_lin=win_lin, conv_a=state_conv_a[l],
                  gdn_conv=state_gdn_conv[l], gdn=state_gdn[l], ffn_conv=state_ffn_conv[l])
        xs, rows, win_s_all, hu, z, gdn_s_all, up = _sample_layer(
            xs, lw, st, bias_c, bias_s, bias_w, _LayerSlot(l, depth, win_s_all), _LayerSlot(l, depth, gdn_s_all),
            B=Bs, T=Ts)
        tb = lambda a: jnp.swapaxes(a.reshape(Ts, Bs, -1), 0, 1)
        outs_s.append((
            tb(rows).reshape(Bs, Ts, 4, G, hd),
            jnp.swapaxes(hu, 0, 1),
            tb(z)[:, Ts - 3:, C_GQKV:C_GZ],
            jnp.swapaxes(up, 0, 1)))
    y_p = xp.reshape(Bp, T, D)
    y_s = jnp.swapaxes(xs.reshape(Ts, Bs, D), 0, 1)
    stack = lambda outs, i: jnp.stack([o[i] for o in outs])
    return (y_p, y_s,
            rows_p_all.reshape(depth, Bp, T, 4, G, hd), stack(outs_s, 0),
            stack(outs_p, 0), win_s_all.reshape(depth, Bs, wb, 2, G, hd),
            stack(outs_p, 1), stack(outs_s, 1), stack(outs_p, 2), stack(outs_s, 2),
            stack(outs_p, 3), gdn_s_all, stack(outs_p, 4), stack(outs_s, 3))
```

```python
import functools
import math

import jax
import jax.numpy as jnp
from jax import lax
from jax.experimental import pallas as pl
from jax.experimental.pallas import tpu as pltpu

F32 = jnp.float32
BF16 = jnp.bfloat16
HIGHEST = lax.Precision.HIGHEST

SUBLANES = 8
LANES = 128
VMEM_LIMIT_BYTES = 52 * 1024 * 1024

HEAD_DIM = 128
NSA_KV_HEADS = 2
NSA_REP = 4
NSA_HEADS = NSA_KV_HEADS * NSA_REP
GDN_HEADS = 4
CMP_BLOCK = 32
CMP_STRIDE = 16
CMP_HIDDEN = 256
SLC_BLOCK = 64
N_SELECT = 16
WINDOW = 512
Q_BLOCK = 128
GDN_CHUNK = 64
REL_BUCKETS = 32
REL_MAX_DIST = 1024
EPS = 1e-6
NEG_INF = -1e30
FORCE_SCORE = 1e4

C_AB, C_AC, C_AH, C_NQ, C_NKV, C_GQKV, C_GZ, C_END = 0, 512, 1024, 1536, 2560, 4096, 5632, 6144
S_GATE, S_GA, S_GB = 0, 24, 28


def _cp(sem):
    return pltpu.CompilerParams(dimension_semantics=sem, vmem_limit_bytes=VMEM_LIMIT_BYTES)


class _LayerOf:
    def __init__(self, stacked, layer):
        self.array, self.layer, self.shape = stacked, layer, stacked.shape[1:]

    def spec(self, block, index_map):
        layer = self.layer
        return pl.BlockSpec((None,) + block, lambda *ij: (layer,) + tuple(index_map(*ij)))


class _LayerSlot:
    def __init__(self, layer, depth, prev):
        self.layer, self.depth, self.prev = layer, depth, prev

    def in_specs(self):
        return [] if self.prev is None else [pl.BlockSpec(memory_space=pl.ANY)]

    def args(self):
        return [] if self.prev is None else [self.prev]

    def aliases(self, n_inputs_before, out_index):
        return {} if self.prev is None else {n_inputs_before: out_index}

    def wrap(self, kernel, n_refs_before):
        if self.prev is None:
            return kernel

        def wrapped(*refs):
            return kernel(*refs[:n_refs_before], *refs[n_refs_before + 1:])
        return wrapped


def _div_pow2(x, d):
    assert d & (d - 1) == 0
    return jnp.right_shift(x, d.bit_length() - 1)


def _mod_pow2(x, d):
    assert d & (d - 1) == 0
    return jnp.bitwise_and(x, d - 1)


def _sigmoid(x):
    return 1.0 / (1.0 + jnp.exp(-x))


def _rms(x, g):
    return x * lax.rsqrt(jnp.mean(x * x, axis=-1, keepdims=True) + EPS) * g


def _l2n(x):
    return x * lax.rsqrt(jnp.sum(x * x, axis=-1, keepdims=True) + EPS)


def _dot_t(a, b, precision=None):
    return lax.dot_general(a, b, (((1,), (1,)), ((), ())), precision=precision, preferred_element_type=F32)


def _ld(ref, c0, c1):
    if len(ref.shape) == 3:
        v = ref[:, :, c0:c1]
        return v.reshape(v.shape[0] * v.shape[1], v.shape[2])
    return ref[:, c0:c1]


def _st(ref, c0, c1, val):
    if len(ref.shape) == 3:
        ref[:, :, c0:c1] = val.reshape(ref.shape[0], ref.shape[1], c1 - c0).astype(ref.dtype)
    else:
        ref[:, c0:c1] = val.astype(ref.dtype)


def _shift(x, halo, s, tstride):
    n = s * tstride
    rows = x.shape[0]
    if tstride % SUBLANES == 0:
        hr = halo.shape[0]
        return jnp.concatenate([halo[hr - n:], x[:rows - n]], axis=0)
    xs = pltpu.roll(x, n, 0)
    hs = pltpu.roll(halo, n, 0)
    rid = lax.broadcasted_iota(jnp.int32, (SUBLANES, x.shape[1]), 0)
    head = jnp.where(rid < n, hs, xs[:SUBLANES])
    return jnp.concatenate([head, xs[SUBLANES:]], axis=0)


def _norm_matmul_kernel(x_ref, g_ref, w_ref, ws_ref, o_ref, os_ref, xn_ref):
    @pl.when(pl.program_id(1) == 0)
    def _():
        xn = _rms(x_ref[...], g_ref[...]).astype(BF16)
        xn_ref[...] = xn
        os_ref[...] = jnp.dot(xn, ws_ref[...], preferred_element_type=F32)

    o_ref[...] = jnp.dot(xn_ref[...], w_ref[...], preferred_element_type=F32)


def _row_tile(rows):
    for tm in (1024, 512):
        if rows % tm == 0:
            return tm
    return rows


def _norm_matmul(x, g, w, w_small, *, tn=512):
    M, K = x.shape
    N = w.shape[1]
    tm = _row_tile(M)
    return pl.pallas_call(
        _norm_matmul_kernel,
        grid=(M // tm, N // tn),
        in_specs=[pl.BlockSpec((tm, K), lambda i, j: (i, 0)),
                  pl.BlockSpec((1, K), lambda i, j: (0, 0)),
                  pl.BlockSpec((K, tn), lambda i, j: (0, j)),
                  pl.BlockSpec((K, LANES), lambda i, j: (0, 0))],
        out_specs=[pl.BlockSpec((tm, tn), lambda i, j: (i, j)), pl.BlockSpec((tm, LANES), lambda i, j: (i, 0))],
        out_shape=[jax.ShapeDtypeStruct((M, N), F32), jax.ShapeDtypeStruct((M, LANES), F32)],
        scratch_shapes=[pltpu.VMEM((tm, K), BF16)],
        compiler_params=_cp(("parallel", "arbitrary")), name="norm_matmul",
    )(x, g.reshape(1, K), w, w_small)


def _prep_kernel(z_ref, zp_ref, zs_ref, hu_ref, hq_ref, caw_ref, gcw_ref, qg_ref, kg_ref, ona_ref,
                 alog_ref, dtb_ref,
                 ya_ref, q_ref, rows_ref, win_ref, kvb_ref, gates_ref, gq_ref, ggb_ref, huo_ref, *lin_refs,
                 tstride, tiles_per_seq):
    hd = HEAD_DIM

    def put_rows(grp, val):
        _st(rows_ref, grp * hd, (grp + 1) * hd, val)
        if lin_refs:
            lin_refs[0][0, pl.ds(grp, val.shape[0], stride=4 * NSA_KV_HEADS), :] = val
    u = _ld(z_ref, C_AC, C_AH) * _ld(z_ref, C_AH, C_NQ)
    if tstride == 1:
        start = (pl.program_id(0) % tiles_per_seq) == 0
        halo_u = jnp.where(start, hu_ref[0], zp_ref[:, C_AC:C_AH] * zp_ref[:, C_AH:C_NQ])
        halo_q = jnp.where(start, hq_ref[0], zp_ref[:, C_GQKV:C_GZ])
    else:
        halo_u = _ld(hu_ref, 0, 512)
        halo_q = _ld(hq_ref, 0, 1536)
    caw = caw_ref[...]
    conv = caw[0:1] * _shift(u, halo_u, 2, tstride) + caw[1:2] * _shift(u, halo_u, 1, tstride) + caw[2:3] * u
    y = _ld(z_ref, C_AB, C_AC) * conv
    _st(ya_ref, 0, 512, _rms(y, ona_ref[...]))
    hu_rows = huo_ref.shape[0] * huo_ref.shape[1] if tstride != 1 else SUBLANES
    if tstride == 1:
        huo_ref[0] = u[u.shape[0] - hu_rows:]
    else:
        _st(huo_ref, 0, 512, u[u.shape[0] - hu_rows:])

    qg = qg_ref[...]
    kg = kg_ref[...]
    for h in range(NSA_HEADS):
        c0 = C_NQ + h * hd
        _st(q_ref, h * hd, (h + 1) * hd, _rms(_ld(z_ref, c0, c0 + hd), qg) * (hd ** -0.5))
    for grp in range(4):
        put_rows(grp, _ld(z_ref, C_NKV + grp * hd, C_NKV + (grp + 1) * hd))
    for g in range(NSA_KV_HEADS):
        c0 = C_NKV + 512 + g * hd
        kn = _rms(_ld(z_ref, c0, c0 + hd), kg)
        put_rows(4 + g, kn)
        _st(kvb_ref, g * hd, (g + 1) * hd, kn)
    for g in range(NSA_KV_HEADS):
        c0 = C_NKV + 768 + g * hd
        vs = _ld(z_ref, c0, c0 + hd)
        put_rows(6 + g, vs)
        _st(kvb_ref, 256 + g * hd, 256 + (g + 1) * hd, vs)
    for g in range(NSA_KV_HEADS):
        c0 = C_NKV + 1024 + g * hd
        kn = _rms(_ld(z_ref, c0, c0 + hd), kg)
        _st(win_ref, g * hd, (g + 1) * hd, kn)
        _st(kvb_ref, 512 + g * hd, 512 + (g + 1) * hd, kn)
    vw = _ld(z_ref, C_NKV + 1280, C_NKV + 1536)
    _st(win_ref, 256, 512, vw)
    _st(kvb_ref, 768, 1024, vw)

    zs = _ld(zs_ref, 0, LANES)
    sg = _sigmoid(zs)
    _st(gates_ref, 0, LANES, sg)
    xs = zs + dtb_ref[...]
    e = jnp.exp(-jnp.abs(xs))
    u1 = 1.0 + e
    log1p_e = jnp.where(u1 == 1.0, e, jnp.log(u1) * (e / jnp.where(u1 == 1.0, 1.0, u1 - 1.0)))
    softplus = jnp.maximum(xs, 0.0) + log1p_e
    gdec = -jnp.exp(alog_ref[...]) * softplus
    rows = zs.shape[0]
    if tstride == 1:
        in_chunk = _mod_pow2(lax.broadcasted_iota(jnp.int32, gdec.shape, 0), GDN_CHUNK)
        step = 1
        while step < GDN_CHUNK:
            gdec = gdec + jnp.where(in_chunk >= step, pltpu.roll(gdec, step, 0), 0.0)
            step *= 2
    for h in range(GDN_HEADS):
        _st(ggb_ref, h * hd, (h + 1) * hd, jnp.broadcast_to(gdec[:, S_GA + h:S_GA + h + 1], (rows, hd)))
        _st(ggb_ref, 512 + h * hd, 512 + (h + 1) * hd, jnp.broadcast_to(sg[:, S_GB + h:S_GB + h + 1], (rows, hd)))

    gcw = gcw_ref[...]
    for part in range(3):
        c0 = C_GQKV + part * 512
        x = _ld(z_ref, c0, c0 + 512)
        hq = halo_q[:, part * 512:(part + 1) * 512]
        w = gcw[:, part * 512:(part + 1) * 512]
        c = (w[0:1] * _shift(x, hq, 3, tstride) + w[1:2] * _shift(x, hq, 2, tstride)
             + w[2:3] * _shift(x, hq, 1, tstride) + w[3:4] * x)
        c = c * _sigmoid(c)
        if part == 2:
            _st(gq_ref, 1024, 1536, c)
        else:
            for h in range(GDN_HEADS):
                v = _l2n(c[:, h * hd:(h + 1) * hd])
                if part == 0:
                    v = v * (hd ** -0.5)
                _st(gq_ref, part * 512 + h * hd, part * 512 + (h + 1) * hd, v)


def _prep(z, zs, hist_u, hist_q, lw, *, sample, nseq, seq_len, slot=None):
    M = z.shape[0]
    wts = [lw["conv_a_w"], lw["gdn_conv_w"], lw["q_norm_g"].reshape(1, -1), lw["k_norm_g"].reshape(1, -1),
           lw["out_norm_a"].reshape(1, -1), lw["alog_v"], lw["dtb_v"]]
    wspecs2 = [pl.BlockSpec(w.shape, lambda i: (0, 0)) for w in wts]
    widths = [(512, BF16), (1024, BF16), (1024, F32), (512, F32), (1024, BF16), (LANES, F32), (1536, F32), (1024, F32)]
    if not sample:
        tm = 256
        tps = seq_len // tm
        in_specs = [pl.BlockSpec((tm, C_END), lambda i: (i, 0)),
                    pl.BlockSpec((SUBLANES, C_END), lambda i: (jnp.maximum(i * (tm // SUBLANES) - 1, 0), 0)),
                    pl.BlockSpec((tm, LANES), lambda i: (i, 0)),
                    pl.BlockSpec((1, SUBLANES, 512), lambda i: (i // tps, 0, 0)),
                    pl.BlockSpec((1, SUBLANES, 1536), lambda i: (i // tps, 0, 0))] + wspecs2
        out_specs = [pl.BlockSpec((tm, w), lambda i: (i, 0)) for w, _ in widths]
        out_specs.append(pl.BlockSpec((1, SUBLANES, 512), lambda i: (i // tps, 0, 0)))
        out_shape = [jax.ShapeDtypeStruct((M, w), d) for w, d in widths]
        out_shape.append(jax.ShapeDtypeStruct((nseq, SUBLANES, 512), F32))
        kvr = 4 * NSA_KV_HEADS
        layer = slot.layer
        out_specs.append(pl.BlockSpec((1, tm * kvr, HEAD_DIM), lambda i: (layer, i, 0)))
        out_shape.append(jax.ShapeDtypeStruct((slot.depth, M * kvr, HEAD_DIM), F32))
        n_in = len(in_specs)
        return pl.pallas_call(
            slot.wrap(functools.partial(_prep_kernel, tstride=1, tiles_per_seq=tps), n_in),
            grid=(M // tm,), in_specs=in_specs + slot.in_specs(), out_specs=out_specs, out_shape=out_shape,
            input_output_aliases=slot.aliases(n_in, len(out_shape) - 1),
            compiler_params=_cp(("arbitrary",)), name="prep_prompt",
        )(z, z, zs, hist_u, hist_q, *wts, *slot.args())
    T = seq_len
    bt = 64
    z3 = z.reshape(T, nseq, C_END)
    zs3 = zs.reshape(T, nseq, LANES)
    in_specs = [pl.BlockSpec((T, bt, C_END), lambda i: (0, i, 0)),
                pl.BlockSpec((SUBLANES, C_END), lambda i: (0, 0)),
                pl.BlockSpec((T, bt, LANES), lambda i: (0, i, 0)),
                pl.BlockSpec((hist_u.shape[0], bt, 512), lambda i: (0, i, 0)),
                pl.BlockSpec((hist_q.shape[0], bt, 1536), lambda i: (0, i, 0))] + wspecs2
    out_specs = [pl.BlockSpec((T, bt, w), lambda i: (0, i, 0)) for w, _ in widths]
    out_specs.append(pl.BlockSpec((hist_u.shape[0], bt, 512), lambda i: (0, i, 0)))
    out_shape = [jax.ShapeDtypeStruct((T, nseq, w), d) for w, d in widths]
    out_shape.append(jax.ShapeDtypeStruct((hist_u.shape[0], nseq, 512), F32))
    outs = pl.pallas_call(
        functools.partial(_prep_kernel, tstride=bt, tiles_per_seq=1),
        grid=(nseq // bt,), in_specs=in_specs, out_specs=out_specs, out_shape=out_shape,
        compiler_params=_cp(("arbitrary",)), name="prep_sample",
    )(z3, z, zs3, hist_u, hist_q, *wts)
    return [o.reshape(M, o.shape[-1]) for o in outs[:-1]] + [outs[-1]]


def _bias_kernel(thr_ref, rb_ref, o_ref, *, a0, an, qs, ks):
    n = pl.program_id(0)
    R, C = o_ref.shape[-2:]

    assert qs >= 0 and ks >= 0
    last = REL_BUCKETS - 1
    cw = min(LANES, C)
    assert C % cw == 0

    def rows8(rc, carry):
        r0 = pl.multiple_of(rc * SUBLANES, SUBLANES)
        base = a0 + an * n + qs * r0
        for c0 in range(0, C, cw):
            dist = (base + qs * lax.broadcasted_iota(jnp.int32, (SUBLANES, cw), 0)
                    - ks * (c0 + lax.broadcasted_iota(jnp.int32, (SUBLANES, cw), 1)))
            d_min = base - ks * (c0 + cw - 1)
            d_max = base + qs * (SUBLANES - 1) - ks * c0

            def fill(bucket, c0=c0):
                for h in range(NSA_HEADS):
                    o_ref[0, h, pl.ds(r0, SUBLANES), c0:c0 + cw] = jnp.full(
                        (SUBLANES, cw), rb_ref[bucket * NSA_HEADS + h], F32)

            def general(dist=dist, c0=c0):
                b = [jnp.full((SUBLANES, cw), rb_ref[h], F32) for h in range(NSA_HEADS)]
                for k in range(1, REL_BUCKETS):
                    reached = dist >= thr_ref[k]
                    b = [jnp.where(reached, rb_ref[k * NSA_HEADS + h], b[h]) for h in range(NSA_HEADS)]
                for h in range(NSA_HEADS):
                    o_ref[0, h, pl.ds(r0, SUBLANES), c0:c0 + cw] = b[h]

            lax.cond(d_min >= thr_ref[last], functools.partial(fill, last),
                     lambda fill=fill, general=general, d_max=d_max: lax.cond(
                         d_max <= 0, functools.partial(fill, 0), general))
        return carry

    lax.fori_loop(0, R // SUBLANES, rows8, 0)


def _bias_table(thr, rb_flat, n, R, C, *, a0, an, qs, ks):
    return pl.pallas_call(
        functools.partial(_bias_kernel, a0=a0, an=an, qs=qs, ks=ks),
        grid=(n,),
        in_specs=[pl.BlockSpec(memory_space=pltpu.SMEM), pl.BlockSpec(memory_space=pltpu.SMEM)],
        out_specs=pl.BlockSpec((1, NSA_HEADS, R, C), lambda i: (i, 0, 0, 0)),
        out_shape=jax.ShapeDtypeStruct((n, NSA_HEADS, R, C), F32),
        compiler_params=_cp(("parallel",)), name="bias_table",
    )(thr, rb_flat)


def _bucket_thresholds():
    n = jnp.arange(REL_MAX_DIST + 1)
    exact = REL_BUCKETS // 2
    nf = jnp.maximum(n, 1).astype(F32)
    far = exact + (jnp.log(nf / exact) / math.log(REL_MAX_DIST / exact) * (REL_BUCKETS - exact)).astype(jnp.int32)
    bucket = jnp.where(n < exact, n, jnp.minimum(far, REL_BUCKETS - 1))
    return jnp.sum(bucket[None, :] < jnp.arange(REL_BUCKETS)[:, None], axis=1).astype(jnp.int32)


def _gelu_tanh(x):
    return x * (0.5 * (1.0 + jnp.tanh(math.sqrt(2.0 / math.pi) * (x + 0.044715 * (x * x * x)))))


def _compress_math(get_rows, ng, pe, w1_ref, w2):
    half = CMP_BLOCK // 2
    rows = [get_rows(r) for r in range(half)]
    top = jnp.dot(jnp.concatenate([(rows[r] + pe[r:r + 1]).astype(BF16) for r in range(half)], axis=-1),
                  w1_ref[0], preferred_element_type=F32)
    bot = jnp.dot(jnp.concatenate([(rows[r] + pe[r + half:r + half + 1]).astype(BF16) for r in range(half)], axis=-1),
                  w1_ref[1], preferred_element_type=F32)
    h = top + pltpu.roll(bot, ng - 1, 0)
    return jnp.dot(_gelu_tanh(h).astype(BF16), w2, preferred_element_type=F32)


def _compress_kernel(x_ref, pe_ref, w1_ref, w2_ref, kg_ref, o_ref):
    ng = x_ref.shape[1] // CMP_STRIDE
    slot = pl.program_id(1)

    def get_rows(r):
        return x_ref[0, pl.ds(r, ng, stride=CMP_STRIDE), :]

    out = _compress_math(get_rows, ng, pe_ref[0], w1_ref.at[0], w2_ref[0])
    o_ref[0, 0, 0] = jnp.where(slot == 0, _rms(out, kg_ref[...]), out).astype(o_ref.dtype)


def _compress_prompt(rows3, lw):
    B, T, _ = rows3.shape
    ng = T // CMP_STRIDE
    G = NSA_KV_HEADS
    return pl.pallas_call(
        _compress_kernel,
        grid=(B, 2, G),
        in_specs=[pl.BlockSpec((1, T, HEAD_DIM), lambda b, s, g: (b, 0, s * G + g)),
                  pl.BlockSpec((1, CMP_BLOCK, HEAD_DIM), lambda b, s, g: (s, 0, 0)),
                  pl.BlockSpec((1,) + lw["cmp_w1"].shape[1:], lambda b, s, g: (s, 0, 0, 0)),
                  pl.BlockSpec((1, CMP_HIDDEN, HEAD_DIM), lambda b, s, g: (s, 0, 0)),
                  pl.BlockSpec((1, HEAD_DIM), lambda b, s, g: (0, 0))],
        out_specs=pl.BlockSpec((1, 1, 1, ng, HEAD_DIM), lambda b, s, g: (b, s, g, 0, 0)),
        out_shape=jax.ShapeDtypeStruct((B, 2, G, ng, HEAD_DIM), BF16),
        compiler_params=_cp(("parallel", "parallel", "parallel")), name="compress_prompt",
    )(rows3, lw["cmp_pe"], lw["cmp_w1"], lw["cmp_w2"], lw["k_norm_g"].reshape(1, -1))


def _masked_softmax(s, mask):
    s = jnp.where(mask, s, NEG_INF)
    m = jnp.max(s, axis=-1, keepdims=True)
    e = jnp.where(mask, jnp.exp(s - m), 0.0)
    z = jnp.sum(e, axis=-1, keepdims=True)
    return e / jnp.where(z > 0, z, 1.0)


def _overlap_matrix(ncp):
    c = CMP_STRIDE * lax.broadcasted_iota(jnp.int32, (ncp, LANES), 0)
    s = SLC_BLOCK * lax.broadcasted_iota(jnp.int32, (ncp, LANES), 1)
    return jnp.where(c < s + SLC_BLOCK, jnp.where(c + CMP_BLOCK > s, 1.0, 0.0), 0.0).astype(F32)


def _select_mask(imp, cur, n_slc):
    R = imp.shape[0]
    blk = lax.broadcasted_iota(jnp.int32, (R, LANES), 1)
    for forced_blk in (0, cur, cur - 1):
        imp = jnp.where(blk == forced_blk, FORCE_SCORE, imp)
    imp = jnp.where(blk <= cur, imp, -1.0)
    imp = jnp.where(blk < n_slc, imp, -2.0)
    cnt = jnp.zeros((R, LANES), F32)
    for k in range(n_slc):
        col = imp[:, k:k + 1]
        cnt = cnt + jnp.where(col > imp, 1.0, jnp.where(col == imp, jnp.where(blk > k, 1.0, 0.0), 0.0))
    return jnp.where(cnt < float(min(N_SELECT, n_slc)), jnp.where(blk < n_slc, 1.0, 0.0), 0.0)


def _expand_matrix(nkeys):
    s = lax.broadcasted_iota(jnp.int32, (LANES, nkeys), 0)
    j = lax.broadcasted_iota(jnp.int32, (LANES, nkeys), 1)
    lo = s * SLC_BLOCK
    return jnp.where(j >= lo, jnp.where(j < lo + SLC_BLOCK, 1.0, 0.0), 0.0).astype(BF16)


KEY_TILE = 1024


def _select_mask_t(imp_t, cur, n_slc):
    blk = lax.broadcasted_iota(jnp.int32, imp_t.shape, 0)
    for forced_blk in (0, cur, cur - 1):
        imp_t = jnp.where(blk == forced_blk, FORCE_SCORE, imp_t)
    imp_t = jnp.where(blk <= cur, imp_t, -1.0)
    cnt = jnp.zeros(imp_t.shape, F32)
    for k in range(n_slc):
        row = imp_t[k:k + 1, :]
        cnt = cnt + jnp.where(row > imp_t, 1.0, jnp.where(row == imp_t, jnp.where(blk > k, 1.0, 0.0), 0.0))
    return jnp.where(cnt < float(min(N_SELECT, n_slc)), 1.0, 0.0)


def _nsa_prompt_kernel(q_ref, kc_ref, vc_ref, ks_ref, vs_ref, kw_ref, vw_ref, bt_ref, bc_ref, gates_ref, ex_ref,
                       o_ref, m_ref, acc_ref, *, T):
    qb = pl.program_id(2)
    QB = Q_BLOCK
    R = NSA_REP
    RQ = R * QB
    hd = HEAD_DIM
    nc = (T - CMP_BLOCK) // CMP_STRIDE + 1
    ncp = kc_ref.shape[-2]
    n_slc = ex_ref.shape[0]
    q = jnp.concatenate([q_ref[:, r * hd:(r + 1) * hd] for r in range(R)], axis=0)

    s = _dot_t(q, kc_ref[0, 0, 0]) + bc_ref[0].reshape(RQ, ncp)
    t_row = lax.broadcasted_iota(jnp.int32, (R, QB, ncp), 1).reshape(RQ, ncp)
    c_col = lax.broadcasted_iota(jnp.int32, (RQ, ncp), 1)
    dist = qb * QB + t_row - (CMP_STRIDE * c_col + CMP_BLOCK - 1)
    p = _masked_softmax(s, jnp.where(c_col < nc, dist, -1) >= 0)
    o_cmp = jnp.dot(p.astype(BF16), vc_ref[0, 0, 0], preferred_element_type=F32)

    psum = p[0:QB]
    for r in range(1, R):
        psum = psum + p[r * QB:(r + 1) * QB]
    s_lo = SLC_BLOCK * lax.broadcasted_iota(jnp.int32, (n_slc, ncp), 0)
    c_lo = CMP_STRIDE * lax.broadcasted_iota(jnp.int32, (n_slc, ncp), 1)
    overlap_t = jnp.where(c_lo < s_lo + SLC_BLOCK, jnp.where(c_lo + CMP_BLOCK > s_lo, 1.0, 0.0), 0.0).astype(F32)
    imp_t = _dot_t(overlap_t, psum, HIGHEST)
    cur = _div_pow2(qb * QB + lax.broadcasted_iota(jnp.int32, (1, QB), 1), SLC_BLOCK)
    sel_t = _select_mask_t(imp_t, cur, n_slc).astype(BF16)

    NCHAIN = 2
    hpc = R // NCHAIN
    crow = hpc * QB

    def bias_rows(c, first_blk, n):
        return jnp.concatenate(
            [bt_ref[jnp.maximum(qb - (first_blk + j), 0), c * hpc:(c + 1) * hpc].reshape(crow, QB) for j in range(n)],
            axis=-1)

    def rel(first_key, nk):
        return (qb * QB - first_key + lax.broadcasted_iota(jnp.int32, (QB, nk), 0)
                - lax.broadcasted_iota(jnp.int32, (QB, nk), 1))

    def with_ones(v):
        return jnp.concatenate([v, jnp.ones(v.shape, BF16)], axis=-1)

    nkb = KEY_TILE // QB
    m_ref[...] = jnp.full(m_ref.shape, NEG_INF, F32)
    acc_ref[...] = jnp.zeros(acc_ref.shape, F32)

    def body(kt, carry):
        off = pl.multiple_of(kt * KEY_TILE, KEY_TILE)
        k = ks_ref[0, pl.ds(off, KEY_TILE), :]
        vext = with_ones(vs_ref[0, pl.ds(off, KEY_TILE), :])
        selk = lax.dot_general(sel_t, ex_ref[:, pl.ds(off, KEY_TILE)], (((0,), (0,)), ((), ())),
                               preferred_element_type=F32)
        d = jnp.where(selk > 0.5, rel(off, KEY_TILE), -1)
        neg = jnp.concatenate([jnp.where(d >= 0, 0.0, NEG_INF)] * hpc, axis=0)
        for c in range(NCHAIN):
            rows = slice(c * crow, (c + 1) * crow)
            sc = _dot_t(q[rows], k) + bias_rows(c, kt * nkb, nkb) + neg
            m_old = m_ref[rows]
            m_new = jnp.maximum(m_old, jnp.max(sc, axis=-1, keepdims=True))
            pe = jnp.exp(sc - m_new).astype(BF16)
            acc_ref[rows] = jnp.exp(m_old - m_new) * acc_ref[rows] + jnp.dot(pe, vext, preferred_element_type=F32)
            m_ref[rows] = m_new
        return carry

    lax.fori_loop(0, qb // nkb + 1, body, 0)
    acc = acc_ref[...]
    den = acc[:, hd:]
    o_slc = acc[:, :hd] / jnp.where(den > 0, den, 1.0)

    wk = min(WINDOW + QB, T)
    w0 = jnp.maximum(qb - WINDOW // QB, 0)
    offw = pl.multiple_of(w0 * QB, QB)
    kwin = kw_ref[0, pl.ds(offw, wk), :]
    vext = with_ones(vw_ref[0, pl.ds(offw, wk), :])
    d = rel(offw, wk)
    d = jnp.where(d <= WINDOW, d, -1)
    neg = jnp.concatenate([jnp.where(d >= 0, 0.0, NEG_INF)] * hpc, axis=0)
    o_win = []
    for c in range(NCHAIN):
        sc = _dot_t(q[c * crow:(c + 1) * crow], kwin) + bias_rows(c, w0, wk // QB) + neg
        e = jnp.exp(sc - jnp.max(sc, axis=-1, keepdims=True)).astype(BF16)
        acc = jnp.dot(e, vext, preferred_element_type=F32)
        o_win.append(acc[:, :hd] / acc[:, hd:])
    o_win = jnp.concatenate(o_win, axis=0)

    gt = gates_ref[0]
    for r in range(R):
        sl = slice(r * QB, (r + 1) * QB)
        o = (gt[:, r:r + 1] * o_cmp[sl] + gt[:, R + r:R + r + 1] * o_slc[sl]
             + gt[:, 2 * R + r:2 * R + r + 1] * o_win[sl])
        o_ref[:, r * hd:(r + 1) * hd] = o


def _nsa_prompt(q, cmp_kv, kvb3, gates_g, bias_tiles, bias_cmp, *, B, T):
    G = NSA_KV_HEADS
    R = NSA_REP
    nqb = T // Q_BLOCK
    ncp = T // CMP_STRIDE
    n_slc = -(-T // SLC_BLOCK)
    expand = (jnp.arange(T)[None, :] // SLC_BLOCK == jnp.arange(n_slc)[:, None]).astype(BF16)
    kv_spec = lambda col: pl.BlockSpec((1, T, HEAD_DIM), lambda b, g, i, col=col: (b, 0, col + g))
    return pl.pallas_call(
        functools.partial(_nsa_prompt_kernel, T=T),
        grid=(B, G, nqb),
        in_specs=[pl.BlockSpec((Q_BLOCK, R * HEAD_DIM), lambda b, g, i: (b * nqb + i, g)),
                  pl.BlockSpec((1, 1, 1, ncp, HEAD_DIM), lambda b, g, i: (b, 0, g, 0, 0)),
                  pl.BlockSpec((1, 1, 1, ncp, HEAD_DIM), lambda b, g, i: (b, 1, g, 0, 0)),
                  kv_spec(0), kv_spec(2), kv_spec(4), kv_spec(6),
                  pl.BlockSpec((nqb, R, Q_BLOCK, Q_BLOCK), lambda b, g, i: (0, g, 0, 0)),
                  pl.BlockSpec((1, R, Q_BLOCK, ncp), lambda b, g, i: (i, g, 0, 0)),
                  pl.BlockSpec((1, Q_BLOCK, LANES), lambda b, g, i: (g, b * nqb + i, 0)),
                  pl.BlockSpec((n_slc, T), lambda b, g, i: (0, 0))],
        out_specs=pl.BlockSpec((Q_BLOCK, R * HEAD_DIM), lambda b, g, i: (b * nqb + i, g)),
        out_shape=jax.ShapeDtypeStruct((B * T, G * R * HEAD_DIM), F32),
        scratch_shapes=[pltpu.VMEM((R * Q_BLOCK, 1), F32),
                        pltpu.VMEM((R * Q_BLOCK, 2 * HEAD_DIM), F32)],
        compiler_params=_cp(("parallel", "parallel", "arbitrary")), name="nsa_prompt",
    )(q, cmp_kv, cmp_kv, kvb3, kvb3, kvb3, kvb3, bias_tiles, bias_cmp, gates_g, expand)


def _nsa_sample_kernel(pt_ref, *refs, npages, tq):
    pages = refs[:npages]
    (win_ref, wnew_ref, q_ref, kvn_ref, gates_ref, pe_ref, w1_ref, w2_ref, kg_ref, bc_ref, bs_ref, bw_ref,
     o_ref, wout_ref) = refs[npages:]
    G, R, hd = NSA_KV_HEADS, NSA_REP, HEAD_DIM
    KVR = 4 * G
    WR = 2 * G
    page = pages[0].shape[2] // KVR
    past = npages * page
    L = past + tq
    nc = (L - CMP_BLOCK) // CMP_STRIDE + 1
    ng = past // CMP_STRIDE
    n_slc = -(-L // SLC_BLOCK)
    rows = R * tq
    wb = win_ref.shape[2] // WR
    wout_ref[0, 0, 0:(wb - tq) * WR, :] = win_ref[0, 0, tq * WR:wb * WR, :]
    wout_ref[0, 0, (wb - tq) * WR:wb * WR, :] = wnew_ref[0]
    pad_new = lambda v: jnp.concatenate([v, jnp.zeros((LANES - v.shape[0], v.shape[1]), v.dtype)], axis=0)
    t_of_row = lax.broadcasted_iota(jnp.int32, (R, tq, 1), 1).reshape(rows, 1)
    pos = past + t_of_row
    ra = lax.broadcasted_iota(jnp.int32, (rows, rows), 0)
    rb = lax.broadcasted_iota(jnp.int32, (rows, rows), 1)
    same_t = jnp.where(_mod_pow2(ra, tq) == _mod_pow2(rb, tq), 1.0, 0.0).astype(F32)
    kg = kg_ref[...]
    kvn = kvn_ref[0]

    for g in range(G):
        q = q_ref[0, g]
        gt = gates_ref[0, g]

        def compress(slot):
            col = slot * G + g

            def get_rows(r):
                return jnp.concatenate(
                    [pg[0, 0, pl.ds(r * KVR + col, page // CMP_STRIDE, stride=CMP_STRIDE * KVR), :] for pg in pages],
                    axis=0)

            return _compress_math(get_rows, ng, pe_ref[slot], w1_ref.at[slot], w2_ref[slot])

        kc = _rms(compress(0), kg).astype(BF16)
        vc = compress(1).astype(BF16)

        c_col = lax.broadcasted_iota(jnp.int32, (rows, ng), 1)
        dist = pos - (CMP_STRIDE * c_col + CMP_BLOCK - 1)
        p = _masked_softmax(_dot_t(q, kc) + bc_ref[g], jnp.where(c_col < nc, dist, -1) >= 0)
        o_cmp = jnp.dot(p.astype(BF16), vc, preferred_element_type=F32)

        psum = jnp.dot(same_t, p, precision=HIGHEST, preferred_element_type=F32)
        imp = jnp.dot(psum, _overlap_matrix(ng), precision=HIGHEST, preferred_element_type=F32)
        sel = _select_mask(imp, _div_pow2(pos, SLC_BLOCK), n_slc)
        nk = past + LANES
        selk = jnp.dot(sel.astype(BF16), _expand_matrix(nk), preferred_element_type=F32)

        k_new = pad_new(kvn[:, g * hd:(g + 1) * hd])
        v_new = pad_new(kvn[:, 256 + g * hd:256 + (g + 1) * hd])
        all_rows = lambda slot, new: jnp.concatenate(
            [pg[0, 0, pl.ds(slot * G + g, page, stride=KVR), :].astype(BF16) for pg in pages] + [new], axis=0)
        sc = _dot_t(q, all_rows(2, k_new)) + bs_ref[g]
        j = lax.broadcasted_iota(jnp.int32, (rows, nk), 1)
        p = _masked_softmax(sc, jnp.where(selk > 0.5, pos - j, -1) >= 0).astype(BF16)
        o_slc = jnp.dot(p, all_rows(3, v_new), preferred_element_type=F32)

        kw_new = pad_new(kvn[:, 512 + g * hd:512 + (g + 1) * hd])
        vw_new = pad_new(kvn[:, 768 + g * hd:768 + (g + 1) * hd])
        kw = win_ref[0, 0, pl.ds(g, wb, stride=WR), :].astype(BF16)
        vw = win_ref[0, 0, pl.ds(G + g, wb, stride=WR), :].astype(BF16)
        sc = _dot_t(q, jnp.concatenate([kw, kw_new], axis=0)) + bw_ref[g]
        j = lax.broadcasted_iota(jnp.int32, (rows, wb + LANES), 1)
        d = pos - (past - wb + j)
        p = _masked_softmax(sc, jnp.where(d <= WINDOW, d, -1) >= 0).astype(BF16)
        o_win = jnp.dot(p, jnp.concatenate([vw, vw_new], axis=0), preferred_element_type=F32)

        o_ref[0, g] = gt[:, 0:1] * o_cmp + gt[:, 1:2] * o_slc + gt[:, 2:3] * o_win


def _nsa_sample(slot, page_table, cache_lin, win_lin, win_new, q_s, kvn_s, gates_s, lw, bias_c, bias_s, bias_w, *, tq):
    B, npages = page_table.shape
    G = NSA_KV_HEADS
    rows = NSA_REP * tq
    l = slot.layer
    page_specs = [pl.BlockSpec((1, 1) + cache_lin.shape[2:], lambda b, pt, p=p: (l, pt[b, p], 0, 0))
                  for p in range(npages)]
    full = lambda a: pl.BlockSpec(a.shape, lambda b, pt, n=a.ndim: (0,) * n)
    wts = [lw["cmp_pe"], lw["cmp_w1"], lw["cmp_w2"], lw["k_norm_g"].reshape(1, -1), bias_c, bias_s, bias_w]
    grid_spec = pltpu.PrefetchScalarGridSpec(
        num_scalar_prefetch=1, grid=(B,),
        in_specs=page_specs + [
            pl.BlockSpec((1, 1) + win_lin.shape[2:], lambda b, pt: (l, b, 0, 0)),
            pl.BlockSpec((1,) + win_new.shape[1:], lambda b, pt: (b, 0, 0)),
            pl.BlockSpec((1, G, rows, HEAD_DIM), lambda b, pt: (b, 0, 0, 0)),
            pl.BlockSpec((1, SUBLANES, kvn_s.shape[2]), lambda b, pt: (b, 0, 0)),
            pl.BlockSpec((1, G, rows, LANES), lambda b, pt: (b, 0, 0, 0))] + [full(w) for w in wts]
        + slot.in_specs(),
        out_specs=[pl.BlockSpec((1, G, rows, HEAD_DIM), lambda b, pt: (b, 0, 0, 0)),
                   pl.BlockSpec((1, 1) + win_lin.shape[2:], lambda b, pt: (l, b, 0, 0))])
    n_in = 1 + npages + 5 + len(wts)
    return pl.pallas_call(
        slot.wrap(functools.partial(_nsa_sample_kernel, npages=npages, tq=tq), n_in),
        grid_spec=grid_spec,
        out_shape=[jax.ShapeDtypeStruct((B, G, rows, HEAD_DIM), F32),
                   jax.ShapeDtypeStruct((slot.depth,) + win_lin.shape[1:], F32)],
        input_output_aliases=slot.aliases(n_in, 1),
        compiler_params=_cp(("arbitrary",)), name="nsa_sample",
    )(page_table, *([cache_lin] * npages), win_lin, win_new, q_s, kvn_s, gates_s, *wts, *slot.args())


def _dot3(a, b):
    ah = a.astype(BF16)
    bh = b.astype(BF16)
    al = (a - ah.astype(F32)).astype(BF16)
    bl = (b - bh.astype(F32)).astype(BF16)
    d = functools.partial(jnp.dot, preferred_element_type=F32)
    return d(ah, bh) + (d(al, bh) + d(ah, bl))


def _gdn_chunk_kernel(gq_ref, ggb_ref, n_ref, k2_ref, qe_ref, o0_ref, eg_ref):
    B, C = gq_ref.shape[:2]
    H = GDN_HEADS
    hd = HEAD_DIM
    W = H * hd
    HC = H * C
    ri = lax.broadcasted_iota(jnp.int32, (HC, HC), 0)
    ci = lax.broadcasted_iota(jnp.int32, (HC, HC), 1)
    tri = jnp.where(_div_pow2(ri, C) == _div_pow2(ci, C), ri - ci, -1)
    own_lanes = (_div_pow2(lax.broadcasted_iota(jnp.int32, (HC, W), 0), C)
                 == _div_pow2(lax.broadcasted_iota(jnp.int32, (HC, W), 1), hd))

    for b in range(B):
        stack = lambda ref, c0: jnp.concatenate([ref[b, :, c0 + h * hd:c0 + (h + 1) * hd] for h in range(H)], axis=0)
        q = stack(gq_ref, 0)
        k = stack(gq_ref, W)
        v = stack(gq_ref, 2 * W)
        gc = stack(ggb_ref, 0)
        beta = stack(ggb_ref, W)
        gcol = jnp.concatenate([gc] * (HC // hd), axis=-1)
        grow = gcol.T
        decay = jnp.exp(jnp.where(tri >= 0, gcol - grow, -jnp.inf))
        kb = k * beta
        g2 = _dot_t(jnp.concatenate([kb, q], axis=0).astype(BF16), k.astype(BF16))
        low = jnp.where(tri > 0, g2[:HC] * decay, 0.0)
        qk = jnp.where(tri >= 0, g2[HC:] * decay, 0.0)
        egc = jnp.exp(gc)
        x = jnp.concatenate([v * beta, kb * egc], axis=-1)
        pw = low
        span = 1
        while span < C:
            last = span * 2 >= C
            rhs = x if last else jnp.concatenate([pw, x], axis=-1)
            r = _dot3(pw, rhs)
            px = r if last else r[:, HC:]
            x = x - px if span == 1 else x + px
            if not last:
                pw = r[:, :HC]
            span *= 2
        uw = x.astype(BF16)
        g_last = jnp.concatenate([jnp.broadcast_to(gc[(h + 1) * C - 1:(h + 1) * C], (C, hd)) for h in range(H)],
                                 axis=0)
        kd = k * jnp.exp(g_last - gc)
        kd_bd = jnp.where(own_lanes, jnp.concatenate([kd] * H, axis=-1), 0.0).astype(BF16)
        nk = lax.dot_general(kd_bd, uw, (((0,), (0,)), ((), ())), preferred_element_type=F32)
        ow = jnp.dot(qk.astype(BF16), uw, preferred_element_type=F32)
        qe = q * egc - ow[:, hd:]
        eg = jnp.exp(g_last)
        for h in range(H):
            n_ref[b, h, 0] = nk[h * hd:(h + 1) * hd, :hd]
            k2_ref[b, h, 0] = nk[h * hd:(h + 1) * hd, hd:].astype(BF16)
            o0_ref[b, h, 0] = ow[h * C:(h + 1) * C, :hd]
            qe_ref[b, h, 0] = qe[h * C:(h + 1) * C].astype(BF16)
            eg_ref[b, h, 0] = eg[h * C:h * C + SUBLANES]


def _gdn_scan_kernel(n_ref, k2_ref, qe_ref, o0_ref, eg_ref, gz_ref, s0_ref, ng_ref, y_ref, so_ref, s_ref):
    n = pl.program_id(0)
    B = gz_ref.shape[0]
    hd = HEAD_DIM

    @pl.when(n == 0)
    def _():
        s_ref[...] = s0_ref[...]

    ng = ng_ref[...]
    for b in range(B):
        for h in range(GDN_HEADS):
            S = s_ref[b, h]
            s16 = S.astype(BF16)
            o = jnp.dot(qe_ref[b, h, 0], s16, preferred_element_type=F32) + o0_ref[b, h, 0]
            s_ref[b, h] = (eg_ref[b, h, 0][0:1] * S + n_ref[b, h, 0]
                           - jnp.dot(k2_ref[b, h, 0], s16, preferred_element_type=F32))
            z = gz_ref[b, :, h * hd:(h + 1) * hd]
            y_ref[b, :, h * hd:(h + 1) * hd] = (_rms(o, ng) * (z * _sigmoid(z))).astype(y_ref.dtype)

    @pl.when(n == pl.num_programs(0) - 1)
    def _():
        so_ref[...] = s_ref[...]


def _gdn_prompt(gq3, ggb3, z3, s0, norm_g):
    B, T, _ = gq3.shape
    C = GDN_CHUNK
    H = GDN_HEADS
    hd = HEAD_DIM
    W = H * hd
    nchunk = T // C
    item = lambda rows: pl.BlockSpec((B, H, 1, rows, hd), lambda n: (0, 0, n, 0, 0))
    shape = lambda rows, dt: jax.ShapeDtypeStruct((B, H, nchunk, rows, hd), dt)
    nn, k2, qe, o0, eg = pl.pallas_call(
        _gdn_chunk_kernel,
        grid=(nchunk,),
        in_specs=[pl.BlockSpec((B, C, 3 * W), lambda n: (0, n, 0)),
                  pl.BlockSpec((B, C, 2 * W), lambda n: (0, n, 0))],
        out_specs=[item(hd), item(hd), item(C), item(C), item(SUBLANES)],
        out_shape=[shape(hd, F32), shape(hd, BF16), shape(C, BF16), shape(C, F32), shape(SUBLANES, F32)],
        compiler_params=_cp(("parallel",)), name="gdn_chunk",
    )(gq3, ggb3)
    return pl.pallas_call(
        _gdn_scan_kernel,
        grid=(nchunk,),
        in_specs=[item(hd), item(hd), item(C), item(C), item(SUBLANES),
                  pl.BlockSpec((B, C, W), lambda n: (0, n, C_GZ // W)),
                  pl.BlockSpec(s0.shape, lambda n: (0, 0, 0, 0)),
                  pl.BlockSpec((1, hd), lambda n: (0, 0))],
        out_specs=[pl.BlockSpec((B, C, W), lambda n: (0, n, 0)),
                   pl.BlockSpec(s0.shape, lambda n: (0, 0, 0, 0))],
        out_shape=[jax.ShapeDtypeStruct((B, T, W), BF16), jax.ShapeDtypeStruct(s0.shape, F32)],
        scratch_shapes=[pltpu.VMEM(s0.shape, F32)],
        compiler_params=_cp(("arbitrary",)), name="gdn_scan",
    )(nn, k2, qe, o0, eg, z3, s0, norm_g.reshape(1, -1))


def _gdn_sample_kernel(kq_ref, v_ref, gb_ref, gz_ref, s0_ref, ng_ref, y_ref, so_ref, *, tq):
    bt = kq_ref.shape[0]
    hd = HEAD_DIM
    ri = lax.broadcasted_iota(jnp.int32, (hd, hd), 0)
    ci = lax.broadcasted_iota(jnp.int32, (hd, hd), 1)
    eye = jnp.where(ri == ci, 1.0, 0.0).astype(F32)
    ng = ng_ref[...]

    def body(bi, carry):
        for h in range(GDN_HEADS):
            kq = kq_ref[bi, h]
            cols = _dot_t(eye, kq, HIGHEST)
            gb = gb_ref[bi, h]
            v = v_ref[bi, h]
            S = s0_ref[bi, h]
            outs = []
            for t in range(tq):
                a = jnp.exp(gb[t:t + 1])
                kc = cols[:, t:t + 1]
                qc = cols[:, tq + t:tq + t + 1]
                Sa = S * a
                stk = jnp.sum(Sa * kc, axis=0, keepdims=True)
                vn = gb[tq + t:tq + t + 1] * (v[t:t + 1] - stk)
                S = Sa + kc * vn
                outs.append(jnp.sum(S * qc, axis=0, keepdims=True))
            so_ref[0, bi, h] = S
            o = jnp.concatenate(outs + [jnp.zeros((SUBLANES - tq, hd), F32)], axis=0)
            z = gz_ref[bi, h]
            y_ref[bi, h] = _rms(o, ng) * (z * _sigmoid(z))
        return carry

    lax.fori_loop(0, bt, body, 0)


def _gdn_sample(slot, kq_s, v_s, gb_s, gz_s, s0, norm_g, *, tq):
    B = kq_s.shape[0]
    bt = 8
    layer = slot.layer
    blk = lambda a: pl.BlockSpec((bt,) + a.shape[1:], lambda i: (i, 0, 0, 0))
    in_specs = [blk(kq_s), blk(v_s), blk(gb_s), blk(gz_s), blk(s0), pl.BlockSpec((1, HEAD_DIM), lambda i: (0, 0))]
    return pl.pallas_call(
        slot.wrap(functools.partial(_gdn_sample_kernel, tq=tq), len(in_specs)),
        grid=(B // bt,),
        in_specs=in_specs + slot.in_specs(),
        out_specs=[blk(v_s), pl.BlockSpec((1, bt) + s0.shape[1:], lambda i: (layer, i, 0, 0, 0))],
        out_shape=[jax.ShapeDtypeStruct(v_s.shape, F32), jax.ShapeDtypeStruct((slot.depth,) + s0.shape, F32)],
        input_output_aliases=slot.aliases(len(in_specs), 1),
        compiler_params=_cp(("parallel",)), name="gdn_sample",
    )(kq_s, v_s, gb_s, gz_s, s0, norm_g.reshape(1, -1), *slot.args())


def _mix_out_kernel(ya_ref, yb_ref, yc_ref, x_ref, onb_ref, w_ref, o_ref, mix_ref):
    @pl.when(pl.program_id(1) == 0)
    def _():
        mix_ref[:, 0:512] = ya_ref[...]
        mix_ref[:, 512:1536] = _rms(yb_ref[...], onb_ref[...]).astype(BF16)
        mix_ref[:, 1536:2048] = yc_ref[...]

    o_ref[...] = x_ref[...] + jnp.dot(mix_ref[...], w_ref[...], preferred_element_type=F32)


def _mix_out(ya, yb, yc, x, onb, w, *, tn=512):
    M, N = x.shape
    tm = _row_tile(M)
    K = w.shape[0]
    return pl.pallas_call(
        _mix_out_kernel,
        grid=(M // tm, N // tn),
        in_specs=[pl.BlockSpec((tm, 512), lambda i, j: (i, 0)),
                  pl.BlockSpec((tm, 1024), lambda i, j: (i, 0)),
                  pl.BlockSpec((tm, 512), lambda i, j: (i, 0)),
                  pl.BlockSpec((tm, tn), lambda i, j: (i, j)),
                  pl.BlockSpec((1, 1024), lambda i, j: (0, 0)),
                  w.spec((K, tn), lambda i, j: (0, j))],
        out_specs=pl.BlockSpec((tm, tn), lambda i, j: (i, j)),
        out_shape=jax.ShapeDtypeStruct((M, N), F32),
        scratch_shapes=[pltpu.VMEM((tm, K), BF16)],
        compiler_params=_cp(("parallel", "arbitrary")), name="mix_out",
    )(ya, yb, yc, x, onb.reshape(1, -1), w.array)


def _matmul_res_kernel(a_ref, w_ref, r_ref, o_ref):
    o_ref[...] = r_ref[...] + jnp.dot(a_ref[...], w_ref[...], preferred_element_type=F32)


def _matmul_res(a, w, res, *, tn=512):
    M, K = a.shape
    N = w.shape[1]
    tm = _row_tile(M)
    return pl.pallas_call(
        _matmul_res_kernel,
        grid=(M // tm, N // tn),
        in_specs=[pl.BlockSpec((tm, K), lambda i, j: (i, 0)),
                  w.spec((K, tn), lambda i, j: (0, j)),
                  pl.BlockSpec((tm, tn), lambda i, j: (i, j))],
        out_specs=pl.BlockSpec((tm, tn), lambda i, j: (i, j)),
        out_shape=jax.ShapeDtypeStruct((M, N), F32),
        compiler_params=_cp(("parallel", "parallel")), name="matmul_res",
    )(a, w.array, res)


def _layer_weights(l, p):
    w_in = p["w_in"][l]
    w_main = jnp.concatenate([w_in[:, :4096], w_in[:, 4120:5656], w_in[:, 5664:6176]], axis=1).astype(BF16)
    w_small = jnp.concatenate([w_in[:, 4096:4120], w_in[:, 5656:5664],
                               jnp.zeros((w_in.shape[0], LANES - 32), w_in.dtype)], axis=1).astype(BF16)
    lane_vec = lambda v: jnp.zeros((1, LANES), F32).at[0, S_GA:S_GA + GDN_HEADS].set(v)
    return {
        "norm_mix_g": p["norm_mix_g"][l], "w_main": w_main, "w_small": w_small,
        "conv_a_w": p["conv_a_w"][l], "q_norm_g": p["q_norm_g"][l], "k_norm_g": p["k_norm_g"][l],
        "cmp_pe": p["cmp_pe"][l],
        "cmp_w1": p["cmp_w1"][l].reshape(2, 2, CMP_BLOCK // 2 * HEAD_DIM, CMP_HIDDEN).astype(BF16),
        "cmp_w2": p["cmp_w2"][l].astype(BF16),
        "gdn_conv_w": p["gdn_conv_w"][l], "alog_v": lane_vec(p["gdn_a_log"][l]), "dtb_v": lane_vec(p["gdn_dt_bias"][l]),
        "gdn_norm_g": p["gdn_norm_g"][l], "out_norm_a": p["out_norm_a"][l], "out_norm_b": p["out_norm_b"][l],
        "w_out": _LayerOf(p["w_out_bf16"], l), "norm_ffn_g": p["norm_ffn_g"][l],
        "ffn_up": _LayerOf(p["ffn_up_bf16"], l), "ffn_conv_w": p["ffn_conv_w"][l],
        "ffn_down": _LayerOf(p["ffn_down_bf16"], l),
    }


HALO_ROWS = 16


def _ffn_up_kernel(x_ref, xp_ref, g_ref, wa_ref, wv_ref, ha_ref, hv_ref, ca_ref, cv_ref,
                   o_ref, hoa_ref, hov_ref, xn_ref, *, tstride, tiles_per_seq):
    tn = wa_ref.shape[1]
    prompt = tstride == 1

    @pl.when(pl.program_id(1) == 0)
    def _():
        g = g_ref[...]
        if prompt:
            xn_ref[0:HALO_ROWS] = _rms(xp_ref[...], g).astype(BF16)
            xn_ref[HALO_ROWS:] = _rms(x_ref[...], g).astype(BF16)
        else:
            xn_ref[...] = _rms(x_ref[...], g).astype(BF16)

    xn = xn_ref[...]

    def part(w_ref, hist_ref, cw_ref, ho_ref):
        up = jnp.dot(xn, w_ref[...], preferred_element_type=F32)
        if prompt:
            start = (pl.program_id(0) % tiles_per_seq) == 0
            halo = jnp.where(start, hist_ref[0], up[HALO_ROWS - SUBLANES:HALO_ROWS])
            x = up[HALO_ROWS:]
            ho_ref[0] = x[x.shape[0] - SUBLANES:]
        else:
            halo = _ld(hist_ref, 0, tn)
            x = up
            _st(ho_ref, 0, tn, x[x.shape[0] - halo.shape[0]:])
        w = cw_ref[...]
        return w[0:1] * _shift(x, halo, 2, tstride) + w[1:2] * _shift(x, halo, 1, tstride) + w[2:3] * x

    a = part(wa_ref, ha_ref, ca_ref, hoa_ref)
    v = part(wv_ref, hv_ref, cv_ref, hov_ref)
    o_ref[...] = (a * _sigmoid(a) * v).astype(o_ref.dtype)


def _ffn_up(h, g, w_up, conv_w, hist, *, sample, nseq, seq_len, tn=512):
    M, K = h.shape
    tm = _row_tile(seq_len)
    F = w_up.shape[1] // 2
    nj = F // tn
    common = [pl.BlockSpec((1, K), lambda i, j: (0, 0)),
              w_up.spec((K, tn), lambda i, j: (0, j)),
              w_up.spec((K, tn), lambda i, j: (0, j + nj))]
    cw = [pl.BlockSpec((3, tn), lambda i, j: (0, j)), pl.BlockSpec((3, tn), lambda i, j: (0, j + nj))]
    if not sample:
        tps = seq_len // tm
        hspec = lambda off: pl.BlockSpec((1, SUBLANES, tn), lambda i, j, off=off: (i // tps, 0, j + off))
        hout = pl.BlockSpec((1, SUBLANES, tn), lambda i, j: (i, 0, j))
        act, tail_a, tail_v = pl.pallas_call(
            functools.partial(_ffn_up_kernel, tstride=1, tiles_per_seq=tps),
            grid=(M // tm, nj),
            in_specs=[pl.BlockSpec((tm, K), lambda i, j: (i, 0)),
                      pl.BlockSpec((HALO_ROWS, K), lambda i, j: (jnp.maximum(i * (tm // HALO_ROWS) - 1, 0), 0))]
            + common + [hspec(0), hspec(nj)] + cw,
            out_specs=[pl.BlockSpec((tm, tn), lambda i, j: (i, j)), hout, hout],
            out_shape=[jax.ShapeDtypeStruct((M, F), BF16)] + [jax.ShapeDtypeStruct((M // tm, SUBLANES, F), F32)] * 2,
            scratch_shapes=[pltpu.VMEM((tm + HALO_ROWS, K), BF16)],
            compiler_params=_cp(("arbitrary", "arbitrary")), name="ffn_up_prompt",
        )(h, h, g.reshape(1, K), w_up.array, w_up.array, hist, hist, conv_w, conv_w)
        return act, tail_a[tps - 1::tps], tail_v[tps - 1::tps]
    hk = hist.shape[0]
    hspec = lambda off: pl.BlockSpec((hk, nseq, tn), lambda i, j, off=off: (0, 0, j + off))
    hout = pl.BlockSpec((hk, nseq, tn), lambda i, j: (0, 0, j))
    return pl.pallas_call(
        functools.partial(_ffn_up_kernel, tstride=nseq, tiles_per_seq=1),
        grid=(1, nj),
        in_specs=[pl.BlockSpec((M, K), lambda i, j: (0, 0)), pl.BlockSpec((HALO_ROWS, K), lambda i, j: (0, 0))]
        + common + [hspec(0), hspec(nj)] + cw,
        out_specs=[pl.BlockSpec((M, tn), lambda i, j: (0, j)), hout, hout],
        out_shape=[jax.ShapeDtypeStruct((M, F), BF16)] + [jax.ShapeDtypeStruct((hk, nseq, F), F32)] * 2,
        scratch_shapes=[pltpu.VMEM((M, K), BF16)],
        compiler_params=_cp(("arbitrary", "arbitrary")), name="ffn_up_sample",
    )(h, h, g.reshape(1, K), w_up.array, w_up.array, hist, hist, conv_w, conv_w)


def _dense_tail(x2, ya, yb, yc, lw, hist_ffn, *, sample, nseq, seq_len):
    h = _mix_out(ya, yb, yc, x2, lw["out_norm_b"], lw["w_out"])
    act, hist_a, hist_v = _ffn_up(h, lw["norm_ffn_g"], lw["ffn_up"], lw["ffn_conv_w"], hist_ffn,
                                  sample=sample, nseq=nseq, seq_len=seq_len)
    return _matmul_res(act, lw["ffn_down"], h), jnp.concatenate([hist_a, hist_v], axis=-1)


def _prompt_layer(x2, lw, bias_tiles, bias_cmp, rows_slot, *, B, T):
    M = B * T
    G, R = NSA_KV_HEADS, NSA_REP
    z, zs = _norm_matmul(x2, lw["norm_mix_g"], lw["w_main"], lw["w_small"])
    hu0 = jnp.zeros((B, SUBLANES, 512), F32)
    hq0 = jnp.zeros((B, SUBLANES, 1536), F32)
    ya, q, rows, win, kvb, gates, gq, ggb, hu, rows_lin = _prep(z, zs, hu0, hq0, lw, sample=False, nseq=B, seq_len=T,
                                                                slot=rows_slot)
    cmp_kv = _compress_prompt(rows.reshape(B, T, 1024), lw)
    gates_g = gates[:, :3 * G * R].reshape(M, 3, G, R).transpose(2, 0, 1, 3).reshape(G, M, 3 * R)
    gates_g = jnp.pad(gates_g, ((0, 0), (0, 0), (0, LANES - 3 * R)))
    yb = _nsa_prompt(q, cmp_kv, kvb.reshape(B, T, 1024), gates_g, bias_tiles, bias_cmp, B=B, T=T)
    s0 = jnp.zeros((B, GDN_HEADS, HEAD_DIM, HEAD_DIM), F32)
    yc, s_new = _gdn_prompt(gq.reshape(B, T, 1536), ggb.reshape(B, T, 1024), z.reshape(B, T, C_END), s0,
                            lw["gdn_norm_g"])
    hf0 = jnp.zeros((B, SUBLANES, lw["ffn_conv_w"].shape[1]), F32)
    out, up = _dense_tail(x2, ya, yb, yc.reshape(M, 512), lw, hf0, sample=False, nseq=B, seq_len=T)
    return out, rows_lin, win, hu, z, s_new, up


def _sample_layer(x2, lw, st, bias_c, bias_s, bias_w, win_slot, gdn_slot, *, B, T):
    M = B * T
    G, R, H, hd = NSA_KV_HEADS, NSA_REP, GDN_HEADS, HEAD_DIM
    z, zs = _norm_matmul(x2, lw["norm_mix_g"], lw["w_main"], lw["w_small"])
    hu0 = jnp.swapaxes(st["conv_a"], 0, 1)
    hq0 = jnp.swapaxes(st["gdn_conv"], 0, 1)
    ya, q, rows, win, kvb, gates, gq, ggb, hu = _prep(z, zs, hu0, hq0, lw, sample=True, nseq=B, seq_len=T)
    tb = lambda a: a.reshape(T, B, -1)
    q_s = tb(q).reshape(T, B, G, R, hd).transpose(1, 2, 3, 0, 4).reshape(B, G, R * T, hd)
    gt = tb(gates)[:, :, :3 * G * R].reshape(T, B, 3, G, R).transpose(1, 3, 4, 0, 2).reshape(B, G, R * T, 3)
    gates_s = jnp.pad(gt, ((0, 0), (0, 0), (0, 0), (0, LANES - 3)))
    kvn_s = jnp.pad(jnp.swapaxes(tb(kvb), 0, 1), ((0, 0), (0, SUBLANES - T), (0, 0)))
    win_new = jnp.swapaxes(tb(win), 0, 1).reshape(B, T * 2 * G, hd)
    yb_s, win = _nsa_sample(win_slot, st["page_table"], st["cache_lin"], st["win_lin"], win_new, q_s, kvn_s, gates_s, lw,
                            bias_c, bias_s, bias_w, tq=T)
    yb = yb_s.reshape(B, G, R, T, hd).transpose(3, 0, 1, 2, 4).reshape(M, G * R * hd)
    gq4 = tb(gq).reshape(T, B, 3, H, hd)
    bh = lambda a: a.transpose(1, 2, 0, 3)
    padt = lambda a: jnp.pad(a, ((0, 0), (0, 0), (0, SUBLANES - T), (0, 0)))
    kq_s = jnp.concatenate([bh(gq4[:, :, 1]), bh(gq4[:, :, 0])], axis=2)
    v_s = padt(bh(gq4[:, :, 2]))
    ggb4 = tb(ggb).reshape(T, B, 2, H, hd)
    gb_s = jnp.concatenate([bh(ggb4[:, :, 0]), bh(ggb4[:, :, 1])], axis=2)
    gz_s = padt(bh(tb(z)[:, :, C_GZ:C_END].reshape(T, B, H, hd)))
    yc_s, s_new = _gdn_sample(gdn_slot, kq_s, v_s, gb_s, gz_s, st["gdn"], lw["gdn_norm_g"], tq=T)
    yc = yc_s[:, :, :T].transpose(2, 0, 1, 3).reshape(M, H * hd).astype(BF16)
    hf0 = jnp.swapaxes(st["ffn_conv"], 0, 1)
    out, up = _dense_tail(x2, ya, yb, yc, lw, hf0, sample=True, nseq=B, seq_len=T)
    return out, rows, win, hu, z, s_new, up


def kernel(x_prompt, x_sample, cache_kv, cache_win, state_conv_a, state_gdn_conv, state_gdn, state_ffn_conv,
           page_table, rel_bias, norm_mix_g, w_in, conv_a_w, q_norm_g, k_norm_g, cmp_pe, cmp_w1, cmp_w2,
           gdn_conv_w, gdn_a_log, gdn_dt_bias, gdn_norm_g, out_norm_a, out_norm_b, w_out, norm_ffn_g,
           ffn_up, ffn_conv_w, ffn_down):
    params = dict(norm_mix_g=norm_mix_g, w_in=w_in, conv_a_w=conv_a_w, q_norm_g=q_norm_g, k_norm_g=k_norm_g,
                  cmp_pe=cmp_pe, cmp_w1=cmp_w1, cmp_w2=cmp_w2, gdn_conv_w=gdn_conv_w, gdn_a_log=gdn_a_log,
                  gdn_dt_bias=gdn_dt_bias, gdn_norm_g=gdn_norm_g, out_norm_a=out_norm_a, out_norm_b=out_norm_b,
                  w_out=w_out, norm_ffn_g=norm_ffn_g, ffn_up=ffn_up, ffn_conv_w=ffn_conv_w, ffn_down=ffn_down,
                  w_out_bf16=w_out.astype(BF16), ffn_up_bf16=ffn_up.astype(BF16), ffn_down_bf16=ffn_down.astype(BF16))
    depth = w_in.shape[0]
    Bp, T, D = x_prompt.shape
    Bs, Ts, _ = x_sample.shape
    G, R, hd = NSA_KV_HEADS, NSA_REP, HEAD_DIM
    n_pool, page = cache_kv.shape[1], cache_kv.shape[2]
    npages = page_table.shape[1]
    past = npages * page
    wb = cache_win.shape[2]
    L = past + Ts
    assert ((L - CMP_BLOCK) // CMP_STRIDE) * CMP_STRIDE + CMP_BLOCK <= past, "compressed blocks must lie in the cache"
    assert T % KEY_TILE == 0 and T >= WINDOW + Q_BLOCK and 3 <= Ts <= SUBLANES // 2

    thr = _bucket_thresholds()
    rb_flat = rel_bias.reshape(-1)
    nqb = T // Q_BLOCK
    bias_tiles = _bias_table(thr, rb_flat, nqb, Q_BLOCK, Q_BLOCK, a0=0, an=Q_BLOCK, qs=1, ks=1)
    bias_cmp = _bias_table(thr, rb_flat, nqb, Q_BLOCK, T // CMP_STRIDE,
                           a0=-(CMP_BLOCK - 1), an=Q_BLOCK, qs=1, ks=CMP_STRIDE)

    def sample_bias(ncols, a0, ks):
        t = _bias_table(thr, rb_flat, 1, SUBLANES, ncols, a0=a0, an=0, qs=1, ks=ks)[0]
        return t[:, :Ts].reshape(G, R * Ts, ncols)

    bias_c = sample_bias(past // CMP_STRIDE, past - (CMP_BLOCK - 1), CMP_STRIDE)
    bias_s = sample_bias(past + LANES, past, 1)
    bias_w = sample_bias(wb + LANES, wb, 1)

    cache_lin = cache_kv.reshape(depth, n_pool, page * 4 * G, hd)
    win_lin = cache_win.reshape(depth, Bs, wb * 2 * G, hd)
    assert wb == WINDOW, "the new window buffer is the old one shifted by the new rows"

    xp = x_prompt.reshape(Bp * T, D)
    xs = jnp.swapaxes(x_sample, 0, 1).reshape(Ts * Bs, D)
    outs_p, outs_s = [], []
    rows_p_all = win_s_all = gdn_s_all = None
    for l in range(depth):
        lw = _layer_weights(l, params)
        xp, rows_p_all, win, hu, z, s_new, up = _prompt_layer(xp, lw, bias_tiles, bias_cmp,
                                                              _LayerSlot(l, depth, rows_p_all), B=Bp, T=T)
        wl = min(WINDOW, T)
        outs_p.append((
            win.reshape(Bp, T, 2, G, hd)[:, T - wl:],
            hu[:, SUBLANES - 2:],
            z.reshape(Bp, T, C_END)[:, T - 3:, C_GQKV:C_GZ],
            s_new,
            up[:, SUBLANES - 2:]))
        st = dict(page_table=page_table, cache_lin=cache_lin, win_lin=win_lin, conv_a=state_conv_a[l],
                  gdn_conv=state_gdn_conv[l], gdn=state_gdn[l], ffn_conv=state_ffn_conv[l])
        xs, rows, win_s_all, hu, z, gdn_s_all, up = _sample_layer(
            xs, lw, st, bias_c, bias_s, bias_w, _LayerSlot(l, depth, win_s_all), _LayerSlot(l, depth, gdn_s_all),
            B=Bs, T=Ts)
        tb = lambda a: jnp.swapaxes(a.reshape(Ts, Bs, -1), 0, 1)
        outs_s.append((
            tb(rows).reshape(Bs, Ts, 4, G, hd),
            jnp.swapaxes(hu, 0, 1),
            tb(z)[:, Ts - 3:, C_GQKV:C_GZ],
            jnp.swapaxes(up, 0, 1)))
    y_p = xp.reshape(Bp, T, D)
    y_s = jnp.swapaxes(xs.reshape(Ts, Bs, D), 0, 1)
    stack = lambda outs, i: jnp.stack([o[i] for o in outs])
    return (y_p, y_s,
            rows_p_all.reshape(depth, Bp, T, 4, G, hd), stack(outs_s, 0),
            stack(outs_p, 0), win_s_all.reshape(depth, Bs, wb, 2, G, hd),
            stack(outs_p, 1), stack(outs_s, 1), stack(outs_p, 2), stack(outs_s, 2),
            stack(outs_p, 3), gdn_s_all, stack(outs_p, 4), stack(outs_s, 3))
```

```python
import functools
import math

import jax
import jax.numpy as jnp
from jax import lax
from jax.experimental import pallas as pl
from jax.experimental.pallas import tpu as pltpu

F32 = jnp.float32
BF16 = jnp.bfloat16
HIGHEST = lax.Precision.HIGHEST

SUBLANES = 8
LANES = 128
VMEM_LIMIT_BYTES = 52 * 1024 * 1024

HEAD_DIM = 128
NSA_KV_HEADS = 2
NSA_REP = 4
NSA_HEADS = NSA_KV_HEADS * NSA_REP
GDN_HEADS = 4
CMP_BLOCK = 32
CMP_STRIDE = 16
CMP_HIDDEN = 256
SLC_BLOCK = 64
N_SELECT = 16
WINDOW = 512
Q_BLOCK = 128
GDN_CHUNK = 64
REL_BUCKETS = 32
REL_MAX_DIST = 1024
EPS = 1e-6
NEG_INF = -1e30
FORCE_SCORE = 1e4

C_AB, C_AC, C_AH, C_NQ, C_NKV, C_GQKV, C_GZ, C_END = 0, 512, 1024, 1536, 2560, 4096, 5632, 6144
S_GATE, S_GA, S_GB = 0, 24, 28


def _cp(sem):
    return pltpu.CompilerParams(dimension_semantics=sem, vmem_limit_bytes=VMEM_LIMIT_BYTES)


class _LayerOf:
    def __init__(self, stacked, layer):
        self.array, self.layer, self.shape = stacked, layer, stacked.shape[1:]

    def spec(self, block, index_map):
        layer = self.layer
        return pl.BlockSpec((None,) + block, lambda *ij: (layer,) + tuple(index_map(*ij)))


class _LayerSlot:
    def __init__(self, layer, depth, prev):
        self.layer, self.depth, self.prev = layer, depth, prev

    def in_specs(self):
        return [] if self.prev is None else [pl.BlockSpec(memory_space=pl.ANY)]

    def args(self):
        return [] if self.prev is None else [self.prev]

    def aliases(self, n_inputs_before, out_index):
        return {} if self.prev is None else {n_inputs_before: out_index}

    def wrap(self, kernel, n_refs_before):
        if self.prev is None:
            return kernel

        def wrapped(*refs):
            return kernel(*refs[:n_refs_before], *refs[n_refs_before + 1:])
        return wrapped


def _div_pow2(x, d):
    assert d & (d - 1) == 0
    return jnp.right_shift(x, d.bit_length() - 1)


def _mod_pow2(x, d):
    assert d & (d - 1) == 0
    return jnp.bitwise_and(x, d - 1)


def _sigmoid(x):
    return 1.0 / (1.0 + jnp.exp(-x))


def _rms(x, g):
    return x * lax.rsqrt(jnp.mean(x * x, axis=-1, keepdims=True) + EPS) * g


def _l2n(x):
    return x * lax.rsqrt(jnp.sum(x * x, axis=-1, keepdims=True) + EPS)


def _dot_t(a, b, precision=None):
    return lax.dot_general(a, b, (((1,), (1,)), ((), ())), precision=precision, preferred_element_type=F32)


def _ld(ref, c0, c1):
    if len(ref.shape) == 3:
        v = ref[:, :, c0:c1]
        return v.reshape(v.shape[0] * v.shape[1], v.shape[2])
    return ref[:, c0:c1]


def _st(ref, c0, c1, val):
    if len(ref.shape) == 3:
        ref[:, :, c0:c1] = val.reshape(ref.shape[0], ref.shape[1], c1 - c0).astype(ref.dtype)
    else:
        ref[:, c0:c1] = val.astype(ref.dtype)


def _shift(x, halo, s, tstride):
    n = s * tstride
    rows = x.shape[0]
    if tstride % SUBLANES == 0:
        hr = halo.shape[0]
        return jnp.concatenate([halo[hr - n:], x[:rows - n]], axis=0)
    xs = pltpu.roll(x, n, 0)
    hs = pltpu.roll(halo, n, 0)
    rid = lax.broadcasted_iota(jnp.int32, (SUBLANES, x.shape[1]), 0)
    head = jnp.where(rid < n, hs, xs[:SUBLANES])
    return jnp.concatenate([head, xs[SUBLANES:]], axis=0)


def _norm_matmul_kernel(x_ref, g_ref, w_ref, ws_ref, o_ref, os_ref, xn_ref):
    @pl.when(pl.program_id(1) == 0)
    def _():
        xn = _rms(x_ref[...], g_ref[...]).astype(BF16)
        xn_ref[...] = xn
        os_ref[...] = jnp.dot(xn, ws_ref[...], preferred_element_type=F32)

    o_ref[...] = jnp.dot(xn_ref[...], w_ref[...], preferred_element_type=F32)


def _row_tile(rows):
    for tm in (1024, 512):
        if rows % tm == 0:
            return tm
    return rows


def _norm_matmul(x, g, w, w_small, *, tn=1024):
    M, K = x.shape
    N = w.shape[1]
    tm = _row_tile(M)
    return pl.pallas_call(
        _norm_matmul_kernel,
        grid=(M // tm, N // tn),
        in_specs=[pl.BlockSpec((tm, K), lambda i, j: (i, 0)),
                  pl.BlockSpec((1, K), lambda i, j: (0, 0)),
                  pl.BlockSpec((K, tn), lambda i, j: (0, j)),
                  pl.BlockSpec((K, LANES), lambda i, j: (0, 0))],
        out_specs=[pl.BlockSpec((tm, tn), lambda i, j: (i, j)), pl.BlockSpec((tm, LANES), lambda i, j: (i, 0))],
        out_shape=[jax.ShapeDtypeStruct((M, N), F32), jax.ShapeDtypeStruct((M, LANES), F32)],
        scratch_shapes=[pltpu.VMEM((tm, K), BF16)],
        compiler_params=_cp(("parallel", "arbitrary")), name="norm_matmul",
    )(x, g.reshape(1, K), w, w_small)


def _prep_kernel(z_ref, zp_ref, zs_ref, hu_ref, hq_ref, caw_ref, gcw_ref, qg_ref, kg_ref, ona_ref,
                 alog_ref, dtb_ref,
                 ya_ref, q_ref, rows_ref, win_ref, kvb_ref, gates_ref, gq_ref, ggb_ref, huo_ref, *lin_refs,
                 tstride, tiles_per_seq):
    hd = HEAD_DIM

    def put_rows(grp, val):
        _st(rows_ref, grp * hd, (grp + 1) * hd, val)
        if lin_refs:
            lin_refs[0][0, pl.ds(grp, val.shape[0], stride=4 * NSA_KV_HEADS), :] = val
    u = _ld(z_ref, C_AC, C_AH) * _ld(z_ref, C_AH, C_NQ)
    if tstride == 1:
        start = (pl.program_id(0) % tiles_per_seq) == 0
        halo_u = jnp.where(start, hu_ref[0], zp_ref[:, C_AC:C_AH] * zp_ref[:, C_AH:C_NQ])
        halo_q = jnp.where(start, hq_ref[0], zp_ref[:, C_GQKV:C_GZ])
    else:
        halo_u = _ld(hu_ref, 0, 512)
        halo_q = _ld(hq_ref, 0, 1536)
    caw = caw_ref[...]
    conv = caw[0:1] * _shift(u, halo_u, 2, tstride) + caw[1:2] * _shift(u, halo_u, 1, tstride) + caw[2:3] * u
    y = _ld(z_ref, C_AB, C_AC) * conv
    _st(ya_ref, 0, 512, _rms(y, ona_ref[...]))
    hu_rows = huo_ref.shape[0] * huo_ref.shape[1] if tstride != 1 else SUBLANES
    if tstride == 1:
        huo_ref[0] = u[u.shape[0] - hu_rows:]
    else:
        _st(huo_ref, 0, 512, u[u.shape[0] - hu_rows:])

    qg = qg_ref[...]
    kg = kg_ref[...]
    for h in range(NSA_HEADS):
        c0 = C_NQ + h * hd
        _st(q_ref, h * hd, (h + 1) * hd, _rms(_ld(z_ref, c0, c0 + hd), qg) * (hd ** -0.5))
    for grp in range(4):
        put_rows(grp, _ld(z_ref, C_NKV + grp * hd, C_NKV + (grp + 1) * hd))
    for g in range(NSA_KV_HEADS):
        c0 = C_NKV + 512 + g * hd
        kn = _rms(_ld(z_ref, c0, c0 + hd), kg)
        put_rows(4 + g, kn)
        _st(kvb_ref, g * hd, (g + 1) * hd, kn)
    for g in range(NSA_KV_HEADS):
        c0 = C_NKV + 768 + g * hd
        vs = _ld(z_ref, c0, c0 + hd)
        put_rows(6 + g, vs)
        _st(kvb_ref, 256 + g * hd, 256 + (g + 1) * hd, vs)
    for g in range(NSA_KV_HEADS):
        c0 = C_NKV + 1024 + g * hd
        kn = _rms(_ld(z_ref, c0, c0 + hd), kg)
        _st(win_ref, g * hd, (g + 1) * hd, kn)
        _st(kvb_ref, 512 + g * hd, 512 + (g + 1) * hd, kn)
    vw = _ld(z_ref, C_NKV + 1280, C_NKV + 1536)
    _st(win_ref, 256, 512, vw)
    _st(kvb_ref, 768, 1024, vw)

    zs = _ld(zs_ref, 0, LANES)
    sg = _sigmoid(zs)
    _st(gates_ref, 0, LANES, sg)
    xs = zs + dtb_ref[...]
    e = jnp.exp(-jnp.abs(xs))
    u1 = 1.0 + e
    log1p_e = jnp.where(u1 == 1.0, e, jnp.log(u1) * (e / jnp.where(u1 == 1.0, 1.0, u1 - 1.0)))
    softplus = jnp.maximum(xs, 0.0) + log1p_e
    gdec = -jnp.exp(alog_ref[...]) * softplus
    rows = zs.shape[0]
    if tstride == 1:
        in_chunk = _mod_pow2(lax.broadcasted_iota(jnp.int32, gdec.shape, 0), GDN_CHUNK)
        step = 1
        while step < GDN_CHUNK:
            gdec = gdec + jnp.where(in_chunk >= step, pltpu.roll(gdec, step, 0), 0.0)
            step *= 2
    for h in range(GDN_HEADS):
        _st(ggb_ref, h * hd, (h + 1) * hd, jnp.broadcast_to(gdec[:, S_GA + h:S_GA + h + 1], (rows, hd)))
        _st(ggb_ref, 512 + h * hd, 512 + (h + 1) * hd, jnp.broadcast_to(sg[:, S_GB + h:S_GB + h + 1], (rows, hd)))

    gcw = gcw_ref[...]
    for part in range(3):
        c0 = C_GQKV + part * 512
        x = _ld(z_ref, c0, c0 + 512)
        hq = halo_q[:, part * 512:(part + 1) * 512]
        w = gcw[:, part * 512:(part + 1) * 512]
        c = (w[0:1] * _shift(x, hq, 3, tstride) + w[1:2] * _shift(x, hq, 2, tstride)
             + w[2:3] * _shift(x, hq, 1, tstride) + w[3:4] * x)
        c = c * _sigmoid(c)
        if part == 2:
            _st(gq_ref, 1024, 1536, c)
        else:
            for h in range(GDN_HEADS):
                v = _l2n(c[:, h * hd:(h + 1) * hd])
                if part == 0:
                    v = v * (hd ** -0.5)
                _st(gq_ref, part * 512 + h * hd, part * 512 + (h + 1) * hd, v)


def _prep(z, zs, hist_u, hist_q, lw, *, sample, nseq, seq_len, slot=None):
    M = z.shape[0]
    wts = [lw["conv_a_w"], lw["gdn_conv_w"], lw["q_norm_g"].reshape(1, -1), lw["k_norm_g"].reshape(1, -1),
           lw["out_norm_a"].reshape(1, -1), lw["alog_v"], lw["dtb_v"]]
    wspecs2 = [pl.BlockSpec(w.shape, lambda i: (0, 0)) for w in wts]
    widths = [(512, BF16), (1024, BF16), (1024, F32), (512, F32), (1024, BF16), (LANES, F32), (1536, F32), (1024, F32)]
    if not sample:
        tm = 256
        tps = seq_len // tm
        in_specs = [pl.BlockSpec((tm, C_END), lambda i: (i, 0)),
                    pl.BlockSpec((SUBLANES, C_END), lambda i: (jnp.maximum(i * (tm // SUBLANES) - 1, 0), 0)),
                    pl.BlockSpec((tm, LANES), lambda i: (i, 0)),
                    pl.BlockSpec((1, SUBLANES, 512), lambda i: (i // tps, 0, 0)),
                    pl.BlockSpec((1, SUBLANES, 1536), lambda i: (i // tps, 0, 0))] + wspecs2
        out_specs = [pl.BlockSpec((tm, w), lambda i: (i, 0)) for w, _ in widths]
        out_specs.append(pl.BlockSpec((1, SUBLANES, 512), lambda i: (i // tps, 0, 0)))
        out_shape = [jax.ShapeDtypeStruct((M, w), d) for w, d in widths]
        out_shape.append(jax.ShapeDtypeStruct((nseq, SUBLANES, 512), F32))
        kvr = 4 * NSA_KV_HEADS
        layer = slot.layer
        out_specs.append(pl.BlockSpec((1, tm * kvr, HEAD_DIM), lambda i: (layer, i, 0)))
        out_shape.append(jax.ShapeDtypeStruct((slot.depth, M * kvr, HEAD_DIM), F32))
        n_in = len(in_specs)
        return pl.pallas_call(
            slot.wrap(functools.partial(_prep_kernel, tstride=1, tiles_per_seq=tps), n_in),
            grid=(M // tm,), in_specs=in_specs + slot.in_specs(), out_specs=out_specs, out_shape=out_shape,
            input_output_aliases=slot.aliases(n_in, len(out_shape) - 1),
            compiler_params=_cp(("arbitrary",)), name="prep_prompt",
        )(z, z, zs, hist_u, hist_q, *wts, *slot.args())
    T = seq_len
    bt = 64
    z3 = z.reshape(T, nseq, C_END)
    zs3 = zs.reshape(T, nseq, LANES)
    in_specs = [pl.BlockSpec((T, bt, C_END), lambda i: (0, i, 0)),
                pl.BlockSpec((SUBLANES, C_END), lambda i: (0, 0)),
                pl.BlockSpec((T, bt, LANES), lambda i: (0, i, 0)),
                pl.BlockSpec((hist_u.shape[0], bt, 512), lambda i: (0, i, 0)),
                pl.BlockSpec((hist_q.shape[0], bt, 1536), lambda i: (0, i, 0))] + wspecs2
    out_specs = [pl.BlockSpec((T, bt, w), lambda i: (0, i, 0)) for w, _ in widths]
    out_specs.append(pl.BlockSpec((hist_u.shape[0], bt, 512), lambda i: (0, i, 0)))
    out_shape = [jax.ShapeDtypeStruct((T, nseq, w), d) for w, d in widths]
    out_shape.append(jax.ShapeDtypeStruct((hist_u.shape[0], nseq, 512), F32))
    outs = pl.pallas_call(
        functools.partial(_prep_kernel, tstride=bt, tiles_per_seq=1),
        grid=(nseq // bt,), in_specs=in_specs, out_specs=out_specs, out_shape=out_shape,
        compiler_params=_cp(("arbitrary",)), name="prep_sample",
    )(z3, z, zs3, hist_u, hist_q, *wts)
    return [o.reshape(M, o.shape[-1]) for o in outs[:-1]] + [outs[-1]]


def _bias_kernel(thr_ref, rb_ref, o_ref, *, a0, an, qs, ks):
    n = pl.program_id(0)
    R, C = o_ref.shape[-2:]

    assert qs >= 0 and ks >= 0
    last = REL_BUCKETS - 1
    cw = min(LANES, C)
    assert C % cw == 0

    def rows8(rc, carry):
        r0 = pl.multiple_of(rc * SUBLANES, SUBLANES)
        base = a0 + an * n + qs * r0
        for c0 in range(0, C, cw):
            dist = (base + qs * lax.broadcasted_iota(jnp.int32, (SUBLANES, cw), 0)
                    - ks * (c0 + lax.broadcasted_iota(jnp.int32, (SUBLANES, cw), 1)))
            d_min = base - ks * (c0 + cw - 1)
            d_max = base + qs * (SUBLANES - 1) - ks * c0

            def fill(bucket, c0=c0):
                for h in range(NSA_HEADS):
                    o_ref[0, h, pl.ds(r0, SUBLANES), c0:c0 + cw] = jnp.full(
                        (SUBLANES, cw), rb_ref[bucket * NSA_HEADS + h], F32)

            def general(dist=dist, c0=c0):
                b = [jnp.full((SUBLANES, cw), rb_ref[h], F32) for h in range(NSA_HEADS)]
                for k in range(1, REL_BUCKETS):
                    reached = dist >= thr_ref[k]
                    b = [jnp.where(reached, rb_ref[k * NSA_HEADS + h], b[h]) for h in range(NSA_HEADS)]
                for h in range(NSA_HEADS):
                    o_ref[0, h, pl.ds(r0, SUBLANES), c0:c0 + cw] = b[h]

            lax.cond(d_min >= thr_ref[last], functools.partial(fill, last),
                     lambda fill=fill, general=general, d_max=d_max: lax.cond(
                         d_max <= 0, functools.partial(fill, 0), general))
        return carry

    lax.fori_loop(0, R // SUBLANES, rows8, 0)


def _bias_table(thr, rb_flat, n, R, C, *, a0, an, qs, ks):
    return pl.pallas_call(
        functools.partial(_bias_kernel, a0=a0, an=an, qs=qs, ks=ks),
        grid=(n,),
        in_specs=[pl.BlockSpec(memory_space=pltpu.SMEM), pl.BlockSpec(memory_space=pltpu.SMEM)],
        out_specs=pl.BlockSpec((1, NSA_HEADS, R, C), lambda i: (i, 0, 0, 0)),
        out_shape=jax.ShapeDtypeStruct((n, NSA_HEADS, R, C), F32),
        compiler_params=_cp(("parallel",)), name="bias_table",
    )(thr, rb_flat)


def _bucket_thresholds():
    n = jnp.arange(REL_MAX_DIST + 1)
    exact = REL_BUCKETS // 2
    nf = jnp.maximum(n, 1).astype(F32)
    far = exact + (jnp.log(nf / exact) / math.log(REL_MAX_DIST / exact) * (REL_BUCKETS - exact)).astype(jnp.int32)
    bucket = jnp.where(n < exact, n, jnp.minimum(far, REL_BUCKETS - 1))
    return jnp.sum(bucket[None, :] < jnp.arange(REL_BUCKETS)[:, None], axis=1).astype(jnp.int32)


def _gelu_tanh(x):
    return x * (0.5 * (1.0 + jnp.tanh(math.sqrt(2.0 / math.pi) * (x + 0.044715 * (x * x * x)))))


def _compress_math(get_rows, ng, pe, w1_ref, w2):
    half = CMP_BLOCK // 2
    rows = [get_rows(r) for r in range(half)]
    top = jnp.dot(jnp.concatenate([(rows[r] + pe[r:r + 1]).astype(BF16) for r in range(half)], axis=-1),
                  w1_ref[0], preferred_element_type=F32)
    bot = jnp.dot(jnp.concatenate([(rows[r] + pe[r + half:r + half + 1]).astype(BF16) for r in range(half)], axis=-1),
                  w1_ref[1], preferred_element_type=F32)
    h = top + pltpu.roll(bot, ng - 1, 0)
    return jnp.dot(_gelu_tanh(h).astype(BF16), w2, preferred_element_type=F32)


def _compress_kernel(x_ref, pe_ref, w1_ref, w2_ref, kg_ref, o_ref):
    ng = x_ref.shape[1] // CMP_STRIDE
    slot = pl.program_id(1)

    def get_rows(r):
        return x_ref[0, pl.ds(r, ng, stride=CMP_STRIDE), :]

    out = _compress_math(get_rows, ng, pe_ref[0], w1_ref.at[0], w2_ref[0])
    o_ref[0, 0, 0] = jnp.where(slot == 0, _rms(out, kg_ref[...]), out).astype(o_ref.dtype)


def _compress_prompt(rows3, lw):
    B, T, _ = rows3.shape
    ng = T // CMP_STRIDE
    G = NSA_KV_HEADS
    return pl.pallas_call(
        _compress_kernel,
        grid=(B, 2, G),
        in_specs=[pl.BlockSpec((1, T, HEAD_DIM), lambda b, s, g: (b, 0, s * G + g)),
                  pl.BlockSpec((1, CMP_BLOCK, HEAD_DIM), lambda b, s, g: (s, 0, 0)),
                  pl.BlockSpec((1,) + lw["cmp_w1"].shape[1:], lambda b, s, g: (s, 0, 0, 0)),
                  pl.BlockSpec((1, CMP_HIDDEN, HEAD_DIM), lambda b, s, g: (s, 0, 0)),
                  pl.BlockSpec((1, HEAD_DIM), lambda b, s, g: (0, 0))],
        out_specs=pl.BlockSpec((1, 1, 1, ng, HEAD_DIM), lambda b, s, g: (b, s, g, 0, 0)),
        out_shape=jax.ShapeDtypeStruct((B, 2, G, ng, HEAD_DIM), BF16),
        compiler_params=_cp(("parallel", "parallel", "parallel")), name="compress_prompt",
    )(rows3, lw["cmp_pe"], lw["cmp_w1"], lw["cmp_w2"], lw["k_norm_g"].reshape(1, -1))


def _masked_softmax(s, mask):
    s = jnp.where(mask, s, NEG_INF)
    m = jnp.max(s, axis=-1, keepdims=True)
    e = jnp.where(mask, jnp.exp(s - m), 0.0)
    z = jnp.sum(e, axis=-1, keepdims=True)
    return e / jnp.where(z > 0, z, 1.0)


def _overlap_matrix(ncp):
    c = CMP_STRIDE * lax.broadcasted_iota(jnp.int32, (ncp, LANES), 0)
    s = SLC_BLOCK * lax.broadcasted_iota(jnp.int32, (ncp, LANES), 1)
    return jnp.where(c < s + SLC_BLOCK, jnp.where(c + CMP_BLOCK > s, 1.0, 0.0), 0.0).astype(F32)


def _select_mask(imp, cur, n_slc):
    R = imp.shape[0]
    blk = lax.broadcasted_iota(jnp.int32, (R, LANES), 1)
    for forced_blk in (0, cur, cur - 1):
        imp = jnp.where(blk == forced_blk, FORCE_SCORE, imp)
    imp = jnp.where(blk <= cur, imp, -1.0)
    imp = jnp.where(blk < n_slc, imp, -2.0)
    cnt = jnp.zeros((R, LANES), F32)
    for k in range(n_slc):
        col = imp[:, k:k + 1]
        cnt = cnt + jnp.where(col > imp, 1.0, jnp.where(col == imp, jnp.where(blk > k, 1.0, 0.0), 0.0))
    return jnp.where(cnt < float(min(N_SELECT, n_slc)), jnp.where(blk < n_slc, 1.0, 0.0), 0.0)


def _expand_matrix(nkeys):
    s = lax.broadcasted_iota(jnp.int32, (LANES, nkeys), 0)
    j = lax.broadcasted_iota(jnp.int32, (LANES, nkeys), 1)
    lo = s * SLC_BLOCK
    return jnp.where(j >= lo, jnp.where(j < lo + SLC_BLOCK, 1.0, 0.0), 0.0).astype(BF16)


KEY_TILE = 1024


def _select_mask_t(imp_t, cur, n_slc):
    blk = lax.broadcasted_iota(jnp.int32, imp_t.shape, 0)
    for forced_blk in (0, cur, cur - 1):
        imp_t = jnp.where(blk == forced_blk, FORCE_SCORE, imp_t)
    imp_t = jnp.where(blk <= cur, imp_t, -1.0)
    cnt = jnp.zeros(imp_t.shape, F32)
    for k in range(n_slc):
        row = imp_t[k:k + 1, :]
        cnt = cnt + jnp.where(row > imp_t, 1.0, jnp.where(row == imp_t, jnp.where(blk > k, 1.0, 0.0), 0.0))
    return jnp.where(cnt < float(min(N_SELECT, n_slc)), 1.0, 0.0)


def _nsa_prompt_kernel(q_ref, kc_ref, vc_ref, ks_ref, vs_ref, kw_ref, vw_ref, bt_ref, bc_ref, gates_ref, ex_ref,
                       o_ref, m_ref, acc_ref, *, T):
    qb = pl.program_id(2)
    QB = Q_BLOCK
    R = NSA_REP
    RQ = R * QB
    hd = HEAD_DIM
    nc = (T - CMP_BLOCK) // CMP_STRIDE + 1
    ncp = kc_ref.shape[-2]
    n_slc = ex_ref.shape[0]
    q = jnp.concatenate([q_ref[:, r * hd:(r + 1) * hd] for r in range(R)], axis=0)

    s = _dot_t(q, kc_ref[0, 0, 0]) + bc_ref[0].reshape(RQ, ncp)
    t_row = lax.broadcasted_iota(jnp.int32, (R, QB, ncp), 1).reshape(RQ, ncp)
    c_col = lax.broadcasted_iota(jnp.int32, (RQ, ncp), 1)
    dist = qb * QB + t_row - (CMP_STRIDE * c_col + CMP_BLOCK - 1)
    p = _masked_softmax(s, jnp.where(c_col < nc, dist, -1) >= 0)
    o_cmp = jnp.dot(p.astype(BF16), vc_ref[0, 0, 0], preferred_element_type=F32)

    psum = p[0:QB]
    for r in range(1, R):
        psum = psum + p[r * QB:(r + 1) * QB]
    s_lo = SLC_BLOCK * lax.broadcasted_iota(jnp.int32, (n_slc, ncp), 0)
    c_lo = CMP_STRIDE * lax.broadcasted_iota(jnp.int32, (n_slc, ncp), 1)
    overlap_t = jnp.where(c_lo < s_lo + SLC_BLOCK, jnp.where(c_lo + CMP_BLOCK > s_lo, 1.0, 0.0), 0.0).astype(F32)
    imp_t = _dot_t(overlap_t, psum, HIGHEST)
    cur = _div_pow2(qb * QB + lax.broadcasted_iota(jnp.int32, (1, QB), 1), SLC_BLOCK)
    sel_t = _select_mask_t(imp_t, cur, n_slc).astype(BF16)

    NCHAIN = 2
    hpc = R // NCHAIN
    crow = hpc * QB

    def bias_rows(c, first_blk, n):
        return jnp.concatenate(
            [bt_ref[jnp.maximum(qb - (first_blk + j), 0), c * hpc:(c + 1) * hpc].reshape(crow, QB) for j in range(n)],
            axis=-1)

    def rel(first_key, nk):
        return (qb * QB - first_key + lax.broadcasted_iota(jnp.int32, (QB, nk), 0)
                - lax.broadcasted_iota(jnp.int32, (QB, nk), 1))

    def with_ones(v):
        return jnp.concatenate([v, jnp.ones(v.shape, BF16)], axis=-1)

    nkb = KEY_TILE // QB
    m_ref[...] = jnp.full(m_ref.shape, NEG_INF, F32)
    acc_ref[...] = jnp.zeros(acc_ref.shape, F32)

    def body(kt, carry):
        off = pl.multiple_of(kt * KEY_TILE, KEY_TILE)
        k = ks_ref[0, pl.ds(off, KEY_TILE), :]
        vext = with_ones(vs_ref[0, pl.ds(off, KEY_TILE), :])
        selk = lax.dot_general(sel_t, ex_ref[:, pl.ds(off, KEY_TILE)], (((0,), (0,)), ((), ())),
                               preferred_element_type=F32)
        d = jnp.where(selk > 0.5, rel(off, KEY_TILE), -1)
        neg = jnp.concatenate([jnp.where(d >= 0, 0.0, NEG_INF)] * hpc, axis=0)
        for c in range(NCHAIN):
            rows = slice(c * crow, (c + 1) * crow)
            sc = _dot_t(q[rows], k) + bias_rows(c, kt * nkb, nkb) + neg
            m_old = m_ref[rows]
            m_new = jnp.maximum(m_old, jnp.max(sc, axis=-1, keepdims=True))
            pe = jnp.exp(sc - m_new).astype(BF16)
            acc_ref[rows] = jnp.exp(m_old - m_new) * acc_ref[rows] + jnp.dot(pe, vext, preferred_element_type=F32)
            m_ref[rows] = m_new
        return carry

    lax.fori_loop(0, qb // nkb + 1, body, 0)
    acc = acc_ref[...]
    den = acc[:, hd:]
    o_slc = acc[:, :hd] / jnp.where(den > 0, den, 1.0)

    wk = min(WINDOW + QB, T)
    w0 = jnp.maximum(qb - WINDOW // QB, 0)
    offw = pl.multiple_of(w0 * QB, QB)
    kwin = kw_ref[0, pl.ds(offw, wk), :]
    vext = with_ones(vw_ref[0, pl.ds(offw, wk), :])
    d = rel(offw, wk)
    d = jnp.where(d <= WINDOW, d, -1)
    neg = jnp.concatenate([jnp.where(d >= 0, 0.0, NEG_INF)] * hpc, axis=0)
    o_win = []
    for c in range(NCHAIN):
        sc = _dot_t(q[c * crow:(c + 1) * crow], kwin) + bias_rows(c, w0, wk // QB) + neg
        e = jnp.exp(sc - jnp.max(sc, axis=-1, keepdims=True)).astype(BF16)
        acc = jnp.dot(e, vext, preferred_element_type=F32)
        o_win.append(acc[:, :hd] / acc[:, hd:])
    o_win = jnp.concatenate(o_win, axis=0)

    gt = gates_ref[0]
    for r in range(R):
        sl = slice(r * QB, (r + 1) * QB)
        o = (gt[:, r:r + 1] * o_cmp[sl] + gt[:, R + r:R + r + 1] * o_slc[sl]
             + gt[:, 2 * R + r:2 * R + r + 1] * o_win[sl])
        o_ref[:, r * hd:(r + 1) * hd] = o


def _nsa_prompt(q, cmp_kv, kvb3, gates_g, bias_tiles, bias_cmp, *, B, T):
    G = NSA_KV_HEADS
    R = NSA_REP
    nqb = T // Q_BLOCK
    ncp = T // CMP_STRIDE
    n_slc = -(-T // SLC_BLOCK)
    expand = (jnp.arange(T)[None, :] // SLC_BLOCK == jnp.arange(n_slc)[:, None]).astype(BF16)
    kv_spec = lambda col: pl.BlockSpec((1, T, HEAD_DIM), lambda b, g, i, col=col: (b, 0, col + g))
    return pl.pallas_call(
        functools.partial(_nsa_prompt_kernel, T=T),
        grid=(B, G, nqb),
        in_specs=[pl.BlockSpec((Q_BLOCK, R * HEAD_DIM), lambda b, g, i: (b * nqb + i, g)),
                  pl.BlockSpec((1, 1, 1, ncp, HEAD_DIM), lambda b, g, i: (b, 0, g, 0, 0)),
                  pl.BlockSpec((1, 1, 1, ncp, HEAD_DIM), lambda b, g, i: (b, 1, g, 0, 0)),
                  kv_spec(0), kv_spec(2), kv_spec(4), kv_spec(6),
                  pl.BlockSpec((nqb, R, Q_BLOCK, Q_BLOCK), lambda b, g, i: (0, g, 0, 0)),
                  pl.BlockSpec((1, R, Q_BLOCK, ncp), lambda b, g, i: (i, g, 0, 0)),
                  pl.BlockSpec((1, Q_BLOCK, LANES), lambda b, g, i: (g, b * nqb + i, 0)),
                  pl.BlockSpec((n_slc, T), lambda b, g, i: (0, 0))],
        out_specs=pl.BlockSpec((Q_BLOCK, R * HEAD_DIM), lambda b, g, i: (b * nqb + i, g)),
        out_shape=jax.ShapeDtypeStruct((B * T, G * R * HEAD_DIM), F32),
        scratch_shapes=[pltpu.VMEM((R * Q_BLOCK, 1), F32),
                        pltpu.VMEM((R * Q_BLOCK, 2 * HEAD_DIM), F32)],
        compiler_params=_cp(("parallel", "parallel", "arbitrary")), name="nsa_prompt",
    )(q, cmp_kv, cmp_kv, kvb3, kvb3, kvb3, kvb3, bias_tiles, bias_cmp, gates_g, expand)


def _nsa_sample_kernel(pt_ref, *refs, npages, tq):
    pages = refs[:npages]
    (win_ref, wnew_ref, q_ref, kvn_ref, gates_ref, pe_ref, w1_ref, w2_ref, kg_ref, bc_ref, bs_ref, bw_ref,
     o_ref, wout_ref) = refs[npages:]
    G, R, hd = NSA_KV_HEADS, NSA_REP, HEAD_DIM
    KVR = 4 * G
    WR = 2 * G
    page = pages[0].shape[2] // KVR
    past = npages * page
    L = past + tq
    nc = (L - CMP_BLOCK) // CMP_STRIDE + 1
    ng = past // CMP_STRIDE
    n_slc = -(-L // SLC_BLOCK)
    rows = R * tq
    wb = win_ref.shape[2] // WR
    wout_ref[0, 0, 0:(wb - tq) * WR, :] = win_ref[0, 0, tq * WR:wb * WR, :]
    wout_ref[0, 0, (wb - tq) * WR:wb * WR, :] = wnew_ref[0]
    pad_new = lambda v: jnp.concatenate([v, jnp.zeros((LANES - v.shape[0], v.shape[1]), v.dtype)], axis=0)
    t_of_row = lax.broadcasted_iota(jnp.int32, (R, tq, 1), 1).reshape(rows, 1)
    pos = past + t_of_row
    ra = lax.broadcasted_iota(jnp.int32, (rows, rows), 0)
    rb = lax.broadcasted_iota(jnp.int32, (rows, rows), 1)
    same_t = jnp.where(_mod_pow2(ra, tq) == _mod_pow2(rb, tq), 1.0, 0.0).astype(F32)
    kg = kg_ref[...]
    kvn = kvn_ref[0]

    for g in range(G):
        q = q_ref[0, g]
        gt = gates_ref[0, g]

        def compress(slot):
            col = slot * G + g

            def get_rows(r):
                return jnp.concatenate(
                    [pg[0, 0, pl.ds(r * KVR + col, page // CMP_STRIDE, stride=CMP_STRIDE * KVR), :] for pg in pages],
                    axis=0)

            return _compress_math(get_rows, ng, pe_ref[slot], w1_ref.at[slot], w2_ref[slot])

        kc = _rms(compress(0), kg).astype(BF16)
        vc = compress(1).astype(BF16)

        c_col = lax.broadcasted_iota(jnp.int32, (rows, ng), 1)
        dist = pos - (CMP_STRIDE * c_col + CMP_BLOCK - 1)
        p = _masked_softmax(_dot_t(q, kc) + bc_ref[g], jnp.where(c_col < nc, dist, -1) >= 0)
        o_cmp = jnp.dot(p.astype(BF16), vc, preferred_element_type=F32)

        psum = jnp.dot(same_t, p, precision=HIGHEST, preferred_element_type=F32)
        imp = jnp.dot(psum, _overlap_matrix(ng), precision=HIGHEST, preferred_element_type=F32)
        sel = _select_mask(imp, _div_pow2(pos, SLC_BLOCK), n_slc)
        nk = past + LANES
        selk = jnp.dot(sel.astype(BF16), _expand_matrix(nk), preferred_element_type=F32)

        k_new = pad_new(kvn[:, g * hd:(g + 1) * hd])
        v_new = pad_new(kvn[:, 256 + g * hd:256 + (g + 1) * hd])
        all_rows = lambda slot, new: jnp.concatenate(
            [pg[0, 0, pl.ds(slot * G + g, page, stride=KVR), :].astype(BF16) for pg in pages] + [new], axis=0)
        sc = _dot_t(q, all_rows(2, k_new)) + bs_ref[g]
        j = lax.broadcasted_iota(jnp.int32, (rows, nk), 1)
        p = _masked_softmax(sc, jnp.where(selk > 0.5, pos - j, -1) >= 0).astype(BF16)
        o_slc = jnp.dot(p, all_rows(3, v_new), preferred_element_type=F32)

        kw_new = pad_new(kvn[:, 512 + g * hd:512 + (g + 1) * hd])
        vw_new = pad_new(kvn[:, 768 + g * hd:768 + (g + 1) * hd])
        kw = win_ref[0, 0, pl.ds(g, wb, stride=WR), :].astype(BF16)
        vw = win_ref[0, 0, pl.ds(G + g, wb, stride=WR), :].astype(BF16)
        sc = _dot_t(q, jnp.concatenate([kw, kw_new], axis=0)) + bw_ref[g]
        j = lax.broadcasted_iota(jnp.int32, (rows, wb + LANES), 1)
        d = pos - (past - wb + j)
        p = _masked_softmax(sc, jnp.where(d <= WINDOW, d, -1) >= 0).astype(BF16)
        o_win = jnp.dot(p, jnp.concatenate([vw, vw_new], axis=0), preferred_element_type=F32)

        o_ref[0, g] = gt[:, 0:1] * o_cmp + gt[:, 1:2] * o_slc + gt[:, 2:3] * o_win


def _nsa_sample(slot, page_table, cache_lin, win_lin, win_new, q_s, kvn_s, gates_s, lw, bias_c, bias_s, bias_w, *, tq):
    B, npages = page_table.shape
    G = NSA_KV_HEADS
    rows = NSA_REP * tq
    l = slot.layer
    page_specs = [pl.BlockSpec((1, 1) + cache_lin.shape[2:], lambda b, pt, p=p: (l, pt[b, p], 0, 0))
                  for p in range(npages)]
    full = lambda a: pl.BlockSpec(a.shape, lambda b, pt, n=a.ndim: (0,) * n)
    wts = [lw["cmp_pe"], lw["cmp_w1"], lw["cmp_w2"], lw["k_norm_g"].reshape(1, -1), bias_c, bias_s, bias_w]
    grid_spec = pltpu.PrefetchScalarGridSpec(
        num_scalar_prefetch=1, grid=(B,),
        in_specs=page_specs + [
            pl.BlockSpec((1, 1) + win_lin.shape[2:], lambda b, pt: (l, b, 0, 0)),
            pl.BlockSpec((1,) + win_new.shape[1:], lambda b, pt: (b, 0, 0)),
            pl.BlockSpec((1, G, rows, HEAD_DIM), lambda b, pt: (b, 0, 0, 0)),
            pl.BlockSpec((1, SUBLANES, kvn_s.shape[2]), lambda b, pt: (b, 0, 0)),
            pl.BlockSpec((1, G, rows, LANES), lambda b, pt: (b, 0, 0, 0))] + [full(w) for w in wts]
        + slot.in_specs(),
        out_specs=[pl.BlockSpec((1, G, rows, HEAD_DIM), lambda b, pt: (b, 0, 0, 0)),
                   pl.BlockSpec((1, 1) + win_lin.shape[2:], lambda b, pt: (l, b, 0, 0))])
    n_in = 1 + npages + 5 + len(wts)
    return pl.pallas_call(
        slot.wrap(functools.partial(_nsa_sample_kernel, npages=npages, tq=tq), n_in),
        grid_spec=grid_spec,
        out_shape=[jax.ShapeDtypeStruct((B, G, rows, HEAD_DIM), F32),
                   jax.ShapeDtypeStruct((slot.depth,) + win_lin.shape[1:], F32)],
        input_output_aliases=slot.aliases(n_in, 1),
        compiler_params=_cp(("arbitrary",)), name="nsa_sample",
    )(page_table, *([cache_lin] * npages), win_lin, win_new, q_s, kvn_s, gates_s, *wts, *slot.args())


def _dot3(a, b):
    ah = a.astype(BF16)
    bh = b.astype(BF16)
    al = (a - ah.astype(F32)).astype(BF16)
    bl = (b - bh.astype(F32)).astype(BF16)
    d = functools.partial(jnp.dot, preferred_element_type=F32)
    return d(ah, bh) + (d(al, bh) + d(ah, bl))


def _gdn_chunk_kernel(gq_ref, ggb_ref, n_ref, k2_ref, qe_ref, o0_ref, eg_ref):
    B, C = gq_ref.shape[:2]
    H = GDN_HEADS
    hd = HEAD_DIM
    W = H * hd
    HC = H * C
    ri = lax.broadcasted_iota(jnp.int32, (HC, HC), 0)
    ci = lax.broadcasted_iota(jnp.int32, (HC, HC), 1)
    tri = jnp.where(_div_pow2(ri, C) == _div_pow2(ci, C), ri - ci, -1)
    own_lanes = (_div_pow2(lax.broadcasted_iota(jnp.int32, (HC, W), 0), C)
                 == _div_pow2(lax.broadcasted_iota(jnp.int32, (HC, W), 1), hd))

    for b in range(B):
        stack = lambda ref, c0: jnp.concatenate([ref[b, :, c0 + h * hd:c0 + (h + 1) * hd] for h in range(H)], axis=0)
        q = stack(gq_ref, 0)
        k = stack(gq_ref, W)
        v = stack(gq_ref, 2 * W)
        gc = stack(ggb_ref, 0)
        beta = stack(ggb_ref, W)
        gcol = jnp.concatenate([gc] * (HC // hd), axis=-1)
        grow = gcol.T
        decay = jnp.exp(jnp.where(tri >= 0, gcol - grow, -jnp.inf))
        kb = k * beta
        g2 = _dot_t(jnp.concatenate([kb, q], axis=0).astype(BF16), k.astype(BF16))
        low = jnp.where(tri > 0, g2[:HC] * decay, 0.0)
        qk = jnp.where(tri >= 0, g2[HC:] * decay, 0.0)
        egc = jnp.exp(gc)
        x = jnp.concatenate([v * beta, kb * egc], axis=-1)
        pw = low
        span = 1
        while span < C:
            last = span * 2 >= C
            rhs = x if last else jnp.concatenate([pw, x], axis=-1)
            r = _dot3(pw, rhs)
            px = r if last else r[:, HC:]
            x = x - px if span == 1 else x + px
            if not last:
                pw = r[:, :HC]
            span *= 2
        uw = x.astype(BF16)
        g_last = jnp.concatenate([jnp.broadcast_to(gc[(h + 1) * C - 1:(h + 1) * C], (C, hd)) for h in range(H)],
                                 axis=0)
        kd = k * jnp.exp(g_last - gc)
        kd_bd = jnp.where(own_lanes, jnp.concatenate([kd] * H, axis=-1), 0.0).astype(BF16)
        nk = lax.dot_general(kd_bd, uw, (((0,), (0,)), ((), ())), preferred_element_type=F32)
        ow = jnp.dot(qk.astype(BF16), uw, preferred_element_type=F32)
        qe = q * egc - ow[:, hd:]
        eg = jnp.exp(g_last)
        for h in range(H):
            n_ref[b, h, 0] = nk[h * hd:(h + 1) * hd, :hd]
            k2_ref[b, h, 0] = nk[h * hd:(h + 1) * hd, hd:].astype(BF16)
            o0_ref[b, h, 0] = ow[h * C:(h + 1) * C, :hd]
            qe_ref[b, h, 0] = qe[h * C:(h + 1) * C].astype(BF16)
            eg_ref[b, h, 0] = eg[h * C:h * C + SUBLANES]


def _gdn_scan_kernel(n_ref, k2_ref, qe_ref, o0_ref, eg_ref, gz_ref, s0_ref, ng_ref, y_ref, so_ref, s_ref):
    n = pl.program_id(0)
    B = gz_ref.shape[0]
    hd = HEAD_DIM

    @pl.when(n == 0)
    def _():
        s_ref[...] = s0_ref[...]

    ng = ng_ref[...]
    for b in range(B):
        for h in range(GDN_HEADS):
            S = s_ref[b, h]
            s16 = S.astype(BF16)
            o = jnp.dot(qe_ref[b, h, 0], s16, preferred_element_type=F32) + o0_ref[b, h, 0]
            s_ref[b, h] = (eg_ref[b, h, 0][0:1] * S + n_ref[b, h, 0]
                           - jnp.dot(k2_ref[b, h, 0], s16, preferred_element_type=F32))
            z = gz_ref[b, :, h * hd:(h + 1) * hd]
            y_ref[b, :, h * hd:(h + 1) * hd] = (_rms(o, ng) * (z * _sigmoid(z))).astype(y_ref.dtype)

    @pl.when(n == pl.num_programs(0) - 1)
    def _():
        so_ref[...] = s_ref[...]


def _gdn_prompt(gq3, ggb3, z3, s0, norm_g):
    B, T, _ = gq3.shape
    C = GDN_CHUNK
    H = GDN_HEADS
    hd = HEAD_DIM
    W = H * hd
    nchunk = T // C
    item = lambda rows: pl.BlockSpec((B, H, 1, rows, hd), lambda n: (0, 0, n, 0, 0))
    shape = lambda rows, dt: jax.ShapeDtypeStruct((B, H, nchunk, rows, hd), dt)
    nn, k2, qe, o0, eg = pl.pallas_call(
        _gdn_chunk_kernel,
        grid=(nchunk,),
        in_specs=[pl.BlockSpec((B, C, 3 * W), lambda n: (0, n, 0)),
                  pl.BlockSpec((B, C, 2 * W), lambda n: (0, n, 0))],
        out_specs=[item(hd), item(hd), item(C), item(C), item(SUBLANES)],
        out_shape=[shape(hd, F32), shape(hd, BF16), shape(C, BF16), shape(C, F32), shape(SUBLANES, F32)],
        compiler_params=_cp(("parallel",)), name="gdn_chunk",
    )(gq3, ggb3)
    return pl.pallas_call(
        _gdn_scan_kernel,
        grid=(nchunk,),
        in_specs=[item(hd), item(hd), item(C), item(C), item(SUBLANES),
                  pl.BlockSpec((B, C, W), lambda n: (0, n, C_GZ // W)),
                  pl.BlockSpec(s0.shape, lambda n: (0, 0, 0, 0)),
                  pl.BlockSpec((1, hd), lambda n: (0, 0))],
        out_specs=[pl.BlockSpec((B, C, W), lambda n: (0, n, 0)),
                   pl.BlockSpec(s0.shape, lambda n: (0, 0, 0, 0))],
        out_shape=[jax.ShapeDtypeStruct((B, T, W), BF16), jax.ShapeDtypeStruct(s0.shape, F32)],
        scratch_shapes=[pltpu.VMEM(s0.shape, F32)],
        compiler_params=_cp(("arbitrary",)), name="gdn_scan",
    )(nn, k2, qe, o0, eg, z3, s0, norm_g.reshape(1, -1))


def _gdn_sample_kernel(kq_ref, v_ref, gb_ref, gz_ref, s0_ref, ng_ref, y_ref, so_ref, *, tq):
    bt = kq_ref.shape[0]
    hd = HEAD_DIM
    ri = lax.broadcasted_iota(jnp.int32, (hd, hd), 0)
    ci = lax.broadcasted_iota(jnp.int32, (hd, hd), 1)
    eye = jnp.where(ri == ci, 1.0, 0.0).astype(F32)
    ng = ng_ref[...]

    def body(bi, carry):
        for h in range(GDN_HEADS):
            kq = kq_ref[bi, h]
            cols = _dot_t(eye, kq, HIGHEST)
            gb = gb_ref[bi, h]
            v = v_ref[bi, h]
            S = s0_ref[bi, h]
            outs = []
            for t in range(tq):
                a = jnp.exp(gb[t:t + 1])
                kc = cols[:, t:t + 1]
                qc = cols[:, tq + t:tq + t + 1]
                Sa = S * a
                stk = jnp.sum(Sa * kc, axis=0, keepdims=True)
                vn = gb[tq + t:tq + t + 1] * (v[t:t + 1] - stk)
                S = Sa + kc * vn
                outs.append(jnp.sum(S * qc, axis=0, keepdims=True))
            so_ref[0, bi, h] = S
            o = jnp.concatenate(outs + [jnp.zeros((SUBLANES - tq, hd), F32)], axis=0)
            z = gz_ref[bi, h]
            y_ref[bi, h] = _rms(o, ng) * (z * _sigmoid(z))
        return carry

    lax.fori_loop(0, bt, body, 0)


def _gdn_sample(slot, kq_s, v_s, gb_s, gz_s, s0, norm_g, *, tq):
    B = kq_s.shape[0]
    bt = 8
    layer = slot.layer
    blk = lambda a: pl.BlockSpec((bt,) + a.shape[1:], lambda i: (i, 0, 0, 0))
    in_specs = [blk(kq_s), blk(v_s), blk(gb_s), blk(gz_s), blk(s0), pl.BlockSpec((1, HEAD_DIM), lambda i: (0, 0))]
    return pl.pallas_call(
        slot.wrap(functools.partial(_gdn_sample_kernel, tq=tq), len(in_specs)),
        grid=(B // bt,),
        in_specs=in_specs + slot.in_specs(),
        out_specs=[blk(v_s), pl.BlockSpec((1, bt) + s0.shape[1:], lambda i: (layer, i, 0, 0, 0))],
        out_shape=[jax.ShapeDtypeStruct(v_s.shape, F32), jax.ShapeDtypeStruct((slot.depth,) + s0.shape, F32)],
        input_output_aliases=slot.aliases(len(in_specs), 1),
        compiler_params=_cp(("parallel",)), name="gdn_sample",
    )(kq_s, v_s, gb_s, gz_s, s0, norm_g.reshape(1, -1), *slot.args())


def _mix_out_kernel(ya_ref, yb_ref, yc_ref, x_ref, onb_ref, w_ref, o_ref, mix_ref):
    @pl.when(pl.program_id(1) == 0)
    def _():
        mix_ref[:, 0:512] = ya_ref[...]
        mix_ref[:, 512:1536] = _rms(yb_ref[...], onb_ref[...]).astype(BF16)
        mix_ref[:, 1536:2048] = yc_ref[...]

    o_ref[...] = x_ref[...] + jnp.dot(mix_ref[...], w_ref[...], preferred_element_type=F32)


def _mix_out(ya, yb, yc, x, onb, w, *, tn=1024):
    M, N = x.shape
    tm = _row_tile(M)
    K = w.shape[0]
    return pl.pallas_call(
        _mix_out_kernel,
        grid=(M // tm, N // tn),
        in_specs=[pl.BlockSpec((tm, 512), lambda i, j: (i, 0)),
                  pl.BlockSpec((tm, 1024), lambda i, j: (i, 0)),
                  pl.BlockSpec((tm, 512), lambda i, j: (i, 0)),
                  pl.BlockSpec((tm, tn), lambda i, j: (i, j)),
                  pl.BlockSpec((1, 1024), lambda i, j: (0, 0)),
                  w.spec((K, tn), lambda i, j: (0, j))],
        out_specs=pl.BlockSpec((tm, tn), lambda i, j: (i, j)),
        out_shape=jax.ShapeDtypeStruct((M, N), F32),
        scratch_shapes=[pltpu.VMEM((tm, K), BF16)],
        compiler_params=_cp(("parallel", "arbitrary")), name="mix_out",
    )(ya, yb, yc, x, onb.reshape(1, -1), w.array)


def _matmul_res_kernel(a_ref, w_ref, r_ref, o_ref):
    o_ref[...] = r_ref[...] + jnp.dot(a_ref[...], w_ref[...], preferred_element_type=F32)


def _matmul_res(a, w, res, *, tn=512):
    M, K = a.shape
    N = w.shape[1]
    tm = _row_tile(M)
    return pl.pallas_call(
        _matmul_res_kernel,
        grid=(M // tm, N // tn),
        in_specs=[pl.BlockSpec((tm, K), lambda i, j: (i, 0)),
                  w.spec((K, tn), lambda i, j: (0, j)),
                  pl.BlockSpec((tm, tn), lambda i, j: (i, j))],
        out_specs=pl.BlockSpec((tm, tn), lambda i, j: (i, j)),
        out_shape=jax.ShapeDtypeStruct((M, N), F32),
        compiler_params=_cp(("parallel", "parallel")), name="matmul_res",
    )(a, w.array, res)


def _layer_weights(l, p):
    w_in = p["w_in"][l]
    w_main = jnp.concatenate([w_in[:, :4096], w_in[:, 4120:5656], w_in[:, 5664:6176]], axis=1).astype(BF16)
    w_small = jnp.concatenate([w_in[:, 4096:4120], w_in[:, 5656:5664],
                               jnp.zeros((w_in.shape[0], LANES - 32), w_in.dtype)], axis=1).astype(BF16)
    lane_vec = lambda v: jnp.zeros((1, LANES), F32).at[0, S_GA:S_GA + GDN_HEADS].set(v)
    return {
        "norm_mix_g": p["norm_mix_g"][l], "w_main": w_main, "w_small": w_small,
        "conv_a_w": p["conv_a_w"][l], "q_norm_g": p["q_norm_g"][l], "k_norm_g": p["k_norm_g"][l],
        "cmp_pe": p["cmp_pe"][l],
        "cmp_w1": p["cmp_w1"][l].reshape(2, 2, CMP_BLOCK // 2 * HEAD_DIM, CMP_HIDDEN).astype(BF16),
        "cmp_w2": p["cmp_w2"][l].astype(BF16),
        "gdn_conv_w": p["gdn_conv_w"][l], "alog_v": lane_vec(p["gdn_a_log"][l]), "dtb_v": lane_vec(p["gdn_dt_bias"][l]),
        "gdn_norm_g": p["gdn_norm_g"][l], "out_norm_a": p["out_norm_a"][l], "out_norm_b": p["out_norm_b"][l],
        "w_out": _LayerOf(p["w_out_bf16"], l), "norm_ffn_g": p["norm_ffn_g"][l],
        "ffn_up": _LayerOf(p["ffn_up_bf16"], l), "ffn_conv_w": p["ffn_conv_w"][l],
        "ffn_down": _LayerOf(p["ffn_down_bf16"], l),
    }


HALO_ROWS = 16


def _ffn_up_kernel(x_ref, xp_ref, g_ref, wa_ref, wv_ref, ha_ref, hv_ref, ca_ref, cv_ref,
                   o_ref, hoa_ref, hov_ref, xn_ref, *, tstride, tiles_per_seq):
    tn = wa_ref.shape[1]
    prompt = tstride == 1

    @pl.when(pl.program_id(1) == 0)
    def _():
        g = g_ref[...]
        if prompt:
            xn_ref[0:HALO_ROWS] = _rms(xp_ref[...], g).astype(BF16)
            xn_ref[HALO_ROWS:] = _rms(x_ref[...], g).astype(BF16)
        else:
            xn_ref[...] = _rms(x_ref[...], g).astype(BF16)

    xn = xn_ref[...]

    def part(w_ref, hist_ref, cw_ref, ho_ref):
        up = jnp.dot(xn, w_ref[...], preferred_element_type=F32)
        if prompt:
            start = (pl.program_id(0) % tiles_per_seq) == 0
            halo = jnp.where(start, hist_ref[0], up[HALO_ROWS - SUBLANES:HALO_ROWS])
            x = up[HALO_ROWS:]
            ho_ref[0] = x[x.shape[0] - SUBLANES:]
        else:
            halo = _ld(hist_ref, 0, tn)
            x = up
            _st(ho_ref, 0, tn, x[x.shape[0] - halo.shape[0]:])
        w = cw_ref[...]
        return w[0:1] * _shift(x, halo, 2, tstride) + w[1:2] * _shift(x, halo, 1, tstride) + w[2:3] * x

    a = part(wa_ref, ha_ref, ca_ref, hoa_ref)
    v = part(wv_ref, hv_ref, cv_ref, hov_ref)
    o_ref[...] = (a * _sigmoid(a) * v).astype(o_ref.dtype)


def _ffn_up(h, g, w_up, conv_w, hist, *, sample, nseq, seq_len, tn=512):
    M, K = h.shape
    tm = _row_tile(seq_len)
    F = w_up.shape[1] // 2
    nj = F // tn
    common = [pl.BlockSpec((1, K), lambda i, j: (0, 0)),
              w_up.spec((K, tn), lambda i, j: (0, j)),
              w_up.spec((K, tn), lambda i, j: (0, j + nj))]
    cw = [pl.BlockSpec((3, tn), lambda i, j: (0, j)), pl.BlockSpec((3, tn), lambda i, j: (0, j + nj))]
    if not sample:
        tps = seq_len // tm
        hspec = lambda off: pl.BlockSpec((1, SUBLANES, tn), lambda i, j, off=off: (i // tps, 0, j + off))
        hout = pl.BlockSpec((1, SUBLANES, tn), lambda i, j: (i, 0, j))
        act, tail_a, tail_v = pl.pallas_call(
            functools.partial(_ffn_up_kernel, tstride=1, tiles_per_seq=tps),
            grid=(M // tm, nj),
            in_specs=[pl.BlockSpec((tm, K), lambda i, j: (i, 0)),
                      pl.BlockSpec((HALO_ROWS, K), lambda i, j: (jnp.maximum(i * (tm // HALO_ROWS) - 1, 0), 0))]
            + common + [hspec(0), hspec(nj)] + cw,
            out_specs=[pl.BlockSpec((tm, tn), lambda i, j: (i, j)), hout, hout],
            out_shape=[jax.ShapeDtypeStruct((M, F), BF16)] + [jax.ShapeDtypeStruct((M // tm, SUBLANES, F), F32)] * 2,
            scratch_shapes=[pltpu.VMEM((tm + HALO_ROWS, K), BF16)],
            compiler_params=_cp(("arbitrary", "arbitrary")), name="ffn_up_prompt",
        )(h, h, g.reshape(1, K), w_up.array, w_up.array, hist, hist, conv_w, conv_w)
        return act, tail_a[tps - 1::tps], tail_v[tps - 1::tps]
    hk = hist.shape[0]
    hspec = lambda off: pl.BlockSpec((hk, nseq, tn), lambda i, j, off=off: (0, 0, j + off))
    hout = pl.BlockSpec((hk, nseq, tn), lambda i, j: (0, 0, j))
    return pl.pallas_call(
        functools.partial(_ffn_up_kernel, tstride=nseq, tiles_per_seq=1),
        grid=(1, nj),
        in_specs=[pl.BlockSpec((M, K), lambda i, j: (0, 0)), pl.BlockSpec((HALO_ROWS, K), lambda i, j: (0, 0))]
        + common + [hspec(0), hspec(nj)] + cw,
        out_specs=[pl.BlockSpec((M, tn), lambda i, j: (0, j)), hout, hout],
        out_shape=[jax.ShapeDtypeStruct((M, F), BF16)] + [jax.ShapeDtypeStruct((hk, nseq, F), F32)] * 2,
        scratch_shapes=[pltpu.VMEM((M, K), BF16)],
        compiler_params=_cp(("arbitrary", "arbitrary")), name="ffn_up_sample",
    )(h, h, g.reshape(1, K), w_up.array, w_up.array, hist, hist, conv_w, conv_w)


def _dense_tail(x2, ya, yb, yc, lw, hist_ffn, *, sample, nseq, seq_len):
    h = _mix_out(ya, yb, yc, x2, lw["out_norm_b"], lw["w_out"])
    act, hist_a, hist_v = _ffn_up(h, lw["norm_ffn_g"], lw["ffn_up"], lw["ffn_conv_w"], hist_ffn,
                                  sample=sample, nseq=nseq, seq_len=seq_len)
    return _matmul_res(act, lw["ffn_down"], h), jnp.concatenate([hist_a, hist_v], axis=-1)


def _prompt_layer(x2, lw, bias_tiles, bias_cmp, rows_slot, *, B, T):
    M = B * T
    G, R = NSA_KV_HEADS, NSA_REP
    z, zs = _norm_matmul(x2, lw["norm_mix_g"], lw["w_main"], lw["w_small"])
    hu0 = jnp.zeros((B, SUBLANES, 512), F32)
    hq0 = jnp.zeros((B, SUBLANES, 1536), F32)
    ya, q, rows, win, kvb, gates, gq, ggb, hu, rows_lin = _prep(z, zs, hu0, hq0, lw, sample=False, nseq=B, seq_len=T,
                                                                slot=rows_slot)
    cmp_kv = _compress_prompt(rows.reshape(B, T, 1024), lw)
    gates_g = gates[:, :3 * G * R].reshape(M, 3, G, R).transpose(2, 0, 1, 3).reshape(G, M, 3 * R)
    gates_g = jnp.pad(gates_g, ((0, 0), (0, 0), (0, LANES - 3 * R)))
    yb = _nsa_prompt(q, cmp_kv, kvb.reshape(B, T, 1024), gates_g, bias_tiles, bias_cmp, B=B, T=T)
    s0 = jnp.zeros((B, GDN_HEADS, HEAD_DIM, HEAD_DIM), F32)
    yc, s_new = _gdn_prompt(gq.reshape(B, T, 1536), ggb.reshape(B, T, 1024), z.reshape(B, T, C_END), s0,
                            lw["gdn_norm_g"])
    hf0 = jnp.zeros((B, SUBLANES, lw["ffn_conv_w"].shape[1]), F32)
    out, up = _dense_tail(x2, ya, yb, yc.reshape(M, 512), lw, hf0, sample=False, nseq=B, seq_len=T)
    return out, rows_lin, win, hu, z, s_new, up


def _sample_layer(x2, lw, st, bias_c, bias_s, bias_w, win_slot, gdn_slot, *, B, T):
    M = B * T
    G, R, H, hd = NSA_KV_HEADS, NSA_REP, GDN_HEADS, HEAD_DIM
    z, zs = _norm_matmul(x2, lw["norm_mix_g"], lw["w_main"], lw["w_small"])
    hu0 = jnp.swapaxes(st["conv_a"], 0, 1)
    hq0 = jnp.swapaxes(st["gdn_conv"], 0, 1)
    ya, q, rows, win, kvb, gates, gq, ggb, hu = _prep(z, zs, hu0, hq0, lw, sample=True, nseq=B, seq_len=T)
    tb = lambda a: a.reshape(T, B, -1)
    q_s = tb(q).reshape(T, B, G, R, hd).transpose(1, 2, 3, 0, 4).reshape(B, G, R * T, hd)
    gt = tb(gates)[:, :, :3 * G * R].reshape(T, B, 3, G, R).transpose(1, 3, 4, 0, 2).reshape(B, G, R * T, 3)
    gates_s = jnp.pad(gt, ((0, 0), (0, 0), (0, 0), (0, LANES - 3)))
    kvn_s = jnp.pad(jnp.swapaxes(tb(kvb), 0, 1), ((0, 0), (0, SUBLANES - T), (0, 0)))
    win_new = jnp.swapaxes(tb(win), 0, 1).reshape(B, T * 2 * G, hd)
    yb_s, win = _nsa_sample(win_slot, st["page_table"], st["cache_lin"], st["win_lin"], win_new, q_s, kvn_s, gates_s, lw,
                            bias_c, bias_s, bias_w, tq=T)
    yb = yb_s.reshape(B, G, R, T, hd).transpose(3, 0, 1, 2, 4).reshape(M, G * R * hd)
    gq4 = tb(gq).reshape(T, B, 3, H, hd)
    bh = lambda a: a.transpose(1, 2, 0, 3)
    padt = lambda a: jnp.pad(a, ((0, 0), (0, 0), (0, SUBLANES - T), (0, 0)))
    kq_s = jnp.concatenate([bh(gq4[:, :, 1]), bh(gq4[:, :, 0])], axis=2)
    v_s = padt(bh(gq4[:, :, 2]))
    ggb4 = tb(ggb).reshape(T, B, 2, H, hd)
    gb_s = jnp.concatenate([bh(ggb4[:, :, 0]), bh(ggb4[:, :, 1])], axis=2)
    gz_s = padt(bh(tb(z)[:, :, C_GZ:C_END].reshape(T, B, H, hd)))
    yc_s, s_new = _gdn_sample(gdn_slot, kq_s, v_s, gb_s, gz_s, st["gdn"], lw["gdn_norm_g"], tq=T)
    yc = yc_s[:, :, :T].transpose(2, 0, 1, 3).reshape(M, H * hd).astype(BF16)
    hf0 = jnp.swapaxes(st["ffn_conv"], 0, 1)
    out, up = _dense_tail(x2, ya, yb, yc, lw, hf0, sample=True, nseq=B, seq_len=T)
    return out, rows, win, hu, z, s_new, up


def kernel(x_prompt, x_sample, cache_kv, cache_win, state_conv_a, state_gdn_conv, state_gdn, state_ffn_conv,
           page_table, rel_bias, norm_mix_g, w_in, conv_a_w, q_norm_g, k_norm_g, cmp_pe, cmp_w1, cmp_w2,
           gdn_conv_w, gdn_a_log, gdn_dt_bias, gdn_norm_g, out_norm_a, out_norm_b, w_out, norm_ffn_g,
           ffn_up, ffn_conv_w, ffn_down):
    params = dict(norm_mix_g=norm_mix_g, w_in=w_in, conv_a_w=conv_a_w, q_norm_g=q_norm_g, k_norm_g=k_norm_g,
                  cmp_pe=cmp_pe, cmp_w1=cmp_w1, cmp_w2=cmp_w2, gdn_conv_w=gdn_conv_w, gdn_a_log=gdn_a_log,
                  gdn_dt_bias=gdn_dt_bias, gdn_norm_g=gdn_norm_g, out_norm_a=out_norm_a, out_norm_b=out_norm_b,
                  w_out=w_out, norm_ffn_g=norm_ffn_g, ffn_up=ffn_up, ffn_conv_w=ffn_conv_w, ffn_down=ffn_down,
                  w_out_bf16=w_out.astype(BF16), ffn_up_bf16=ffn_up.astype(BF16), ffn_down_bf16=ffn_down.astype(BF16))
    depth = w_in.shape[0]
    Bp, T, D = x_prompt.shape
    Bs, Ts, _ = x_sample.shape
    G, R, hd = NSA_KV_HEADS, NSA_REP, HEAD_DIM
    n_pool, page = cache_kv.shape[1], cache_kv.shape[2]
    npages = page_table.shape[1]
    past = npages * page
    wb = cache_win.shape[2]
    L = past + Ts
    assert ((L - CMP_BLOCK) // CMP_STRIDE) * CMP_STRIDE + CMP_BLOCK <= past, "compressed blocks must lie in the cache"
    assert T % KEY_TILE == 0 and T >= WINDOW + Q_BLOCK and 3 <= Ts <= SUBLANES // 2

    thr = _bucket_thresholds()
    rb_flat = rel_bias.reshape(-1)
    nqb = T // Q_BLOCK
    bias_tiles = _bias_table(thr, rb_flat, nqb, Q_BLOCK, Q_BLOCK, a0=0, an=Q_BLOCK, qs=1, ks=1)
    bias_cmp = _bias_table(thr, rb_flat, nqb, Q_BLOCK, T // CMP_STRIDE,
                           a0=-(CMP_BLOCK - 1), an=Q_BLOCK, qs=1, ks=CMP_STRIDE)

    def sample_bias(ncols, a0, ks):
        t = _bias_table(thr, rb_flat, 1, SUBLANES, ncols, a0=a0, an=0, qs=1, ks=ks)[0]
        return t[:, :Ts].reshape(G, R * Ts, ncols)

    bias_c = sample_bias(past // CMP_STRIDE, past - (CMP_BLOCK - 1), CMP_STRIDE)
    bias_s = sample_bias(past + LANES, past, 1)
    bias_w = sample_bias(wb + LANES, wb, 1)

    cache_lin = cache_kv.reshape(depth, n_pool, page * 4 * G, hd)
    win_lin = cache_win.reshape(depth, Bs, wb * 2 * G, hd)
    assert wb == WINDOW, "the new window buffer is the old one shifted by the new rows"

    xp = x_prompt.reshape(Bp * T, D)
    xs = jnp.swapaxes(x_sample, 0, 1).reshape(Ts * Bs, D)
    outs_p, outs_s = [], []
    rows_p_all = win_s_all = gdn_s_all = None
    for l in range(depth):
        lw = _layer_weights(l, params)
        xp, rows_p_all, win, hu, z, s_new, up = _prompt_layer(xp, lw, bias_tiles, bias_cmp,
                                                              _LayerSlot(l, depth, rows_p_all), B=Bp, T=T)
        wl = min(WINDOW, T)
        outs_p.append((
            win.reshape(Bp, T, 2, G, hd)[:, T - wl:],
            hu[:, SUBLANES - 2:],
            z.reshape(Bp, T, C_END)[:, T - 3:, C_GQKV:C_GZ],
            s_new,
            up[:, SUBLANES - 2:]))
        st = dict(page_table=page_table, cache_lin=cache_lin, win_lin=win_lin, conv_a=state_conv_a[l],
                  gdn_conv=state_gdn_conv[l], gdn=state_gdn[l], ffn_conv=state_ffn_conv[l])
        xs, rows, win_s_all, hu, z, gdn_s_all, up = _sample_layer(
            xs, lw, st, bias_c, bias_s, bias_w, _LayerSlot(l, depth, win_s_all), _LayerSlot(l, depth, gdn_s_all),
            B=Bs, T=Ts)
        tb = lambda a: jnp.swapaxes(a.reshape(Ts, Bs, -1), 0, 1)
        outs_s.append((
            tb(rows).reshape(Bs, Ts, 4, G, hd),
            jnp.swapaxes(hu, 0, 1),
            tb(z)[:, Ts - 3:, C_GQKV:C_GZ],
            jnp.swapaxes(up, 0, 1)))
    y_p = xp.reshape(Bp, T, D)
    y_s = jnp.swapaxes(xs.reshape(Ts, Bs, D), 0, 1)
    stack = lambda outs, i: jnp.stack([o[i] for o in outs])
    return (y_p, y_s,
            rows_p_all.reshape(depth, Bp, T, 4, G, hd), stack(outs_s, 0),
            stack(outs_p, 0), win_s_all.reshape(depth, Bs, wb, 2, G, hd),
            stack(outs_p, 1), stack(outs_s, 1), stack(outs_p, 2), stack(outs_s, 2),
            stack(outs_p, 3), gdn_s_all, stack(outs_p, 4), stack(outs_s, 3))
```

```python
import functools
import math

import jax
import jax.numpy as jnp
from jax import lax
from jax.experimental import pallas as pl
from jax.experimental.pallas import tpu as pltpu

F32 = jnp.float32
BF16 = jnp.bfloat16
HIGHEST = lax.Precision.HIGHEST

SUBLANES = 8
LANES = 128
VMEM_LIMIT_BYTES = 52 * 1024 * 1024

HEAD_DIM = 128
NSA_KV_HEADS = 2
NSA_REP = 4
NSA_HEADS = NSA_KV_HEADS * NSA_REP
GDN_HEADS = 4
CMP_BLOCK = 32
CMP_STRIDE = 16
CMP_HIDDEN = 256
SLC_BLOCK = 64
N_SELECT = 16
WINDOW = 512
Q_BLOCK = 128
GDN_CHUNK = 64
REL_BUCKETS = 32
REL_MAX_DIST = 1024
EPS = 1e-6
NEG_INF = -1e30
FORCE_SCORE = 1e4

C_AB, C_AC, C_AH, C_NQ, C_NKV, C_GQKV, C_GZ, C_END = 0, 512, 1024, 1536, 2560, 4096, 5632, 6144
S_GATE, S_GA, S_GB = 0, 24, 28


def _cp(sem):
    return pltpu.CompilerParams(dimension_semantics=sem, vmem_limit_bytes=VMEM_LIMIT_BYTES)


class _LayerOf:
    def __init__(self, stacked, layer):
        self.array, self.layer, self.shape = stacked, layer, stacked.shape[1:]

    def spec(self, block, index_map):
        layer = self.layer
        return pl.BlockSpec((None,) + block, lambda *ij: (layer,) + tuple(index_map(*ij)))


class _LayerSlot:
    def __init__(self, layer, depth, prev):
        self.layer, self.depth, self.prev = layer, depth, prev

    def in_specs(self):
        return [] if self.prev is None else [pl.BlockSpec(memory_space=pl.ANY)]

    def args(self):
        return [] if self.prev is None else [self.prev]

    def aliases(self, n_inputs_before, out_index):
        return {} if self.prev is None else {n_inputs_before: out_index}

    def wrap(self, kernel, n_refs_before):
        if self.prev is None:
            return kernel

        def wrapped(*refs):
            return kernel(*refs[:n_refs_before], *refs[n_refs_before + 1:])
        return wrapped


def _div_pow2(x, d):
    assert d & (d - 1) == 0
    return jnp.right_shift(x, d.bit_length() - 1)


def _mod_pow2(x, d):
    assert d & (d - 1) == 0
    return jnp.bitwise_and(x, d - 1)


def _sigmoid(x):
    return 1.0 / (1.0 + jnp.exp(-x))


def _rms(x, g):
    return x * lax.rsqrt(jnp.mean(x * x, axis=-1, keepdims=True) + EPS) * g


def _l2n(x):
    return x * lax.rsqrt(jnp.sum(x * x, axis=-1, keepdims=True) + EPS)


def _dot_t(a, b, precision=None):
    return lax.dot_general(a, b, (((1,), (1,)), ((), ())), precision=precision, preferred_element_type=F32)


def _ld(ref, c0, c1):
    if len(ref.shape) == 3:
        v = ref[:, :, c0:c1]
        return v.reshape(v.shape[0] * v.shape[1], v.shape[2])
    return ref[:, c0:c1]


def _st(ref, c0, c1, val):
    if len(ref.shape) == 3:
        ref[:, :, c0:c1] = val.reshape(ref.shape[0], ref.shape[1], c1 - c0).astype(ref.dtype)
    else:
        ref[:, c0:c1] = val.astype(ref.dtype)


def _shift(x, halo, s, tstride):
    n = s * tstride
    rows = x.shape[0]
    if tstride % SUBLANES == 0:
        hr = halo.shape[0]
        return jnp.concatenate([halo[hr - n:], x[:rows - n]], axis=0)
    xs = pltpu.roll(x, n, 0)
    hs = pltpu.roll(halo, n, 0)
    rid = lax.broadcasted_iota(jnp.int32, (SUBLANES, x.shape[1]), 0)
    head = jnp.where(rid < n, hs, xs[:SUBLANES])
    return jnp.concatenate([head, xs[SUBLANES:]], axis=0)


def _norm_matmul_kernel(x_ref, g_ref, w_ref, ws_ref, o_ref, os_ref, xn_ref):
    @pl.when(pl.program_id(1) == 0)
    def _():
        xn = _rms(x_ref[...], g_ref[...]).astype(BF16)
        xn_ref[...] = xn
        os_ref[...] = jnp.dot(xn, ws_ref[...], preferred_element_type=F32)

    o_ref[...] = jnp.dot(xn_ref[...], w_ref[...], preferred_element_type=F32)


def _row_tile(rows):
    for tm in (1024, 512):
        if rows % tm == 0:
            return tm
    return rows


def _norm_matmul(x, g, w, w_small, *, tn=1024):
    M, K = x.shape
    N = w.shape[1]
    tm = _row_tile(M)
    return pl.pallas_call(
        _norm_matmul_kernel,
        grid=(M // tm, N // tn),
        in_specs=[pl.BlockSpec((tm, K), lambda i, j: (i, 0)),
                  pl.BlockSpec((1, K), lambda i, j: (0, 0)),
                  pl.BlockSpec((K, tn), lambda i, j: (0, j)),
                  pl.BlockSpec((K, LANES), lambda i, j: (0, 0))],
        out_specs=[pl.BlockSpec((tm, tn), lambda i, j: (i, j)), pl.BlockSpec((tm, LANES), lambda i, j: (i, 0))],
        out_shape=[jax.ShapeDtypeStruct((M, N), F32), jax.ShapeDtypeStruct((M, LANES), F32)],
        scratch_shapes=[pltpu.VMEM((tm, K), BF16)],
        compiler_params=_cp(("parallel", "arbitrary")), name="norm_matmul",
    )(x, g.reshape(1, K), w, w_small)


def _prep_kernel(z_ref, zp_ref, zs_ref, hu_ref, hq_ref, caw_ref, gcw_ref, qg_ref, kg_ref, ona_ref,
                 alog_ref, dtb_ref,
                 ya_ref, q_ref, rows_ref, win_ref, kvb_ref, gates_ref, gq_ref, ggb_ref, huo_ref, *lin_refs,
                 tstride, tiles_per_seq):
    hd = HEAD_DIM

    def put_rows(grp, val):
        _st(rows_ref, grp * hd, (grp + 1) * hd, val)
        if lin_refs:
            lin_refs[0][0, pl.ds(grp, val.shape[0], stride=4 * NSA_KV_HEADS), :] = val
    u = _ld(z_ref, C_AC, C_AH) * _ld(z_ref, C_AH, C_NQ)
    if tstride == 1:
        start = (pl.program_id(0) % tiles_per_seq) == 0
        halo_u = jnp.where(start, hu_ref[0], zp_ref[:, C_AC:C_AH] * zp_ref[:, C_AH:C_NQ])
        halo_q = jnp.where(start, hq_ref[0], zp_ref[:, C_GQKV:C_GZ])
    else:
        halo_u = _ld(hu_ref, 0, 512)
        halo_q = _ld(hq_ref, 0, 1536)
    caw = caw_ref[...]
    conv = caw[0:1] * _shift(u, halo_u, 2, tstride) + caw[1:2] * _shift(u, halo_u, 1, tstride) + caw[2:3] * u
    y = _ld(z_ref, C_AB, C_AC) * conv
    _st(ya_ref, 0, 512, _rms(y, ona_ref[...]))
    hu_rows = huo_ref.shape[0] * huo_ref.shape[1] if tstride != 1 else SUBLANES
    if tstride == 1:
        huo_ref[0] = u[u.shape[0] - hu_rows:]
    else:
        _st(huo_ref, 0, 512, u[u.shape[0] - hu_rows:])

    qg = qg_ref[...]
    kg = kg_ref[...]
    for h in range(NSA_HEADS):
        c0 = C_NQ + h * hd
        _st(q_ref, h * hd, (h + 1) * hd, _rms(_ld(z_ref, c0, c0 + hd), qg) * (hd ** -0.5))
    for grp in range(4):
        put_rows(grp, _ld(z_ref, C_NKV + grp * hd, C_NKV + (grp + 1) * hd))
    for g in range(NSA_KV_HEADS):
        c0 = C_NKV + 512 + g * hd
        kn = _rms(_ld(z_ref, c0, c0 + hd), kg)
        put_rows(4 + g, kn)
        _st(kvb_ref, g * hd, (g + 1) * hd, kn)
    for g in range(NSA_KV_HEADS):
        c0 = C_NKV + 768 + g * hd
        vs = _ld(z_ref, c0, c0 + hd)
        put_rows(6 + g, vs)
        _st(kvb_ref, 256 + g * hd, 256 + (g + 1) * hd, vs)
    for g in range(NSA_KV_HEADS):
        c0 = C_NKV + 1024 + g * hd
        kn = _rms(_ld(z_ref, c0, c0 + hd), kg)
        _st(win_ref, g * hd, (g + 1) * hd, kn)
        _st(kvb_ref, 512 + g * hd, 512 + (g + 1) * hd, kn)
    vw = _ld(z_ref, C_NKV + 1280, C_NKV + 1536)
    _st(win_ref, 256, 512, vw)
    _st(kvb_ref, 768, 1024, vw)

    zs = _ld(zs_ref, 0, LANES)
    sg = _sigmoid(zs)
    _st(gates_ref, 0, LANES, sg)
    xs = zs + dtb_ref[...]
    e = jnp.exp(-jnp.abs(xs))
    u1 = 1.0 + e
    log1p_e = jnp.where(u1 == 1.0, e, jnp.log(u1) * (e / jnp.where(u1 == 1.0, 1.0, u1 - 1.0)))
    softplus = jnp.maximum(xs, 0.0) + log1p_e
    gdec = -jnp.exp(alog_ref[...]) * softplus
    rows = zs.shape[0]
    if tstride == 1:
        in_chunk = _mod_pow2(lax.broadcasted_iota(jnp.int32, gdec.shape, 0), GDN_CHUNK)
        step = 1
        while step < GDN_CHUNK:
            gdec = gdec + jnp.where(in_chunk >= step, pltpu.roll(gdec, step, 0), 0.0)
            step *= 2
    for h in range(GDN_HEADS):
        _st(ggb_ref, h * hd, (h + 1) * hd, jnp.broadcast_to(gdec[:, S_GA + h:S_GA + h + 1], (rows, hd)))
        _st(ggb_ref, 512 + h * hd, 512 + (h + 1) * hd, jnp.broadcast_to(sg[:, S_GB + h:S_GB + h + 1], (rows, hd)))

    gcw = gcw_ref[...]
    for part in range(3):
        c0 = C_GQKV + part * 512
        x = _ld(z_ref, c0, c0 + 512)
        hq = halo_q[:, part * 512:(part + 1) * 512]
        w = gcw[:, part * 512:(part + 1) * 512]
        c = (w[0:1] * _shift(x, hq, 3, tstride) + w[1:2] * _shift(x, hq, 2, tstride)
             + w[2:3] * _shift(x, hq, 1, tstride) + w[3:4] * x)
        c = c * _sigmoid(c)
        if part == 2:
            _st(gq_ref, 1024, 1536, c)
        else:
            for h in range(GDN_HEADS):
                v = _l2n(c[:, h * hd:(h + 1) * hd])
                if part == 0:
                    v = v * (hd ** -0.5)
                _st(gq_ref, part * 512 + h * hd, part * 512 + (h + 1) * hd, v)


def _prep(z, zs, hist_u, hist_q, lw, *, sample, nseq, seq_len, slot=None):
    M = z.shape[0]
    wts = [lw["conv_a_w"], lw["gdn_conv_w"], lw["q_norm_g"].reshape(1, -1), lw["k_norm_g"].reshape(1, -1),
           lw["out_norm_a"].reshape(1, -1), lw["alog_v"], lw["dtb_v"]]
    wspecs2 = [pl.BlockSpec(w.shape, lambda i: (0, 0)) for w in wts]
    widths = [(512, BF16), (1024, BF16), (1024, F32), (512, F32), (1024, BF16), (LANES, F32), (1536, F32), (1024, F32)]
    if not sample:
        tm = 256
        tps = seq_len // tm
        in_specs = [pl.BlockSpec((tm, C_END), lambda i: (i, 0)),
                    pl.BlockSpec((SUBLANES, C_END), lambda i: (jnp.maximum(i * (tm // SUBLANES) - 1, 0), 0)),
                    pl.BlockSpec((tm, LANES), lambda i: (i, 0)),
                    pl.BlockSpec((1, SUBLANES, 512), lambda i: (i // tps, 0, 0)),
                    pl.BlockSpec((1, SUBLANES, 1536), lambda i: (i // tps, 0, 0))] + wspecs2
        out_specs = [pl.BlockSpec((tm, w), lambda i: (i, 0)) for w, _ in widths]
        out_specs.append(pl.BlockSpec((1, SUBLANES, 512), lambda i: (i // tps, 0, 0)))
        out_shape = [jax.ShapeDtypeStruct((M, w), d) for w, d in widths]
        out_shape.append(jax.ShapeDtypeStruct((nseq, SUBLANES, 512), F32))
        kvr = 4 * NSA_KV_HEADS
        layer = slot.layer
        out_specs.append(pl.BlockSpec((1, tm * kvr, HEAD_DIM), lambda i: (layer, i, 0)))
        out_shape.append(jax.ShapeDtypeStruct((slot.depth, M * kvr, HEAD_DIM), F32))
        n_in = len(in_specs)
        return pl.pallas_call(
            slot.wrap(functools.partial(_prep_kernel, tstride=1, tiles_per_seq=tps), n_in),
            grid=(M // tm,), in_specs=in_specs + slot.in_specs(), out_specs=out_specs, out_shape=out_shape,
            input_output_aliases=slot.aliases(n_in, len(out_shape) - 1),
            compiler_params=_cp(("arbitrary",)), name="prep_prompt",
        )(z, z, zs, hist_u, hist_q, *wts, *slot.args())
    T = seq_len
    bt = 64
    z3 = z.reshape(T, nseq, C_END)
    zs3 = zs.reshape(T, nseq, LANES)
    in_specs = [pl.BlockSpec((T, bt, C_END), lambda i: (0, i, 0)),
                pl.BlockSpec((SUBLANES, C_END), lambda i: (0, 0)),
                pl.BlockSpec((T, bt, LANES), lambda i: (0, i, 0)),
                pl.BlockSpec((hist_u.shape[0], bt, 512), lambda i: (0, i, 0)),
                pl.BlockSpec((hist_q.shape[0], bt, 1536), lambda i: (0, i, 0))] + wspecs2
    out_specs = [pl.BlockSpec((T, bt, w), lambda i: (0, i, 0)) for w, _ in widths]
    out_specs.append(pl.BlockSpec((hist_u.shape[0], bt, 512), lambda i: (0, i, 0)))
    out_shape = [jax.ShapeDtypeStruct((T, nseq, w), d) for w, d in widths]
    out_shape.append(jax.ShapeDtypeStruct((hist_u.shape[0], nseq, 512), F32))
    outs = pl.pallas_call(
        functools.partial(_prep_kernel, tstride=bt, tiles_per_seq=1),
        grid=(nseq // bt,), in_specs=in_specs, out_specs=out_specs, out_shape=out_shape,
        compiler_params=_cp(("arbitrary",)), name="prep_sample",
    )(z3, z, zs3, hist_u, hist_q, *wts)
    return [o.reshape(M, o.shape[-1]) for o in outs[:-1]] + [outs[-1]]


def _bias_kernel(thr_ref, rb_ref, o_ref, *, a0, an, qs, ks):
    n = pl.program_id(0)
    R, C = o_ref.shape[-2:]

    assert qs >= 0 and ks >= 0
    last = REL_BUCKETS - 1
    cw = min(LANES, C)
    assert C % cw == 0

    def rows8(rc, carry):
        r0 = pl.multiple_of(rc * SUBLANES, SUBLANES)
        base = a0 + an * n + qs * r0
        for c0 in range(0, C, cw):
            dist = (base + qs * lax.broadcasted_iota(jnp.int32, (SUBLANES, cw), 0)
                    - ks * (c0 + lax.broadcasted_iota(jnp.int32, (SUBLANES, cw), 1)))
            d_min = base - ks * (c0 + cw - 1)
            d_max = base + qs * (SUBLANES - 1) - ks * c0

            def fill(bucket, c0=c0):
                for h in range(NSA_HEADS):
                    o_ref[0, h, pl.ds(r0, SUBLANES), c0:c0 + cw] = jnp.full(
                        (SUBLANES, cw), rb_ref[bucket * NSA_HEADS + h], F32)

            def general(dist=dist, c0=c0):
                b = [jnp.full((SUBLANES, cw), rb_ref[h], F32) for h in range(NSA_HEADS)]
                for k in range(1, REL_BUCKETS):
                    reached = dist >= thr_ref[k]
                    b = [jnp.where(reached, rb_ref[k * NSA_HEADS + h], b[h]) for h in range(NSA_HEADS)]
                for h in range(NSA_HEADS):
                    o_ref[0, h, pl.ds(r0, SUBLANES), c0:c0 + cw] = b[h]

            lax.cond(d_min >= thr_ref[last], functools.partial(fill, last),
                     lambda fill=fill, general=general, d_max=d_max: lax.cond(
                         d_max <= 0, functools.partial(fill, 0), general))
        return carry

    lax.fori_loop(0, R // SUBLANES, rows8, 0)


def _bias_table(thr, rb_flat, n, R, C, *, a0, an, qs, ks):
    return pl.pallas_call(
        functools.partial(_bias_kernel, a0=a0, an=an, qs=qs, ks=ks),
        grid=(n,),
        in_specs=[pl.BlockSpec(memory_space=pltpu.SMEM), pl.BlockSpec(memory_space=pltpu.SMEM)],
        out_specs=pl.BlockSpec((1, NSA_HEADS, R, C), lambda i: (i, 0, 0, 0)),
        out_shape=jax.ShapeDtypeStruct((n, NSA_HEADS, R, C), F32),
        compiler_params=_cp(("parallel",)), name="bias_table",
    )(thr, rb_flat)


def _bucket_thresholds():
    n = jnp.arange(REL_MAX_DIST + 1)
    exact = REL_BUCKETS // 2
    nf = jnp.maximum(n, 1).astype(F32)
    far = exact + (jnp.log(nf / exact) / math.log(REL_MAX_DIST / exact) * (REL_BUCKETS - exact)).astype(jnp.int32)
    bucket = jnp.where(n < exact, n, jnp.minimum(far, REL_BUCKETS - 1))
    return jnp.sum(bucket[None, :] < jnp.arange(REL_BUCKETS)[:, None], axis=1).astype(jnp.int32)


def _gelu_tanh(x):
    return x * (0.5 * (1.0 + jnp.tanh(math.sqrt(2.0 / math.pi) * (x + 0.044715 * (x * x * x)))))


def _compress_math(get_rows, ng, pe, w1_ref, w2):
    half = CMP_BLOCK // 2
    rows = [get_rows(r) for r in range(half)]
    top = jnp.dot(jnp.concatenate([(rows[r] + pe[r:r + 1]).astype(BF16) for r in range(half)], axis=-1),
                  w1_ref[0], preferred_element_type=F32)
    bot = jnp.dot(jnp.concatenate([(rows[r] + pe[r + half:r + half + 1]).astype(BF16) for r in range(half)], axis=-1),
                  w1_ref[1], preferred_element_type=F32)
    h = top + pltpu.roll(bot, ng - 1, 0)
    return jnp.dot(_gelu_tanh(h).astype(BF16), w2, preferred_element_type=F32)


def _compress_kernel(x_ref, pe_ref, w1_ref, w2_ref, kg_ref, o_ref):
    ng = x_ref.shape[1] // CMP_STRIDE
    slot = pl.program_id(1)

    def get_rows(r):
        return x_ref[0, pl.ds(r, ng, stride=CMP_STRIDE), :]

    out = _compress_math(get_rows, ng, pe_ref[0], w1_ref.at[0], w2_ref[0])
    o_ref[0, 0, 0] = jnp.where(slot == 0, _rms(out, kg_ref[...]), out).astype(o_ref.dtype)


def _compress_prompt(rows3, lw):
    B, T, _ = rows3.shape
    ng = T // CMP_STRIDE
    G = NSA_KV_HEADS
    return pl.pallas_call(
        _compress_kernel,
        grid=(B, 2, G),
        in_specs=[pl.BlockSpec((1, T, HEAD_DIM), lambda b, s, g: (b, 0, s * G + g)),
                  pl.BlockSpec((1, CMP_BLOCK, HEAD_DIM), lambda b, s, g: (s, 0, 0)),
                  pl.BlockSpec((1,) + lw["cmp_w1"].shape[1:], lambda b, s, g: (s, 0, 0, 0)),
                  pl.BlockSpec((1, CMP_HIDDEN, HEAD_DIM), lambda b, s, g: (s, 0, 0)),
                  pl.BlockSpec((1, HEAD_DIM), lambda b, s, g: (0, 0))],
        out_specs=pl.BlockSpec((1, 1, 1, ng, HEAD_DIM), lambda b, s, g: (b, s, g, 0, 0)),
        out_shape=jax.ShapeDtypeStruct((B, 2, G, ng, HEAD_DIM), BF16),
        compiler_params=_cp(("parallel", "parallel", "parallel")), name="compress_prompt",
    )(rows3, lw["cmp_pe"], lw["cmp_w1"], lw["cmp_w2"], lw["k_norm_g"].reshape(1, -1))


def _masked_softmax(s, mask):
    s = jnp.where(mask, s, NEG_INF)
    m = jnp.max(s, axis=-1, keepdims=True)
    e = jnp.where(mask, jnp.exp(s - m), 0.0)
    z = jnp.sum(e, axis=-1, keepdims=True)
    return e / jnp.where(z > 0, z, 1.0)


def _overlap_matrix(ncp):
    c = CMP_STRIDE * lax.broadcasted_iota(jnp.int32, (ncp, LANES), 0)
    s = SLC_BLOCK * lax.broadcasted_iota(jnp.int32, (ncp, LANES), 1)
    return jnp.where(c < s + SLC_BLOCK, jnp.where(c + CMP_BLOCK > s, 1.0, 0.0), 0.0).astype(F32)


def _select_mask(imp, cur, n_slc):
    R = imp.shape[0]
    blk = lax.broadcasted_iota(jnp.int32, (R, LANES), 1)
    for forced_blk in (0, cur, cur - 1):
        imp = jnp.where(blk == forced_blk, FORCE_SCORE, imp)
    imp = jnp.where(blk <= cur, imp, -1.0)
    imp = jnp.where(blk < n_slc, imp, -2.0)
    cnt = jnp.zeros((R, LANES), F32)
    for k in range(n_slc):
        col = imp[:, k:k + 1]
        cnt = cnt + jnp.where(col > imp, 1.0, jnp.where(col == imp, jnp.where(blk > k, 1.0, 0.0), 0.0))
    return jnp.where(cnt < float(min(N_SELECT, n_slc)), jnp.where(blk < n_slc, 1.0, 0.0), 0.0)


def _expand_matrix(nkeys):
    s = lax.broadcasted_iota(jnp.int32, (LANES, nkeys), 0)
    j = lax.broadcasted_iota(jnp.int32, (LANES, nkeys), 1)
    lo = s * SLC_BLOCK
    return jnp.where(j >= lo, jnp.where(j < lo + SLC_BLOCK, 1.0, 0.0), 0.0).astype(BF16)


KEY_TILE = 1024


def _select_mask_t(imp_t, cur, n_slc):
    blk = lax.broadcasted_iota(jnp.int32, imp_t.shape, 0)
    for forced_blk in (0, cur, cur - 1):
        imp_t = jnp.where(blk == forced_blk, FORCE_SCORE, imp_t)
    imp_t = jnp.where(blk <= cur, imp_t, -1.0)
    cnt = jnp.zeros(imp_t.shape, F32)
    for k in range(n_slc):
        row = imp_t[k:k + 1, :]
        cnt = cnt + jnp.where(row > imp_t, 1.0, jnp.where(row == imp_t, jnp.where(blk > k, 1.0, 0.0), 0.0))
    return jnp.where(cnt < float(min(N_SELECT, n_slc)), 1.0, 0.0)


def _nsa_prompt_kernel(q_ref, kc_ref, vc_ref, ks_ref, vs_ref, kw_ref, vw_ref, bt_ref, bc_ref, gates_ref, ex_ref,
                       o_ref, m_ref, acc_ref, *, T):
    qb = pl.program_id(2)
    QB = Q_BLOCK
    R = NSA_REP
    RQ = R * QB
    hd = HEAD_DIM
    nc = (T - CMP_BLOCK) // CMP_STRIDE + 1
    ncp = kc_ref.shape[-2]
    n_slc = ex_ref.shape[0]
    q = jnp.concatenate([q_ref[:, r * hd:(r + 1) * hd] for r in range(R)], axis=0)

    s = _dot_t(q, kc_ref[0, 0, 0]) + bc_ref[0].reshape(RQ, ncp)
    t_row = lax.broadcasted_iota(jnp.int32, (R, QB, ncp), 1).reshape(RQ, ncp)
    c_col = lax.broadcasted_iota(jnp.int32, (RQ, ncp), 1)
    dist = qb * QB + t_row - (CMP_STRIDE * c_col + CMP_BLOCK - 1)
    p = _masked_softmax(s, jnp.where(c_col < nc, dist, -1) >= 0)
    o_cmp = jnp.dot(p.astype(BF16), vc_ref[0, 0, 0], preferred_element_type=F32)

    psum = p[0:QB]
    for r in range(1, R):
        psum = psum + p[r * QB:(r + 1) * QB]
    s_lo = SLC_BLOCK * lax.broadcasted_iota(jnp.int32, (n_slc, ncp), 0)
    c_lo = CMP_STRIDE * lax.broadcasted_iota(jnp.int32, (n_slc, ncp), 1)
    overlap_t = jnp.where(c_lo < s_lo + SLC_BLOCK, jnp.where(c_lo + CMP_BLOCK > s_lo, 1.0, 0.0), 0.0).astype(F32)
    imp_t = _dot_t(overlap_t, psum, HIGHEST)
    cur = _div_pow2(qb * QB + lax.broadcasted_iota(jnp.int32, (1, QB), 1), SLC_BLOCK)
    sel_t = _select_mask_t(imp_t, cur, n_slc).astype(BF16)

    NCHAIN = 2
    hpc = R // NCHAIN
    crow = hpc * QB

    def bias_rows(c, first_blk, n):
        return jnp.concatenate(
            [bt_ref[jnp.maximum(qb - (first_blk + j), 0), c * hpc:(c + 1) * hpc].reshape(crow, QB) for j in range(n)],
            axis=-1)

    def rel(first_key, nk):
        return (qb * QB - first_key + lax.broadcasted_iota(jnp.int32, (QB, nk), 0)
                - lax.broadcasted_iota(jnp.int32, (QB, nk), 1))

    def with_ones(v):
        return jnp.concatenate([v, jnp.ones(v.shape, BF16)], axis=-1)

    nkb = KEY_TILE // QB
    m_ref[...] = jnp.full(m_ref.shape, NEG_INF, F32)
    acc_ref[...] = jnp.zeros(acc_ref.shape, F32)

    def body(kt, carry):
        off = pl.multiple_of(kt * KEY_TILE, KEY_TILE)
        k = ks_ref[0, pl.ds(off, KEY_TILE), :]
        vext = with_ones(vs_ref[0, pl.ds(off, KEY_TILE), :])
        selk = lax.dot_general(sel_t, ex_ref[:, pl.ds(off, KEY_TILE)], (((0,), (0,)), ((), ())),
                               preferred_element_type=F32)
        d = jnp.where(selk > 0.5, rel(off, KEY_TILE), -1)
        neg = jnp.concatenate([jnp.where(d >= 0, 0.0, NEG_INF)] * hpc, axis=0)
        for c in range(NCHAIN):
            rows = slice(c * crow, (c + 1) * crow)
            sc = _dot_t(q[rows], k) + bias_rows(c, kt * nkb, nkb) + neg
            m_old = m_ref[rows]
            m_new = jnp.maximum(m_old, jnp.max(sc, axis=-1, keepdims=True))
            pe = jnp.exp(sc - m_new).astype(BF16)
            acc_ref[rows] = jnp.exp(m_old - m_new) * acc_ref[rows] + jnp.dot(pe, vext, preferred_element_type=F32)
            m_ref[rows] = m_new
        return carry

    lax.fori_loop(0, qb // nkb + 1, body, 0)
    acc = acc_ref[...]
    den = acc[:, hd:]
    o_slc = acc[:, :hd] / jnp.where(den > 0, den, 1.0)

    wk = min(WINDOW + QB, T)
    w0 = jnp.maximum(qb - WINDOW // QB, 0)
    offw = pl.multiple_of(w0 * QB, QB)
    kwin = kw_ref[0, pl.ds(offw, wk), :]
    vext = with_ones(vw_ref[0, pl.ds(offw, wk), :])
    d = rel(offw, wk)
    d = jnp.where(d <= WINDOW, d, -1)
    neg = jnp.concatenate([jnp.where(d >= 0, 0.0, NEG_INF)] * hpc, axis=0)
    o_win = []
    for c in range(NCHAIN):
        sc = _dot_t(q[c * crow:(c + 1) * crow], kwin) + bias_rows(c, w0, wk // QB) + neg
        e = jnp.exp(sc - jnp.max(sc, axis=-1, keepdims=True)).astype(BF16)
        acc = jnp.dot(e, vext, preferred_element_type=F32)
        o_win.append(acc[:, :hd] / acc[:, hd:])
    o_win = jnp.concatenate(o_win, axis=0)

    gt = gates_ref[0]
    for r in range(R):
        sl = slice(r * QB, (r + 1) * QB)
        o = (gt[:, r:r + 1] * o_cmp[sl] + gt[:, R + r:R + r + 1] * o_slc[sl]
             + gt[:, 2 * R + r:2 * R + r + 1] * o_win[sl])
        o_ref[:, r * hd:(r + 1) * hd] = o


def _nsa_prompt(q, cmp_kv, kvb3, gates_g, bias_tiles, bias_cmp, *, B, T):
    G = NSA_KV_HEADS
    R = NSA_REP
    nqb = T // Q_BLOCK
    ncp = T // CMP_STRIDE
    n_slc = -(-T // SLC_BLOCK)
    expand = (jnp.arange(T)[None, :] // SLC_BLOCK == jnp.arange(n_slc)[:, None]).astype(BF16)
    kv_spec = lambda col: pl.BlockSpec((1, T, HEAD_DIM), lambda b, g, i, col=col: (b, 0, col + g))
    return pl.pallas_call(
        functools.partial(_nsa_prompt_kernel, T=T),
        grid=(B, G, nqb),
        in_specs=[pl.BlockSpec((Q_BLOCK, R * HEAD_DIM), lambda b, g, i: (b * nqb + i, g)),
                  pl.BlockSpec((1, 1, 1, ncp, HEAD_DIM), lambda b, g, i: (b, 0, g, 0, 0)),
                  pl.BlockSpec((1, 1, 1, ncp, HEAD_DIM), lambda b, g, i: (b, 1, g, 0, 0)),
                  kv_spec(0), kv_spec(2), kv_spec(4), kv_spec(6),
                  pl.BlockSpec((nqb, R, Q_BLOCK, Q_BLOCK), lambda b, g, i: (0, g, 0, 0)),
                  pl.BlockSpec((1, R, Q_BLOCK, ncp), lambda b, g, i: (i, g, 0, 0)),
                  pl.BlockSpec((1, Q_BLOCK, LANES), lambda b, g, i: (g, b * nqb + i, 0)),
                  pl.BlockSpec((n_slc, T), lambda b, g, i: (0, 0))],
        out_specs=pl.BlockSpec((Q_BLOCK, R * HEAD_DIM), lambda b, g, i: (b * nqb + i, g)),
        out_shape=jax.ShapeDtypeStruct((B * T, G * R * HEAD_DIM), F32),
        scratch_shapes=[pltpu.VMEM((R * Q_BLOCK, 1), F32),
                        pltpu.VMEM((R * Q_BLOCK, 2 * HEAD_DIM), F32)],
        compiler_params=_cp(("parallel", "parallel", "arbitrary")), name="nsa_prompt",
    )(q, cmp_kv, cmp_kv, kvb3, kvb3, kvb3, kvb3, bias_tiles, bias_cmp, gates_g, expand)


def _nsa_sample_kernel(pt_ref, *refs, npages, tq):
    pages = refs[:npages]
    (win_ref, wnew_ref, q_ref, kvn_ref, gates_ref, pe_ref, w1_ref, w2_ref, kg_ref, bc_ref, bs_ref, bw_ref,
     o_ref, wout_ref) = refs[npages:]
    G, R, hd = NSA_KV_HEADS, NSA_REP, HEAD_DIM
    KVR = 4 * G
    WR = 2 * G
    page = pages[0].shape[2] // KVR
    past = npages * page
    L = past + tq
    nc = (L - CMP_BLOCK) // CMP_STRIDE + 1
    ng = past // CMP_STRIDE
    n_slc = -(-L // SLC_BLOCK)
    rows = R * tq
    wb = win_ref.shape[2] // WR
    wout_ref[0, 0, 0:(wb - tq) * WR, :] = win_ref[0, 0, tq * WR:wb * WR, :]
    wout_ref[0, 0, (wb - tq) * WR:wb * WR, :] = wnew_ref[0]
    pad_new = lambda v: jnp.concatenate([v, jnp.zeros((LANES - v.shape[0], v.shape[1]), v.dtype)], axis=0)
    t_of_row = lax.broadcasted_iota(jnp.int32, (R, tq, 1), 1).reshape(rows, 1)
    pos = past + t_of_row
    ra = lax.broadcasted_iota(jnp.int32, (rows, rows), 0)
    rb = lax.broadcasted_iota(jnp.int32, (rows, rows), 1)
    same_t = jnp.where(_mod_pow2(ra, tq) == _mod_pow2(rb, tq), 1.0, 0.0).astype(F32)
    kg = kg_ref[...]
    kvn = kvn_ref[0]

    def compress(slot):
        def get_rows(r):
            return jnp.concatenate(
                [pg[0, 0, pl.ds(r * KVR + slot * G + grp, page // CMP_STRIDE, stride=CMP_STRIDE * KVR), :]
                 for grp in range(G) for pg in pages], axis=0)

        return _compress_math(get_rows, G * ng, pe_ref[slot], w1_ref.at[slot], w2_ref[slot])

    cmp_all = [compress(0), compress(1)]

    for g in range(G):
        q = q_ref[0, g]
        gt = gates_ref[0, g]

        kc = _rms(cmp_all[0][g * ng:(g + 1) * ng], kg).astype(BF16)
        vc = cmp_all[1][g * ng:(g + 1) * ng].astype(BF16)

        c_col = lax.broadcasted_iota(jnp.int32, (rows, ng), 1)
        dist = pos - (CMP_STRIDE * c_col + CMP_BLOCK - 1)
        p = _masked_softmax(_dot_t(q, kc) + bc_ref[g], jnp.where(c_col < nc, dist, -1) >= 0)
        o_cmp = jnp.dot(p.astype(BF16), vc, preferred_element_type=F32)

        psum = jnp.dot(same_t, p, precision=HIGHEST, preferred_element_type=F32)
        imp = jnp.dot(psum, _overlap_matrix(ng), precision=HIGHEST, preferred_element_type=F32)
        sel = _select_mask(imp, _div_pow2(pos, SLC_BLOCK), n_slc)
        nk = past + LANES
        selk = jnp.dot(sel.astype(BF16), _expand_matrix(nk), preferred_element_type=F32)

        k_new = pad_new(kvn[:, g * hd:(g + 1) * hd])
        v_new = pad_new(kvn[:, 256 + g * hd:256 + (g + 1) * hd])
        all_rows = lambda slot, new: jnp.concatenate(
            [pg[0, 0, pl.ds(slot * G + g, page, stride=KVR), :].astype(BF16) for pg in pages] + [new], axis=0)
        sc = _dot_t(q, all_rows(2, k_new)) + bs_ref[g]
        j = lax.broadcasted_iota(jnp.int32, (rows, nk), 1)
        p = _masked_softmax(sc, jnp.where(selk > 0.5, pos - j, -1) >= 0).astype(BF16)
        o_slc = jnp.dot(p, all_rows(3, v_new), preferred_element_type=F32)

        kw_new = pad_new(kvn[:, 512 + g * hd:512 + (g + 1) * hd])
        vw_new = pad_new(kvn[:, 768 + g * hd:768 + (g + 1) * hd])
        kw = win_ref[0, 0, pl.ds(g, wb, stride=WR), :].astype(BF16)
        vw = win_ref[0, 0, pl.ds(G + g, wb, stride=WR), :].astype(BF16)
        sc = _dot_t(q, jnp.concatenate([kw, kw_new], axis=0)) + bw_ref[g]
        j = lax.broadcasted_iota(jnp.int32, (rows, wb + LANES), 1)
        d = pos - (past - wb + j)
        p = _masked_softmax(sc, jnp.where(d <= WINDOW, d, -1) >= 0).astype(BF16)
        o_win = jnp.dot(p, jnp.concatenate([vw, vw_new], axis=0), preferred_element_type=F32)

        o_ref[0, g] = gt[:, 0:1] * o_cmp + gt[:, 1:2] * o_slc + gt[:, 2:3] * o_win


def _nsa_sample(slot, page_table, cache_lin, win_lin, win_new, q_s, kvn_s, gates_s, lw, bias_c, bias_s, bias_w, *, tq):
    B, npages = page_table.shape
    G = NSA_KV_HEADS
    rows = NSA_REP * tq
    l = slot.layer
    page_specs = [pl.BlockSpec((1, 1) + cache_lin.shape[2:], lambda b, pt, p=p: (l, pt[b, p], 0, 0))
                  for p in range(npages)]
    full = lambda a: pl.BlockSpec(a.shape, lambda b, pt, n=a.ndim: (0,) * n)
    wts = [lw["cmp_pe"], lw["cmp_w1"], lw["cmp_w2"], lw["k_norm_g"].reshape(1, -1), bias_c, bias_s, bias_w]
    grid_spec = pltpu.PrefetchScalarGridSpec(
        num_scalar_prefetch=1, grid=(B,),
        in_specs=page_specs + [
            pl.BlockSpec((1, 1) + win_lin.shape[2:], lambda b, pt: (l, b, 0, 0)),
            pl.BlockSpec((1,) + win_new.shape[1:], lambda b, pt: (b, 0, 0)),
            pl.BlockSpec((1, G, rows, HEAD_DIM), lambda b, pt: (b, 0, 0, 0)),
            pl.BlockSpec((1, SUBLANES, kvn_s.shape[2]), lambda b, pt: (b, 0, 0)),
            pl.BlockSpec((1, G, rows, LANES), lambda b, pt: (b, 0, 0, 0))] + [full(w) for w in wts]
        + slot.in_specs(),
        out_specs=[pl.BlockSpec((1, G, rows, HEAD_DIM), lambda b, pt: (b, 0, 0, 0)),
                   pl.BlockSpec((1, 1) + win_lin.shape[2:], lambda b, pt: (l, b, 0, 0))])
    n_in = 1 + npages + 5 + len(wts)
    return pl.pallas_call(
        slot.wrap(functools.partial(_nsa_sample_kernel, npages=npages, tq=tq), n_in),
        grid_spec=grid_spec,
        out_shape=[jax.ShapeDtypeStruct((B, G, rows, HEAD_DIM), F32),
                   jax.ShapeDtypeStruct((slot.depth,) + win_lin.shape[1:], F32)],
        input_output_aliases=slot.aliases(n_in, 1),
        compiler_params=_cp(("arbitrary",)), name="nsa_sample",
    )(page_table, *([cache_lin] * npages), win_lin, win_new, q_s, kvn_s, gates_s, *wts, *slot.args())


def _dot3(a, b):
    ah = a.astype(BF16)
    bh = b.astype(BF16)
    al = (a - ah.astype(F32)).astype(BF16)
    bl = (b - bh.astype(F32)).astype(BF16)
    d = functools.partial(jnp.dot, preferred_element_type=F32)
    return d(ah, bh) + (d(al, bh) + d(ah, bl))


def _gdn_chunk_kernel(gq_ref, ggb_ref, n_ref, k2_ref, qe_ref, o0_ref, eg_ref):
    B, C = gq_ref.shape[:2]
    H = GDN_HEADS
    hd = HEAD_DIM
    W = H * hd
    HC = H * C
    ri = lax.broadcasted_iota(jnp.int32, (HC, HC), 0)
    ci = lax.broadcasted_iota(jnp.int32, (HC, HC), 1)
    tri = jnp.where(_div_pow2(ri, C) == _div_pow2(ci, C), ri - ci, -1)
    own_lanes = (_div_pow2(lax.broadcasted_iota(jnp.int32, (HC, W), 0), C)
                 == _div_pow2(lax.broadcasted_iota(jnp.int32, (HC, W), 1), hd))

    for b in range(B):
        stack = lambda ref, c0: jnp.concatenate([ref[b, :, c0 + h * hd:c0 + (h + 1) * hd] for h in range(H)], axis=0)
        q = stack(gq_ref, 0)
        k = stack(gq_ref, W)
        v = stack(gq_ref, 2 * W)
        gc = stack(ggb_ref, 0)
        beta = stack(ggb_ref, W)
        gcol = jnp.concatenate([gc] * (HC // hd), axis=-1)
        grow = gcol.T
        decay = jnp.exp(jnp.where(tri >= 0, gcol - grow, -jnp.inf))
        kb = k * beta
        g2 = _dot_t(jnp.concatenate([kb, q], axis=0).astype(BF16), k.astype(BF16))
        low = jnp.where(tri > 0, g2[:HC] * decay, 0.0)
        qk = jnp.where(tri >= 0, g2[HC:] * decay, 0.0)
        egc = jnp.exp(gc)
        x = jnp.concatenate([v * beta, kb * egc], axis=-1)
        pw = low
        span = 1
        while span < C:
            last = span * 2 >= C
            rhs = x if last else jnp.concatenate([pw, x], axis=-1)
            r = _dot3(pw, rhs)
            px = r if last else r[:, HC:]
            x = x - px if span == 1 else x + px
            if not last:
                pw = r[:, :HC]
            span *= 2
        uw = x.astype(BF16)
        g_last = jnp.concatenate([jnp.broadcast_to(gc[(h + 1) * C - 1:(h + 1) * C], (C, hd)) for h in range(H)],
                                 axis=0)
        kd = k * jnp.exp(g_last - gc)
        kd_bd = jnp.where(own_lanes, jnp.concatenate([kd] * H, axis=-1), 0.0).astype(BF16)
        nk = lax.dot_general(kd_bd, uw, (((0,), (0,)), ((), ())), preferred_element_type=F32)
        ow = jnp.dot(qk.astype(BF16), uw, preferred_element_type=F32)
        qe = q * egc - ow[:, hd:]
        eg = jnp.exp(g_last)
        for h in range(H):
            n_ref[b, h, 0] = nk[h * hd:(h + 1) * hd, :hd]
            k2_ref[b, h, 0] = nk[h * hd:(h + 1) * hd, hd:].astype(BF16)
            o0_ref[b, h, 0] = ow[h * C:(h + 1) * C, :hd]
            qe_ref[b, h, 0] = qe[h * C:(h + 1) * C].astype(BF16)
            eg_ref[b, h, 0] = eg[h * C:h * C + SUBLANES]


def _gdn_scan_kernel(n_ref, k2_ref, qe_ref, o0_ref, eg_ref, gz_ref, s0_ref, ng_ref, y_ref, so_ref, s_ref):
    n = pl.program_id(0)
    B = gz_ref.shape[0]
    hd = HEAD_DIM

    @pl.when(n == 0)
    def _():
        s_ref[...] = s0_ref[...]

    ng = ng_ref[...]
    for b in range(B):
        for h in range(GDN_HEADS):
            S = s_ref[b, h]
            s16 = S.astype(BF16)
            o = jnp.dot(qe_ref[b, h, 0], s16, preferred_element_type=F32) + o0_ref[b, h, 0]
            s_ref[b, h] = (eg_ref[b, h, 0][0:1] * S + n_ref[b, h, 0]
                           - jnp.dot(k2_ref[b, h, 0], s16, preferred_element_type=F32))
            z = gz_ref[b, :, h * hd:(h + 1) * hd]
            y_ref[b, :, h * hd:(h + 1) * hd] = (_rms(o, ng) * (z * _sigmoid(z))).astype(y_ref.dtype)

    @pl.when(n == pl.num_programs(0) - 1)
    def _():
        so_ref[...] = s_ref[...]


def _gdn_prompt(gq3, ggb3, z3, s0, norm_g):
    B, T, _ = gq3.shape
    C = GDN_CHUNK
    H = GDN_HEADS
    hd = HEAD_DIM
    W = H * hd
    nchunk = T // C
    item = lambda rows: pl.BlockSpec((B, H, 1, rows, hd), lambda n: (0, 0, n, 0, 0))
    shape = lambda rows, dt: jax.ShapeDtypeStruct((B, H, nchunk, rows, hd), dt)
    nn, k2, qe, o0, eg = pl.pallas_call(
        _gdn_chunk_kernel,
        grid=(nchunk,),
        in_specs=[pl.BlockSpec((B, C, 3 * W), lambda n: (0, n, 0)),
                  pl.BlockSpec((B, C, 2 * W), lambda n: (0, n, 0))],
        out_specs=[item(hd), item(hd), item(C), item(C), item(SUBLANES)],
        out_shape=[shape(hd, F32), shape(hd, BF16), shape(C, BF16), shape(C, F32), shape(SUBLANES, F32)],
        compiler_params=_cp(("parallel",)), name="gdn_chunk",
    )(gq3, ggb3)
    return pl.pallas_call(
        _gdn_scan_kernel,
        grid=(nchunk,),
        in_specs=[item(hd), item(hd), item(C), item(C), item(SUBLANES),
                  pl.BlockSpec((B, C, W), lambda n: (0, n, C_GZ // W)),
                  pl.BlockSpec(s0.shape, lambda n: (0, 0, 0, 0)),
                  pl.BlockSpec((1, hd), lambda n: (0, 0))],
        out_specs=[pl.BlockSpec((B, C, W), lambda n: (0, n, 0)),
                   pl.BlockSpec(s0.shape, lambda n: (0, 0, 0, 0))],
        out_shape=[jax.ShapeDtypeStruct((B, T, W), BF16), jax.ShapeDtypeStruct(s0.shape, F32)],
        scratch_shapes=[pltpu.VMEM(s0.shape, F32)],
        compiler_params=_cp(("arbitrary",)), name="gdn_scan",
    )(nn, k2, qe, o0, eg, z3, s0, norm_g.reshape(1, -1))


def _gdn_sample_kernel(kq_ref, v_ref, gb_ref, gz_ref, s0_ref, ng_ref, y_ref, so_ref, *, tq):
    bt = kq_ref.shape[0]
    hd = HEAD_DIM
    ri = lax.broadcasted_iota(jnp.int32, (hd, hd), 0)
    ci = lax.broadcasted_iota(jnp.int32, (hd, hd), 1)
    eye = jnp.where(ri == ci, 1.0, 0.0).astype(F32)
    ng = ng_ref[...]

    def body(bi, carry):
        for h in range(GDN_HEADS):
            kq = kq_ref[bi, h]
            cols = _dot_t(eye, kq, HIGHEST)
            gb = gb_ref[bi, h]
            v = v_ref[bi, h]
            S = s0_ref[bi, h]
            outs = []
            for t in range(tq):
                a = jnp.exp(gb[t:t + 1])
                kc = cols[:, t:t + 1]
                qc = cols[:, tq + t:tq + t + 1]
                Sa = S * a
                stk = jnp.sum(Sa * kc, axis=0, keepdims=True)
                vn = gb[tq + t:tq + t + 1] * (v[t:t + 1] - stk)
                S = Sa + kc * vn
                outs.append(jnp.sum(S * qc, axis=0, keepdims=True))
            so_ref[0, bi, h] = S
            o = jnp.concatenate(outs + [jnp.zeros((SUBLANES - tq, hd), F32)], axis=0)
            z = gz_ref[bi, h]
            y_ref[bi, h] = _rms(o, ng) * (z * _sigmoid(z))
        return carry

    lax.fori_loop(0, bt, body, 0)


def _gdn_sample(slot, kq_s, v_s, gb_s, gz_s, s0, norm_g, *, tq):
    B = kq_s.shape[0]
    bt = 8
    layer = slot.layer
    blk = lambda a: pl.BlockSpec((bt,) + a.shape[1:], lambda i: (i, 0, 0, 0))
    in_specs = [blk(kq_s), blk(v_s), blk(gb_s), blk(gz_s), blk(s0), pl.BlockSpec((1, HEAD_DIM), lambda i: (0, 0))]
    return pl.pallas_call(
        slot.wrap(functools.partial(_gdn_sample_kernel, tq=tq), len(in_specs)),
        grid=(B // bt,),
        in_specs=in_specs + slot.in_specs(),
        out_specs=[blk(v_s), pl.BlockSpec((1, bt) + s0.shape[1:], lambda i: (layer, i, 0, 0, 0))],
        out_shape=[jax.ShapeDtypeStruct(v_s.shape, F32), jax.ShapeDtypeStruct((slot.depth,) + s0.shape, F32)],
        input_output_aliases=slot.aliases(len(in_specs), 1),
        compiler_params=_cp(("parallel",)), name="gdn_sample",
    )(kq_s, v_s, gb_s, gz_s, s0, norm_g.reshape(1, -1), *slot.args())


def _mix_out_kernel(ya_ref, yb_ref, yc_ref, x_ref, onb_ref, w_ref, o_ref, mix_ref):
    @pl.when(pl.program_id(1) == 0)
    def _():
        mix_ref[:, 0:512] = ya_ref[...]
        mix_ref[:, 512:1536] = _rms(yb_ref[...], onb_ref[...]).astype(BF16)
        mix_ref[:, 1536:2048] = yc_ref[...]

    o_ref[...] = x_ref[...] + jnp.dot(mix_ref[...], w_ref[...], preferred_element_type=F32)


def _mix_out(ya, yb, yc, x, onb, w, *, tn=1024):
    M, N = x.shape
    tm = _row_tile(M)
    K = w.shape[0]
    return pl.pallas_call(
        _mix_out_kernel,
        grid=(M // tm, N // tn),
        in_specs=[pl.BlockSpec((tm, 512), lambda i, j: (i, 0)),
                  pl.BlockSpec((tm, 1024), lambda i, j: (i, 0)),
                  pl.BlockSpec((tm, 512), lambda i, j: (i, 0)),
                  pl.BlockSpec((tm, tn), lambda i, j: (i, j)),
                  pl.BlockSpec((1, 1024), lambda i, j: (0, 0)),
                  w.spec((K, tn), lambda i, j: (0, j))],
        out_specs=pl.BlockSpec((tm, tn), lambda i, j: (i, j)),
        out_shape=jax.ShapeDtypeStruct((M, N), F32),
        scratch_shapes=[pltpu.VMEM((tm, K), BF16)],
        compiler_params=_cp(("parallel", "arbitrary")), name="mix_out",
    )(ya, yb, yc, x, onb.reshape(1, -1), w.array)


def _matmul_res_kernel(a_ref, w_ref, r_ref, o_ref):
    o_ref[...] = r_ref[...] + jnp.dot(a_ref[...], w_ref[...], preferred_element_type=F32)


def _matmul_res(a, w, res, *, tn=512):
    M, K = a.shape
    N = w.shape[1]
    tm = _row_tile(M)
    return pl.pallas_call(
        _matmul_res_kernel,
        grid=(M // tm, N // tn),
        in_specs=[pl.BlockSpec((tm, K), lambda i, j: (i, 0)),
                  w.spec((K, tn), lambda i, j: (0, j)),
                  pl.BlockSpec((tm, tn), lambda i, j: (i, j))],
        out_specs=pl.BlockSpec((tm, tn), lambda i, j: (i, j)),
        out_shape=jax.ShapeDtypeStruct((M, N), F32),
        compiler_params=_cp(("parallel", "parallel")), name="matmul_res",
    )(a, w.array, res)


def _layer_weights(l, p):
    w_in = p["w_in"][l]
    w_main = jnp.concatenate([w_in[:, :4096], w_in[:, 4120:5656], w_in[:, 5664:6176]], axis=1).astype(BF16)
    w_small = jnp.concatenate([w_in[:, 4096:4120], w_in[:, 5656:5664],
                               jnp.zeros((w_in.shape[0], LANES - 32), w_in.dtype)], axis=1).astype(BF16)
    lane_vec = lambda v: jnp.zeros((1, LANES), F32).at[0, S_GA:S_GA + GDN_HEADS].set(v)
    return {
        "norm_mix_g": p["norm_mix_g"][l], "w_main": w_main, "w_small": w_small,
        "conv_a_w": p["conv_a_w"][l], "q_norm_g": p["q_norm_g"][l], "k_norm_g": p["k_norm_g"][l],
        "cmp_pe": p["cmp_pe"][l],
        "cmp_w1": p["cmp_w1"][l].reshape(2, 2, CMP_BLOCK // 2 * HEAD_DIM, CMP_HIDDEN).astype(BF16),
        "cmp_w2": p["cmp_w2"][l].astype(BF16),
        "gdn_conv_w": p["gdn_conv_w"][l], "alog_v": lane_vec(p["gdn_a_log"][l]), "dtb_v": lane_vec(p["gdn_dt_bias"][l]),
        "gdn_norm_g": p["gdn_norm_g"][l], "out_norm_a": p["out_norm_a"][l], "out_norm_b": p["out_norm_b"][l],
        "w_out": _LayerOf(p["w_out_bf16"], l), "norm_ffn_g": p["norm_ffn_g"][l],
        "ffn_up": _LayerOf(p["ffn_up_bf16"], l), "ffn_conv_w": p["ffn_conv_w"][l],
        "ffn_down": _LayerOf(p["ffn_down_bf16"], l),
    }


HALO_ROWS = 16


def _ffn_up_kernel(x_ref, xp_ref, g_ref, wa_ref, wv_ref, ha_ref, hv_ref, ca_ref, cv_ref,
                   o_ref, hoa_ref, hov_ref, xn_ref, *, tstride, tiles_per_seq):
    tn = wa_ref.shape[1]
    prompt = tstride == 1

    @pl.when(pl.program_id(1) == 0)
    def _():
        g = g_ref[...]
        if prompt:
            xn_ref[0:HALO_ROWS] = _rms(xp_ref[...], g).astype(BF16)
            xn_ref[HALO_ROWS:] = _rms(x_ref[...], g).astype(BF16)
        else:
            xn_ref[...] = _rms(x_ref[...], g).astype(BF16)

    xn = xn_ref[...]

    def part(w_ref, hist_ref, cw_ref, ho_ref):
        up = jnp.dot(xn, w_ref[...], preferred_element_type=F32)
        if prompt:
            start = (pl.program_id(0) % tiles_per_seq) == 0
            halo = jnp.where(start, hist_ref[0], up[HALO_ROWS - SUBLANES:HALO_ROWS])
            x = up[HALO_ROWS:]
            ho_ref[0] = x[x.shape[0] - SUBLANES:]
        else:
            halo = _ld(hist_ref, 0, tn)
            x = up
            _st(ho_ref, 0, tn, x[x.shape[0] - halo.shape[0]:])
        w = cw_ref[...]
        return w[0:1] * _shift(x, halo, 2, tstride) + w[1:2] * _shift(x, halo, 1, tstride) + w[2:3] * x

    a = part(wa_ref, ha_ref, ca_ref, hoa_ref)
    v = part(wv_ref, hv_ref, cv_ref, hov_ref)
    o_ref[...] = (a * _sigmoid(a) * v).astype(o_ref.dtype)


def _ffn_up(h, g, w_up, conv_w, hist, *, sample, nseq, seq_len, tn=512):
    M, K = h.shape
    tm = _row_tile(seq_len)
    F = w_up.shape[1] // 2
    nj = F // tn
    common = [pl.BlockSpec((1, K), lambda i, j: (0, 0)),
              w_up.spec((K, tn), lambda i, j: (0, j)),
              w_up.spec((K, tn), lambda i, j: (0, j + nj))]
    cw = [pl.BlockSpec((3, tn), lambda i, j: (0, j)), pl.BlockSpec((3, tn), lambda i, j: (0, j + nj))]
    if not sample:
        tps = seq_len // tm
        hspec = lambda off: pl.BlockSpec((1, SUBLANES, tn), lambda i, j, off=off: (i // tps, 0, j + off))
        hout = pl.BlockSpec((1, SUBLANES, tn), lambda i, j: (i, 0, j))
        act, tail_a, tail_v = pl.pallas_call(
            functools.partial(_ffn_up_kernel, tstride=1, tiles_per_seq=tps),
            grid=(M // tm, nj),
            in_specs=[pl.BlockSpec((tm, K), lambda i, j: (i, 0)),
                      pl.BlockSpec((HALO_ROWS, K), lambda i, j: (jnp.maximum(i * (tm // HALO_ROWS) - 1, 0), 0))]
            + common + [hspec(0), hspec(nj)] + cw,
            out_specs=[pl.BlockSpec((tm, tn), lambda i, j: (i, j)), hout, hout],
            out_shape=[jax.ShapeDtypeStruct((M, F), BF16)] + [jax.ShapeDtypeStruct((M // tm, SUBLANES, F), F32)] * 2,
            scratch_shapes=[pltpu.VMEM((tm + HALO_ROWS, K), BF16)],
            compiler_params=_cp(("arbitrary", "arbitrary")), name="ffn_up_prompt",
        )(h, h, g.reshape(1, K), w_up.array, w_up.array, hist, hist, conv_w, conv_w)
        return act, tail_a[tps - 1::tps], tail_v[tps - 1::tps]
    hk = hist.shape[0]
    hspec = lambda off: pl.BlockSpec((hk, nseq, tn), lambda i, j, off=off: (0, 0, j + off))
    hout = pl.BlockSpec((hk, nseq, tn), lambda i, j: (0, 0, j))
    return pl.pallas_call(
        functools.partial(_ffn_up_kernel, tstride=nseq, tiles_per_seq=1),
        grid=(1, nj),
        in_specs=[pl.BlockSpec((M, K), lambda i, j: (0, 0)), pl.BlockSpec((HALO_ROWS, K), lambda i, j: (0, 0))]
        + common + [hspec(0), hspec(nj)] + cw,
        out_specs=[pl.BlockSpec((M, tn), lambda i, j: (0, j)), hout, hout],
        out_shape=[jax.ShapeDtypeStruct((M, F), BF16)] + [jax.ShapeDtypeStruct((hk, nseq, F), F32)] * 2,
        scratch_shapes=[pltpu.VMEM((M, K), BF16)],
        compiler_params=_cp(("arbitrary", "arbitrary")), name="ffn_up_sample",
    )(h, h, g.reshape(1, K), w_up.array, w_up.array, hist, hist, conv_w, conv_w)


def _dense_tail(x2, ya, yb, yc, lw, hist_ffn, *, sample, nseq, seq_len):
    h = _mix_out(ya, yb, yc, x2, lw["out_norm_b"], lw["w_out"])
    act, hist_a, hist_v = _ffn_up(h, lw["norm_ffn_g"], lw["ffn_up"], lw["ffn_conv_w"], hist_ffn,
                                  sample=sample, nseq=nseq, seq_len=seq_len)
    return _matmul_res(act, lw["ffn_down"], h), jnp.concatenate([hist_a, hist_v], axis=-1)


def _prompt_layer(x2, lw, bias_tiles, bias_cmp, rows_slot, *, B, T):
    M = B * T
    G, R = NSA_KV_HEADS, NSA_REP
    z, zs = _norm_matmul(x2, lw["norm_mix_g"], lw["w_main"], lw["w_small"])
    hu0 = jnp.zeros((B, SUBLANES, 512), F32)
    hq0 = jnp.zeros((B, SUBLANES, 1536), F32)
    ya, q, rows, win, kvb, gates, gq, ggb, hu, rows_lin = _prep(z, zs, hu0, hq0, lw, sample=False, nseq=B, seq_len=T,
                                                                slot=rows_slot)
    cmp_kv = _compress_prompt(rows.reshape(B, T, 1024), lw)
    gates_g = gates[:, :3 * G * R].reshape(M, 3, G, R).transpose(2, 0, 1, 3).reshape(G, M, 3 * R)
    gates_g = jnp.pad(gates_g, ((0, 0), (0, 0), (0, LANES - 3 * R)))
    yb = _nsa_prompt(q, cmp_kv, kvb.reshape(B, T, 1024), gates_g, bias_tiles, bias_cmp, B=B, T=T)
    s0 = jnp.zeros((B, GDN_HEADS, HEAD_DIM, HEAD_DIM), F32)
    yc, s_new = _gdn_prompt(gq.reshape(B, T, 1536), ggb.reshape(B, T, 1024), z.reshape(B, T, C_END), s0,
                            lw["gdn_norm_g"])
    hf0 = jnp.zeros((B, SUBLANES, lw["ffn_conv_w"].shape[1]), F32)
    out, up = _dense_tail(x2, ya, yb, yc.reshape(M, 512), lw, hf0, sample=False, nseq=B, seq_len=T)
    return out, rows_lin, win, hu, z, s_new, up


def _sample_layer(x2, lw, st, bias_c, bias_s, bias_w, win_slot, gdn_slot, *, B, T):
    M = B * T
    G, R, H, hd = NSA_KV_HEADS, NSA_REP, GDN_HEADS, HEAD_DIM
    z, zs = _norm_matmul(x2, lw["norm_mix_g"], lw["w_main"], lw["w_small"])
    hu0 = jnp.swapaxes(st["conv_a"], 0, 1)
    hq0 = jnp.swapaxes(st["gdn_conv"], 0, 1)
    ya, q, rows, win, kvb, gates, gq, ggb, hu = _prep(z, zs, hu0, hq0, lw, sample=True, nseq=B, seq_len=T)
    tb = lambda a: a.reshape(T, B, -1)
    q_s = tb(q).reshape(T, B, G, R, hd).transpose(1, 2, 3, 0, 4).reshape(B, G, R * T, hd)
    gt = tb(gates)[:, :, :3 * G * R].reshape(T, B, 3, G, R).transpose(1, 3, 4, 0, 2).reshape(B, G, R * T, 3)
    gates_s = jnp.pad(gt, ((0, 0), (0, 0), (0, 0), (0, LANES - 3)))
    kvn_s = jnp.pad(jnp.swapaxes(tb(kvb), 0, 1), ((0, 0), (0, SUBLANES - T), (0, 0)))
    win_new = jnp.swapaxes(tb(win), 0, 1).reshape(B, T * 2 * G, hd)
    yb_s, win = _nsa_sample(win_slot, st["page_table"], st["cache_lin"], st["win_lin"], win_new, q_s, kvn_s, gates_s, lw,
                            bias_c, bias_s, bias_w, tq=T)
    yb = yb_s.reshape(B, G, R, T, hd).transpose(3, 0, 1, 2, 4).reshape(M, G * R * hd)
    gq4 = tb(gq).reshape(T, B, 3, H, hd)
    bh = lambda a: a.transpose(1, 2, 0, 3)
    padt = lambda a: jnp.pad(a, ((0, 0), (0, 0), (0, SUBLANES - T), (0, 0)))
    kq_s = jnp.concatenate([bh(gq4[:, :, 1]), bh(gq4[:, :, 0])], axis=2)
    v_s = padt(bh(gq4[:, :, 2]))
    ggb4 = tb(ggb).reshape(T, B, 2, H, hd)
    gb_s = jnp.concatenate([bh(ggb4[:, :, 0]), bh(ggb4[:, :, 1])], axis=2)
    gz_s = padt(bh(tb(z)[:, :, C_GZ:C_END].reshape(T, B, H, hd)))
    yc_s, s_new = _gdn_sample(gdn_slot, kq_s, v_s, gb_s, gz_s, st["gdn"], lw["gdn_norm_g"], tq=T)
    yc = yc_s[:, :, :T].transpose(2, 0, 1, 3).reshape(M, H * hd).astype(BF16)
    hf0 = jnp.swapaxes(st["ffn_conv"], 0, 1)
    out, up = _dense_tail(x2, ya, yb, yc, lw, hf0, sample=True, nseq=B, seq_len=T)
    return out, rows, win, hu, z, s_new, up


def kernel(x_prompt, x_sample, cache_kv, cache_win, state_conv_a, state_gdn_conv, state_gdn, state_ffn_conv,
           page_table, rel_bias, norm_mix_g, w_in, conv_a_w, q_norm_g, k_norm_g, cmp_pe, cmp_w1, cmp_w2,
           gdn_conv_w, gdn_a_log, gdn_dt_bias, gdn_norm_g, out_norm_a, out_norm_b, w_out, norm_ffn_g,
           ffn_up, ffn_conv_w, ffn_down):
    params = dict(norm_mix_g=norm_mix_g, w_in=w_in, conv_a_w=conv_a_w, q_norm_g=q_norm_g, k_norm_g=k_norm_g,
                  cmp_pe=cmp_pe, cmp_w1=cmp_w1, cmp_w2=cmp_w2, gdn_conv_w=gdn_conv_w, gdn_a_log=gdn_a_log,
                  gdn_dt_bias=gdn_dt_bias, gdn_norm_g=gdn_norm_g, out_norm_a=out_norm_a, out_norm_b=out_norm_b,
                  w_out=w_out, norm_ffn_g=norm_ffn_g, ffn_up=ffn_up, ffn_conv_w=ffn_conv_w, ffn_down=ffn_down,
                  w_out_bf16=w_out.astype(BF16), ffn_up_bf16=ffn_up.astype(BF16), ffn_down_bf16=ffn_down.astype(BF16))
    depth = w_in.shape[0]
    Bp, T, D = x_prompt.shape
    Bs, Ts, _ = x_sample.shape
    G, R, hd = NSA_KV_HEADS, NSA_REP, HEAD_DIM
    n_pool, page = cache_kv.shape[1], cache_kv.shape[2]
    npages = page_table.shape[1]
    past = npages * page
    wb = cache_win.shape[2]
    L = past + Ts
    assert ((L - CMP_BLOCK) // CMP_STRIDE) * CMP_STRIDE + CMP_BLOCK <= past, "compressed blocks must lie in the cache"
    assert T % KEY_TILE == 0 and T >= WINDOW + Q_BLOCK and 3 <= Ts <= SUBLANES // 2

    thr = _bucket_thresholds()
    rb_flat = rel_bias.reshape(-1)
    nqb = T // Q_BLOCK
    bias_tiles = _bias_table(thr, rb_flat, nqb, Q_BLOCK, Q_BLOCK, a0=0, an=Q_BLOCK, qs=1, ks=1)
    bias_cmp = _bias_table(thr, rb_flat, nqb, Q_BLOCK, T // CMP_STRIDE,
                           a0=-(CMP_BLOCK - 1), an=Q_BLOCK, qs=1, ks=CMP_STRIDE)

    def sample_bias(ncols, a0, ks):
        t = _bias_table(thr, rb_flat, 1, SUBLANES, ncols, a0=a0, an=0, qs=1, ks=ks)[0]
        return t[:, :Ts].reshape(G, R * Ts, ncols)

    bias_c = sample_bias(past // CMP_STRIDE, past - (CMP_BLOCK - 1), CMP_STRIDE)
    bias_s = sample_bias(past + LANES, past, 1)
    bias_w = sample_bias(wb + LANES, wb, 1)

    cache_lin = cache_kv.reshape(depth, n_pool, page * 4 * G, hd)
    win_lin = cache_win.reshape(depth, Bs, wb * 2 * G, hd)
    assert wb == WINDOW, "the new window buffer is the old one shifted by the new rows"

    xp = x_prompt.reshape(Bp * T, D)
    xs = jnp.swapaxes(x_sample, 0, 1).reshape(Ts * Bs, D)
    outs_p, outs_s = [], []
    rows_p_all = win_s_all = gdn_s_all = None
    for l in range(depth):
        lw = _layer_weights(l, params)
        xp, rows_p_all, win, hu, z, s_new, up = _prompt_layer(xp, lw, bias_tiles, bias_cmp,
                                                              _LayerSlot(l, depth, rows_p_all), B=Bp, T=T)
        wl = min(WINDOW, T)
        outs_p.append((
            win.reshape(Bp, T, 2, G, hd)[:, T - wl:],
            hu[:, SUBLANES - 2:],
            z.reshape(Bp, T, C_END)[:, T - 3:, C_GQKV:C_GZ],
            s_new,
            up[:, SUBLANES - 2:]))
        st = dict(page_table=page_table, cache_lin=cache_lin, win_lin=win_lin, conv_a=state_conv_a[l],
                  gdn_conv=state_gdn_conv[l], gdn=state_gdn[l], ffn_conv=state_ffn_conv[l])
        xs, rows, win_s_all, hu, z, gdn_s_all, up = _sample_layer(
            xs, lw, st, bias_c, bias_s, bias_w, _LayerSlot(l, depth, win_s_all), _LayerSlot(l, depth, gdn_s_all),
            B=Bs, T=Ts)
        tb = lambda a: jnp.swapaxes(a.reshape(Ts, Bs, -1), 0, 1)
        outs_s.append((
            tb(rows).reshape(Bs, Ts, 4, G, hd),
            jnp.swapaxes(hu, 0, 1),
            tb(z)[:, Ts - 3:, C_GQKV:C_GZ],
            jnp.swapaxes(up, 0, 1)))
    y_p = xp.reshape(Bp, T, D)
    y_s = jnp.swapaxes(xs.reshape(Ts, Bs, D), 0, 1)
    stack = lambda outs, i: jnp.stack([o[i] for o in outs])
    return (y_p, y_s,
            rows_p_all.reshape(depth, Bp, T, 4, G, hd), stack(outs_s, 0),
            stack(outs_p, 0), win_s_all.reshape(depth, Bs, wb, 2, G, hd),
            stack(outs_p, 1), stack(outs_s, 1), stack(outs_p, 2), stack(outs_s, 2),
            stack(outs_p, 3), gdn_s_all, stack(outs_p, 4), stack(outs_s, 3))
```
